```python
import jax, jax.numpy as jnp
from jax import lax
import numpy as np

D_MODEL = 2048
BATCH = 8
SEQ = 2048
DEPTH = 2

GRID_W = 64
CTX_LEN = 256
N_MOD = 9
D_FF = 5632
LN_EPS = 1e-5
DEEPNORM_ALPHA = (2 * DEPTH) ** 0.25
DEEPNORM_BETA = (8 * DEPTH) ** -0.25
FFN_RES = 0.5

A_HEADS = 8
A_KV_HEADS = 2
A_GROUP = A_HEADS // A_KV_HEADS
A_HEAD_DIM = 128
A_Q_DIM = A_HEADS * A_HEAD_DIM
A_KV_DIM = A_KV_HEADS * A_HEAD_DIM
WINDOW = 128
A_BLOCK = 128
N_SIDE = -(-WINDOW // A_BLOCK)
ROPE_BASE = 10000.0
MASK_VALUE = -1e30

R_HEAD_DIM = 64
R_DIM = D_MODEL // 2
R_HEADS = R_DIM // R_HEAD_DIM
DECAY_LORA = 96
ICLR_LORA = 96
GATE_LORA = 256
CONV_W = 3
R_GN_EPS = 64e-5

G_HEADS = 4
G_KEY_DIM = D_MODEL // 2
G_VAL_DIM = D_MODEL
G_DK = G_KEY_DIM // G_HEADS
G_DV = G_VAL_DIM // G_HEADS
G_GATE_RANK = 16
G_GATE_NORM = 16.0
G_CHUNK = 64
G_EPS = 1e-5

ATT_SPLIT = (A_Q_DIM, A_KV_DIM, A_KV_DIM)
RWKV_SPLIT = (R_DIM, R_DIM, R_DIM, 2 * DECAY_LORA, 2 * ICLR_LORA, GATE_LORA)
GLA_SPLIT = (G_KEY_DIM, G_KEY_DIM, G_VAL_DIM, G_VAL_DIM, 2 * G_GATE_RANK)
GATE_SPLIT = (D_MODEL, D_MODEL, D_MODEL)
ATT_COLS = sum(ATT_SPLIT)
RWKV_COLS = sum(RWKV_SPLIT)
GLA_COLS = sum(GLA_SPLIT)
GATE_COLS = sum(GATE_SPLIT)
GROUP_SPLIT = (ATT_COLS, RWKV_COLS, GLA_COLS, GATE_COLS)
IN_COLS = sum(GROUP_SPLIT)

kernel_name = "hybrid_dit_swa_rwkv7_gla_block"


def _split(z, sizes):
    idx = [int(i) for i in np.cumsum(sizes)[:-1]]
    return jnp.split(z, idx, axis=-1)


def _heads(t, n_heads):
    return t.reshape(*t.shape[:-1], n_heads, t.shape[-1] // n_heads)


def _layer_norm(x):
    xf = x.astype(jnp.float32)
    mu = jnp.mean(xf, -1, keepdims=True)
    var = jnp.mean(jnp.square(xf - mu), -1, keepdims=True)
    return ((xf - mu) * lax.rsqrt(var + LN_EPS)).astype(x.dtype)


def _post_norm(x_res, y, g, b):
    return _layer_norm(DEEPNORM_ALPHA * x_res + y) * g + b


def _modulate(x, shift, scale):
    return _layer_norm(x) * (1 + scale) + shift


def _swiglu(h, w_gu, w_down):
    gate, up = jnp.split(h @ w_gu, 2, axis=-1)
    return (jax.nn.silu(gate) * up) @ w_down


def _ffn_sublayer(x, shift, scale, gate, w_gu, w_down, g, b):
    y = _swiglu(_modulate(x, shift, scale), w_gu, w_down)
    return _post_norm(x, FFN_RES * gate * y, g, b)


def _axial_rope(n, dtype):
    rows = n // GRID_W
    row = jnp.repeat(jnp.arange(rows, dtype=jnp.float32), GRID_W)
    col = jnp.tile(jnp.arange(GRID_W, dtype=jnp.float32), rows)
    n_freq = A_HEAD_DIM // 4
    inv_freq = ROPE_BASE ** (-jnp.arange(n_freq, dtype=jnp.float32) / n_freq)
    ang = jnp.stack([row, col], -1)[:, :, None] * inv_freq
    return jnp.cos(ang)[:, None].astype(dtype), jnp.sin(ang)[:, None].astype(dtype)


def _apply_rope(x, cos, sin):
    xr = x.reshape(*x.shape[:-1], 2, 2, A_HEAD_DIM // 4)
    x1, x2 = xr[..., 0, :], xr[..., 1, :]
    out = jnp.stack([x1 * cos - x2 * sin, x2 * cos + x1 * sin], axis=-2)
    return out.reshape(x.shape)


def _window_attention_latent(q, q_plain, k, v, k_ctx, v_ctx, sink):
    B, S = q.shape[:2]
    C = k_ctx.shape[1]
    nb = S // A_BLOCK
    n_band = 2 * N_SIDE + 1
    band = n_band * A_BLOCK
    scale = A_HEAD_DIM ** -0.5
    qb = q.reshape(B, nb, A_BLOCK, A_KV_HEADS, A_GROUP, A_HEAD_DIM)
    qpb = q_plain.reshape(B, nb, A_BLOCK, A_KV_HEADS, A_GROUP, A_HEAD_DIM)
    pad = ((0, 0), (N_SIDE * A_BLOCK, N_SIDE * A_BLOCK), (0, 0), (0, 0))
    kp = jnp.pad(k, pad).reshape(B, nb + 2 * N_SIDE, A_BLOCK, A_KV_HEADS, A_HEAD_DIM)
    vp = jnp.pad(v, pad).reshape(B, nb + 2 * N_SIDE, A_BLOCK, A_KV_HEADS, A_HEAD_DIM)
    kband = jnp.concatenate([kp[:, j:j + nb] for j in range(n_band)], axis=2)
    vband = jnp.concatenate([vp[:, j:j + nb] for j in range(n_band)], axis=2)
    s_loc = jnp.einsum('bnqhgd,bnkhd->bnhgqk', qb, kband).astype(jnp.float32) * scale
    blk = jnp.arange(nb)[:, None, None]
    qpos = blk * A_BLOCK + jnp.arange(A_BLOCK)[None, :, None]
    kpos = (blk - N_SIDE) * A_BLOCK + jnp.arange(band)[None, None, :]
    valid = (jnp.abs(qpos - kpos) <= WINDOW) & (kpos >= 0) & (kpos < S)
    s_loc = jnp.where(valid[None, :, None, None], s_loc, MASK_VALUE)
    s_ctx = jnp.einsum('bnqhgd,bchd->bnhgqc', qpb, k_ctx).astype(jnp.float32) * scale
    s_sink = jnp.broadcast_to(sink.reshape(A_KV_HEADS, A_GROUP, 1, 1).astype(jnp.float32),
                              s_loc.shape[:-1] + (1,))
    p = jax.nn.softmax(jnp.concatenate([s_loc, s_ctx, s_sink], axis=-1), axis=-1)
    p_loc = p[..., :band].astype(v.dtype)
    p_ctx = p[..., band:band + C].astype(v.dtype)
    o = (jnp.einsum('bnhgqk,bnkhd->bnqhgd', p_loc, vband)
         + jnp.einsum('bnhgqc,bchd->bnqhgd', p_ctx, v_ctx))
    return o.reshape(B, S, A_Q_DIM)


def _context_attention(q, k, v, sink):
    B, C = q.shape[:2]
    qg = q.reshape(B, C, A_KV_HEADS, A_GROUP, A_HEAD_DIM)
    s = jnp.einsum('bqhgd,bkhd->bhgqk', qg, k).astype(jnp.float32) * (A_HEAD_DIM ** -0.5)
    s_sink = jnp.broadcast_to(sink.reshape(A_KV_HEADS, A_GROUP, 1, 1).astype(jnp.float32),
                              s.shape[:-1] + (1,))
    p = jax.nn.softmax(jnp.concatenate([s, s_sink], axis=-1), axis=-1)
    o = jnp.einsum('bhgqk,bkhd->bqhgd', p[..., :C].astype(v.dtype), v)
    return o.reshape(B, C, A_Q_DIM)


def _centred_dwconv(z, w):
    L = z.shape[1]
    pad = CONV_W // 2
    zp = jnp.pad(z, ((0, 0), (pad, pad), (0, 0)))
    out = zp[:, 0:L] * w[0]
    for i in range(1, CONV_W):
        out = out + zp[:, i:i + L] * w[i]
    return out


def _rwkv_inputs(z, p):
    z = _centred_dwconv(z, p['rwkv_conv'])
    r, k, v, wc, ac, gc = _split(z, RWKV_SPLIT)
    wc = jnp.split(wc, 2, axis=-1)
    ac = jnp.split(ac, 2, axis=-1)
    g = jax.nn.sigmoid(gc) @ p['rwkv_g2']
    kk = _heads((k * p['rwkv_k_k']).astype(jnp.float32), R_HEADS)
    kk = kk / jnp.maximum(jnp.linalg.norm(kk, axis=-1, keepdims=True), 1e-12)
    dirs = []
    for d in range(2):
        w = -jax.nn.softplus(-(p['rwkv_w0'][d] + jnp.tanh(wc[d]) @ p['rwkv_w2'][d])) - 0.5
        decay = jnp.exp(-jnp.exp(w.astype(jnp.float32)))
        a = jax.nn.sigmoid(p['rwkv_a0'][d] + ac[d] @ p['rwkv_a2'][d])
        kd = k * (1 + (a - 1) * p['rwkv_k_a'])
        dirs.append((_heads(decay, R_HEADS), _heads(kd, R_HEADS), _heads(a, R_HEADS)))
    return _heads(r, R_HEADS), _heads(v, R_HEADS), kk, g, dirs


def _rwkv7_scan(r, decay, k, v, kk, a, s0, reverse):
    dt = r.dtype
    xs = tuple(jnp.moveaxis(t.astype(jnp.float32), 1, 0) for t in (r, decay, k, v, kk, a))

    def step(s, inp):
        r_t, w_t, k_t, v_t, kk_t, a_t = inp
        sa = jnp.einsum('bhij,bhj->bhi', s, -kk_t)
        s = (s * w_t[:, :, None, :] + sa[..., None] * (kk_t * a_t)[:, :, None, :]
             + v_t[..., None] * k_t[:, :, None, :])
        return s, jnp.einsum('bhij,bhj->bhi', s, r_t)

    s_fin, ys = lax.scan(step, s0, xs, reverse=reverse)
    return jnp.moveaxis(ys, 0, 1).astype(dt), s_fin


def _rwkv_output(ys, r, v, kds, g, p, dtype):
    o = (ys[0] + ys[1]).astype(jnp.float32)
    mu = jnp.mean(o, -1, keepdims=True)
    var = jnp.mean(jnp.square(o - mu), -1, keepdims=True)
    o = (o - mu) * lax.rsqrt(var + R_GN_EPS)
    o = o * p['rwkv_ln_w'].reshape(R_HEADS, R_HEAD_DIM) + p['rwkv_ln_b'].reshape(R_HEADS, R_HEAD_DIM)
    bonus = sum(jnp.sum(r * kd * p['rwkv_r_k'], axis=-1, keepdims=True) * v for kd in kds)
    B, L = o.shape[:2]
    return ((o + bonus).reshape(B, L, R_DIM) * g).astype(dtype)


def _rwkv_mixer(z_lat, z_ctx, p, with_ctx_out):
    r_c, v_c, kk_c, g_c, dirs_c = _rwkv_inputs(z_ctx, p)
    r_l, v_l, kk_l, g_l, dirs_l = _rwkv_inputs(z_lat, p)
    s0 = jnp.zeros((z_lat.shape[0], R_HEADS, R_HEAD_DIM, R_HEAD_DIM), jnp.float32)
    ys_l, ys_c = [], []
    for d in range(2):
        rev = d == 1
        dec_c, k_c, a_c = dirs_c[d]
        y_c, s_c = _rwkv7_scan(r_c, dec_c, k_c, v_c, kk_c, a_c, s0, rev)
        dec_l, k_l, a_l = dirs_l[d]
        y_l, _ = _rwkv7_scan(r_l, dec_l, k_l, v_l, kk_l, a_l, s_c, rev)
        ys_c.append(y_c)
        ys_l.append(y_l)
    out_l = _rwkv_output(ys_l, r_l, v_l, [dl[1] for dl in dirs_l], g_l, p, z_lat.dtype)
    out_c = _rwkv_output(ys_c, r_c, v_c, [dc[1] for dc in dirs_c], g_c, p, z_ctx.dtype) if with_ctx_out else None
    return out_l, out_c


def _gla_inputs(z, p):
    q, k, v, g, ac = _split(z, GLA_SPLIT)
    ac = jnp.split(ac, 2, axis=-1)
    q = _heads(q, G_HEADS) * (G_DK ** -0.5)
    log_a = [_heads(jax.nn.log_sigmoid((ac[d] @ p['gla_wa2'][d] + p['gla_ba'][d]).astype(jnp.float32))
                    / G_GATE_NORM, G_HEADS) for d in range(2)]
    return q, _heads(k, G_HEADS), _heads(v, G_HEADS), g, log_a


def _gla_chunked(q, k, v, log_a, s0):
    dt = v.dtype
    B, L, H, _ = q.shape
    Dv = v.shape[-1]
    nc = L // G_CHUNK

    def chunks(t):
        return t.reshape(B, nc, G_CHUNK, H, t.shape[-1]).astype(jnp.float32)

    qc, kc, vc, lac = chunks(q), chunks(k), chunks(v), chunks(log_a)
    b = jnp.cumsum(lac, axis=2)
    b_last = b[:, :, -1]
    q_dec = qc * jnp.exp(b)
    k_inv = kc * jnp.exp(-b)
    k_end = kc * jnp.exp(b_last[:, :, None] - b)
    causal = jnp.tril(jnp.ones((G_CHUNK, G_CHUNK), bool))
    att = jnp.where(causal, jnp.einsum('bnihd,bnjhd->bnhij', q_dec, k_inv), 0.0)
    o_intra = jnp.einsum('bnhij,bnjhe->bnihe', att, vc)

    def step(s, inp):
        q_t, k_t, v_t, dl = inp
        o = jnp.einsum('bihd,bhde->bihe', q_t, s)
        s = s * dl[..., None] + jnp.einsum('bjhd,bjhe->bhde', k_t, v_t)
        return s, o

    xs = tuple(jnp.moveaxis(t, 1, 0) for t in (q_dec, k_end, vc, jnp.exp(b_last)))
    s_fin, o_inter = lax.scan(step, s0, xs)
    o = o_intra + jnp.moveaxis(o_inter, 0, 1)
    return o.reshape(B, L, H, Dv).astype(dt), s_fin


def _gla_output(o, g, p):
    of = o.astype(jnp.float32)
    of = of * lax.rsqrt(jnp.mean(jnp.square(of), -1, keepdims=True) + G_EPS) * p['gla_norm_w']
    B, L = o.shape[:2]
    return (of.reshape(B, L, G_VAL_DIM) * jax.nn.silu(g.astype(jnp.float32))).astype(g.dtype)


def _gla_mixer(z_lat, z_ctx, p, with_ctx_out):
    q_c, k_c, v_c, g_c, la_c = _gla_inputs(z_ctx, p)
    q_l, k_l, v_l, g_l, la_l = _gla_inputs(z_lat, p)
    s0 = jnp.zeros((z_lat.shape[0], G_HEADS, G_DK, G_DV), jnp.float32)

    def flip(t):
        return jnp.flip(t, axis=1)

    y_cf, s_cf = _gla_chunked(q_c, k_c, v_c, la_c[0], s0)
    y_lf, _ = _gla_chunked(q_l, k_l, v_l, la_l[0], s_cf)
    y_cb, s_cb = _gla_chunked(flip(q_c), flip(k_c), flip(v_c), flip(la_c[1]), s0)
    y_lb, _ = _gla_chunked(flip(q_l), flip(k_l), flip(v_l), flip(la_l[1]), s_cb)
    out_l = _gla_output(y_lf + flip(y_lb), g_l, p)
    out_c = _gla_output(y_cf + flip(y_cb), g_c, p) if with_ctx_out else None
    return out_l, out_c


def _merge(o_att, o_rwkv, o_gla, gate_cols, p):
    g_att, g_rwkv, g_gla = _split(jax.nn.sigmoid(gate_cols), GATE_SPLIT)
    merged = (g_att * (o_att @ p['w_branch_att']) + g_rwkv * (o_rwkv @ p['w_branch_rwkv'])
              + g_gla * (o_gla @ p['w_branch_gla']))
    return merged @ p['w_out']


def _token_mixing(h_lat, h_ctx, p, cos, sin, with_ctx_out):
    z_lat = h_lat @ p['w_in']
    z_ctx = h_ctx @ p['w_in']
    att_l, rwkv_l, gla_l, gate_l = _split(z_lat, GROUP_SPLIT)
    att_c, rwkv_c, gla_c, gate_c = _split(z_ctx, GROUP_SPLIT)
    q_l, k_l, v_l = _split(att_l, ATT_SPLIT)
    q_c, k_c, v_c = _split(att_c, ATT_SPLIT)
    q_l, k_l, v_l = _heads(q_l, A_HEADS), _heads(k_l, A_KV_HEADS), _heads(v_l, A_KV_HEADS)
    q_c, k_c, v_c = _heads(q_c, A_HEADS), _heads(k_c, A_KV_HEADS), _heads(v_c, A_KV_HEADS)
    o_att_l = _window_attention_latent(_apply_rope(q_l, cos, sin), q_l, _apply_rope(k_l, cos, sin), v_l,
                                       k_c, v_c, p['attn_sink'])
    o_rwkv_l, o_rwkv_c = _rwkv_mixer(rwkv_l, rwkv_c, p, with_ctx_out)
    o_gla_l, o_gla_c = _gla_mixer(gla_l, gla_c, p, with_ctx_out)
    y_lat = _merge(o_att_l, o_rwkv_l, o_gla_l, gate_l, p)
    y_ctx = None
    if with_ctx_out:
        o_att_c = _context_attention(q_c, k_c, v_c, p['attn_sink'])
        y_ctx = _merge(o_att_c, o_rwkv_c, o_gla_c, gate_c, p)
    return y_lat, y_ctx


def setup_inputs(seed: int = 0) -> dict:
    key = jax.random.key(seed)
    ks = jax.random.split(key, 32)
    D = D_MODEL

    def nrm(k, shape, s):
        return jax.random.normal(k, shape, jnp.float32) * s

    centre_tap = (jnp.arange(CONV_W) == CONV_W // 2).astype(jnp.float32)[:, None]
    return {
        "x": nrm(ks[0], (BATCH, SEQ, D), 1.0),
        "c": nrm(ks[1], (BATCH, D), 1.0),
        "ctx": nrm(ks[2], (BATCH, CTX_LEN, D), 1.0),
        "c_ctx": nrm(ks[3], (D,), 1.0),
        "ada_w": nrm(ks[4], (DEPTH, D, N_MOD * D), D ** -0.5),
        "ada_b": nrm(ks[5], (DEPTH, N_MOD * D), 0.02),
        "ln_g": 1.0 + nrm(ks[6], (DEPTH, 3, D), 0.02),
        "ln_b": nrm(ks[7], (DEPTH, 3, D), 0.02),
        "ffn_w_gu": nrm(ks[8], (DEPTH, 2, D, 2 * D_FF), D ** -0.5),
        "ffn_w_down": nrm(ks[9], (DEPTH, 2, D_FF, D), DEEPNORM_BETA * D_FF ** -0.5),
        "w_in": nrm(ks[10], (DEPTH, D, IN_COLS), D ** -0.5),
        "attn_sink": nrm(ks[11], (DEPTH, A_HEADS), 0.5),
        "rwkv_conv": centre_tap + nrm(ks[12], (DEPTH, CONV_W, RWKV_COLS), 0.1),
        "rwkv_w0": jax.random.uniform(ks[13], (DEPTH, 2, R_DIM), jnp.float32, -5.0, 0.0),
        "rwkv_w2": nrm(ks[14], (DEPTH, 2, DECAY_LORA, R_DIM), 0.1 * DECAY_LORA ** -0.5),
        "rwkv_a0": nrm(ks[15], (DEPTH, 2, R_DIM), 0.1),
        "rwkv_a2": nrm(ks[16], (DEPTH, 2, ICLR_LORA, R_DIM), ICLR_LORA ** -0.5),
        "rwkv_g2": nrm(ks[17], (DEPTH, GATE_LORA, R_DIM), GATE_LORA ** -0.5),
        "rwkv_k_k": 0.85 + nrm(ks[18], (DEPTH, R_DIM), 0.02),
        "rwkv_k_a": 1.0 + nrm(ks[19], (DEPTH, R_DIM), 0.02),
        "rwkv_r_k": nrm(ks[20], (DEPTH, R_HEADS, R_HEAD_DIM), 0.1),
        "rwkv_ln_w": 1.0 + nrm(ks[21], (DEPTH, R_DIM), 0.02),
        "rwkv_ln_b": nrm(ks[22], (DEPTH, R_DIM), 0.02),
        "gla_wa2": nrm(ks[23], (DEPTH, 2, G_GATE_RANK, G_KEY_DIM), G_GATE_RANK ** -0.5),
        "gla_ba": nrm(ks[24], (DEPTH, 2, G_KEY_DIM), 0.1),
        "gla_norm_w": 1.0 + nrm(ks[25], (DEPTH, G_DV), 0.02),
        "w_branch_att": nrm(ks[26], (DEPTH, A_Q_DIM, D), A_Q_DIM ** -0.5),
        "w_branch_rwkv": nrm(ks[27], (DEPTH, R_DIM, D), R_DIM ** -0.5),
        "w_branch_gla": nrm(ks[28], (DEPTH, G_VAL_DIM, D), G_VAL_DIM ** -0.5),
        "w_out": nrm(ks[29], (DEPTH, D, D), DEEPNORM_BETA * D ** -0.5),
    }


def reference(x, c, ctx, c_ctx, ada_w, ada_b, ln_g, ln_b, ffn_w_gu, ffn_w_down, w_in, attn_sink,
              rwkv_conv, rwkv_w0, rwkv_w2, rwkv_a0, rwkv_a2, rwkv_g2, rwkv_k_k, rwkv_k_a, rwkv_r_k,
              rwkv_ln_w, rwkv_ln_b, gla_wa2, gla_ba, gla_norm_w, w_branch_att, w_branch_rwkv,
              w_branch_gla, w_out):
    cos, sin = _axial_rope(x.shape[1], x.dtype)
    xc = ctx
    for l in range(DEPTH):
        last = l == DEPTH - 1
        p = {
            'w_in': w_in[l], 'attn_sink': attn_sink[l],
            'rwkv_conv': rwkv_conv[l], 'rwkv_w0': rwkv_w0[l], 'rwkv_w2': rwkv_w2[l],
            'rwkv_a0': rwkv_a0[l], 'rwkv_a2': rwkv_a2[l], 'rwkv_g2': rwkv_g2[l],
            'rwkv_k_k': rwkv_k_k[l], 'rwkv_k_a': rwkv_k_a[l], 'rwkv_r_k': rwkv_r_k[l],
            'rwkv_ln_w': rwkv_ln_w[l], 'rwkv_ln_b': rwkv_ln_b[l],
            'gla_wa2': gla_wa2[l], 'gla_ba': gla_ba[l], 'gla_norm_w': gla_norm_w[l],
            'w_branch_att': w_branch_att[l], 'w_branch_rwkv': w_branch_rwkv[l],
            'w_branch_gla': w_branch_gla[l], 'w_out': w_out[l],
        }
        ml = jnp.split((jax.nn.silu(c) @ ada_w[l] + ada_b[l])[:, None, :], N_MOD, axis=-1)
        mc = jnp.split(jax.nn.silu(c_ctx) @ ada_w[l] + ada_b[l], N_MOD, axis=-1)
        x = _ffn_sublayer(x, ml[0], ml[1], ml[2], ffn_w_gu[l, 0], ffn_w_down[l, 0], ln_g[l, 0], ln_b[l, 0])
        xc = _ffn_sublayer(xc, mc[0], mc[1], mc[2], ffn_w_gu[l, 0], ffn_w_down[l, 0], ln_g[l, 0], ln_b[l, 0])
        h_l = _modulate(x, ml[3], ml[4])
        h_c = _modulate(xc, mc[3], mc[4])
        y_l, y_c = _token_mixing(h_l, h_c, p, cos, sin, not last)
        x = _post_norm(x, ml[5] * y_l, ln_g[l, 1], ln_b[l, 1])
        x = _ffn_sublayer(x, ml[6], ml[7], ml[8], ffn_w_gu[l, 1], ffn_w_down[l, 1], ln_g[l, 2], ln_b[l, 2])
        if not last:
            xc = _post_norm(xc, mc[5] * y_c, ln_g[l, 1], ln_b[l, 1])
            xc = _ffn_sublayer(xc, mc[6], mc[7], mc[8], ffn_w_gu[l, 1], ffn_w_down[l, 1], ln_g[l, 2], ln_b[l, 2])
    return x
```

```python
import functools
import math

import jax
import jax.numpy as jnp
import numpy as np
from jax import lax
from jax.experimental import pallas as pl
from jax.experimental.pallas import tpu as pltpu

F32 = jnp.float32
BF16 = jnp.bfloat16
HI = lax.Precision.HIGHEST

N_MOD = 9
D_FF = 5632
LN_EPS = 1e-5
FFN_RES = 0.5

A_HEADS = 8
A_KV_HEADS = 2
A_GROUP = A_HEADS // A_KV_HEADS
A_HEAD_DIM = 128
A_Q_DIM = A_HEADS * A_HEAD_DIM
A_KV_DIM = A_KV_HEADS * A_HEAD_DIM
WINDOW = 128
A_BLOCK = 128
GRID_W = 64
ROPE_BASE = 10000.0
MASK_VALUE = -1e30

R_HEAD_DIM = 64
DECAY_LORA = 96
ICLR_LORA = 96
GATE_LORA = 256
R_GN_EPS = 64e-5
R_CHUNK = 64
LORA_PAD = 128

G_HEADS = 4
G_GATE_RANK = 16
G_GATE_NORM = 16.0
G_CHUNK = 64
G_EPS = 1e-5

MOD_ROWS = 16
VMEM_LIMIT = 58 * 1024 * 1024

NN = (((1,), (0,)), ((), ()))
NT = (((1,), (1,)), ((), ()))
TN = (((0,), (0,)), ((), ()))


def _dot(a, b, dims=NN, prec=None):
    return lax.dot_general(a, b, dims, precision=prec, preferred_element_type=F32)


def _sigmoid(x):
    return 1.0 / (1.0 + jnp.exp(-x))


def _log_sigmoid(x):
    return jnp.minimum(x, 0.0) - jnp.log(1.0 + jnp.exp(-jnp.abs(x)))


def _ln(x):
    mu = jnp.mean(x, -1, keepdims=True)
    xc = x - mu
    var = jnp.mean(xc * xc, -1, keepdims=True)
    return xc * lax.rsqrt(var + LN_EPS)


def _params(*sem):
    return pltpu.CompilerParams(dimension_semantics=sem, vmem_limit_bytes=VMEM_LIMIT)


class _Dims:
    def __init__(self, B, S, CT, D):
        self.B, self.S, self.CT, self.D = B, S, CT, D
        self.MC, self.ML = B * CT, B * S
        self.M = self.MC + self.ML
        self.R = D // 2
        self.GK = D // 2
        self.GV = D
        self.RH = self.R // R_HEAD_DIM
        self.rw_cols = 3 * self.R + GATE_LORA + 4 * LORA_PAD
        c = 0
        self.c_rwkv = c; c += self.rw_cols
        self.c_att_k = c; c += A_KV_DIM
        self.c_gate = c; c += 3 * D
        self.c_gla_v = c; c += self.GV
        self.c_gla_g = c; c += self.GV
        self.c_gla_q = c; c += self.GK
        self.c_gla_k = c; c += self.GK
        self.c_att_q = c; c += A_Q_DIM
        self.c_att_v = c; c += A_KV_DIM
        self.c_gla_ac = c; c += LORA_PAD
        self.tn_in = 1280
        self.NP = -(-c // self.tn_in) * self.tn_in
        self.tm = math.gcd(512, math.gcd(self.MC, S))
        self.tmr = math.gcd(256, math.gcd(CT, S))

    def mod_row(self, i, tm):
        nct = self.MC // tm
        tpb = self.S // tm
        return jnp.where(i < nct, self.B, (i - nct) // tpb)

    def mod_spec(self, m, tm):
        return pl.BlockSpec((None, 1, self.D), lambda i, *_: (self.mod_row(i, tm) * N_MOD + m, 0, 0))

    def chunk_block(self, b, d, s, chunk):
        ncc, ncl = self.CT // chunk, self.S // chunk
        j = s - ncc
        ctx_blk = b * ncc + jnp.where(d == 0, s, ncc - 1 - s)
        lat_blk = self.B * ncc + b * ncl + jnp.where(d == 0, j, ncl - 1 - j)
        return jnp.where(s < ncc, ctx_blk, lat_blk)


def _ada_kernel(c_ref, w_ref, b_ref, o_ref):
    cc = c_ref[...]
    s = (cc * _sigmoid(cc)).astype(BF16)
    o_ref[...] = _dot(s, w_ref[...].astype(BF16)) + b_ref[...]


def _ada(cc, ada_w, ada_b):
    L, D, N = ada_w.shape
    tn = 1024
    return pl.pallas_call(
        _ada_kernel,
        grid=(L, N // tn),
        in_specs=[
            pl.BlockSpec((MOD_ROWS, D), lambda l, n: (0, 0)),
            pl.BlockSpec((None, D, tn), lambda l, n: (l, 0, n)),
            pl.BlockSpec((None, 1, tn), lambda l, n: (l, 0, n)),
        ],
        out_specs=pl.BlockSpec((None, MOD_ROWS, tn), lambda l, n: (l, 0, n)),
        out_shape=jax.ShapeDtypeStruct((L, MOD_ROWS, N), F32),
        compiler_params=_params("parallel", "arbitrary"),
        name="ada",
    )(cc, ada_w, ada_b.reshape(L, 1, N))


def _ffn_kernel(alpha, x_ref, sh_ref, sc_ref, gt_ref, wg_ref, wu_ref, wd_ref, g_ref, b_ref, o_ref,
                h_ref, acc_ref):
    f = pl.program_id(1)

    @pl.when(f == 0)
    def _():
        h = _ln(x_ref[...]) * (1.0 + sc_ref[...]) + sh_ref[...]
        h_ref[...] = h.astype(BF16)
        acc_ref[...] = jnp.zeros_like(acc_ref)

    h = h_ref[...]
    g = _dot(h, wg_ref[...])
    u = _dot(h, wu_ref[...])
    a = (g * _sigmoid(g) * u).astype(BF16)
    acc_ref[...] += _dot(a, wd_ref[...])

    @pl.when(f == pl.num_programs(1) - 1)
    def _():
        z = alpha * x_ref[...] + (FFN_RES * gt_ref[...]) * acc_ref[...]
        o_ref[...] = _ln(z) * g_ref[...] + b_ref[...]


def _ffn(dm, alpha, x, mods, m0, w_gu, w_down, g, b):
    D, tm, tf = dm.D, dm.tm, 512
    nf = D_FF // tf
    return pl.pallas_call(
        functools.partial(_ffn_kernel, alpha),
        grid=(dm.M // tm, nf),
        in_specs=[
            pl.BlockSpec((tm, D), lambda i, f: (i, 0)),
            dm.mod_spec(m0, tm), dm.mod_spec(m0 + 1, tm), dm.mod_spec(m0 + 2, tm),
            pl.BlockSpec((D, tf), lambda i, f: (0, f)),
            pl.BlockSpec((D, tf), lambda i, f: (0, f + nf)),
            pl.BlockSpec((tf, D), lambda i, f: (f, 0)),
            pl.BlockSpec((1, D), lambda i, f: (0, 0)),
            pl.BlockSpec((1, D), lambda i, f: (0, 0)),
        ],
        out_specs=pl.BlockSpec((tm, D), lambda i, f: (i, 0)),
        out_shape=jax.ShapeDtypeStruct((dm.M, D), F32),
        scratch_shapes=[pltpu.VMEM((tm, D), BF16), pltpu.VMEM((tm, D), F32)],
        compiler_params=_params("parallel", "arbitrary"),
        name="ffn",
    )(x, mods, mods, mods, w_gu, w_gu, w_down, g.reshape(1, D), b.reshape(1, D))


def _inproj_kernel(x_ref, sh_ref, sc_ref, w_ref, o_ref, h_ref):
    @pl.when(pl.program_id(1) == 0)
    def _():
        h = _ln(x_ref[...]) * (1.0 + sc_ref[...]) + sh_ref[...]
        h_ref[...] = h.astype(BF16)

    o_ref[...] = _dot(h_ref[...], w_ref[...])


def _inproj(dm, x, mods, w_in_p):
    D, tm, tn = dm.D, dm.tm, dm.tn_in
    return pl.pallas_call(
        _inproj_kernel,
        grid=(dm.M // tm, dm.NP // tn),
        in_specs=[
            pl.BlockSpec((tm, D), lambda i, n: (i, 0)),
            dm.mod_spec(3, tm), dm.mod_spec(4, tm),
            pl.BlockSpec((D, tn), lambda i, n: (0, n)),
        ],
        out_specs=pl.BlockSpec((tm, tn), lambda i, n: (i, n)),
        out_shape=jax.ShapeDtypeStruct((dm.M, dm.NP), F32),
        scratch_shapes=[pltpu.VMEM((tm, D), BF16)],
        compiler_params=_params("parallel", "arbitrary"),
        name="inproj",
    )(x, mods, mods, w_in_p)


def _rope(x, cos, sin_signed):
    lane = lax.broadcasted_iota(jnp.int32, x.shape, 1)
    quarter = A_HEAD_DIM // 4
    swapped = jnp.where((lane % (2 * quarter)) < quarter,
                        pltpu.roll(x, A_HEAD_DIM - quarter, 1), pltpu.roll(x, quarter, 1))
    return x * cos + swapped * sin_signed


def _attn_kernel(S, nb, sink_ref, q_ref, kp_ref, kc_ref, kn_ref, vp_ref, vc_ref, vn_ref, kx_ref, vx_ref,
                 cos_ref, sin_ref, o_ref):
    n = pl.program_id(1)
    scale = A_HEAD_DIM ** -0.5

    def table(ref, blk):
        return ref[pl.ds(pl.multiple_of(blk * A_BLOCK, A_BLOCK), A_BLOCK), :]

    blk_p, blk_n = jnp.maximum(n - 1, 0), jnp.minimum(n + 1, nb - 1)
    cos_c, sin_c = table(cos_ref, n), table(sin_ref, n)
    cos_p, sin_p = table(cos_ref, blk_p), table(sin_ref, blk_p)
    cos_n, sin_n = table(cos_ref, blk_n), table(sin_ref, blk_n)

    rows, band = A_GROUP * A_BLOCK, 3 * A_BLOCK
    qpos = n * A_BLOCK + lax.broadcasted_iota(jnp.int32, (rows, band), 0) % A_BLOCK
    kpos = (n - 1) * A_BLOCK + lax.broadcasted_iota(jnp.int32, (rows, band), 1)
    valid = (jnp.abs(qpos - kpos) <= WINDOW) & (kpos >= 0) & (kpos < S)

    for kvh in range(A_KV_HEADS):
        ks = slice(kvh * A_HEAD_DIM, (kvh + 1) * A_HEAD_DIM)
        kb = jnp.concatenate([_rope(kp_ref[:, ks], cos_p, sin_p), _rope(kc_ref[:, ks], cos_c, sin_c),
                              _rope(kn_ref[:, ks], cos_n, sin_n)], axis=0).astype(BF16)
        vb = jnp.concatenate([vp_ref[:, ks], vc_ref[:, ks], vn_ref[:, ks]], axis=0).astype(BF16)
        kx = kx_ref[:, ks].astype(BF16)
        vx = vx_ref[:, ks].astype(BF16)
        heads = [kvh * A_GROUP + g for g in range(A_GROUP)]
        qs = [q_ref[:, h * A_HEAD_DIM:(h + 1) * A_HEAD_DIM] for h in heads]
        q_rope = jnp.concatenate([_rope(q, cos_c, sin_c) for q in qs], axis=0).astype(BF16)
        q_plain = jnp.concatenate(qs, axis=0).astype(BF16)
        sink = jnp.concatenate([jnp.full((A_BLOCK, 1), sink_ref[h], F32) for h in heads], axis=0)
        s_loc = jnp.where(valid, _dot(q_rope, kb, NT) * scale, MASK_VALUE)
        s_ctx = _dot(q_plain, kx, NT) * scale
        m = jnp.maximum(jnp.maximum(jnp.max(s_loc, -1, keepdims=True), jnp.max(s_ctx, -1, keepdims=True)), sink)
        p_loc = jnp.exp(s_loc - m)
        p_ctx = jnp.exp(s_ctx - m)
        den = jnp.sum(p_loc, -1, keepdims=True) + jnp.sum(p_ctx, -1, keepdims=True) + jnp.exp(sink - m)
        o = (_dot(p_loc.astype(BF16), vb) + _dot(p_ctx.astype(BF16), vx)) / den
        for g, h in enumerate(heads):
            o_ref[:, h * A_HEAD_DIM:(h + 1) * A_HEAD_DIM] = o[g * A_BLOCK:(g + 1) * A_BLOCK].astype(BF16)


def _attn_latent(dm, z, sink, cos, sin):
    B, S = dm.B, dm.S
    nb = S // A_BLOCK
    base = dm.MC // A_BLOCK
    qc = dm.c_att_q // A_Q_DIM
    kc = dm.c_att_k // A_KV_DIM
    vc = dm.c_att_v // A_KV_DIM

    def rows(shift):
        return lambda b, n: base + b * nb + jnp.clip(n + shift, 0, nb - 1)

    def kv_spec(col, shift):
        r = rows(shift)
        return pl.BlockSpec((A_BLOCK, A_KV_DIM), lambda b, n: (r(b, n), col))

    return pl.pallas_call(
        functools.partial(_attn_kernel, S, nb),
        grid=(B, nb),
        in_specs=[
            pl.BlockSpec(memory_space=pltpu.SMEM),
            pl.BlockSpec((A_BLOCK, A_Q_DIM), lambda b, n: (base + b * nb + n, qc)),
            kv_spec(kc, -1), kv_spec(kc, 0), kv_spec(kc, 1),
            kv_spec(vc, -1), kv_spec(vc, 0), kv_spec(vc, 1),
            pl.BlockSpec((dm.CT, A_KV_DIM), lambda b, n: (b, kc)),
            pl.BlockSpec((dm.CT, A_KV_DIM), lambda b, n: (b, vc)),
            pl.BlockSpec((S, A_HEAD_DIM), lambda b, n: (0, 0)),
            pl.BlockSpec((S, A_HEAD_DIM), lambda b, n: (0, 0)),
        ],
        out_specs=pl.BlockSpec((A_BLOCK, A_Q_DIM), lambda b, n: (b * nb + n, 0)),
        out_shape=jax.ShapeDtypeStruct((dm.ML, A_Q_DIM), BF16),
        compiler_params=_params("parallel", "arbitrary"),
        name="attn_latent",
    )(sink, z, z, z, z, z, z, z, z, z, cos, sin)


def _attn_ctx_kernel(CT, sink_ref, q_ref, k_ref, v_ref, o_ref):
    scale = A_HEAD_DIM ** -0.5
    for kvh in range(A_KV_HEADS):
        ks = slice(kvh * A_HEAD_DIM, (kvh + 1) * A_HEAD_DIM)
        k = k_ref[:, ks].astype(BF16)
        v = v_ref[:, ks].astype(BF16)
        heads = [kvh * A_GROUP + g for g in range(A_GROUP)]
        q = jnp.concatenate([q_ref[:, h * A_HEAD_DIM:(h + 1) * A_HEAD_DIM] for h in heads], axis=0).astype(BF16)
        sink = jnp.concatenate([jnp.full((CT, 1), sink_ref[h], F32) for h in heads], axis=0)
        s = _dot(q, k, NT) * scale
        m = jnp.maximum(jnp.max(s, -1, keepdims=True), sink)
        p = jnp.exp(s - m)
        den = jnp.sum(p, -1, keepdims=True) + jnp.exp(sink - m)
        o = _dot(p.astype(BF16), v) / den
        for g, h in enumerate(heads):
            o_ref[:, h * A_HEAD_DIM:(h + 1) * A_HEAD_DIM] = o[g * CT:(g + 1) * CT].astype(BF16)


def _attn_ctx(dm, z, sink):
    CT = dm.CT
    return pl.pallas_call(
        functools.partial(_attn_ctx_kernel, CT),
        grid=(dm.B,),
        in_specs=[
            pl.BlockSpec(memory_space=pltpu.SMEM),
            pl.BlockSpec((CT, A_Q_DIM), lambda b: (b, dm.c_att_q // A_Q_DIM)),
            pl.BlockSpec((CT, A_KV_DIM), lambda b: (b, dm.c_att_k // A_KV_DIM)),
            pl.BlockSpec((CT, A_KV_DIM), lambda b: (b, dm.c_att_v // A_KV_DIM)),
        ],
        out_specs=pl.BlockSpec((CT, A_Q_DIM), lambda b: (b, 0)),
        out_shape=jax.ShapeDtypeStruct((dm.MC, A_Q_DIM), BF16),
        compiler_params=_params("parallel"),
        name="attn_ctx",
    )(sink, z, z, z)


def _head_sum(x, e1_ref, e2_ref):
    return _dot(_dot(x, e1_ref[...], prec=HI), e2_ref[...], prec=HI)


def _rwkv_prep_kernel(dm, z_ref, zp_ref, zn_ref, cw_ref, g2_ref, w0_ref, w2_ref, a0_ref, a2_ref, kk_ref, ka_ref,
                      rk_ref, e1_ref, e2_ref,
                      r_out, v_out, kk_out, lw_out, kd_out, b_out, bonus_out, g_out):
    i = pl.program_id(0)
    tm, R = dm.tmr, dm.R
    nct = dm.MC // tm
    start = jnp.where(i < nct, (i * tm) % dm.CT, ((i - nct) * tm) % dm.S)
    seqlen = jnp.where(i < nct, dm.CT, dm.S)
    has_prev = (start != 0).astype(F32)
    has_next = (start + tm != seqlen).astype(F32)

    z = z_ref[...]
    rowid = lax.broadcasted_iota(jnp.int32, (tm, 1), 0)
    z_prev = jnp.where(rowid == 0, zp_ref[7:8, :] * has_prev, pltpu.roll(z, 1, 0))
    z_next = jnp.where(rowid == tm - 1, zn_ref[0:1, :] * has_next, pltpu.roll(z, tm - 1, 0))
    zc = z_prev * cw_ref[0:1, :] + z * cw_ref[1:2, :] + z_next * cw_ref[2:3, :]

    r = zc[:, 0:R]
    k = zc[:, R:2 * R]
    v = zc[:, 2 * R:3 * R]
    c0 = 3 * R
    gc = zc[:, c0:c0 + GATE_LORA]
    c0 += GATE_LORA
    g_out[...] = _dot(_sigmoid(gc).astype(BF16), g2_ref[...])

    kkr = k * kk_ref[...]
    norm = jnp.sqrt(_head_sum(kkr * kkr, e1_ref, e2_ref))
    kk = kkr / jnp.maximum(norm, 1e-12)
    r_out[...] = r
    v_out[...] = v
    kk_out[...] = kk

    bonus = jnp.zeros_like(v)
    for d in range(2):
        wc = zc[:, c0 + d * LORA_PAD:c0 + (d + 1) * LORA_PAD]
        ac = zc[:, c0 + (2 + d) * LORA_PAD:c0 + (3 + d) * LORA_PAD]
        wl = w0_ref[d:d + 1, :] + _dot(jnp.tanh(wc).astype(BF16), w2_ref[d])
        w = _log_sigmoid(wl) - 0.5
        lw_out[d] = -jnp.exp(w)
        a = _sigmoid(a0_ref[d:d + 1, :] + _dot(ac.astype(BF16), a2_ref[d]))
        kd = k * (1.0 + (a - 1.0) * ka_ref[...])
        kd_out[d] = kd
        b_out[d] = kk * a
        bonus = bonus + _head_sum(r * kd * rk_ref[...], e1_ref, e2_ref) * v
    bonus_out[...] = bonus


def _rwkv_prep(dm, z, p):
    tm, R, M, W = dm.tmr, dm.R, dm.M, dm.rw_cols
    nblk8 = M // 8
    row = pl.BlockSpec((tm, R), lambda i: (i, 0))
    row2 = pl.BlockSpec((2, tm, R), lambda i: (0, i, 0))

    def full(a):
        return pl.BlockSpec(a.shape, lambda i: (0,) * a.ndim)

    consts = [p['conv'], p['g2'], p['w0'], p['w2'], p['a0'], p['a2'], p['k_k'], p['k_a'], p['r_k'], p['e1'], p['e2']]
    one = jax.ShapeDtypeStruct((M, R), F32)
    two = jax.ShapeDtypeStruct((2, M, R), F32)
    return pl.pallas_call(
        functools.partial(_rwkv_prep_kernel, dm),
        grid=(M // tm,),
        in_specs=[
            pl.BlockSpec((tm, W), lambda i: (i, 0)),
            pl.BlockSpec((8, W), lambda i: (jnp.maximum(i * (tm // 8) - 1, 0), 0)),
            pl.BlockSpec((8, W), lambda i: (jnp.minimum((i + 1) * (tm // 8), nblk8 - 1), 0)),
        ] + [full(a) for a in consts],
        out_specs=[row, row, row, row2, row2, row2, row, row],
        out_shape=[one, one, one, two, two, two, one, one],
        compiler_params=_params("parallel"),
        name="rwkv_prep",
    )(z, z, z, *consts)


def _rwkv_chunk_kernel(n_sub, r_ref, v_ref, kk_ref, lw_ref, kd_ref, b_ref, g_out, h_out, q_out, y0_out):
    C, N = R_CHUNK, R_HEAD_DIM
    ti = lax.broadcasted_iota(jnp.int32, (C, C), 0)
    si = lax.broadcasted_iota(jnp.int32, (C, C), 1)
    eye = (ti == si).astype(F32)

    def body(c, carry):
        rs = pl.ds(pl.multiple_of(c * C, C), C)
        r2, v2, kk2 = r_ref[rs, :], v_ref[rs, :], kk_ref[rs, :]
        for d in range(2):
            incl = (ti >= si) if d == 0 else (ti <= si)
            strict = (ti > si) if d == 0 else (ti < si)
            lw2 = lw_ref[d, rs, :]
            cl2 = _dot(incl.astype(F32), lw2, prec=HI)
            tot2 = jnp.sum(lw2, axis=0, keepdims=True)
            e_pos2 = jnp.exp(cl2)
            e_neg2 = jnp.exp(-cl2)
            e_prev2 = jnp.exp(cl2 - lw2)
            e_end2 = jnp.exp(tot2 - cl2)
            g_tot2 = jnp.exp(tot2)
            kd2, b2 = kd_ref[d, rs, :], b_ref[d, rs, :]
            kkm2, bp2, kp2 = kk2 * e_prev2, b2 * e_neg2, kd2 * e_neg2
            rp2, be2, ke2 = r2 * e_pos2, b2 * e_end2, kd2 * e_end2
            for hh in range(2):
                hs = slice(hh * N, (hh + 1) * N)
                kkm, bp, kp, rp, be, ke, v = kkm2[:, hs], bp2[:, hs], kp2[:, hs], rp2[:, hs], be2[:, hs], ke2[:, hs], v2[:, hs]
                lb = jnp.where(strict, _dot(kkm, bp, NT, HI), 0.0)
                lk = jnp.where(strict, _dot(kkm, kp, NT, HI), 0.0)
                mb = jnp.where(incl, _dot(rp, bp, NT, HI), 0.0)
                mk = jnp.where(incl, _dot(rp, kp, NT, HI), 0.0)
                pw = -lb
                t_inv = eye + pw
                for _ in range(int(math.log2(C)) - 1):
                    pw = _dot(pw, pw, prec=HI)
                    t_inv = t_inv + _dot(t_inv, pw, prec=HI)
                a_m = _dot(t_inv, kkm, prec=HI)
                d_m = -_dot(t_inv, _dot(lk, v, prec=HI), prec=HI)
                q_out[d, rs, hs] = rp - _dot(mb, a_m, prec=HI)
                y0_out[d, rs, hs] = _dot(mb, d_m, prec=HI) + _dot(mk, v, prec=HI)
                g_out[d, rs, hs] = eye * g_tot2[:, hs] - _dot(be, a_m, TN, HI)
                h_out[d, rs, hs] = _dot(be, d_m, TN, HI) + _dot(ke, v, TN, HI)
        return carry

    lax.fori_loop(0, n_sub, body, 0)


def _rwkv_chunk(dm, r, v, kk, lw, kd, b):
    M, R = dm.M, dm.R
    rows = dm.tmr
    pair = 2 * R_HEAD_DIM
    one = pl.BlockSpec((rows, pair), lambda i, h: (i, h))
    two = pl.BlockSpec((2, rows, pair), lambda i, h: (0, i, h))
    out = jax.ShapeDtypeStruct((2, M, R), F32)
    return pl.pallas_call(
        functools.partial(_rwkv_chunk_kernel, rows // R_CHUNK),
        grid=(M // rows, R // pair),
        in_specs=[one, one, one, two, two, two],
        out_specs=[two, two, two, two],
        out_shape=[out, out, out, out],
        compiler_params=_params("parallel", "parallel"),
        name="rwkv_chunk",
    )(r, v, kk, lw, kd, b)


def _rwkv_scan_kernel(RH, g_ref, h_ref, q_ref, y0_ref, y_ref, st_ref):
    N = R_HEAD_DIM

    @pl.when(pl.program_id(2) == 0)
    def _():
        st_ref[...] = jnp.zeros_like(st_ref)

    for h in range(RH):
        hs = slice(h * N, (h + 1) * N)
        st = st_ref[:, hs]
        y_ref[:, hs] = _dot(q_ref[:, hs], st, prec=HI) + y0_ref[:, hs]
        st_ref[:, hs] = _dot(g_ref[:, hs], st, prec=HI) + h_ref[:, hs]


def _rwkv_scan(dm, g_m, h_m, q_m, y0_m):
    C, R = R_CHUNK, dm.R
    steps = (dm.CT + dm.S) // C
    spec = pl.BlockSpec((None, C, R), lambda b, d, s: (d, dm.chunk_block(b, d, s, C), 0))
    return pl.pallas_call(
        functools.partial(_rwkv_scan_kernel, dm.RH),
        grid=(dm.B, 2, steps),
        in_specs=[spec, spec, spec, spec],
        out_specs=spec,
        out_shape=jax.ShapeDtypeStruct((2, dm.M, R), F32),
        scratch_shapes=[pltpu.VMEM((R_HEAD_DIM, R), F32)],
        compiler_params=_params("parallel", "parallel", "arbitrary"),
        name="rwkv_scan",
    )(g_m, h_m, q_m, y0_m)


def _rwkv_out_kernel(y_ref, bonus_ref, g_ref, lnw_ref, lnb_ref, e1_ref, e2_ref, o_ref):
    o = y_ref[0] + y_ref[1]
    inv_n = 1.0 / R_HEAD_DIM
    mu = _head_sum(o, e1_ref, e2_ref) * inv_n
    oc = o - mu
    var = _head_sum(oc * oc, e1_ref, e2_ref) * inv_n
    o = oc * lax.rsqrt(var + R_GN_EPS) * lnw_ref[...] + lnb_ref[...]
    o_ref[...] = ((o + bonus_ref[...]) * g_ref[...]).astype(BF16)


def _rwkv_out(dm, y, bonus, g, p):
    tm, R, M = dm.tmr, dm.R, dm.M
    row = pl.BlockSpec((tm, R), lambda i: (i, 0))

    def full(a):
        return pl.BlockSpec(a.shape, lambda i: (0,) * a.ndim)

    consts = [p['ln_w'], p['ln_b'], p['e1'], p['e2']]
    return pl.pallas_call(
        _rwkv_out_kernel,
        grid=(M // tm,),
        in_specs=[pl.BlockSpec((2, tm, R), lambda i: (0, i, 0)), row, row] + [full(a) for a in consts],
        out_specs=row,
        out_shape=jax.ShapeDtypeStruct((M, R), BF16),
        compiler_params=_params("parallel"),
        name="rwkv_out",
    )(y, bonus, g, *consts)


def _gla_kernel(dm, q_ref, k_ref, v_ref, ac_ref, wa_ref, ba_ref, y_ref, st_ref):
    C = G_CHUNK
    dk, dv = dm.GK // G_HEADS, dm.GV // G_HEADS
    d = pl.program_id(1)

    @pl.when(pl.program_id(2) == 0)
    def _():
        st_ref[...] = jnp.zeros_like(st_ref)

    ti = lax.broadcasted_iota(jnp.int32, (C, C), 0)
    si = lax.broadcasted_iota(jnp.int32, (C, C), 1)
    causal = jnp.where(d == 0, ti - si, si - ti) >= 0
    la = _log_sigmoid(_dot(ac_ref[...], wa_ref[...], prec=HI) + ba_ref[...]) * (1.0 / G_GATE_NORM)
    cb = _dot(causal.astype(F32), la, prec=HI)
    tot = jnp.sum(la, axis=0, keepdims=True)
    q_dec = (q_ref[...] * (dk ** -0.5) * jnp.exp(cb)).astype(BF16)
    k = k_ref[...]
    k_inv = (k * jnp.exp(-cb)).astype(BF16)
    k_end = (k * jnp.exp(tot - cb)).astype(BF16)
    e_last = jnp.exp(tot)
    for h in range(G_HEADS):
        ks = slice(h * dk, (h + 1) * dk)
        vs = slice(h * dv, (h + 1) * dv)
        v = v_ref[:, vs].astype(BF16)
        att = jnp.where(causal, _dot(q_dec[:, ks], k_inv[:, ks], NT), 0.0)
        st = st_ref[h]
        y_ref[:, vs] = _dot(att.astype(BF16), v) + _dot(q_dec[:, ks], st.astype(BF16), NT)
        st_ref[h] = st * e_last[:, ks] + _dot(v, k_end[:, ks], TN)


def _gla(dm, z, wa_p, ba):
    C = G_CHUNK
    steps = (dm.CT + dm.S) // C
    GK, GV = dm.GK, dm.GV

    def zspec(width, col):
        return pl.BlockSpec((C, width), lambda b, d, s: (dm.chunk_block(b, d, s, C), col // width))

    return pl.pallas_call(
        functools.partial(_gla_kernel, dm),
        grid=(dm.B, 2, steps),
        in_specs=[
            zspec(GK, dm.c_gla_q), zspec(GK, dm.c_gla_k), zspec(GV, dm.c_gla_v), zspec(LORA_PAD, dm.c_gla_ac),
            pl.BlockSpec((None, LORA_PAD, GK), lambda b, d, s: (d, 0, 0)),
            pl.BlockSpec((None, 1, GK), lambda b, d, s: (d, 0, 0)),
        ],
        out_specs=pl.BlockSpec((None, C, GV), lambda b, d, s: (d, dm.chunk_block(b, d, s, C), 0)),
        out_shape=jax.ShapeDtypeStruct((2, dm.M, GV), F32),
        scratch_shapes=[pltpu.VMEM((G_HEADS, GV // G_HEADS, GK // G_HEADS), F32)],
        compiler_params=_params("parallel", "parallel", "arbitrary"),
        name="gla",
    )(z, z, z, z, wa_p, ba)


def _gla_out_kernel(dv, y_ref, g_ref, nw_ref, o_ref):
    o = y_ref[0] + y_ref[1]
    g = g_ref[...]
    gate = g * _sigmoid(g)
    for h in range(G_HEADS):
        vs = slice(h * dv, (h + 1) * dv)
        oh = o[:, vs]
        oh = oh * lax.rsqrt(jnp.mean(oh * oh, -1, keepdims=True) + G_EPS) * nw_ref[...]
        o_ref[:, vs] = (oh * gate[:, vs]).astype(BF16)


def _gla_out(dm, y, z, norm_w):
    tm, GV, M = dm.tmr, dm.GV, dm.M
    dv = GV // G_HEADS
    return pl.pallas_call(
        functools.partial(_gla_out_kernel, dv),
        grid=(M // tm,),
        in_specs=[
            pl.BlockSpec((2, tm, GV), lambda i: (0, i, 0)),
            pl.BlockSpec((tm, GV), lambda i: (i, dm.c_gla_g // GV)),
            pl.BlockSpec((1, dv), lambda i: (0, 0)),
        ],
        out_specs=pl.BlockSpec((tm, GV), lambda i: (i, 0)),
        out_shape=jax.ShapeDtypeStruct((M, GV), BF16),
        compiler_params=_params("parallel"),
        name="gla_out",
    )(y, z, norm_w.reshape(1, dv))


def _merge_kernel(alpha, x_ref, mg_ref, oa_ref, or_ref, og_ref, ga_ref, gr_ref, gg_ref, wa_ref, wr_ref, wg_ref,
                  wo_ref, g_ref, b_ref, o_ref, acc_ref):
    n = pl.program_id(1)

    @pl.when(n == 0)
    def _():
        acc_ref[...] = jnp.zeros_like(acc_ref)

    merged = (_sigmoid(ga_ref[...]) * _dot(oa_ref[...], wa_ref[...])
              + _sigmoid(gr_ref[...]) * _dot(or_ref[...], wr_ref[...])
              + _sigmoid(gg_ref[...]) * _dot(og_ref[...], wg_ref[...]))
    acc_ref[...] += _dot(merged.astype(BF16), wo_ref[...])

    @pl.when(n == pl.num_programs(1) - 1)
    def _():
        z = alpha * x_ref[...] + mg_ref[...] * acc_ref[...]
        o_ref[...] = _ln(z) * g_ref[...] + b_ref[...]


def _merge(dm, alpha, x, mods, z, o_att, o_rwkv, o_gla, wba, wbr, wbg, w_out, g, b):
    D, tm, tn = dm.D, dm.tm, 512
    gate0 = dm.c_gate // tn
    per = D // tn

    def gate_spec(j):
        return pl.BlockSpec((tm, tn), lambda i, n: (i, gate0 + j * per + n))

    def resident(width):
        return pl.BlockSpec((tm, width), lambda i, n: (i, 0))

    def wcol(rows):
        return pl.BlockSpec((rows, tn), lambda i, n: (0, n))

    return pl.pallas_call(
        functools.partial(_merge_kernel, alpha),
        grid=(dm.M // tm, per),
        in_specs=[
            resident(D), dm.mod_spec(5, tm),
            resident(A_Q_DIM), resident(dm.R), resident(dm.GV),
            gate_spec(0), gate_spec(1), gate_spec(2),
            wcol(A_Q_DIM), wcol(dm.R), wcol(dm.GV),
            pl.BlockSpec((tn, D), lambda i, n: (n, 0)),
            pl.BlockSpec((1, D), lambda i, n: (0, 0)),
            pl.BlockSpec((1, D), lambda i, n: (0, 0)),
        ],
        out_specs=resident(D),
        out_shape=jax.ShapeDtypeStruct((dm.M, D), F32),
        scratch_shapes=[pltpu.VMEM((tm, D), F32)],
        compiler_params=_params("parallel", "arbitrary"),
        name="merge",
    )(x, mods, o_att, o_rwkv, o_gla, z, z, z, wba, wbr, wbg, w_out, g.reshape(1, D), b.reshape(1, D))


def _pad_to(a, size, axis):
    pad = [(0, 0)] * a.ndim
    pad[axis] = (0, size - a.shape[axis])
    return jnp.pad(a, pad)


def _rwkv_cols(a, R):
    c = 3 * R
    rkv = a[..., :c]
    wc0, wc1 = a[..., c:c + DECAY_LORA], a[..., c + DECAY_LORA:c + 2 * DECAY_LORA]
    c += 2 * DECAY_LORA
    ac0, ac1 = a[..., c:c + ICLR_LORA], a[..., c + ICLR_LORA:c + 2 * ICLR_LORA]
    c += 2 * ICLR_LORA
    gc = a[..., c:c + GATE_LORA]
    ax = a.ndim - 1
    return jnp.concatenate([rkv, gc] + [_pad_to(t, LORA_PAD, ax) for t in (wc0, wc1, ac0, ac1)], axis=-1)


def _in_proj_weights(dm, w_in):
    D, R, GK, GV = dm.D, dm.R, dm.GK, dm.GV
    att_cols = A_Q_DIM + 2 * A_KV_DIM
    rw_src = 3 * R + 2 * DECAY_LORA + 2 * ICLR_LORA + GATE_LORA
    gla_src = 2 * GK + 2 * GV + 2 * G_GATE_RANK
    att, rwkv, gla, gate = jnp.split(w_in, np.cumsum([att_cols, rw_src, gla_src]).tolist(), axis=-1)
    a_q, a_k, a_v = att[:, :A_Q_DIM], att[:, A_Q_DIM:A_Q_DIM + A_KV_DIM], att[:, A_Q_DIM + A_KV_DIM:]
    g_q, g_k = gla[:, :GK], gla[:, GK:2 * GK]
    g_v, g_g = gla[:, 2 * GK:2 * GK + GV], gla[:, 2 * GK + GV:2 * GK + 2 * GV]
    g_ac = _pad_to(gla[:, 2 * GK + 2 * GV:], LORA_PAD, 1)
    w = jnp.concatenate([_rwkv_cols(rwkv, R), a_k, gate, g_v, g_g, g_q, g_k, a_q, a_v, g_ac], axis=-1)
    return _pad_to(w, dm.NP, 1).astype(BF16)


def _rope_tables(S):
    rows = S // GRID_W
    row = jnp.repeat(jnp.arange(rows, dtype=F32), GRID_W)
    col = jnp.tile(jnp.arange(GRID_W, dtype=F32), rows)
    n_freq = A_HEAD_DIM // 4
    inv_freq = ROPE_BASE ** (-jnp.arange(n_freq, dtype=F32) / n_freq)
    ang_r, ang_c = row[:, None] * inv_freq, col[:, None] * inv_freq
    cos = jnp.concatenate([jnp.cos(ang_r)] * 2 + [jnp.cos(ang_c)] * 2, axis=-1)
    sin = jnp.concatenate([-jnp.sin(ang_r), jnp.sin(ang_r), -jnp.sin(ang_c), jnp.sin(ang_c)], axis=-1)
    return cos, sin


def kernel(x, c, ctx, c_ctx, ada_w, ada_b, ln_g, ln_b, ffn_w_gu, ffn_w_down, w_in, attn_sink, rwkv_conv, rwkv_w0,
           rwkv_w2, rwkv_a0, rwkv_a2, rwkv_g2, rwkv_k_k, rwkv_k_a, rwkv_r_k, rwkv_ln_w, rwkv_ln_b, gla_wa2, gla_ba,
           gla_norm_w, w_branch_att, w_branch_rwkv, w_branch_gla, w_out):
    B, S, D = x.shape
    CT = ctx.shape[1]
    depth = ada_w.shape[0]
    dm = _Dims(B, S, CT, D)
    R = dm.R
    alpha = float((2 * depth) ** 0.25)
    assert B + 1 <= MOD_ROWS

    cc = _pad_to(jnp.concatenate([c, c_ctx[None, :]], axis=0), MOD_ROWS, 0)
    mods_all = _ada(cc, ada_w, ada_b)
    cos, sin = _rope_tables(S)
    head_of_lane = jnp.arange(R) // R_HEAD_DIM
    e1 = (head_of_lane[:, None] == jnp.arange(LORA_PAD)[None, :]).astype(F32)
    e2 = e1.T

    xs = jnp.concatenate([ctx.reshape(dm.MC, D), x.reshape(dm.ML, D)], axis=0)
    for l in range(depth):
        mods = mods_all[l].reshape(MOD_ROWS * N_MOD, 1, D)
        w_gu = ffn_w_gu[l].astype(BF16)
        w_dn = ffn_w_down[l].astype(BF16)
        xs = _ffn(dm, alpha, xs, mods, 0, w_gu[0], w_dn[0], ln_g[l, 0], ln_b[l, 0])

        z = _inproj(dm, xs, mods, _in_proj_weights(dm, w_in[l]))
        o_att = jnp.concatenate([_attn_ctx(dm, z, attn_sink[l]), _attn_latent(dm, z, attn_sink[l], cos, sin)], axis=0)

        rp = {
            'conv': _rwkv_cols(rwkv_conv[l], R),
            'g2': rwkv_g2[l].astype(BF16),
            'w0': rwkv_w0[l], 'a0': rwkv_a0[l],
            'w2': _pad_to(rwkv_w2[l], LORA_PAD, 1).astype(BF16),
            'a2': _pad_to(rwkv_a2[l], LORA_PAD, 1).astype(BF16),
            'k_k': rwkv_k_k[l].reshape(1, R), 'k_a': rwkv_k_a[l].reshape(1, R), 'r_k': rwkv_r_k[l].reshape(1, R),
            'ln_w': rwkv_ln_w[l].reshape(1, R), 'ln_b': rwkv_ln_b[l].reshape(1, R),
            'e1': e1, 'e2': e2,
        }
        r, v, kk, lw, kd, b, bonus, g = _rwkv_prep(dm, z, rp)
        g_m, h_m, q_m, y0_m = _rwkv_chunk(dm, r, v, kk, lw, kd, b)
        y_rwkv = _rwkv_scan(dm, g_m, h_m, q_m, y0_m)
        o_rwkv = _rwkv_out(dm, y_rwkv, bonus, g, rp)

        wa_p = jnp.stack([_pad_to(jnp.pad(gla_wa2[l, d], ((d * G_GATE_RANK, 0), (0, 0))), LORA_PAD, 0)
                          for d in range(2)])
        y_gla = _gla(dm, z, wa_p, gla_ba[l].reshape(2, 1, dm.GK))
        o_gla = _gla_out(dm, y_gla, z, gla_norm_w[l])

        xs = _merge(dm, alpha, xs, mods, z, o_att, o_rwkv, o_gla, w_branch_att[l].astype(BF16),
                    w_branch_rwkv[l].astype(BF16), w_branch_gla[l].astype(BF16), w_out[l].astype(BF16),
                    ln_g[l, 1], ln_b[l, 1])
        xs = _ffn(dm, alpha, xs, mods, 6, w_gu[1], w_dn[1], ln_g[l, 2], ln_b[l, 2])
    return xs[dm.MC:].reshape(B, S, D)
```

```python
import functools
import math

import jax
import jax.numpy as jnp
import numpy as np
from jax import lax
from jax.experimental import pallas as pl
from jax.experimental.pallas import tpu as pltpu

F32 = jnp.float32
BF16 = jnp.bfloat16
HI = lax.Precision.HIGHEST

N_MOD = 9
D_FF = 5632
LN_EPS = 1e-5
FFN_RES = 0.5

A_HEADS = 8
A_KV_HEADS = 2
A_GROUP = A_HEADS // A_KV_HEADS
A_HEAD_DIM = 128
A_Q_DIM = A_HEADS * A_HEAD_DIM
A_KV_DIM = A_KV_HEADS * A_HEAD_DIM
WINDOW = 128
A_BLOCK = 128
GRID_W = 64
ROPE_BASE = 10000.0
MASK_VALUE = -1e30

R_HEAD_DIM = 64
DECAY_LORA = 96
ICLR_LORA = 96
GATE_LORA = 256
R_GN_EPS = 64e-5
R_CHUNK = 64
LORA_PAD = 128

G_HEADS = 4
G_GATE_RANK = 16
G_GATE_NORM = 16.0
G_CHUNK = 64
G_EPS = 1e-5

MOD_ROWS = 16
VMEM_LIMIT = 58 * 1024 * 1024

NN = (((1,), (0,)), ((), ()))
NT = (((1,), (1,)), ((), ()))
TN = (((0,), (0,)), ((), ()))


def _dot(a, b, dims=NN, prec=None):
    return lax.dot_general(a, b, dims, precision=prec, preferred_element_type=F32)


def _sigmoid(x):
    return 1.0 / (1.0 + jnp.exp(-x))


def _log_sigmoid(x):
    return jnp.minimum(x, 0.0) - jnp.log(1.0 + jnp.exp(-jnp.abs(x)))


def _ln(x):
    mu = jnp.mean(x, -1, keepdims=True)
    xc = x - mu
    var = jnp.mean(xc * xc, -1, keepdims=True)
    return xc * lax.rsqrt(var + LN_EPS)


def _params(*sem):
    return pltpu.CompilerParams(dimension_semantics=sem, vmem_limit_bytes=VMEM_LIMIT)


class _Dims:
    def __init__(self, B, S, CT, D):
        self.B, self.S, self.CT, self.D = B, S, CT, D
        self.MC, self.ML = B * CT, B * S
        self.M = self.MC + self.ML
        self.R = D // 2
        self.GK = D // 2
        self.GV = D
        self.RH = self.R // R_HEAD_DIM
        self.rw_cols = 3 * self.R + GATE_LORA + 4 * LORA_PAD
        c = 0
        self.c_rwkv = c; c += self.rw_cols
        self.c_att_k = c; c += A_KV_DIM
        self.c_gate = c; c += 3 * D
        self.c_gla_v = c; c += self.GV
        self.c_gla_g = c; c += self.GV
        self.c_gla_q = c; c += self.GK
        self.c_gla_k = c; c += self.GK
        self.c_att_q = c; c += A_Q_DIM
        self.c_att_v = c; c += A_KV_DIM
        self.c_gla_ac = c; c += LORA_PAD
        self.tn_in = 1280
        self.NP = -(-c // self.tn_in) * self.tn_in
        self.tm = math.gcd(512, math.gcd(self.MC, S))
        self.tmr = math.gcd(256, math.gcd(CT, S))

    def mod_row(self, i, tm):
        nct = self.MC // tm
        tpb = self.S // tm
        return jnp.where(i < nct, self.B, (i - nct) // tpb)

    def mod_spec(self, m, tm):
        return pl.BlockSpec((None, 1, self.D), lambda i, *_: (self.mod_row(i, tm) * N_MOD + m, 0, 0))

    def chunk_block(self, b, d, s, chunk):
        ncc, ncl = self.CT // chunk, self.S // chunk
        j = s - ncc
        ctx_blk = b * ncc + jnp.where(d == 0, s, ncc - 1 - s)
        lat_blk = self.B * ncc + b * ncl + jnp.where(d == 0, j, ncl - 1 - j)
        return jnp.where(s < ncc, ctx_blk, lat_blk)


def _ada_kernel(c_ref, w_ref, b_ref, o_ref):
    cc = c_ref[...]
    s = (cc * _sigmoid(cc)).astype(BF16)
    o_ref[...] = _dot(s, w_ref[...].astype(BF16)) + b_ref[...]


def _ada(cc, ada_w, ada_b):
    L, D, N = ada_w.shape
    tn = 1024
    return pl.pallas_call(
        _ada_kernel,
        grid=(L, N // tn),
        in_specs=[
            pl.BlockSpec((MOD_ROWS, D), lambda l, n: (0, 0)),
            pl.BlockSpec((None, D, tn), lambda l, n: (l, 0, n)),
            pl.BlockSpec((None, 1, tn), lambda l, n: (l, 0, n)),
        ],
        out_specs=pl.BlockSpec((None, MOD_ROWS, tn), lambda l, n: (l, 0, n)),
        out_shape=jax.ShapeDtypeStruct((L, MOD_ROWS, N), F32),
        compiler_params=_params("parallel", "arbitrary"),
        name="ada",
    )(cc, ada_w, ada_b.reshape(L, 1, N))


def _ffn_kernel(alpha, x_ref, sh_ref, sc_ref, gt_ref, wg_ref, wu_ref, wd_ref, g_ref, b_ref, o_ref,
                h_ref, acc_ref):
    f = pl.program_id(1)

    @pl.when(f == 0)
    def _():
        h = _ln(x_ref[...]) * (1.0 + sc_ref[...]) + sh_ref[...]
        h_ref[...] = h.astype(BF16)
        acc_ref[...] = jnp.zeros_like(acc_ref)

    h = h_ref[...]
    g = _dot(h, wg_ref[...])
    u = _dot(h, wu_ref[...])
    a = (g * _sigmoid(g) * u).astype(BF16)
    acc_ref[...] += _dot(a, wd_ref[...])

    @pl.when(f == pl.num_programs(1) - 1)
    def _():
        z = alpha * x_ref[...] + (FFN_RES * gt_ref[...]) * acc_ref[...]
        o_ref[...] = _ln(z) * g_ref[...] + b_ref[...]


def _ffn(dm, alpha, x, mods, m0, w_gu, w_down, g, b):
    D, tm, tf = dm.D, dm.tm, 512
    nf = D_FF // tf
    return pl.pallas_call(
        functools.partial(_ffn_kernel, alpha),
        grid=(dm.M // tm, nf),
        in_specs=[
            pl.BlockSpec((tm, D), lambda i, f: (i, 0)),
            dm.mod_spec(m0, tm), dm.mod_spec(m0 + 1, tm), dm.mod_spec(m0 + 2, tm),
            pl.BlockSpec((D, tf), lambda i, f: (0, f)),
            pl.BlockSpec((D, tf), lambda i, f: (0, f + nf)),
            pl.BlockSpec((tf, D), lambda i, f: (f, 0)),
            pl.BlockSpec((1, D), lambda i, f: (0, 0)),
            pl.BlockSpec((1, D), lambda i, f: (0, 0)),
        ],
        out_specs=pl.BlockSpec((tm, D), lambda i, f: (i, 0)),
        out_shape=jax.ShapeDtypeStruct((dm.M, D), F32),
        scratch_shapes=[pltpu.VMEM((tm, D), BF16), pltpu.VMEM((tm, D), F32)],
        compiler_params=_params("parallel", "arbitrary"),
        name="ffn",
    )(x, mods, mods, mods, w_gu, w_gu, w_down, g.reshape(1, D), b.reshape(1, D))


def _inproj_kernel(x_ref, sh_ref, sc_ref, w_ref, o_ref, h_ref):
    @pl.when(pl.program_id(1) == 0)
    def _():
        h = _ln(x_ref[...]) * (1.0 + sc_ref[...]) + sh_ref[...]
        h_ref[...] = h.astype(BF16)

    o_ref[...] = _dot(h_ref[...], w_ref[...])


def _inproj(dm, x, mods, w_in_p):
    D, tm, tn = dm.D, dm.tm, dm.tn_in
    return pl.pallas_call(
        _inproj_kernel,
        grid=(dm.M // tm, dm.NP // tn),
        in_specs=[
            pl.BlockSpec((tm, D), lambda i, n: (i, 0)),
            dm.mod_spec(3, tm), dm.mod_spec(4, tm),
            pl.BlockSpec((D, tn), lambda i, n: (0, n)),
        ],
        out_specs=pl.BlockSpec((tm, tn), lambda i, n: (i, n)),
        out_shape=jax.ShapeDtypeStruct((dm.M, dm.NP), F32),
        scratch_shapes=[pltpu.VMEM((tm, D), BF16)],
        compiler_params=_params("parallel", "arbitrary"),
        name="inproj",
    )(x, mods, mods, w_in_p)


def _rope(x, cos, sin_signed):
    lane = lax.broadcasted_iota(jnp.int32, x.shape, 1)
    quarter = A_HEAD_DIM // 4
    swapped = jnp.where((lane % (2 * quarter)) < quarter,
                        pltpu.roll(x, A_HEAD_DIM - quarter, 1), pltpu.roll(x, quarter, 1))
    return x * cos + swapped * sin_signed


def _attn_kernel(S, nb, sink_ref, q_ref, kp_ref, kc_ref, kn_ref, vp_ref, vc_ref, vn_ref, kx_ref, vx_ref,
                 cos_ref, sin_ref, o_ref):
    n = pl.program_id(1)
    scale = A_HEAD_DIM ** -0.5

    def table(ref, blk):
        return ref[pl.ds(pl.multiple_of(blk * A_BLOCK, A_BLOCK), A_BLOCK), :]

    blk_p, blk_n = jnp.maximum(n - 1, 0), jnp.minimum(n + 1, nb - 1)
    cos_c, sin_c = table(cos_ref, n), table(sin_ref, n)
    cos_p, sin_p = table(cos_ref, blk_p), table(sin_ref, blk_p)
    cos_n, sin_n = table(cos_ref, blk_n), table(sin_ref, blk_n)

    rows, band = A_GROUP * A_BLOCK, 3 * A_BLOCK
    qpos = n * A_BLOCK + lax.broadcasted_iota(jnp.int32, (rows, band), 0) % A_BLOCK
    kpos = (n - 1) * A_BLOCK + lax.broadcasted_iota(jnp.int32, (rows, band), 1)
    valid = (jnp.abs(qpos - kpos) <= WINDOW) & (kpos >= 0) & (kpos < S)

    for kvh in range(A_KV_HEADS):
        ks = slice(kvh * A_HEAD_DIM, (kvh + 1) * A_HEAD_DIM)
        kb = jnp.concatenate([_rope(kp_ref[:, ks], cos_p, sin_p), _rope(kc_ref[:, ks], cos_c, sin_c),
                              _rope(kn_ref[:, ks], cos_n, sin_n)], axis=0).astype(BF16)
        vb = jnp.concatenate([vp_ref[:, ks], vc_ref[:, ks], vn_ref[:, ks]], axis=0).astype(BF16)
        kx = kx_ref[:, ks].astype(BF16)
        vx = vx_ref[:, ks].astype(BF16)
        heads = [kvh * A_GROUP + g for g in range(A_GROUP)]
        qs = [q_ref[:, h * A_HEAD_DIM:(h + 1) * A_HEAD_DIM] for h in heads]
        q_rope = jnp.concatenate([_rope(q, cos_c, sin_c) for q in qs], axis=0).astype(BF16)
        q_plain = jnp.concatenate(qs, axis=0).astype(BF16)
        sink = jnp.concatenate([jnp.full((A_BLOCK, 1), sink_ref[h], F32) for h in heads], axis=0)
        s_loc = jnp.where(valid, _dot(q_rope, kb, NT) * scale, MASK_VALUE)
        s_ctx = _dot(q_plain, kx, NT) * scale
        m = jnp.maximum(jnp.maximum(jnp.max(s_loc, -1, keepdims=True), jnp.max(s_ctx, -1, keepdims=True)), sink)
        p_loc = jnp.exp(s_loc - m)
        p_ctx = jnp.exp(s_ctx - m)
        den = jnp.sum(p_loc, -1, keepdims=True) + jnp.sum(p_ctx, -1, keepdims=True) + jnp.exp(sink - m)
        o = (_dot(p_loc.astype(BF16), vb) + _dot(p_ctx.astype(BF16), vx)) / den
        for g, h in enumerate(heads):
            o_ref[:, h * A_HEAD_DIM:(h + 1) * A_HEAD_DIM] = o[g * A_BLOCK:(g + 1) * A_BLOCK].astype(BF16)


def _attn_latent(dm, z, sink, cos, sin):
    B, S = dm.B, dm.S
    nb = S // A_BLOCK
    base = dm.MC // A_BLOCK
    qc = dm.c_att_q // A_Q_DIM
    kc = dm.c_att_k // A_KV_DIM
    vc = dm.c_att_v // A_KV_DIM

    def rows(shift):
        return lambda b, n: base + b * nb + jnp.clip(n + shift, 0, nb - 1)

    def kv_spec(col, shift):
        r = rows(shift)
        return pl.BlockSpec((A_BLOCK, A_KV_DIM), lambda b, n: (r(b, n), col))

    return pl.pallas_call(
        functools.partial(_attn_kernel, S, nb),
        grid=(B, nb),
        in_specs=[
            pl.BlockSpec(memory_space=pltpu.SMEM),
            pl.BlockSpec((A_BLOCK, A_Q_DIM), lambda b, n: (base + b * nb + n, qc)),
            kv_spec(kc, -1), kv_spec(kc, 0), kv_spec(kc, 1),
            kv_spec(vc, -1), kv_spec(vc, 0), kv_spec(vc, 1),
            pl.BlockSpec((dm.CT, A_KV_DIM), lambda b, n: (b, kc)),
            pl.BlockSpec((dm.CT, A_KV_DIM), lambda b, n: (b, vc)),
            pl.BlockSpec((S, A_HEAD_DIM), lambda b, n: (0, 0)),
            pl.BlockSpec((S, A_HEAD_DIM), lambda b, n: (0, 0)),
        ],
        out_specs=pl.BlockSpec((A_BLOCK, A_Q_DIM), lambda b, n: (b * nb + n, 0)),
        out_shape=jax.ShapeDtypeStruct((dm.ML, A_Q_DIM), BF16),
        compiler_params=_params("parallel", "arbitrary"),
        name="attn_latent",
    )(sink, z, z, z, z, z, z, z, z, z, cos, sin)


def _attn_ctx_kernel(CT, sink_ref, q_ref, k_ref, v_ref, o_ref):
    scale = A_HEAD_DIM ** -0.5
    for kvh in range(A_KV_HEADS):
        ks = slice(kvh * A_HEAD_DIM, (kvh + 1) * A_HEAD_DIM)
        k = k_ref[:, ks].astype(BF16)
        v = v_ref[:, ks].astype(BF16)
        heads = [kvh * A_GROUP + g for g in range(A_GROUP)]
        q = jnp.concatenate([q_ref[:, h * A_HEAD_DIM:(h + 1) * A_HEAD_DIM] for h in heads], axis=0).astype(BF16)
        sink = jnp.concatenate([jnp.full((CT, 1), sink_ref[h], F32) for h in heads], axis=0)
        s = _dot(q, k, NT) * scale
        m = jnp.maximum(jnp.max(s, -1, keepdims=True), sink)
        p = jnp.exp(s - m)
        den = jnp.sum(p, -1, keepdims=True) + jnp.exp(sink - m)
        o = _dot(p.astype(BF16), v) / den
        for g, h in enumerate(heads):
            o_ref[:, h * A_HEAD_DIM:(h + 1) * A_HEAD_DIM] = o[g * CT:(g + 1) * CT].astype(BF16)


def _attn_ctx(dm, z, sink):
    CT = dm.CT
    return pl.pallas_call(
        functools.partial(_attn_ctx_kernel, CT),
        grid=(dm.B,),
        in_specs=[
            pl.BlockSpec(memory_space=pltpu.SMEM),
            pl.BlockSpec((CT, A_Q_DIM), lambda b: (b, dm.c_att_q // A_Q_DIM)),
            pl.BlockSpec((CT, A_KV_DIM), lambda b: (b, dm.c_att_k // A_KV_DIM)),
            pl.BlockSpec((CT, A_KV_DIM), lambda b: (b, dm.c_att_v // A_KV_DIM)),
        ],
        out_specs=pl.BlockSpec((CT, A_Q_DIM), lambda b: (b, 0)),
        out_shape=jax.ShapeDtypeStruct((dm.MC, A_Q_DIM), BF16),
        compiler_params=_params("parallel"),
        name="attn_ctx",
    )(sink, z, z, z)


def _head_sum(x, e1_ref, e2_ref):
    def split_dot(a, ind):
        hi = a.astype(BF16)
        lo = (a - hi.astype(F32)).astype(BF16)
        return _dot(hi, ind) + _dot(lo, ind)

    return split_dot(split_dot(x, e1_ref[...]), e2_ref[...])


def _rwkv_prep_kernel(dm, z_ref, zp_ref, zn_ref, cw_ref, g2_ref, w0_ref, w2_ref, a0_ref, a2_ref, kk_ref, ka_ref,
                      rk_ref, e1_ref, e2_ref,
                      r_out, v_out, kk_out, lw_out, kd_out, b_out, bonus_out, g_out):
    i = pl.program_id(0)
    tm, R = dm.tmr, dm.R
    nct = dm.MC // tm
    start = jnp.where(i < nct, (i * tm) % dm.CT, ((i - nct) * tm) % dm.S)
    seqlen = jnp.where(i < nct, dm.CT, dm.S)
    has_prev = (start != 0).astype(F32)
    has_next = (start + tm != seqlen).astype(F32)

    z = z_ref[...]
    rowid = lax.broadcasted_iota(jnp.int32, (tm, 1), 0)
    z_prev = jnp.where(rowid == 0, zp_ref[7:8, :] * has_prev, pltpu.roll(z, 1, 0))
    z_next = jnp.where(rowid == tm - 1, zn_ref[0:1, :] * has_next, pltpu.roll(z, tm - 1, 0))
    zc = z_prev * cw_ref[0:1, :] + z * cw_ref[1:2, :] + z_next * cw_ref[2:3, :]

    r = zc[:, 0:R]
    k = zc[:, R:2 * R]
    v = zc[:, 2 * R:3 * R]
    c0 = 3 * R
    gc = zc[:, c0:c0 + GATE_LORA]
    c0 += GATE_LORA
    g_out[...] = _dot(_sigmoid(gc).astype(BF16), g2_ref[...])

    kkr = k * kk_ref[...]
    norm = jnp.sqrt(_head_sum(kkr * kkr, e1_ref, e2_ref))
    kk = kkr / jnp.maximum(norm, 1e-12)
    r_out[...] = r
    v_out[...] = v
    kk_out[...] = kk

    bonus = jnp.zeros_like(v)
    for d in range(2):
        wc = zc[:, c0 + d * LORA_PAD:c0 + (d + 1) * LORA_PAD]
        ac = zc[:, c0 + (2 + d) * LORA_PAD:c0 + (3 + d) * LORA_PAD]
        wl = w0_ref[d:d + 1, :] + _dot(jnp.tanh(wc).astype(BF16), w2_ref[d])
        w = _log_sigmoid(wl) - 0.5
        lw_out[d] = -jnp.exp(w)
        a = _sigmoid(a0_ref[d:d + 1, :] + _dot(ac.astype(BF16), a2_ref[d]))
        kd = k * (1.0 + (a - 1.0) * ka_ref[...])
        kd_out[d] = kd
        b_out[d] = kk * a
        bonus = bonus + _head_sum(r * kd * rk_ref[...], e1_ref, e2_ref) * v
    bonus_out[...] = bonus


def _rwkv_prep(dm, z, p):
    tm, R, M, W = dm.tmr, dm.R, dm.M, dm.rw_cols
    nblk8 = M // 8
    row = pl.BlockSpec((tm, R), lambda i: (i, 0))
    row2 = pl.BlockSpec((2, tm, R), lambda i: (0, i, 0))

    def full(a):
        return pl.BlockSpec(a.shape, lambda i: (0,) * a.ndim)

    consts = [p['conv'], p['g2'], p['w0'], p['w2'], p['a0'], p['a2'], p['k_k'], p['k_a'], p['r_k'], p['e1'], p['e2']]
    one = jax.ShapeDtypeStruct((M, R), F32)
    two = jax.ShapeDtypeStruct((2, M, R), F32)
    return pl.pallas_call(
        functools.partial(_rwkv_prep_kernel, dm),
        grid=(M // tm,),
        in_specs=[
            pl.BlockSpec((tm, W), lambda i: (i, 0)),
            pl.BlockSpec((8, W), lambda i: (jnp.maximum(i * (tm // 8) - 1, 0), 0)),
            pl.BlockSpec((8, W), lambda i: (jnp.minimum((i + 1) * (tm // 8), nblk8 - 1), 0)),
        ] + [full(a) for a in consts],
        out_specs=[row, row, row, row2, row2, row2, row, row],
        out_shape=[one, one, one, two, two, two, one, one],
        compiler_params=_params("parallel"),
        name="rwkv_prep",
    )(z, z, z, *consts)


R_SLAB = 4 * R_HEAD_DIM


def _block_diag(x, mask):
    return jnp.where(mask, jnp.concatenate([x] * 4, axis=0), 0.0).astype(BF16)


def _diag_blocks(full, mask):
    fm = jnp.where(mask, full, 0.0)
    n = R_HEAD_DIM
    return fm[0:n] + fm[n:2 * n] + fm[2 * n:3 * n] + fm[3 * n:4 * n]


def _slab_masks():
    n, g = R_HEAD_DIM, R_SLAB
    blk = (lax.broadcasted_iota(jnp.int32, (g, g), 0) // n) == (lax.broadcasted_iota(jnp.int32, (g, g), 1) // n)
    t_idx = lax.broadcasted_iota(jnp.int32, (n, g), 0)
    s_idx = lax.broadcasted_iota(jnp.int32, (n, g), 1) % n
    return blk, t_idx, s_idx


def _rwkv_chunk_kernel(n_sub, r_ref, v_ref, kk_ref, lw_ref, kd_ref, b_ref, g_out, h_out, q_out, y0_out):
    C = R_CHUNK
    blk, t_idx, s_idx = _slab_masks()
    eye4 = (t_idx == s_idx).astype(F32)
    ti = lax.broadcasted_iota(jnp.int32, (C, C), 0)
    si = lax.broadcasted_iota(jnp.int32, (C, C), 1)

    def bd(x):
        return _block_diag(x, blk)

    def setup(c, d):
        rs = pl.ds(c * C, C)
        r, v, kk = r_ref[rs, :], v_ref[rs, :], kk_ref[rs, :]
        tri = ((ti >= si) if d == 0 else (ti <= si)).astype(F32)
        lw = lw_ref[d, rs, :]
        cl = _dot(tri, lw, prec=HI)
        tot = jnp.sum(lw, axis=0, keepdims=True)
        e_neg = jnp.exp(-cl)
        e_end = jnp.exp(tot - cl)
        kd, b = kd_ref[d, rs, :], b_ref[d, rs, :]
        ch = dict(rs=rs, d=d, tot=tot, v_b=v.astype(BF16), v_bd=bd(v),
                  incl=(t_idx >= s_idx) if d == 0 else (t_idx <= s_idx),
                  strict=(t_idx > s_idx) if d == 0 else (t_idx < s_idx),
                  kkm=kk * jnp.exp(cl - lw), rp=r * jnp.exp(cl),
                  bp=b * e_neg, kp=kd * e_neg, be=(b * e_end).astype(BF16), ke=(kd * e_end).astype(BF16))
        ch['lhs'] = jnp.concatenate([ch['kkm'], ch['rp']], axis=0).astype(BF16)
        return ch

    chains = [setup(c, d) for c in range(n_sub) for d in range(2)]
    for ch in chains:
        gram_b = _dot(ch['lhs'], bd(ch.pop('bp')), NT)
        gram_k = _dot(ch.pop('lhs'), bd(ch.pop('kp')), NT)
        lb = jnp.where(ch['strict'], gram_b[:C], 0.0)
        ch['mb'] = jnp.where(ch['incl'], gram_b[C:], 0.0).astype(BF16)
        ch['lk'] = jnp.where(ch['strict'], gram_k[:C], 0.0).astype(BF16)
        ch['mk'] = jnp.where(ch['incl'], gram_k[C:], 0.0).astype(BF16)
        ch['pw'] = -lb
        ch['t_inv'] = eye4 - lb
    for ch in chains:
        ch['pw'] = _dot(ch['pw'].astype(BF16), bd(ch['pw']))
        ch['lkv'] = _dot(ch.pop('lk'), ch['v_bd'])
    for _ in range(int(math.log2(C)) - 2):
        for ch in chains:
            both = _dot(jnp.concatenate([ch['pw'], ch['t_inv']], axis=0).astype(BF16), bd(ch['pw']))
            ch['pw'], ch['t_inv'] = both[:C], ch['t_inv'] + both[C:]
    for ch in chains:
        ch['t_inv'] = (ch['t_inv'] + _dot(ch['t_inv'].astype(BF16), bd(ch.pop('pw')))).astype(BF16)
    for ch in chains:
        ch['a_m'] = _dot(ch['t_inv'], bd(ch.pop('kkm')))
        ch['d_m'] = -_dot(ch.pop('t_inv'), bd(ch.pop('lkv')))
    for ch in chains:
        d, rs = ch['d'], ch['rs']
        a_m, d_m = ch['a_m'], ch['d_m']
        q_out[d, rs, :] = (ch['rp'] - _dot(ch['mb'], bd(a_m))).astype(BF16)
        y0_out[d, rs, :] = _dot(ch['mb'], bd(d_m)) + _dot(ch['mk'], ch['v_bd'])
        g_cross = _dot(ch['be'], a_m.astype(BF16), TN)
        h_cross = _dot(ch['be'], d_m.astype(BF16), TN) + _dot(ch['ke'], ch['v_b'], TN)
        g_out[d, rs, :] = (eye4 * jnp.exp(ch['tot']) - _diag_blocks(g_cross, blk)).astype(BF16)
        h_out[d, rs, :] = _diag_blocks(h_cross, blk)


def _rwkv_chunk(dm, r, v, kk, lw, kd, b):
    M, R = dm.M, dm.R
    rows = dm.tmr
    one = pl.BlockSpec((rows, R_SLAB), lambda i, h: (i, h))
    two = pl.BlockSpec((2, rows, R_SLAB), lambda i, h: (0, i, h))
    f32 = jax.ShapeDtypeStruct((2, M, R), F32)
    b16 = jax.ShapeDtypeStruct((2, M, R), BF16)
    return pl.pallas_call(
        functools.partial(_rwkv_chunk_kernel, rows // R_CHUNK),
        grid=(M // rows, R // R_SLAB),
        in_specs=[one, one, one, two, two, two],
        out_specs=[two, two, two, two],
        out_shape=[b16, f32, b16, f32],
        compiler_params=_params("parallel", "parallel"),
        name="rwkv_chunk",
    )(r, v, kk, lw, kd, b)


def _rwkv_scan_kernel(n_slab, g_ref, h_ref, q_ref, y0_ref, y_ref, st_ref):
    C = R_CHUNK

    @pl.when(pl.program_id(2) == 0)
    def _():
        st_ref[...] = jnp.zeros_like(st_ref)

    blk, _, _ = _slab_masks()
    for j in range(n_slab):
        ls = slice(j * R_SLAB, (j + 1) * R_SLAB)
        st_bd = _block_diag(st_ref[:, ls], blk)
        both = _dot(jnp.concatenate([q_ref[:, ls], g_ref[:, ls]], axis=0), st_bd)
        y_ref[:, ls] = both[:C] + y0_ref[:, ls]
        st_ref[:, ls] = both[C:] + h_ref[:, ls]


def _rwkv_scan(dm, g_m, h_m, q_m, y0_m):
    C, R = R_CHUNK, dm.R
    steps = (dm.CT + dm.S) // C
    spec = pl.BlockSpec((None, C, R), lambda b, d, s: (d, dm.chunk_block(b, d, s, C), 0))
    return pl.pallas_call(
        functools.partial(_rwkv_scan_kernel, R // R_SLAB),
        grid=(dm.B, 2, steps),
        in_specs=[spec, spec, spec, spec],
        out_specs=spec,
        out_shape=jax.ShapeDtypeStruct((2, dm.M, R), F32),
        scratch_shapes=[pltpu.VMEM((R_HEAD_DIM, R), F32)],
        compiler_params=_params("parallel", "parallel", "arbitrary"),
        name="rwkv_scan",
    )(g_m, h_m, q_m, y0_m)


def _rwkv_out_kernel(y_ref, bonus_ref, g_ref, lnw_ref, lnb_ref, e1_ref, e2_ref, o_ref):
    o = y_ref[0] + y_ref[1]
    inv_n = 1.0 / R_HEAD_DIM
    mu = _head_sum(o, e1_ref, e2_ref) * inv_n
    oc = o - mu
    var = _head_sum(oc * oc, e1_ref, e2_ref) * inv_n
    o = oc * lax.rsqrt(var + R_GN_EPS) * lnw_ref[...] + lnb_ref[...]
    o_ref[...] = ((o + bonus_ref[...]) * g_ref[...]).astype(BF16)


def _rwkv_out(dm, y, bonus, g, p):
    tm, R, M = dm.tmr, dm.R, dm.M
    row = pl.BlockSpec((tm, R), lambda i: (i, 0))

    def full(a):
        return pl.BlockSpec(a.shape, lambda i: (0,) * a.ndim)

    consts = [p['ln_w'], p['ln_b'], p['e1'], p['e2']]
    return pl.pallas_call(
        _rwkv_out_kernel,
        grid=(M // tm,),
        in_specs=[pl.BlockSpec((2, tm, R), lambda i: (0, i, 0)), row, row] + [full(a) for a in consts],
        out_specs=row,
        out_shape=jax.ShapeDtypeStruct((M, R), BF16),
        compiler_params=_params("parallel"),
        name="rwkv_out",
    )(y, bonus, g, *consts)


def _gla_kernel(dm, q_ref, k_ref, v_ref, ac_ref, wa_ref, ba_ref, y_ref, st_ref):
    C = G_CHUNK
    dk, dv = dm.GK // G_HEADS, dm.GV // G_HEADS
    d = pl.program_id(1)

    @pl.when(pl.program_id(2) == 0)
    def _():
        st_ref[...] = jnp.zeros_like(st_ref)

    ti = lax.broadcasted_iota(jnp.int32, (C, C), 0)
    si = lax.broadcasted_iota(jnp.int32, (C, C), 1)
    causal = jnp.where(d == 0, ti - si, si - ti) >= 0
    la = _log_sigmoid(_dot(ac_ref[...], wa_ref[...], prec=HI) + ba_ref[...]) * (1.0 / G_GATE_NORM)
    cb = _dot(causal.astype(F32), la, prec=HI)
    tot = jnp.sum(la, axis=0, keepdims=True)
    q_dec = (q_ref[...] * (dk ** -0.5) * jnp.exp(cb)).astype(BF16)
    k = k_ref[...]
    k_inv = (k * jnp.exp(-cb)).astype(BF16)
    k_end = (k * jnp.exp(tot - cb)).astype(BF16)
    e_last = jnp.exp(tot)
    for h in range(G_HEADS):
        ks = slice(h * dk, (h + 1) * dk)
        vs = slice(h * dv, (h + 1) * dv)
        v = v_ref[:, vs].astype(BF16)
        att = jnp.where(causal, _dot(q_dec[:, ks], k_inv[:, ks], NT), 0.0)
        st = st_ref[h]
        y_ref[:, vs] = _dot(att.astype(BF16), v) + _dot(q_dec[:, ks], st.astype(BF16), NT)
        st_ref[h] = st * e_last[:, ks] + _dot(v, k_end[:, ks], TN)


def _gla(dm, z, wa_p, ba):
    C = G_CHUNK
    steps = (dm.CT + dm.S) // C
    GK, GV = dm.GK, dm.GV

    def zspec(width, col):
        return pl.BlockSpec((C, width), lambda b, d, s: (dm.chunk_block(b, d, s, C), col // width))

    return pl.pallas_call(
        functools.partial(_gla_kernel, dm),
        grid=(dm.B, 2, steps),
        in_specs=[
            zspec(GK, dm.c_gla_q), zspec(GK, dm.c_gla_k), zspec(GV, dm.c_gla_v), zspec(LORA_PAD, dm.c_gla_ac),
            pl.BlockSpec((None, LORA_PAD, GK), lambda b, d, s: (d, 0, 0)),
            pl.BlockSpec((None, 1, GK), lambda b, d, s: (d, 0, 0)),
        ],
        out_specs=pl.BlockSpec((None, C, GV), lambda b, d, s: (d, dm.chunk_block(b, d, s, C), 0)),
        out_shape=jax.ShapeDtypeStruct((2, dm.M, GV), F32),
        scratch_shapes=[pltpu.VMEM((G_HEADS, GV // G_HEADS, GK // G_HEADS), F32)],
        compiler_params=_params("parallel", "parallel", "arbitrary"),
        name="gla",
    )(z, z, z, z, wa_p, ba)


def _gla_out_kernel(dv, y_ref, g_ref, nw_ref, o_ref):
    o = y_ref[0] + y_ref[1]
    g = g_ref[...]
    gate = g * _sigmoid(g)
    for h in range(G_HEADS):
        vs = slice(h * dv, (h + 1) * dv)
        oh = o[:, vs]
        oh = oh * lax.rsqrt(jnp.mean(oh * oh, -1, keepdims=True) + G_EPS) * nw_ref[...]
        o_ref[:, vs] = (oh * gate[:, vs]).astype(BF16)


def _gla_out(dm, y, z, norm_w):
    tm, GV, M = dm.tmr, dm.GV, dm.M
    dv = GV // G_HEADS
    return pl.pallas_call(
        functools.partial(_gla_out_kernel, dv),
        grid=(M // tm,),
        in_specs=[
            pl.BlockSpec((2, tm, GV), lambda i: (0, i, 0)),
            pl.BlockSpec((tm, GV), lambda i: (i, dm.c_gla_g // GV)),
            pl.BlockSpec((1, dv), lambda i: (0, 0)),
        ],
        out_specs=pl.BlockSpec((tm, GV), lambda i: (i, 0)),
        out_shape=jax.ShapeDtypeStruct((M, GV), BF16),
        compiler_params=_params("parallel"),
        name="gla_out",
    )(y, z, norm_w.reshape(1, dv))


def _merge_kernel(alpha, x_ref, mg_ref, oa_ref, or_ref, og_ref, ga_ref, gr_ref, gg_ref, wa_ref, wr_ref, wg_ref,
                  wo_ref, g_ref, b_ref, o_ref, acc_ref):
    n = pl.program_id(1)

    @pl.when(n == 0)
    def _():
        acc_ref[...] = jnp.zeros_like(acc_ref)

    merged = (_sigmoid(ga_ref[...]) * _dot(oa_ref[...], wa_ref[...])
              + _sigmoid(gr_ref[...]) * _dot(or_ref[...], wr_ref[...])
              + _sigmoid(gg_ref[...]) * _dot(og_ref[...], wg_ref[...]))
    acc_ref[...] += _dot(merged.astype(BF16), wo_ref[...])

    @pl.when(n == pl.num_programs(1) - 1)
    def _():
        z = alpha * x_ref[...] + mg_ref[...] * acc_ref[...]
        o_ref[...] = _ln(z) * g_ref[...] + b_ref[...]


def _merge(dm, alpha, x, mods, z, o_att, o_rwkv, o_gla, wba, wbr, wbg, w_out, g, b):
    D, tm, tn = dm.D, dm.tm, 512
    gate0 = dm.c_gate // tn
    per = D // tn

    def gate_spec(j):
        return pl.BlockSpec((tm, tn), lambda i, n: (i, gate0 + j * per + n))

    def resident(width):
        return pl.BlockSpec((tm, width), lambda i, n: (i, 0))

    def wcol(rows):
        return pl.BlockSpec((rows, tn), lambda i, n: (0, n))

    return pl.pallas_call(
        functools.partial(_merge_kernel, alpha),
        grid=(dm.M // tm, per),
        in_specs=[
            resident(D), dm.mod_spec(5, tm),
            resident(A_Q_DIM), resident(dm.R), resident(dm.GV),
            gate_spec(0), gate_spec(1), gate_spec(2),
            wcol(A_Q_DIM), wcol(dm.R), wcol(dm.GV),
            pl.BlockSpec((tn, D), lambda i, n: (n, 0)),
            pl.BlockSpec((1, D), lambda i, n: (0, 0)),
            pl.BlockSpec((1, D), lambda i, n: (0, 0)),
        ],
        out_specs=resident(D),
        out_shape=jax.ShapeDtypeStruct((dm.M, D), F32),
        scratch_shapes=[pltpu.VMEM((tm, D), F32)],
        compiler_params=_params("parallel", "arbitrary"),
        name="merge",
    )(x, mods, o_att, o_rwkv, o_gla, z, z, z, wba, wbr, wbg, w_out, g.reshape(1, D), b.reshape(1, D))


def _pad_to(a, size, axis):
    pad = [(0, 0)] * a.ndim
    pad[axis] = (0, size - a.shape[axis])
    return jnp.pad(a, pad)


def _rwkv_cols(a, R):
    c = 3 * R
    rkv = a[..., :c]
    wc0, wc1 = a[..., c:c + DECAY_LORA], a[..., c + DECAY_LORA:c + 2 * DECAY_LORA]
    c += 2 * DECAY_LORA
    ac0, ac1 = a[..., c:c + ICLR_LORA], a[..., c + ICLR_LORA:c + 2 * ICLR_LORA]
    c += 2 * ICLR_LORA
    gc = a[..., c:c + GATE_LORA]
    ax = a.ndim - 1
    return jnp.concatenate([rkv, gc] + [_pad_to(t, LORA_PAD, ax) for t in (wc0, wc1, ac0, ac1)], axis=-1)


def _in_proj_weights(dm, w_in):
    D, R, GK, GV = dm.D, dm.R, dm.GK, dm.GV
    att_cols = A_Q_DIM + 2 * A_KV_DIM
    rw_src = 3 * R + 2 * DECAY_LORA + 2 * ICLR_LORA + GATE_LORA
    gla_src = 2 * GK + 2 * GV + 2 * G_GATE_RANK
    att, rwkv, gla, gate = jnp.split(w_in, np.cumsum([att_cols, rw_src, gla_src]).tolist(), axis=-1)
    a_q, a_k, a_v = att[:, :A_Q_DIM], att[:, A_Q_DIM:A_Q_DIM + A_KV_DIM], att[:, A_Q_DIM + A_KV_DIM:]
    g_q, g_k = gla[:, :GK], gla[:, GK:2 * GK]
    g_v, g_g = gla[:, 2 * GK:2 * GK + GV], gla[:, 2 * GK + GV:2 * GK + 2 * GV]
    g_ac = _pad_to(gla[:, 2 * GK + 2 * GV:], LORA_PAD, 1)
    w = jnp.concatenate([_rwkv_cols(rwkv, R), a_k, gate, g_v, g_g, g_q, g_k, a_q, a_v, g_ac], axis=-1)
    return _pad_to(w, dm.NP, 1).astype(BF16)


def _rope_tables(S):
    rows = S // GRID_W
    row = jnp.repeat(jnp.arange(rows, dtype=F32), GRID_W)
    col = jnp.tile(jnp.arange(GRID_W, dtype=F32), rows)
    n_freq = A_HEAD_DIM // 4
    inv_freq = ROPE_BASE ** (-jnp.arange(n_freq, dtype=F32) / n_freq)
    ang_r, ang_c = row[:, None] * inv_freq, col[:, None] * inv_freq
    cos = jnp.concatenate([jnp.cos(ang_r)] * 2 + [jnp.cos(ang_c)] * 2, axis=-1)
    sin = jnp.concatenate([-jnp.sin(ang_r), jnp.sin(ang_r), -jnp.sin(ang_c), jnp.sin(ang_c)], axis=-1)
    return cos, sin


def kernel(x, c, ctx, c_ctx, ada_w, ada_b, ln_g, ln_b, ffn_w_gu, ffn_w_down, w_in, attn_sink, rwkv_conv, rwkv_w0,
           rwkv_w2, rwkv_a0, rwkv_a2, rwkv_g2, rwkv_k_k, rwkv_k_a, rwkv_r_k, rwkv_ln_w, rwkv_ln_b, gla_wa2, gla_ba,
           gla_norm_w, w_branch_att, w_branch_rwkv, w_branch_gla, w_out):
    B, S, D = x.shape
    CT = ctx.shape[1]
    depth = ada_w.shape[0]
    dm = _Dims(B, S, CT, D)
    R = dm.R
    alpha = float((2 * depth) ** 0.25)
    assert B + 1 <= MOD_ROWS

    cc = _pad_to(jnp.concatenate([c, c_ctx[None, :]], axis=0), MOD_ROWS, 0)
    mods_all = _ada(cc, ada_w, ada_b)
    cos, sin = _rope_tables(S)
    head_of_lane = jnp.arange(R) // R_HEAD_DIM
    e1 = (head_of_lane[:, None] == jnp.arange(LORA_PAD)[None, :]).astype(BF16)
    e2 = e1.T

    xs = jnp.concatenate([ctx.reshape(dm.MC, D), x.reshape(dm.ML, D)], axis=0)
    for l in range(depth):
        mods = mods_all[l].reshape(MOD_ROWS * N_MOD, 1, D)
        w_gu = ffn_w_gu[l].astype(BF16)
        w_dn = ffn_w_down[l].astype(BF16)
        xs = _ffn(dm, alpha, xs, mods, 0, w_gu[0], w_dn[0], ln_g[l, 0], ln_b[l, 0])

        z = _inproj(dm, xs, mods, _in_proj_weights(dm, w_in[l]))
        o_att = jnp.concatenate([_attn_ctx(dm, z, attn_sink[l]), _attn_latent(dm, z, attn_sink[l], cos, sin)], axis=0)

        rp = {
            'conv': _rwkv_cols(rwkv_conv[l], R),
            'g2': rwkv_g2[l].astype(BF16),
            'w0': rwkv_w0[l], 'a0': rwkv_a0[l],
            'w2': _pad_to(rwkv_w2[l], LORA_PAD, 1).astype(BF16),
            'a2': _pad_to(rwkv_a2[l], LORA_PAD, 1).astype(BF16),
            'k_k': rwkv_k_k[l].reshape(1, R), 'k_a': rwkv_k_a[l].reshape(1, R), 'r_k': rwkv_r_k[l].reshape(1, R),
            'ln_w': rwkv_ln_w[l].reshape(1, R), 'ln_b': rwkv_ln_b[l].reshape(1, R),
            'e1': e1, 'e2': e2,
        }
        r, v, kk, lw, kd, b, bonus, g = _rwkv_prep(dm, z, rp)
        g_m, h_m, q_m, y0_m = _rwkv_chunk(dm, r, v, kk, lw, kd, b)
        y_rwkv = _rwkv_scan(dm, g_m, h_m, q_m, y0_m)
        o_rwkv = _rwkv_out(dm, y_rwkv, bonus, g, rp)

        wa_p = jnp.stack([_pad_to(jnp.pad(gla_wa2[l, d], ((d * G_GATE_RANK, 0), (0, 0))), LORA_PAD, 0)
                          for d in range(2)])
        y_gla = _gla(dm, z, wa_p, gla_ba[l].reshape(2, 1, dm.GK))
        o_gla = _gla_out(dm, y_gla, z, gla_norm_w[l])

        xs = _merge(dm, alpha, xs, mods, z, o_att, o_rwkv, o_gla, w_branch_att[l].astype(BF16),
                    w_branch_rwkv[l].astype(BF16), w_branch_gla[l].astype(BF16), w_out[l].astype(BF16),
                    ln_g[l, 1], ln_b[l, 1])
        xs = _ffn(dm, alpha, xs, mods, 6, w_gu[1], w_dn[1], ln_g[l, 2], ln_b[l, 2])
    return xs[dm.MC:].reshape(B, S, D)
```

```python
import functools
import math

import jax
import jax.numpy as jnp
import numpy as np
from jax import lax
from jax.experimental import pallas as pl
from jax.experimental.pallas import tpu as pltpu

F32 = jnp.float32
BF16 = jnp.bfloat16
HI = lax.Precision.HIGHEST

N_MOD = 9
D_FF = 5632
LN_EPS = 1e-5
FFN_RES = 0.5

A_HEADS = 8
A_KV_HEADS = 2
A_GROUP = A_HEADS // A_KV_HEADS
A_HEAD_DIM = 128
A_Q_DIM = A_HEADS * A_HEAD_DIM
A_KV_DIM = A_KV_HEADS * A_HEAD_DIM
WINDOW = 128
A_BLOCK = 128
GRID_W = 64
ROPE_BASE = 10000.0
MASK_VALUE = -1e30

R_HEAD_DIM = 64
DECAY_LORA = 96
ICLR_LORA = 96
GATE_LORA = 256
R_GN_EPS = 64e-5
R_CHUNK = 64
LORA_PAD = 128

G_HEADS = 4
G_GATE_RANK = 16
G_GATE_NORM = 16.0
G_CHUNK = 64
G_EPS = 1e-5

MOD_ROWS = 16
VMEM_LIMIT = 58 * 1024 * 1024

NN = (((1,), (0,)), ((), ()))
NT = (((1,), (1,)), ((), ()))
TN = (((0,), (0,)), ((), ()))


def _dot(a, b, dims=NN, prec=None):
    return lax.dot_general(a, b, dims, precision=prec, preferred_element_type=F32)


def _sigmoid(x):
    return 1.0 / (1.0 + jnp.exp(-x))


def _log_sigmoid(x):
    return jnp.minimum(x, 0.0) - jnp.log(1.0 + jnp.exp(-jnp.abs(x)))


def _ln(x):
    mu = jnp.mean(x, -1, keepdims=True)
    xc = x - mu
    var = jnp.mean(xc * xc, -1, keepdims=True)
    return xc * lax.rsqrt(var + LN_EPS)


def _params(*sem):
    return pltpu.CompilerParams(dimension_semantics=sem, vmem_limit_bytes=VMEM_LIMIT)


class _Dims:
    def __init__(self, B, S, CT, D):
        self.B, self.S, self.CT, self.D = B, S, CT, D
        self.MC, self.ML = B * CT, B * S
        self.M = self.MC + self.ML
        self.R = D // 2
        self.GK = D // 2
        self.GV = D
        self.RH = self.R // R_HEAD_DIM
        self.rw_cols = 3 * self.R + GATE_LORA + 4 * LORA_PAD
        c = 0
        self.c_rwkv = c; c += self.rw_cols
        self.c_att_k = c; c += A_KV_DIM
        self.c_gate = c; c += 3 * D
        self.c_gla_v = c; c += self.GV
        self.c_gla_g = c; c += self.GV
        self.c_gla_q = c; c += self.GK
        self.c_gla_k = c; c += self.GK
        self.c_att_q = c; c += A_Q_DIM
        self.c_att_v = c; c += A_KV_DIM
        self.c_gla_ac = c; c += LORA_PAD
        self.tn_in = 1280
        self.NP = -(-c // self.tn_in) * self.tn_in
        self.tm = math.gcd(512, math.gcd(self.MC, S))
        self.tmr = math.gcd(256, math.gcd(CT, S))

    def mod_row(self, i, tm):
        nct = self.MC // tm
        tpb = self.S // tm
        return jnp.where(i < nct, self.B, (i - nct) // tpb)

    def mod_spec(self, m, tm, off=0):
        return pl.BlockSpec((None, 1, self.D), lambda i, *_: (self.mod_row(i + off, tm) * N_MOD + m, 0, 0))

    def chunk_block(self, b, d, s, chunk):
        ncc, ncl = self.CT // chunk, self.S // chunk
        j = s - ncc
        ctx_blk = b * ncc + jnp.where(d == 0, s, ncc - 1 - s)
        lat_blk = self.B * ncc + b * ncl + jnp.where(d == 0, j, ncl - 1 - j)
        return jnp.where(s < ncc, ctx_blk, lat_blk)


def _ada_kernel(c_ref, w_ref, b_ref, o_ref):
    cc = c_ref[...]
    s = (cc * _sigmoid(cc)).astype(BF16)
    o_ref[...] = _dot(s, w_ref[...].astype(BF16)) + b_ref[...]


def _ada(cc, ada_w, ada_b):
    L, D, N = ada_w.shape
    tn = 1024
    return pl.pallas_call(
        _ada_kernel,
        grid=(L, N // tn),
        in_specs=[
            pl.BlockSpec((MOD_ROWS, D), lambda l, n: (0, 0)),
            pl.BlockSpec((None, D, tn), lambda l, n: (l, 0, n)),
            pl.BlockSpec((None, 1, tn), lambda l, n: (l, 0, n)),
        ],
        out_specs=pl.BlockSpec((None, MOD_ROWS, tn), lambda l, n: (l, 0, n)),
        out_shape=jax.ShapeDtypeStruct((L, MOD_ROWS, N), F32),
        compiler_params=_params("parallel", "arbitrary"),
        name="ada",
    )(cc, ada_w, ada_b.reshape(L, 1, N))


def _ffn_kernel(alpha, x_ref, sh_ref, sc_ref, gt_ref, wg_ref, wu_ref, wd_ref, g_ref, b_ref, o_ref,
                h_ref, acc_ref):
    f = pl.program_id(1)

    @pl.when(f == 0)
    def _():
        h = _ln(x_ref[...]) * (1.0 + sc_ref[...]) + sh_ref[...]
        h_ref[...] = h.astype(BF16)
        acc_ref[...] = jnp.zeros_like(acc_ref)

    h = h_ref[...]
    g = _dot(h, wg_ref[...])
    u = _dot(h, wu_ref[...])
    a = (g * _sigmoid(g) * u).astype(BF16)
    acc_ref[...] += _dot(a, wd_ref[...])

    @pl.when(f == pl.num_programs(1) - 1)
    def _():
        z = alpha * x_ref[...] + (FFN_RES * gt_ref[...]) * acc_ref[...]
        o_ref[...] = _ln(z) * g_ref[...] + b_ref[...]


def _ffn(dm, alpha, x, mods, m0, w_gu, w_down, l, j, g, b):
    D, tm, tf = dm.D, dm.tm, 512
    nf = D_FF // tf
    rows = x.shape[0]
    off = (dm.M - rows) // tm
    return pl.pallas_call(
        functools.partial(_ffn_kernel, alpha),
        grid=(rows // tm, nf),
        in_specs=[
            pl.BlockSpec((tm, D), lambda i, f: (i, 0)),
            dm.mod_spec(m0, tm, off), dm.mod_spec(m0 + 1, tm, off), dm.mod_spec(m0 + 2, tm, off),
            pl.BlockSpec((None, None, D, tf), lambda i, f: (l, j, 0, f)),
            pl.BlockSpec((None, None, D, tf), lambda i, f: (l, j, 0, f + nf)),
            pl.BlockSpec((None, None, tf, D), lambda i, f: (l, j, f, 0)),
            pl.BlockSpec((1, D), lambda i, f: (0, 0)),
            pl.BlockSpec((1, D), lambda i, f: (0, 0)),
        ],
        out_specs=pl.BlockSpec((tm, D), lambda i, f: (i, 0)),
        out_shape=jax.ShapeDtypeStruct((rows, D), F32),
        scratch_shapes=[pltpu.VMEM((tm, D), BF16), pltpu.VMEM((tm, D), F32)],
        compiler_params=_params("parallel", "arbitrary"),
        name="ffn",
    )(x, mods, mods, mods, w_gu, w_gu, w_down, g.reshape(1, D), b.reshape(1, D))


def _inproj_kernel(x_ref, sh_ref, sc_ref, w_ref, o_ref, h_ref):
    @pl.when(pl.program_id(1) == 0)
    def _():
        h = _ln(x_ref[...]) * (1.0 + sc_ref[...]) + sh_ref[...]
        h_ref[...] = h.astype(BF16)

    o_ref[...] = _dot(h_ref[...], w_ref[...])


def _inproj(dm, x, mods, w_in_p):
    D, tm, tn = dm.D, dm.tm, dm.tn_in
    return pl.pallas_call(
        _inproj_kernel,
        grid=(dm.M // tm, dm.NP // tn),
        in_specs=[
            pl.BlockSpec((tm, D), lambda i, n: (i, 0)),
            dm.mod_spec(3, tm), dm.mod_spec(4, tm),
            pl.BlockSpec((D, tn), lambda i, n: (0, n)),
        ],
        out_specs=pl.BlockSpec((tm, tn), lambda i, n: (i, n)),
        out_shape=jax.ShapeDtypeStruct((dm.M, dm.NP), F32),
        scratch_shapes=[pltpu.VMEM((tm, D), BF16)],
        compiler_params=_params("parallel", "arbitrary"),
        name="inproj",
    )(x, mods, mods, w_in_p)


def _rope(x, cos, sin_signed):
    lane = lax.broadcasted_iota(jnp.int32, x.shape, 1)
    quarter = A_HEAD_DIM // 4
    swapped = jnp.where((lane % (2 * quarter)) < quarter,
                        pltpu.roll(x, A_HEAD_DIM - quarter, 1), pltpu.roll(x, quarter, 1))
    return x * cos + swapped * sin_signed


def _attn_kernel(S, nb, sink_ref, q_ref, kp_ref, kc_ref, kn_ref, vp_ref, vc_ref, vn_ref, kx_ref, vx_ref,
                 cos_ref, sin_ref, o_ref):
    n = pl.program_id(1)
    scale = A_HEAD_DIM ** -0.5

    def table(ref, blk):
        return ref[pl.ds(pl.multiple_of(blk * A_BLOCK, A_BLOCK), A_BLOCK), :]

    blk_p, blk_n = jnp.maximum(n - 1, 0), jnp.minimum(n + 1, nb - 1)
    cos_c, sin_c = table(cos_ref, n), table(sin_ref, n)
    cos_p, sin_p = table(cos_ref, blk_p), table(sin_ref, blk_p)
    cos_n, sin_n = table(cos_ref, blk_n), table(sin_ref, blk_n)

    rows, band = A_GROUP * A_BLOCK, 3 * A_BLOCK
    qpos = n * A_BLOCK + lax.broadcasted_iota(jnp.int32, (rows, band), 0) % A_BLOCK
    kpos = (n - 1) * A_BLOCK + lax.broadcasted_iota(jnp.int32, (rows, band), 1)
    valid = (jnp.abs(qpos - kpos) <= WINDOW) & (kpos >= 0) & (kpos < S)

    for kvh in range(A_KV_HEADS):
        ks = slice(kvh * A_HEAD_DIM, (kvh + 1) * A_HEAD_DIM)
        kb = jnp.concatenate([_rope(kp_ref[:, ks], cos_p, sin_p), _rope(kc_ref[:, ks], cos_c, sin_c),
                              _rope(kn_ref[:, ks], cos_n, sin_n)], axis=0).astype(BF16)
        vb = jnp.concatenate([vp_ref[:, ks], vc_ref[:, ks], vn_ref[:, ks]], axis=0).astype(BF16)
        kx = kx_ref[:, ks].astype(BF16)
        vx = vx_ref[:, ks].astype(BF16)
        heads = [kvh * A_GROUP + g for g in range(A_GROUP)]
        qs = [q_ref[:, h * A_HEAD_DIM:(h + 1) * A_HEAD_DIM] for h in heads]
        q_rope = jnp.concatenate([_rope(q, cos_c, sin_c) for q in qs], axis=0).astype(BF16)
        q_plain = jnp.concatenate(qs, axis=0).astype(BF16)
        sink = jnp.concatenate([jnp.full((A_BLOCK, 1), sink_ref[h], F32) for h in heads], axis=0)
        s_loc = jnp.where(valid, _dot(q_rope, kb, NT) * scale, MASK_VALUE)
        s_ctx = _dot(q_plain, kx, NT) * scale
        m = jnp.maximum(jnp.maximum(jnp.max(s_loc, -1, keepdims=True), jnp.max(s_ctx, -1, keepdims=True)), sink)
        p_loc = jnp.exp(s_loc - m)
        p_ctx = jnp.exp(s_ctx - m)
        den = jnp.sum(p_loc, -1, keepdims=True) + jnp.sum(p_ctx, -1, keepdims=True) + jnp.exp(sink - m)
        o = (_dot(p_loc.astype(BF16), vb) + _dot(p_ctx.astype(BF16), vx)) / den
        for g, h in enumerate(heads):
            o_ref[:, h * A_HEAD_DIM:(h + 1) * A_HEAD_DIM] = o[g * A_BLOCK:(g + 1) * A_BLOCK].astype(BF16)


def _attn_latent(dm, z, sink, cos, sin):
    B, S = dm.B, dm.S
    nb = S // A_BLOCK
    base = dm.MC // A_BLOCK
    qc = dm.c_att_q // A_Q_DIM
    kc = dm.c_att_k // A_KV_DIM
    vc = dm.c_att_v // A_KV_DIM

    def rows(shift):
        return lambda b, n: base + b * nb + jnp.clip(n + shift, 0, nb - 1)

    def kv_spec(col, shift):
        r = rows(shift)
        return pl.BlockSpec((A_BLOCK, A_KV_DIM), lambda b, n: (r(b, n), col))

    return pl.pallas_call(
        functools.partial(_attn_kernel, S, nb),
        grid=(B, nb),
        in_specs=[
            pl.BlockSpec(memory_space=pltpu.SMEM),
            pl.BlockSpec((A_BLOCK, A_Q_DIM), lambda b, n: (base + b * nb + n, qc)),
            kv_spec(kc, -1), kv_spec(kc, 0), kv_spec(kc, 1),
            kv_spec(vc, -1), kv_spec(vc, 0), kv_spec(vc, 1),
            pl.BlockSpec((dm.CT, A_KV_DIM), lambda b, n: (b, kc)),
            pl.BlockSpec((dm.CT, A_KV_DIM), lambda b, n: (b, vc)),
            pl.BlockSpec((S, A_HEAD_DIM), lambda b, n: (0, 0)),
            pl.BlockSpec((S, A_HEAD_DIM), lambda b, n: (0, 0)),
        ],
        out_specs=pl.BlockSpec((A_BLOCK, A_Q_DIM), lambda b, n: (b * nb + n, 0)),
        out_shape=jax.ShapeDtypeStruct((dm.ML, A_Q_DIM), BF16),
        compiler_params=_params("parallel", "arbitrary"),
        name="attn_latent",
    )(sink, z, z, z, z, z, z, z, z, z, cos, sin)


def _attn_ctx_kernel(CT, sink_ref, q_ref, k_ref, v_ref, o_ref):
    scale = A_HEAD_DIM ** -0.5
    for kvh in range(A_KV_HEADS):
        ks = slice(kvh * A_HEAD_DIM, (kvh + 1) * A_HEAD_DIM)
        k = k_ref[:, ks].astype(BF16)
        v = v_ref[:, ks].astype(BF16)
        heads = [kvh * A_GROUP + g for g in range(A_GROUP)]
        q = jnp.concatenate([q_ref[:, h * A_HEAD_DIM:(h + 1) * A_HEAD_DIM] for h in heads], axis=0).astype(BF16)
        sink = jnp.concatenate([jnp.full((CT, 1), sink_ref[h], F32) for h in heads], axis=0)
        s = _dot(q, k, NT) * scale
        m = jnp.maximum(jnp.max(s, -1, keepdims=True), sink)
        p = jnp.exp(s - m)
        den = jnp.sum(p, -1, keepdims=True) + jnp.exp(sink - m)
        o = _dot(p.astype(BF16), v) / den
        for g, h in enumerate(heads):
            o_ref[:, h * A_HEAD_DIM:(h + 1) * A_HEAD_DIM] = o[g * CT:(g + 1) * CT].astype(BF16)


def _attn_ctx(dm, z, sink):
    CT = dm.CT
    return pl.pallas_call(
        functools.partial(_attn_ctx_kernel, CT),
        grid=(dm.B,),
        in_specs=[
            pl.BlockSpec(memory_space=pltpu.SMEM),
            pl.BlockSpec((CT, A_Q_DIM), lambda b: (b, dm.c_att_q // A_Q_DIM)),
            pl.BlockSpec((CT, A_KV_DIM), lambda b: (b, dm.c_att_k // A_KV_DIM)),
            pl.BlockSpec((CT, A_KV_DIM), lambda b: (b, dm.c_att_v // A_KV_DIM)),
        ],
        out_specs=pl.BlockSpec((CT, A_Q_DIM), lambda b: (b, 0)),
        out_shape=jax.ShapeDtypeStruct((dm.MC, A_Q_DIM), BF16),
        compiler_params=_params("parallel"),
        name="attn_ctx",
    )(sink, z, z, z)


def _head_sum(x, e1_ref, e2_ref):
    def split_dot(a, ind):
        hi = a.astype(BF16)
        lo = (a - hi.astype(F32)).astype(BF16)
        return _dot(hi, ind) + _dot(lo, ind)

    return split_dot(split_dot(x, e1_ref[...]), e2_ref[...])


def _rwkv_prep_kernel(dm, z_ref, zp_ref, zn_ref, cw_ref, g2_ref, w0_ref, w2_ref, a0_ref, a2_ref, kk_ref, ka_ref,
                      rk_ref, e1_ref, e2_ref,
                      r_out, v_out, kk_out, lw_out, kd_out, b_out, bonus_out, g_out):
    i = pl.program_id(0)
    tm, R = dm.tmr, dm.R
    nct = dm.MC // tm
    start = jnp.where(i < nct, (i * tm) % dm.CT, ((i - nct) * tm) % dm.S)
    seqlen = jnp.where(i < nct, dm.CT, dm.S)
    has_prev = (start != 0).astype(F32)
    has_next = (start + tm != seqlen).astype(F32)

    z = z_ref[...]
    rowid = lax.broadcasted_iota(jnp.int32, (tm, 1), 0)
    z_prev = jnp.where(rowid == 0, zp_ref[7:8, :] * has_prev, pltpu.roll(z, 1, 0))
    z_next = jnp.where(rowid == tm - 1, zn_ref[0:1, :] * has_next, pltpu.roll(z, tm - 1, 0))
    zc = z_prev * cw_ref[0:1, :] + z * cw_ref[1:2, :] + z_next * cw_ref[2:3, :]

    r = zc[:, 0:R]
    k = zc[:, R:2 * R]
    v = zc[:, 2 * R:3 * R]
    c0 = 3 * R
    gc = zc[:, c0:c0 + GATE_LORA]
    c0 += GATE_LORA
    g_out[...] = _dot(_sigmoid(gc).astype(BF16), g2_ref[...])

    kkr = k * kk_ref[...]
    norm = jnp.sqrt(_head_sum(kkr * kkr, e1_ref, e2_ref))
    kk = kkr / jnp.maximum(norm, 1e-12)
    r_out[...] = r
    v_out[...] = v
    kk_out[...] = kk

    bonus = jnp.zeros_like(v)
    for d in range(2):
        wc = zc[:, c0 + d * LORA_PAD:c0 + (d + 1) * LORA_PAD]
        ac = zc[:, c0 + (2 + d) * LORA_PAD:c0 + (3 + d) * LORA_PAD]
        wl = w0_ref[d:d + 1, :] + _dot(jnp.tanh(wc).astype(BF16), w2_ref[d])
        w = _log_sigmoid(wl) - 0.5
        lw_out[d] = -jnp.exp(w)
        a = _sigmoid(a0_ref[d:d + 1, :] + _dot(ac.astype(BF16), a2_ref[d]))
        kd = k * (1.0 + (a - 1.0) * ka_ref[...])
        kd_out[d] = kd
        b_out[d] = kk * a
        bonus = bonus + _head_sum(r * kd * rk_ref[...], e1_ref, e2_ref) * v
    bonus_out[...] = bonus


def _rwkv_prep(dm, z, p):
    tm, R, M, W = dm.tmr, dm.R, dm.M, dm.rw_cols
    nblk8 = M // 8
    row = pl.BlockSpec((tm, R), lambda i: (i, 0))
    row2 = pl.BlockSpec((2, tm, R), lambda i: (0, i, 0))

    def full(a):
        return pl.BlockSpec(a.shape, lambda i: (0,) * a.ndim)

    consts = [p['conv'], p['g2'], p['w0'], p['w2'], p['a0'], p['a2'], p['k_k'], p['k_a'], p['r_k'], p['e1'], p['e2']]
    one = jax.ShapeDtypeStruct((M, R), F32)
    two = jax.ShapeDtypeStruct((2, M, R), F32)
    return pl.pallas_call(
        functools.partial(_rwkv_prep_kernel, dm),
        grid=(M // tm,),
        in_specs=[
            pl.BlockSpec((tm, W), lambda i: (i, 0)),
            pl.BlockSpec((8, W), lambda i: (jnp.maximum(i * (tm // 8) - 1, 0), 0)),
            pl.BlockSpec((8, W), lambda i: (jnp.minimum((i + 1) * (tm // 8), nblk8 - 1), 0)),
        ] + [full(a) for a in consts],
        out_specs=[row, row, row, row2, row2, row2, row, row],
        out_shape=[one, one, one, two, two, two, one, one],
        compiler_params=_params("parallel"),
        name="rwkv_prep",
    )(z, z, z, *consts)


R_SLAB = 4 * R_HEAD_DIM


def _block_diag(x, mask):
    return jnp.where(mask, jnp.concatenate([x] * 4, axis=0), 0.0).astype(BF16)


def _diag_blocks(full, mask):
    fm = jnp.where(mask, full, 0.0)
    n = R_HEAD_DIM
    return fm[0:n] + fm[n:2 * n] + fm[2 * n:3 * n] + fm[3 * n:4 * n]


def _slab_masks():
    n, g = R_HEAD_DIM, R_SLAB
    blk = (lax.broadcasted_iota(jnp.int32, (g, g), 0) // n) == (lax.broadcasted_iota(jnp.int32, (g, g), 1) // n)
    t_idx = lax.broadcasted_iota(jnp.int32, (n, g), 0)
    s_idx = lax.broadcasted_iota(jnp.int32, (n, g), 1) % n
    return blk, t_idx, s_idx


def _rwkv_chunk_kernel(n_sub, r_ref, v_ref, kk_ref, lw_ref, kd_ref, b_ref, g_out, h_out, q_out, y0_out):
    C = R_CHUNK
    blk, t_idx, s_idx = _slab_masks()
    eye4 = (t_idx == s_idx).astype(F32)
    ti = lax.broadcasted_iota(jnp.int32, (C, C), 0)
    si = lax.broadcasted_iota(jnp.int32, (C, C), 1)

    def bd(x):
        return _block_diag(x, blk)

    def setup(c, d):
        rs = pl.ds(c * C, C)
        r, v, kk = r_ref[rs, :], v_ref[rs, :], kk_ref[rs, :]
        tri = ((ti >= si) if d == 0 else (ti <= si)).astype(F32)
        lw = lw_ref[d, rs, :]
        cl = _dot(tri, lw, prec=HI)
        tot = jnp.sum(lw, axis=0, keepdims=True)
        e_neg = jnp.exp(-cl)
        e_end = jnp.exp(tot - cl)
        kd, b = kd_ref[d, rs, :], b_ref[d, rs, :]
        ch = dict(rs=rs, d=d, tot=tot, v_b=v.astype(BF16), v_bd=bd(v),
                  incl=(t_idx >= s_idx) if d == 0 else (t_idx <= s_idx),
                  strict=(t_idx > s_idx) if d == 0 else (t_idx < s_idx),
                  kkm=kk * jnp.exp(cl - lw), rp=r * jnp.exp(cl),
                  bp=b * e_neg, kp=kd * e_neg, be=(b * e_end).astype(BF16), ke=(kd * e_end).astype(BF16))
        ch['lhs'] = jnp.concatenate([ch['kkm'], ch['rp']], axis=0).astype(BF16)
        return ch

    chains = [setup(c, d) for c in range(n_sub) for d in range(2)]
    for ch in chains:
        gram_b = _dot(ch['lhs'], bd(ch.pop('bp')), NT)
        gram_k = _dot(ch.pop('lhs'), bd(ch.pop('kp')), NT)
        lb = jnp.where(ch['strict'], gram_b[:C], 0.0)
        ch['mb'] = jnp.where(ch['incl'], gram_b[C:], 0.0).astype(BF16)
        ch['lk'] = jnp.where(ch['strict'], gram_k[:C], 0.0).astype(BF16)
        ch['mk'] = jnp.where(ch['incl'], gram_k[C:], 0.0).astype(BF16)
        ch['pw'] = -lb
        ch['t_inv'] = eye4 - lb
    for ch in chains:
        ch['pw'] = _dot(ch['pw'].astype(BF16), bd(ch['pw']))
        ch['lkv'] = _dot(ch.pop('lk'), ch['v_bd'])
    for _ in range(int(math.log2(C)) - 2):
        for ch in chains:
            both = _dot(jnp.concatenate([ch['pw'], ch['t_inv']], axis=0).astype(BF16), bd(ch['pw']))
            ch['pw'], ch['t_inv'] = both[:C], ch['t_inv'] + both[C:]
    for ch in chains:
        ch['t_inv'] = (ch['t_inv'] + _dot(ch['t_inv'].astype(BF16), bd(ch.pop('pw')))).astype(BF16)
    for ch in chains:
        ch['a_m'] = _dot(ch['t_inv'], bd(ch.pop('kkm')))
        ch['d_m'] = -_dot(ch.pop('t_inv'), bd(ch.pop('lkv')))
    for ch in chains:
        d, rs = ch['d'], ch['rs']
        a_m, d_m = ch['a_m'], ch['d_m']
        q_out[d, rs, :] = (ch['rp'] - _dot(ch['mb'], bd(a_m))).astype(BF16)
        y0_out[d, rs, :] = _dot(ch['mb'], bd(d_m)) + _dot(ch['mk'], ch['v_bd'])
        g_cross = _dot(ch['be'], a_m.astype(BF16), TN)
        h_cross = _dot(ch['be'], d_m.astype(BF16), TN) + _dot(ch['ke'], ch['v_b'], TN)
        g_out[d, rs, :] = (eye4 * jnp.exp(ch['tot']) - _diag_blocks(g_cross, blk)).astype(BF16)
        h_out[d, rs, :] = _diag_blocks(h_cross, blk)


def _rwkv_chunk(dm, r, v, kk, lw, kd, b):
    M, R = dm.M, dm.R
    rows = dm.tmr
    one = pl.BlockSpec((rows, R_SLAB), lambda i, h: (i, h))
    two = pl.BlockSpec((2, rows, R_SLAB), lambda i, h: (0, i, h))
    f32 = jax.ShapeDtypeStruct((2, M, R), F32)
    b16 = jax.ShapeDtypeStruct((2, M, R), BF16)
    return pl.pallas_call(
        functools.partial(_rwkv_chunk_kernel, rows // R_CHUNK),
        grid=(M // rows, R // R_SLAB),
        in_specs=[one, one, one, two, two, two],
        out_specs=[two, two, two, two],
        out_shape=[b16, f32, b16, f32],
        compiler_params=_params("parallel", "parallel"),
        name="rwkv_chunk",
    )(r, v, kk, lw, kd, b)


def _rwkv_scan_kernel(n_slab, n_in, g_ref, h_ref, q_ref, y0_ref, y_ref, st_ref):
    C = R_CHUNK
    d = pl.program_id(1)

    @pl.when(pl.program_id(2) == 0)
    def _():
        st_ref[...] = jnp.zeros_like(st_ref)

    blk, _, _ = _slab_masks()
    for j in range(n_in):
        cj = jnp.where(d == 0, j, n_in - 1 - j)
        rs = pl.ds(pl.multiple_of(cj * C, C), C)
        for sl in range(n_slab):
            ls = slice(sl * R_SLAB, (sl + 1) * R_SLAB)
            st_bd = _block_diag(st_ref[:, ls], blk)
            both = _dot(jnp.concatenate([q_ref[rs, ls], g_ref[rs, ls]], axis=0), st_bd)
            y_ref[rs, ls] = both[:C] + y0_ref[rs, ls]
            st_ref[:, ls] = both[C:] + h_ref[rs, ls]


def _rwkv_scan(dm, g_m, h_m, q_m, y0_m):
    R, rows = dm.R, dm.tmr
    steps = (dm.CT + dm.S) // rows
    spec = pl.BlockSpec((None, rows, R), lambda b, d, s: (d, dm.chunk_block(b, d, s, rows), 0))
    return pl.pallas_call(
        functools.partial(_rwkv_scan_kernel, R // R_SLAB, rows // R_CHUNK),
        grid=(dm.B, 2, steps),
        in_specs=[spec, spec, spec, spec],
        out_specs=spec,
        out_shape=jax.ShapeDtypeStruct((2, dm.M, R), F32),
        scratch_shapes=[pltpu.VMEM((R_HEAD_DIM, R), F32)],
        compiler_params=_params("parallel", "parallel", "arbitrary"),
        name="rwkv_scan",
    )(g_m, h_m, q_m, y0_m)


def _rwkv_out_kernel(y_ref, bonus_ref, g_ref, lnw_ref, lnb_ref, e1_ref, e2_ref, o_ref):
    o = y_ref[0] + y_ref[1]
    inv_n = 1.0 / R_HEAD_DIM
    mu = _head_sum(o, e1_ref, e2_ref) * inv_n
    oc = o - mu
    var = _head_sum(oc * oc, e1_ref, e2_ref) * inv_n
    o = oc * lax.rsqrt(var + R_GN_EPS) * lnw_ref[...] + lnb_ref[...]
    o_ref[...] = ((o + bonus_ref[...]) * g_ref[...]).astype(BF16)


def _rwkv_out(dm, y, bonus, g, p):
    tm, R, M = dm.tmr, dm.R, dm.M
    row = pl.BlockSpec((tm, R), lambda i: (i, 0))

    def full(a):
        return pl.BlockSpec(a.shape, lambda i: (0,) * a.ndim)

    consts = [p['ln_w'], p['ln_b'], p['e1'], p['e2']]
    return pl.pallas_call(
        _rwkv_out_kernel,
        grid=(M // tm,),
        in_specs=[pl.BlockSpec((2, tm, R), lambda i: (0, i, 0)), row, row] + [full(a) for a in consts],
        out_specs=row,
        out_shape=jax.ShapeDtypeStruct((M, R), BF16),
        compiler_params=_params("parallel"),
        name="rwkv_out",
    )(y, bonus, g, *consts)


def _gla_kernel(dm, n_in, q_ref, k_ref, v_ref, ac_ref, wa_ref, ba_ref, y_ref, st_ref):
    C = G_CHUNK
    dk, dv = dm.GK // G_HEADS, dm.GV // G_HEADS
    d = pl.program_id(1)

    @pl.when(pl.program_id(2) == 0)
    def _():
        st_ref[...] = jnp.zeros_like(st_ref)

    ti = lax.broadcasted_iota(jnp.int32, (C, C), 0)
    si = lax.broadcasted_iota(jnp.int32, (C, C), 1)
    causal = jnp.where(d == 0, ti - si, si - ti) >= 0

    def body(j, carry):
        cj = jnp.where(d == 0, j, n_in - 1 - j)
        rs = pl.ds(pl.multiple_of(cj * C, C), C)
        la = _log_sigmoid(_dot(ac_ref[rs, :], wa_ref[...], prec=HI) + ba_ref[...]) * (1.0 / G_GATE_NORM)
        cb = _dot(causal.astype(F32), la, prec=HI)
        tot = jnp.sum(la, axis=0, keepdims=True)
        q_dec = (q_ref[rs, :] * (dk ** -0.5) * jnp.exp(cb)).astype(BF16)
        k = k_ref[rs, :]
        k_inv = (k * jnp.exp(-cb)).astype(BF16)
        k_end = (k * jnp.exp(tot - cb)).astype(BF16)
        e_last = jnp.exp(tot)
        for h in range(G_HEADS):
            ks = slice(h * dk, (h + 1) * dk)
            vs = slice(h * dv, (h + 1) * dv)
            v = v_ref[rs, vs].astype(BF16)
            att = jnp.where(causal, _dot(q_dec[:, ks], k_inv[:, ks], NT), 0.0)
            st = st_ref[h]
            y_ref[rs, vs] = _dot(att.astype(BF16), v) + _dot(q_dec[:, ks], st.astype(BF16), NT)
            st_ref[h] = st * e_last[:, ks] + _dot(v, k_end[:, ks], TN)
        return carry

    lax.fori_loop(0, n_in, body, 0)


def _gla(dm, z, wa_p, ba):
    rows = dm.tmr
    steps = (dm.CT + dm.S) // rows
    GK, GV = dm.GK, dm.GV

    def zspec(width, col):
        return pl.BlockSpec((rows, width), lambda b, d, s: (dm.chunk_block(b, d, s, rows), col // width))

    return pl.pallas_call(
        functools.partial(_gla_kernel, dm, rows // G_CHUNK),
        grid=(dm.B, 2, steps),
        in_specs=[
            zspec(GK, dm.c_gla_q), zspec(GK, dm.c_gla_k), zspec(GV, dm.c_gla_v), zspec(LORA_PAD, dm.c_gla_ac),
            pl.BlockSpec((None, LORA_PAD, GK), lambda b, d, s: (d, 0, 0)),
            pl.BlockSpec((None, 1, GK), lambda b, d, s: (d, 0, 0)),
        ],
        out_specs=pl.BlockSpec((None, rows, GV), lambda b, d, s: (d, dm.chunk_block(b, d, s, rows), 0)),
        out_shape=jax.ShapeDtypeStruct((2, dm.M, GV), F32),
        scratch_shapes=[pltpu.VMEM((G_HEADS, GV // G_HEADS, GK // G_HEADS), F32)],
        compiler_params=_params("parallel", "parallel", "arbitrary"),
        name="gla",
    )(z, z, z, z, wa_p, ba)


def _gla_out_kernel(dv, y_ref, g_ref, nw_ref, o_ref):
    o = y_ref[0] + y_ref[1]
    g = g_ref[...]
    gate = g * _sigmoid(g)
    for h in range(G_HEADS):
        vs = slice(h * dv, (h + 1) * dv)
        oh = o[:, vs]
        oh = oh * lax.rsqrt(jnp.mean(oh * oh, -1, keepdims=True) + G_EPS) * nw_ref[...]
        o_ref[:, vs] = (oh * gate[:, vs]).astype(BF16)


def _gla_out(dm, y, z, norm_w):
    tm, GV, M = dm.tmr, dm.GV, dm.M
    dv = GV // G_HEADS
    return pl.pallas_call(
        functools.partial(_gla_out_kernel, dv),
        grid=(M // tm,),
        in_specs=[
            pl.BlockSpec((2, tm, GV), lambda i: (0, i, 0)),
            pl.BlockSpec((tm, GV), lambda i: (i, dm.c_gla_g // GV)),
            pl.BlockSpec((1, dv), lambda i: (0, 0)),
        ],
        out_specs=pl.BlockSpec((tm, GV), lambda i: (i, 0)),
        out_shape=jax.ShapeDtypeStruct((M, GV), BF16),
        compiler_params=_params("parallel"),
        name="gla_out",
    )(y, z, norm_w.reshape(1, dv))


def _merge_kernel(alpha, x_ref, mg_ref, oa_ref, or_ref, og_ref, ga_ref, gr_ref, gg_ref, wa_ref, wr_ref, wg_ref,
                  wo_ref, g_ref, b_ref, o_ref, acc_ref):
    n = pl.program_id(1)

    @pl.when(n == 0)
    def _():
        acc_ref[...] = jnp.zeros_like(acc_ref)

    merged = (_sigmoid(ga_ref[...]) * _dot(oa_ref[...], wa_ref[...])
              + _sigmoid(gr_ref[...]) * _dot(or_ref[...], wr_ref[...])
              + _sigmoid(gg_ref[...]) * _dot(og_ref[...], wg_ref[...]))
    acc_ref[...] += _dot(merged.astype(BF16), wo_ref[...])

    @pl.when(n == pl.num_programs(1) - 1)
    def _():
        z = alpha * x_ref[...] + mg_ref[...] * acc_ref[...]
        o_ref[...] = _ln(z) * g_ref[...] + b_ref[...]


def _merge(dm, alpha, x, mods, z, o_att, o_rwkv, o_gla, wba, wbr, wbg, w_out, l, g, b, lat_only):
    D, tm, tn = dm.D, dm.tm, 512
    gate0 = dm.c_gate // tn
    per = D // tn
    off = dm.MC // tm if lat_only else 0
    rows = dm.M - off * tm

    def gate_spec(j):
        return pl.BlockSpec((tm, tn), lambda i, n: (i + off, gate0 + j * per + n))

    def resident(width):
        return pl.BlockSpec((tm, width), lambda i, n: (i + off, 0))

    def wcol(rows_w):
        return pl.BlockSpec((None, rows_w, tn), lambda i, n: (l, 0, n))

    return pl.pallas_call(
        functools.partial(_merge_kernel, alpha),
        grid=(rows // tm, per),
        in_specs=[
            resident(D), dm.mod_spec(5, tm, off),
            resident(A_Q_DIM), resident(dm.R), resident(dm.GV),
            gate_spec(0), gate_spec(1), gate_spec(2),
            wcol(A_Q_DIM), wcol(dm.R), wcol(dm.GV),
            pl.BlockSpec((None, tn, D), lambda i, n: (l, n, 0)),
            pl.BlockSpec((1, D), lambda i, n: (0, 0)),
            pl.BlockSpec((1, D), lambda i, n: (0, 0)),
        ],
        out_specs=pl.BlockSpec((tm, D), lambda i, n: (i, 0)),
        out_shape=jax.ShapeDtypeStruct((rows, D), F32),
        scratch_shapes=[pltpu.VMEM((tm, D), F32)],
        compiler_params=_params("parallel", "arbitrary"),
        name="merge",
    )(x, mods, o_att, o_rwkv, o_gla, z, z, z, wba, wbr, wbg, w_out, g.reshape(1, D), b.reshape(1, D))


def _pad_to(a, size, axis):
    pad = [(0, 0)] * a.ndim
    pad[axis] = (0, size - a.shape[axis])
    return jnp.pad(a, pad)


def _rwkv_cols(a, R):
    c = 3 * R
    rkv = a[..., :c]
    wc0, wc1 = a[..., c:c + DECAY_LORA], a[..., c + DECAY_LORA:c + 2 * DECAY_LORA]
    c += 2 * DECAY_LORA
    ac0, ac1 = a[..., c:c + ICLR_LORA], a[..., c + ICLR_LORA:c + 2 * ICLR_LORA]
    c += 2 * ICLR_LORA
    gc = a[..., c:c + GATE_LORA]
    ax = a.ndim - 1
    return jnp.concatenate([rkv, gc] + [_pad_to(t, LORA_PAD, ax) for t in (wc0, wc1, ac0, ac1)], axis=-1)


def _in_proj_weights(dm, w_in):
    D, R, GK, GV = dm.D, dm.R, dm.GK, dm.GV
    att_cols = A_Q_DIM + 2 * A_KV_DIM
    rw_src = 3 * R + 2 * DECAY_LORA + 2 * ICLR_LORA + GATE_LORA
    gla_src = 2 * GK + 2 * GV + 2 * G_GATE_RANK
    att, rwkv, gla, gate = jnp.split(w_in, np.cumsum([att_cols, rw_src, gla_src]).tolist(), axis=-1)
    a_q, a_k, a_v = att[:, :A_Q_DIM], att[:, A_Q_DIM:A_Q_DIM + A_KV_DIM], att[:, A_Q_DIM + A_KV_DIM:]
    g_q, g_k = gla[:, :GK], gla[:, GK:2 * GK]
    g_v, g_g = gla[:, 2 * GK:2 * GK + GV], gla[:, 2 * GK + GV:2 * GK + 2 * GV]
    g_ac = _pad_to(gla[:, 2 * GK + 2 * GV:], LORA_PAD, 1)
    w = jnp.concatenate([_rwkv_cols(rwkv, R), a_k, gate, g_v, g_g, g_q, g_k, a_q, a_v, g_ac], axis=-1)
    return _pad_to(w, dm.NP, 1).astype(BF16)


def _rope_tables(S):
    rows = S // GRID_W
    row = jnp.repeat(jnp.arange(rows, dtype=F32), GRID_W)
    col = jnp.tile(jnp.arange(GRID_W, dtype=F32), rows)
    n_freq = A_HEAD_DIM // 4
    inv_freq = ROPE_BASE ** (-jnp.arange(n_freq, dtype=F32) / n_freq)
    ang_r, ang_c = row[:, None] * inv_freq, col[:, None] * inv_freq
    cos = jnp.concatenate([jnp.cos(ang_r)] * 2 + [jnp.cos(ang_c)] * 2, axis=-1)
    sin = jnp.concatenate([-jnp.sin(ang_r), jnp.sin(ang_r), -jnp.sin(ang_c), jnp.sin(ang_c)], axis=-1)
    return cos, sin


def kernel(x, c, ctx, c_ctx, ada_w, ada_b, ln_g, ln_b, ffn_w_gu, ffn_w_down, w_in, attn_sink, rwkv_conv, rwkv_w0,
           rwkv_w2, rwkv_a0, rwkv_a2, rwkv_g2, rwkv_k_k, rwkv_k_a, rwkv_r_k, rwkv_ln_w, rwkv_ln_b, gla_wa2, gla_ba,
           gla_norm_w, w_branch_att, w_branch_rwkv, w_branch_gla, w_out):
    B, S, D = x.shape
    CT = ctx.shape[1]
    depth = ada_w.shape[0]
    dm = _Dims(B, S, CT, D)
    R = dm.R
    alpha = float((2 * depth) ** 0.25)
    assert B + 1 <= MOD_ROWS

    cc = _pad_to(jnp.concatenate([c, c_ctx[None, :]], axis=0), MOD_ROWS, 0)
    mods_all = _ada(cc, ada_w, ada_b)
    cos, sin = _rope_tables(S)
    head_of_lane = jnp.arange(R) // R_HEAD_DIM
    e1 = (head_of_lane[:, None] == jnp.arange(LORA_PAD)[None, :]).astype(BF16)
    e2 = e1.T

    w_gu, w_dn = ffn_w_gu.astype(BF16), ffn_w_down.astype(BF16)
    wba, wbr = w_branch_att.astype(BF16), w_branch_rwkv.astype(BF16)
    wbg, wo = w_branch_gla.astype(BF16), w_out.astype(BF16)

    xs = jnp.concatenate([ctx.reshape(dm.MC, D), x.reshape(dm.ML, D)], axis=0)
    for l in range(depth):
        last = l == depth - 1
        mods = mods_all[l].reshape(MOD_ROWS * N_MOD, 1, D)
        xs = _ffn(dm, alpha, xs, mods, 0, w_gu, w_dn, l, 0, ln_g[l, 0], ln_b[l, 0])

        z = _inproj(dm, xs, mods, _in_proj_weights(dm, w_in[l]))
        o_att = jnp.concatenate([_attn_ctx(dm, z, attn_sink[l]), _attn_latent(dm, z, attn_sink[l], cos, sin)], axis=0)

        rp = {
            'conv': _rwkv_cols(rwkv_conv[l], R),
            'g2': rwkv_g2[l].astype(BF16),
            'w0': rwkv_w0[l], 'a0': rwkv_a0[l],
            'w2': _pad_to(rwkv_w2[l], LORA_PAD, 1).astype(BF16),
            'a2': _pad_to(rwkv_a2[l], LORA_PAD, 1).astype(BF16),
            'k_k': rwkv_k_k[l].reshape(1, R), 'k_a': rwkv_k_a[l].reshape(1, R), 'r_k': rwkv_r_k[l].reshape(1, R),
            'ln_w': rwkv_ln_w[l].reshape(1, R), 'ln_b': rwkv_ln_b[l].reshape(1, R),
            'e1': e1, 'e2': e2,
        }
        r, v, kk, lw, kd, b, bonus, g = _rwkv_prep(dm, z, rp)
        g_m, h_m, q_m, y0_m = _rwkv_chunk(dm, r, v, kk, lw, kd, b)
        y_rwkv = _rwkv_scan(dm, g_m, h_m, q_m, y0_m)
        o_rwkv = _rwkv_out(dm, y_rwkv, bonus, g, rp)

        wa_p = jnp.stack([_pad_to(jnp.pad(gla_wa2[l, d], ((d * G_GATE_RANK, 0), (0, 0))), LORA_PAD, 0)
                          for d in range(2)])
        y_gla = _gla(dm, z, wa_p, gla_ba[l].reshape(2, 1, dm.GK))
        o_gla = _gla_out(dm, y_gla, z, gla_norm_w[l])

        xs = _merge(dm, alpha, xs, mods, z, o_att, o_rwkv, o_gla, wba, wbr, wbg, wo, l, ln_g[l, 1], ln_b[l, 1], last)
        xs = _ffn(dm, alpha, xs, mods, 6, w_gu, w_dn, l, 1, ln_g[l, 2], ln_b[l, 2])
    return xs.reshape(B, S, D)
```

```python
import functools
import math

import jax
import jax.numpy as jnp
import numpy as np
from jax import lax
from jax.experimental import pallas as pl
from jax.experimental.pallas import tpu as pltpu

F32 = jnp.float32
BF16 = jnp.bfloat16
HI = lax.Precision.HIGHEST

N_MOD = 9
D_FF = 5632
LN_EPS = 1e-5
FFN_RES = 0.5

A_HEADS = 8
A_KV_HEADS = 2
A_GROUP = A_HEADS // A_KV_HEADS
A_HEAD_DIM = 128
A_Q_DIM = A_HEADS * A_HEAD_DIM
A_KV_DIM = A_KV_HEADS * A_HEAD_DIM
WINDOW = 128
A_BLOCK = 128
GRID_W = 64
ROPE_BASE = 10000.0
MASK_VALUE = -1e30

R_HEAD_DIM = 64
DECAY_LORA = 96
ICLR_LORA = 96
GATE_LORA = 256
R_GN_EPS = 64e-5
R_CHUNK = 64
LORA_PAD = 128
HALO = 16

G_HEADS = 4
G_GATE_RANK = 16
G_GATE_NORM = 16.0
G_CHUNK = 64
G_EPS = 1e-5

MOD_ROWS = 16
VMEM_LIMIT = 58 * 1024 * 1024

NN = (((1,), (0,)), ((), ()))
NT = (((1,), (1,)), ((), ()))
TN = (((0,), (0,)), ((), ()))


def _dot(a, b, dims=NN, prec=None):
    return lax.dot_general(a, b, dims, precision=prec, preferred_element_type=F32)


def _cumsum_dot(tri, x):
    hi = x.astype(BF16)
    rest = x - hi.astype(F32)
    mid = rest.astype(BF16)
    lo = (rest - mid.astype(F32)).astype(BF16)
    return _dot(tri, hi) + _dot(tri, mid) + _dot(tri, lo)


def _sigmoid(x):
    return 1.0 / (1.0 + jnp.exp(-x))


def _log_sigmoid(x):
    return jnp.minimum(x, 0.0) - jnp.log(1.0 + jnp.exp(-jnp.abs(x)))


def _ln(x):
    mu = jnp.mean(x, -1, keepdims=True)
    xc = x - mu
    var = jnp.mean(xc * xc, -1, keepdims=True)
    return xc * lax.rsqrt(var + LN_EPS)


def _params(*sem):
    return pltpu.CompilerParams(dimension_semantics=sem, vmem_limit_bytes=VMEM_LIMIT)


class _Dims:
    def __init__(self, B, S, CT, D):
        self.B, self.S, self.CT, self.D = B, S, CT, D
        self.MC, self.ML = B * CT, B * S
        self.M = self.MC + self.ML
        self.R = D // 2
        self.GK = D // 2
        self.GV = D
        self.RH = self.R // R_HEAD_DIM
        self.rw_cols = 3 * self.R + GATE_LORA + 4 * LORA_PAD
        c = 0
        self.c_rwkv = c; c += self.rw_cols
        self.c_att_k = c; c += A_KV_DIM
        self.c_gate = c; c += 3 * D
        self.c_gla_v = c; c += self.GV
        self.c_gla_g = c; c += self.GV
        self.c_gla_q = c; c += self.GK
        self.c_gla_k = c; c += self.GK
        self.c_att_q = c; c += A_Q_DIM
        self.c_att_v = c; c += A_KV_DIM
        self.c_gla_ac = c; c += LORA_PAD
        self.tn_in = 1280
        self.NP = -(-c // self.tn_in) * self.tn_in
        self.tm = math.gcd(512, math.gcd(self.MC, S))
        self.tm_in = math.gcd(1024, math.gcd(self.MC, S))
        self.tmr = math.gcd(256, math.gcd(CT, S))

    def mod_row(self, i, tm):
        nct = self.MC // tm
        tpb = self.S // tm
        return jnp.where(i < nct, self.B, (i - nct) // tpb)

    def mod_spec(self, m, tm, off=0):
        return pl.BlockSpec((None, 1, self.D), lambda i, *_: (self.mod_row(i + off, tm) * N_MOD + m, 0, 0))

    def chunk_block(self, b, d, s, chunk):
        ncc, ncl = self.CT // chunk, self.S // chunk
        j = s - ncc
        ctx_blk = b * ncc + jnp.where(d == 0, s, ncc - 1 - s)
        lat_blk = self.B * ncc + b * ncl + jnp.where(d == 0, j, ncl - 1 - j)
        return jnp.where(s < ncc, ctx_blk, lat_blk)


def _ada_kernel(c_ref, w_ref, b_ref, o_ref):
    cc = c_ref[...]
    s = (cc * _sigmoid(cc)).astype(BF16)
    o_ref[...] = _dot(s, w_ref[...].astype(BF16)) + b_ref[...]


def _ada(cc, ada_w, ada_b):
    L, D, N = ada_w.shape
    tn = 1024
    return pl.pallas_call(
        _ada_kernel,
        grid=(L, N // tn),
        in_specs=[
            pl.BlockSpec((MOD_ROWS, D), lambda l, n: (0, 0)),
            pl.BlockSpec((None, D, tn), lambda l, n: (l, 0, n)),
            pl.BlockSpec((None, 1, tn), lambda l, n: (l, 0, n)),
        ],
        out_specs=pl.BlockSpec((None, MOD_ROWS, tn), lambda l, n: (l, 0, n)),
        out_shape=jax.ShapeDtypeStruct((L, MOD_ROWS, N), F32),
        compiler_params=_params("parallel", "arbitrary"),
        name="ada",
    )(cc, ada_w, ada_b.reshape(L, 1, N))


def _ffn_kernel(alpha, x_ref, sh_ref, sc_ref, gt_ref, wg_ref, wu_ref, wd_ref, g_ref, b_ref, o_ref,
                h_ref, acc_ref):
    f = pl.program_id(1)

    @pl.when(f == 0)
    def _():
        h = _ln(x_ref[...]) * (1.0 + sc_ref[...]) + sh_ref[...]
        h_ref[...] = h.astype(BF16)
        acc_ref[...] = jnp.zeros_like(acc_ref)

    h = h_ref[...]
    g = _dot(h, wg_ref[...])
    u = _dot(h, wu_ref[...])
    a = (g * _sigmoid(g) * u).astype(BF16)
    acc_ref[...] += _dot(a, wd_ref[...])

    @pl.when(f == pl.num_programs(1) - 1)
    def _():
        z = alpha * x_ref[...] + (FFN_RES * gt_ref[...]) * acc_ref[...]
        o_ref[...] = _ln(z) * g_ref[...] + b_ref[...]


def _ffn(dm, alpha, x, mods, m0, w_gu, w_down, l, j, g, b):
    D, tm, tf = dm.D, dm.tm, 512
    nf = D_FF // tf
    rows = x.shape[0]
    off = (dm.M - rows) // tm
    return pl.pallas_call(
        functools.partial(_ffn_kernel, alpha),
        grid=(rows // tm, nf),
        in_specs=[
            pl.BlockSpec((tm, D), lambda i, f: (i, 0)),
            dm.mod_spec(m0, tm, off), dm.mod_spec(m0 + 1, tm, off), dm.mod_spec(m0 + 2, tm, off),
            pl.BlockSpec((None, None, D, tf), lambda i, f: (l, j, 0, f)),
            pl.BlockSpec((None, None, D, tf), lambda i, f: (l, j, 0, f + nf)),
            pl.BlockSpec((None, None, tf, D), lambda i, f: (l, j, f, 0)),
            pl.BlockSpec((1, D), lambda i, f: (0, 0)),
            pl.BlockSpec((1, D), lambda i, f: (0, 0)),
        ],
        out_specs=pl.BlockSpec((tm, D), lambda i, f: (i, 0)),
        out_shape=jax.ShapeDtypeStruct((rows, D), F32),
        scratch_shapes=[pltpu.VMEM((tm, D), BF16), pltpu.VMEM((tm, D), F32)],
        compiler_params=_params("parallel", "arbitrary"),
        name="ffn",
    )(x, mods, mods, mods, w_gu, w_gu, w_down, g.reshape(1, D), b.reshape(1, D))


def _inproj_kernel(x_ref, sh_ref, sc_ref, w_ref, o_ref, h_ref):
    @pl.when(pl.program_id(1) == 0)
    def _():
        h = _ln(x_ref[...]) * (1.0 + sc_ref[...]) + sh_ref[...]
        h_ref[...] = h.astype(BF16)

    o_ref[...] = _dot(h_ref[...], w_ref[...]).astype(BF16)


def _inproj(dm, x, mods, w_in_p):
    D, tm, tn = dm.D, dm.tm_in, dm.tn_in
    return pl.pallas_call(
        _inproj_kernel,
        grid=(dm.M // tm, dm.NP // tn),
        in_specs=[
            pl.BlockSpec((tm, D), lambda i, n: (i, 0)),
            dm.mod_spec(3, tm), dm.mod_spec(4, tm),
            pl.BlockSpec((D, tn), lambda i, n: (0, n)),
        ],
        out_specs=pl.BlockSpec((tm, tn), lambda i, n: (i, n)),
        out_shape=jax.ShapeDtypeStruct((dm.M, dm.NP), BF16),
        scratch_shapes=[pltpu.VMEM((tm, D), BF16)],
        compiler_params=_params("parallel", "arbitrary"),
        name="inproj",
    )(x, mods, mods, w_in_p)


def _rope(x, cos, sin_signed):
    lane = lax.broadcasted_iota(jnp.int32, x.shape, 1)
    quarter = A_HEAD_DIM // 4
    swapped = jnp.where((lane % (2 * quarter)) < quarter,
                        pltpu.roll(x, A_HEAD_DIM - quarter, 1), pltpu.roll(x, quarter, 1))
    return x * cos + swapped * sin_signed


def _attn_kernel(S, nb, sink_ref, q_ref, kp_ref, kc_ref, kn_ref, vp_ref, vc_ref, vn_ref, kx_ref, vx_ref,
                 cos_ref, sin_ref, o_ref):
    n = pl.program_id(1)
    scale = A_HEAD_DIM ** -0.5

    def table(ref, blk):
        return ref[pl.ds(pl.multiple_of(blk * A_BLOCK, A_BLOCK), A_BLOCK), :]

    blk_p, blk_n = jnp.maximum(n - 1, 0), jnp.minimum(n + 1, nb - 1)
    cos_c, sin_c = table(cos_ref, n), table(sin_ref, n)
    cos_p, sin_p = table(cos_ref, blk_p), table(sin_ref, blk_p)
    cos_n, sin_n = table(cos_ref, blk_n), table(sin_ref, blk_n)

    rows, band = A_GROUP * A_BLOCK, 3 * A_BLOCK
    qpos = n * A_BLOCK + lax.broadcasted_iota(jnp.int32, (rows, band), 0) % A_BLOCK
    kpos = (n - 1) * A_BLOCK + lax.broadcasted_iota(jnp.int32, (rows, band), 1)
    valid = (jnp.abs(qpos - kpos) <= WINDOW) & (kpos >= 0) & (kpos < S)

    for kvh in range(A_KV_HEADS):
        ks = slice(kvh * A_HEAD_DIM, (kvh + 1) * A_HEAD_DIM)
        kb = jnp.concatenate([_rope(kp_ref[:, ks].astype(F32), cos_p, sin_p),
                              _rope(kc_ref[:, ks].astype(F32), cos_c, sin_c),
                              _rope(kn_ref[:, ks].astype(F32), cos_n, sin_n)], axis=0).astype(BF16)
        vb = jnp.concatenate([vp_ref[:, ks], vc_ref[:, ks], vn_ref[:, ks]], axis=0)
        kx = kx_ref[:, ks]
        vx = vx_ref[:, ks]
        heads = [kvh * A_GROUP + g for g in range(A_GROUP)]
        qs = [q_ref[:, h * A_HEAD_DIM:(h + 1) * A_HEAD_DIM] for h in heads]
        q_rope = jnp.concatenate([_rope(q.astype(F32), cos_c, sin_c) for q in qs], axis=0).astype(BF16)
        q_plain = jnp.concatenate(qs, axis=0)
        sink = jnp.concatenate([jnp.full((A_BLOCK, 1), sink_ref[h], F32) for h in heads], axis=0)
        s_loc = jnp.where(valid, _dot(q_rope, kb, NT) * scale, MASK_VALUE)
        s_ctx = _dot(q_plain, kx, NT) * scale
        m = jnp.maximum(jnp.maximum(jnp.max(s_loc, -1, keepdims=True), jnp.max(s_ctx, -1, keepdims=True)), sink)
        p_loc = jnp.exp(s_loc - m)
        p_ctx = jnp.exp(s_ctx - m)
        den = jnp.sum(p_loc, -1, keepdims=True) + jnp.sum(p_ctx, -1, keepdims=True) + jnp.exp(sink - m)
        o = (_dot(p_loc.astype(BF16), vb) + _dot(p_ctx.astype(BF16), vx)) / den
        for g, h in enumerate(heads):
            o_ref[:, h * A_HEAD_DIM:(h + 1) * A_HEAD_DIM] = o[g * A_BLOCK:(g + 1) * A_BLOCK].astype(BF16)


def _attn_latent(dm, z, sink, cos, sin):
    B, S = dm.B, dm.S
    nb = S // A_BLOCK
    base = dm.MC // A_BLOCK
    qc = dm.c_att_q // A_Q_DIM
    kc = dm.c_att_k // A_KV_DIM
    vc = dm.c_att_v // A_KV_DIM

    def rows(shift):
        return lambda b, n: base + b * nb + jnp.clip(n + shift, 0, nb - 1)

    def kv_spec(col, shift):
        r = rows(shift)
        return pl.BlockSpec((A_BLOCK, A_KV_DIM), lambda b, n: (r(b, n), col))

    return pl.pallas_call(
        functools.partial(_attn_kernel, S, nb),
        grid=(B, nb),
        in_specs=[
            pl.BlockSpec(memory_space=pltpu.SMEM),
            pl.BlockSpec((A_BLOCK, A_Q_DIM), lambda b, n: (base + b * nb + n, qc)),
            kv_spec(kc, -1), kv_spec(kc, 0), kv_spec(kc, 1),
            kv_spec(vc, -1), kv_spec(vc, 0), kv_spec(vc, 1),
            pl.BlockSpec((dm.CT, A_KV_DIM), lambda b, n: (b, kc)),
            pl.BlockSpec((dm.CT, A_KV_DIM), lambda b, n: (b, vc)),
            pl.BlockSpec((S, A_HEAD_DIM), lambda b, n: (0, 0)),
            pl.BlockSpec((S, A_HEAD_DIM), lambda b, n: (0, 0)),
        ],
        out_specs=pl.BlockSpec((A_BLOCK, A_Q_DIM), lambda b, n: (b * nb + n, 0)),
        out_shape=jax.ShapeDtypeStruct((dm.ML, A_Q_DIM), BF16),
        compiler_params=_params("parallel", "arbitrary"),
        name="attn_latent",
    )(sink, z, z, z, z, z, z, z, z, z, cos, sin)


def _attn_ctx_kernel(CT, sink_ref, q_ref, k_ref, v_ref, o_ref):
    scale = A_HEAD_DIM ** -0.5
    for kvh in range(A_KV_HEADS):
        ks = slice(kvh * A_HEAD_DIM, (kvh + 1) * A_HEAD_DIM)
        k = k_ref[:, ks]
        v = v_ref[:, ks]
        heads = [kvh * A_GROUP + g for g in range(A_GROUP)]
        q = jnp.concatenate([q_ref[:, h * A_HEAD_DIM:(h + 1) * A_HEAD_DIM] for h in heads], axis=0)
        sink = jnp.concatenate([jnp.full((CT, 1), sink_ref[h], F32) for h in heads], axis=0)
        s = _dot(q, k, NT) * scale
        m = jnp.maximum(jnp.max(s, -1, keepdims=True), sink)
        p = jnp.exp(s - m)
        den = jnp.sum(p, -1, keepdims=True) + jnp.exp(sink - m)
        o = _dot(p.astype(BF16), v) / den
        for g, h in enumerate(heads):
            o_ref[:, h * A_HEAD_DIM:(h + 1) * A_HEAD_DIM] = o[g * CT:(g + 1) * CT].astype(BF16)


def _attn_ctx(dm, z, sink):
    CT = dm.CT
    return pl.pallas_call(
        functools.partial(_attn_ctx_kernel, CT),
        grid=(dm.B,),
        in_specs=[
            pl.BlockSpec(memory_space=pltpu.SMEM),
            pl.BlockSpec((CT, A_Q_DIM), lambda b: (b, dm.c_att_q // A_Q_DIM)),
            pl.BlockSpec((CT, A_KV_DIM), lambda b: (b, dm.c_att_k // A_KV_DIM)),
            pl.BlockSpec((CT, A_KV_DIM), lambda b: (b, dm.c_att_v // A_KV_DIM)),
        ],
        out_specs=pl.BlockSpec((CT, A_Q_DIM), lambda b: (b, 0)),
        out_shape=jax.ShapeDtypeStruct((dm.MC, A_Q_DIM), BF16),
        compiler_params=_params("parallel"),
        name="attn_ctx",
    )(sink, z, z, z)


def _head_sum(x, e1_ref, e2_ref):
    def split_dot(a, ind):
        hi = a.astype(BF16)
        lo = (a - hi.astype(F32)).astype(BF16)
        return _dot(hi, ind) + _dot(lo, ind)

    return split_dot(split_dot(x, e1_ref[...]), e2_ref[...])


def _rwkv_prep_kernel(dm, z_ref, zp_ref, zn_ref, cw_ref, g2_ref, w0_ref, w2_ref, a0_ref, a2_ref, kk_ref, ka_ref,
                      rk_ref, e1_ref, e2_ref,
                      r_out, v_out, kk_out, lw_out, kd_out, b_out, bonus_out, g_out):
    i = pl.program_id(0)
    tm, R = dm.tmr, dm.R
    nct = dm.MC // tm
    start = jnp.where(i < nct, (i * tm) % dm.CT, ((i - nct) * tm) % dm.S)
    seqlen = jnp.where(i < nct, dm.CT, dm.S)
    has_prev = (start != 0).astype(F32)
    has_next = (start + tm != seqlen).astype(F32)

    z = z_ref[...].astype(F32)
    rowid = lax.broadcasted_iota(jnp.int32, (tm, 1), 0)
    halo_prev = zp_ref[HALO - 1:HALO, :].astype(F32) * has_prev
    halo_next = zn_ref[0:1, :].astype(F32) * has_next
    z_prev = jnp.where(rowid == 0, halo_prev, pltpu.roll(z, 1, 0))
    z_next = jnp.where(rowid == tm - 1, halo_next, pltpu.roll(z, tm - 1, 0))
    zc = z_prev * cw_ref[0:1, :] + z * cw_ref[1:2, :] + z_next * cw_ref[2:3, :]

    r = zc[:, 0:R]
    k = zc[:, R:2 * R]
    v = zc[:, 2 * R:3 * R]
    c0 = 3 * R
    gc = zc[:, c0:c0 + GATE_LORA]
    c0 += GATE_LORA
    g_out[...] = _dot(_sigmoid(gc).astype(BF16), g2_ref[...])

    kkr = k * kk_ref[...]
    norm = jnp.sqrt(_head_sum(kkr * kkr, e1_ref, e2_ref))
    kk = kkr / jnp.maximum(norm, 1e-12)
    r_out[...] = r
    v_out[...] = v
    kk_out[...] = kk

    bonus = jnp.zeros_like(v)
    for d in range(2):
        wc = zc[:, c0 + d * LORA_PAD:c0 + (d + 1) * LORA_PAD]
        ac = zc[:, c0 + (2 + d) * LORA_PAD:c0 + (3 + d) * LORA_PAD]
        wl = w0_ref[d:d + 1, :] + _dot(jnp.tanh(wc).astype(BF16), w2_ref[d])
        w = _log_sigmoid(wl) - 0.5
        lw_out[d] = -jnp.exp(w)
        a = _sigmoid(a0_ref[d:d + 1, :] + _dot(ac.astype(BF16), a2_ref[d]))
        kd = k * (1.0 + (a - 1.0) * ka_ref[...])
        kd_out[d] = kd
        b_out[d] = kk * a
        bonus = bonus + _head_sum(r * kd * rk_ref[...], e1_ref, e2_ref) * v
    bonus_out[...] = bonus


def _rwkv_prep(dm, z, p):
    tm, R, M, W = dm.tmr, dm.R, dm.M, dm.rw_cols
    n_halo = M // HALO
    row = pl.BlockSpec((tm, R), lambda i: (i, 0))
    row2 = pl.BlockSpec((2, tm, R), lambda i: (0, i, 0))

    def full(a):
        return pl.BlockSpec(a.shape, lambda i: (0,) * a.ndim)

    consts = [p['conv'], p['g2'], p['w0'], p['w2'], p['a0'], p['a2'], p['k_k'], p['k_a'], p['r_k'], p['e1'], p['e2']]
    one = jax.ShapeDtypeStruct((M, R), F32)
    two = jax.ShapeDtypeStruct((2, M, R), F32)
    return pl.pallas_call(
        functools.partial(_rwkv_prep_kernel, dm),
        grid=(M // tm,),
        in_specs=[
            pl.BlockSpec((tm, W), lambda i: (i, 0)),
            pl.BlockSpec((HALO, W), lambda i: (jnp.maximum(i * (tm // HALO) - 1, 0), 0)),
            pl.BlockSpec((HALO, W), lambda i: (jnp.minimum((i + 1) * (tm // HALO), n_halo - 1), 0)),
        ] + [full(a) for a in consts],
        out_specs=[row, row, row, row2, row2, row2, row, row],
        out_shape=[one, one, one, two, two, two, one, one],
        compiler_params=_params("parallel"),
        name="rwkv_prep",
    )(z, z, z, *consts)


R_SLAB = 4 * R_HEAD_DIM


def _block_diag(x, mask):
    return jnp.where(mask, jnp.concatenate([x] * 4, axis=0), 0.0).astype(BF16)


def _diag_blocks(full, mask):
    fm = jnp.where(mask, full, 0.0)
    n = R_HEAD_DIM
    return fm[0:n] + fm[n:2 * n] + fm[2 * n:3 * n] + fm[3 * n:4 * n]


def _slab_masks():
    n, g = R_HEAD_DIM, R_SLAB
    blk = (lax.broadcasted_iota(jnp.int32, (g, g), 0) // n) == (lax.broadcasted_iota(jnp.int32, (g, g), 1) // n)
    t_idx = lax.broadcasted_iota(jnp.int32, (n, g), 0)
    s_idx = lax.broadcasted_iota(jnp.int32, (n, g), 1) % n
    return blk, t_idx, s_idx


def _rwkv_chunk_kernel(n_sub, r_ref, v_ref, kk_ref, lw_ref, kd_ref, b_ref, g_out, h_out, q_out, y0_out):
    C = R_CHUNK
    blk, t_idx, s_idx = _slab_masks()
    eye4 = (t_idx == s_idx).astype(F32)
    ti = lax.broadcasted_iota(jnp.int32, (C, C), 0)
    si = lax.broadcasted_iota(jnp.int32, (C, C), 1)

    def bd(x):
        return _block_diag(x, blk)

    def setup(c, d):
        rs = pl.ds(c * C, C)
        r, v, kk = r_ref[rs, :], v_ref[rs, :], kk_ref[rs, :]
        tri = ((ti >= si) if d == 0 else (ti <= si)).astype(BF16)
        lw = lw_ref[d, rs, :]
        cl = _cumsum_dot(tri, lw)
        tot = jnp.sum(lw, axis=0, keepdims=True)
        e_neg = jnp.exp(-cl)
        e_end = jnp.exp(tot - cl)
        kd, b = kd_ref[d, rs, :], b_ref[d, rs, :]
        ch = dict(rs=rs, d=d, tot=tot, v_b=v.astype(BF16), v_bd=bd(v),
                  incl=(t_idx >= s_idx) if d == 0 else (t_idx <= s_idx),
                  strict=(t_idx > s_idx) if d == 0 else (t_idx < s_idx),
                  kkm=kk * jnp.exp(cl - lw), rp=r * jnp.exp(cl),
                  bp=b * e_neg, kp=kd * e_neg, be=(b * e_end).astype(BF16), ke=(kd * e_end).astype(BF16))
        ch['lhs'] = jnp.concatenate([ch['kkm'], ch['rp']], axis=0).astype(BF16)
        return ch

    chains = [setup(c, d) for c in range(n_sub) for d in range(2)]
    for ch in chains:
        gram_b = _dot(ch['lhs'], bd(ch.pop('bp')), NT)
        gram_k = _dot(ch.pop('lhs'), bd(ch.pop('kp')), NT)
        lb = jnp.where(ch['strict'], gram_b[:C], 0.0)
        ch['mb'] = jnp.where(ch['incl'], gram_b[C:], 0.0).astype(BF16)
        ch['lk'] = jnp.where(ch['strict'], gram_k[:C], 0.0).astype(BF16)
        ch['mk'] = jnp.where(ch['incl'], gram_k[C:], 0.0).astype(BF16)
        ch['pw'] = -lb
        ch['t_inv'] = eye4 - lb
    for ch in chains:
        ch['pw'] = _dot(ch['pw'].astype(BF16), bd(ch['pw']))
        ch['lkv'] = _dot(ch.pop('lk'), ch['v_bd'])
    for _ in range(int(math.log2(C)) - 2):
        for ch in chains:
            both = _dot(jnp.concatenate([ch['pw'], ch['t_inv']], axis=0).astype(BF16), bd(ch['pw']))
            ch['pw'], ch['t_inv'] = both[:C], ch['t_inv'] + both[C:]
    for ch in chains:
        ch['t_inv'] = (ch['t_inv'] + _dot(ch['t_inv'].astype(BF16), bd(ch.pop('pw')))).astype(BF16)
    for ch in chains:
        ch['a_m'] = _dot(ch['t_inv'], bd(ch.pop('kkm')))
        ch['d_m'] = -_dot(ch.pop('t_inv'), bd(ch.pop('lkv')))
    for ch in chains:
        d, rs = ch['d'], ch['rs']
        a_m, d_m = ch['a_m'], ch['d_m']
        q_out[d, rs, :] = (ch['rp'] - _dot(ch['mb'], bd(a_m))).astype(BF16)
        y0_out[d, rs, :] = _dot(ch['mb'], bd(d_m)) + _dot(ch['mk'], ch['v_bd'])
        g_cross = _dot(ch['be'], a_m.astype(BF16), TN)
        h_cross = _dot(jnp.concatenate([ch['be'], ch['ke']], axis=0),
                       jnp.concatenate([d_m.astype(BF16), ch['v_b']], axis=0), TN)
        g_out[d, rs, :] = (eye4 * jnp.exp(ch['tot']) - _diag_blocks(g_cross, blk)).astype(BF16)
        h_out[d, rs, :] = _diag_blocks(h_cross, blk)


def _rwkv_chunk(dm, r, v, kk, lw, kd, b):
    M, R = dm.M, dm.R
    rows = dm.tmr
    one = pl.BlockSpec((rows, R_SLAB), lambda i, h: (i, h))
    two = pl.BlockSpec((2, rows, R_SLAB), lambda i, h: (0, i, h))
    f32 = jax.ShapeDtypeStruct((2, M, R), F32)
    b16 = jax.ShapeDtypeStruct((2, M, R), BF16)
    return pl.pallas_call(
        functools.partial(_rwkv_chunk_kernel, rows // R_CHUNK),
        grid=(M // rows, R // R_SLAB),
        in_specs=[one, one, one, two, two, two],
        out_specs=[two, two, two, two],
        out_shape=[b16, f32, b16, f32],
        compiler_params=_params("parallel", "parallel"),
        name="rwkv_chunk",
    )(r, v, kk, lw, kd, b)


def _rwkv_scan_kernel(n_slab, n_in, g_ref, h_ref, q_ref, y0_ref, y_ref, st_ref):
    C = R_CHUNK
    d = pl.program_id(1)

    @pl.when(pl.program_id(2) == 0)
    def _():
        st_ref[...] = jnp.zeros_like(st_ref)

    blk, _, _ = _slab_masks()
    for j in range(n_in):
        cj = jnp.where(d == 0, j, n_in - 1 - j)
        rs = pl.ds(pl.multiple_of(cj * C, C), C)
        for sl in range(n_slab):
            ls = slice(sl * R_SLAB, (sl + 1) * R_SLAB)
            st_bd = _block_diag(st_ref[:, ls], blk)
            both = _dot(jnp.concatenate([q_ref[rs, ls], g_ref[rs, ls]], axis=0), st_bd)
            y_ref[rs, ls] = both[:C] + y0_ref[rs, ls]
            st_ref[:, ls] = both[C:] + h_ref[rs, ls]


def _rwkv_scan(dm, g_m, h_m, q_m, y0_m):
    R, rows = dm.R, dm.tmr
    steps = (dm.CT + dm.S) // rows
    spec = pl.BlockSpec((None, rows, R), lambda b, d, s: (d, dm.chunk_block(b, d, s, rows), 0))
    return pl.pallas_call(
        functools.partial(_rwkv_scan_kernel, R // R_SLAB, rows // R_CHUNK),
        grid=(dm.B, 2, steps),
        in_specs=[spec, spec, spec, spec],
        out_specs=spec,
        out_shape=jax.ShapeDtypeStruct((2, dm.M, R), F32),
        scratch_shapes=[pltpu.VMEM((R_HEAD_DIM, R), F32)],
        compiler_params=_params("parallel", "parallel", "arbitrary"),
        name="rwkv_scan",
    )(g_m, h_m, q_m, y0_m)


def _rwkv_out_kernel(y_ref, bonus_ref, g_ref, lnw_ref, lnb_ref, e1_ref, e2_ref, o_ref):
    o = y_ref[0] + y_ref[1]
    inv_n = 1.0 / R_HEAD_DIM
    mu = _head_sum(o, e1_ref, e2_ref) * inv_n
    oc = o - mu
    var = _head_sum(oc * oc, e1_ref, e2_ref) * inv_n
    o = oc * lax.rsqrt(var + R_GN_EPS) * lnw_ref[...] + lnb_ref[...]
    o_ref[...] = ((o + bonus_ref[...]) * g_ref[...]).astype(BF16)


def _rwkv_out(dm, y, bonus, g, p):
    tm, R, M = dm.tmr, dm.R, dm.M
    row = pl.BlockSpec((tm, R), lambda i: (i, 0))

    def full(a):
        return pl.BlockSpec(a.shape, lambda i: (0,) * a.ndim)

    consts = [p['ln_w'], p['ln_b'], p['e1'], p['e2']]
    return pl.pallas_call(
        _rwkv_out_kernel,
        grid=(M // tm,),
        in_specs=[pl.BlockSpec((2, tm, R), lambda i: (0, i, 0)), row, row] + [full(a) for a in consts],
        out_specs=row,
        out_shape=jax.ShapeDtypeStruct((M, R), BF16),
        compiler_params=_params("parallel"),
        name="rwkv_out",
    )(y, bonus, g, *consts)


def _gla_kernel(dm, n_in, q_ref, k_ref, v_ref, ac_ref, wa_ref, ba_ref, y_ref, st_ref, qd_ref, ke_ref, el_ref):
    C = G_CHUNK
    dk, dv = dm.GK // G_HEADS, dm.GV // G_HEADS
    d = pl.program_id(1)

    @pl.when(pl.program_id(2) == 0)
    def _():
        st_ref[...] = jnp.zeros_like(st_ref)

    ti = lax.broadcasted_iota(jnp.int32, (C, C), 0)
    si = lax.broadcasted_iota(jnp.int32, (C, C), 1)
    causal = jnp.where(d == 0, ti - si, si - ti) >= 0
    tri = causal.astype(BF16)

    ac = ac_ref[...]
    la = _log_sigmoid(_dot(ac, wa_ref[0]) + _dot(ac, wa_ref[1]) + ba_ref[...]) * (1.0 / G_GATE_NORM)
    cbs, tots = [], []
    for c in range(n_in):
        la_c = la[c * C:(c + 1) * C]
        cbs.append(_cumsum_dot(tri, la_c))
        tots.append(jnp.broadcast_to(jnp.sum(la_c, axis=0, keepdims=True), la_c.shape))
    cb = jnp.concatenate(cbs, axis=0)
    tot = jnp.concatenate(tots, axis=0)
    q_dec = (q_ref[...].astype(F32) * (dk ** -0.5) * jnp.exp(cb)).astype(BF16)
    k = k_ref[...].astype(F32)
    k_inv = (k * jnp.exp(-cb)).astype(BF16)
    k_end = (k * jnp.exp(tot - cb)).astype(BF16)
    e_last = jnp.exp(tot)
    for c in range(n_in):
        rs = slice(c * C, (c + 1) * C)
        qd_ref[c] = q_dec[rs]
        ke_ref[c] = k_end[rs]
        el_ref[c] = e_last[c * C:c * C + 1]
    for c in range(n_in):
        rs = slice(c * C, (c + 1) * C)
        for h in range(G_HEADS):
            ks = slice(h * dk, (h + 1) * dk)
            vs = slice(h * dv, (h + 1) * dv)
            att = jnp.where(causal, _dot(q_dec[rs, ks], k_inv[rs, ks], NT), 0.0)
            y_ref[rs, vs] = _dot(att.astype(BF16), v_ref[rs, vs])

    def body(j, carry):
        cj = jnp.where(d == 0, j, n_in - 1 - j)
        rs = pl.ds(pl.multiple_of(cj * C, C), C)
        q_dec, k_end, e_last = qd_ref[cj], ke_ref[cj], el_ref[cj]
        for h in range(G_HEADS):
            ks = slice(h * dk, (h + 1) * dk)
            vs = slice(h * dv, (h + 1) * dv)
            st = st_ref[h]
            y_ref[rs, vs] += _dot(q_dec[:, ks], st.astype(BF16), NT)
            st_ref[h] = st * e_last[:, ks] + _dot(v_ref[rs, vs], k_end[:, ks], TN)
        return carry

    lax.fori_loop(0, n_in, body, 0)


def _gla(dm, z, wa_p, ba):
    rows = dm.tmr
    steps = (dm.CT + dm.S) // rows
    GK, GV = dm.GK, dm.GV

    def zspec(width, col):
        return pl.BlockSpec((rows, width), lambda b, d, s: (dm.chunk_block(b, d, s, rows), col // width))

    return pl.pallas_call(
        functools.partial(_gla_kernel, dm, rows // G_CHUNK),
        grid=(dm.B, 2, steps),
        in_specs=[
            zspec(GK, dm.c_gla_q), zspec(GK, dm.c_gla_k), zspec(GV, dm.c_gla_v), zspec(LORA_PAD, dm.c_gla_ac),
            pl.BlockSpec((None, 2, LORA_PAD, GK), lambda b, d, s: (d, 0, 0, 0)),
            pl.BlockSpec((None, 1, GK), lambda b, d, s: (d, 0, 0)),
        ],
        out_specs=pl.BlockSpec((None, rows, GV), lambda b, d, s: (d, dm.chunk_block(b, d, s, rows), 0)),
        out_shape=jax.ShapeDtypeStruct((2, dm.M, GV), F32),
        scratch_shapes=[pltpu.VMEM((G_HEADS, GV // G_HEADS, GK // G_HEADS), F32),
                        pltpu.VMEM((rows // G_CHUNK, G_CHUNK, GK), BF16),
                        pltpu.VMEM((rows // G_CHUNK, G_CHUNK, GK), BF16),
                        pltpu.VMEM((rows // G_CHUNK, 1, GK), F32)],
        compiler_params=_params("parallel", "parallel", "arbitrary"),
        name="gla",
    )(z, z, z, z, wa_p, ba)


def _gla_out_kernel(dv, y_ref, g_ref, nw_ref, o_ref):
    o = y_ref[0] + y_ref[1]
    g = g_ref[...].astype(F32)
    gate = g * _sigmoid(g)
    for h in range(G_HEADS):
        vs = slice(h * dv, (h + 1) * dv)
        oh = o[:, vs]
        oh = oh * lax.rsqrt(jnp.mean(oh * oh, -1, keepdims=True) + G_EPS) * nw_ref[...]
        o_ref[:, vs] = (oh * gate[:, vs]).astype(BF16)


def _gla_out(dm, y, z, norm_w):
    tm, GV, M = dm.tmr, dm.GV, dm.M
    dv = GV // G_HEADS
    return pl.pallas_call(
        functools.partial(_gla_out_kernel, dv),
        grid=(M // tm,),
        in_specs=[
            pl.BlockSpec((2, tm, GV), lambda i: (0, i, 0)),
            pl.BlockSpec((tm, GV), lambda i: (i, dm.c_gla_g // GV)),
            pl.BlockSpec((1, dv), lambda i: (0, 0)),
        ],
        out_specs=pl.BlockSpec((tm, GV), lambda i: (i, 0)),
        out_shape=jax.ShapeDtypeStruct((M, GV), BF16),
        compiler_params=_params("parallel"),
        name="gla_out",
    )(y, z, norm_w.reshape(1, dv))


def _merge_kernel(alpha, x_ref, mg_ref, oa_ref, or_ref, og_ref, ga_ref, gr_ref, gg_ref, wa_ref, wr_ref, wg_ref,
                  wo_ref, g_ref, b_ref, o_ref, acc_ref):
    n = pl.program_id(1)

    @pl.when(n == 0)
    def _():
        acc_ref[...] = jnp.zeros_like(acc_ref)

    merged = (_sigmoid(ga_ref[...].astype(F32)) * _dot(oa_ref[...], wa_ref[...])
              + _sigmoid(gr_ref[...].astype(F32)) * _dot(or_ref[...], wr_ref[...])
              + _sigmoid(gg_ref[...].astype(F32)) * _dot(og_ref[...], wg_ref[...]))
    acc_ref[...] += _dot(merged.astype(BF16), wo_ref[...])

    @pl.when(n == pl.num_programs(1) - 1)
    def _():
        z = alpha * x_ref[...] + mg_ref[...] * acc_ref[...]
        o_ref[...] = _ln(z) * g_ref[...] + b_ref[...]


def _merge(dm, alpha, x, mods, z, o_att, o_rwkv, o_gla, wba, wbr, wbg, w_out, l, g, b, lat_only):
    D, tm, tn = dm.D, dm.tm, 512
    gate0 = dm.c_gate // tn
    per = D // tn
    off = dm.MC // tm if lat_only else 0
    rows = dm.M - off * tm

    def gate_spec(j):
        return pl.BlockSpec((tm, tn), lambda i, n: (i + off, gate0 + j * per + n))

    def resident(width):
        return pl.BlockSpec((tm, width), lambda i, n: (i + off, 0))

    def wcol(rows_w):
        return pl.BlockSpec((None, rows_w, tn), lambda i, n: (l, 0, n))

    return pl.pallas_call(
        functools.partial(_merge_kernel, alpha),
        grid=(rows // tm, per),
        in_specs=[
            resident(D), dm.mod_spec(5, tm, off),
            resident(A_Q_DIM), resident(dm.R), resident(dm.GV),
            gate_spec(0), gate_spec(1), gate_spec(2),
            wcol(A_Q_DIM), wcol(dm.R), wcol(dm.GV),
            pl.BlockSpec((None, tn, D), lambda i, n: (l, n, 0)),
            pl.BlockSpec((1, D), lambda i, n: (0, 0)),
            pl.BlockSpec((1, D), lambda i, n: (0, 0)),
        ],
        out_specs=pl.BlockSpec((tm, D), lambda i, n: (i, 0)),
        out_shape=jax.ShapeDtypeStruct((rows, D), F32),
        scratch_shapes=[pltpu.VMEM((tm, D), F32)],
        compiler_params=_params("parallel", "arbitrary"),
        name="merge",
    )(x, mods, o_att, o_rwkv, o_gla, z, z, z, wba, wbr, wbg, w_out, g.reshape(1, D), b.reshape(1, D))


def _pad_to(a, size, axis):
    pad = [(0, 0)] * a.ndim
    pad[axis] = (0, size - a.shape[axis])
    return jnp.pad(a, pad)


def _rwkv_cols(a, R):
    c = 3 * R
    rkv = a[..., :c]
    wc0, wc1 = a[..., c:c + DECAY_LORA], a[..., c + DECAY_LORA:c + 2 * DECAY_LORA]
    c += 2 * DECAY_LORA
    ac0, ac1 = a[..., c:c + ICLR_LORA], a[..., c + ICLR_LORA:c + 2 * ICLR_LORA]
    c += 2 * ICLR_LORA
    gc = a[..., c:c + GATE_LORA]
    ax = a.ndim - 1
    return jnp.concatenate([rkv, gc] + [_pad_to(t, LORA_PAD, ax) for t in (wc0, wc1, ac0, ac1)], axis=-1)


def _in_proj_weights(dm, w_in):
    D, R, GK, GV = dm.D, dm.R, dm.GK, dm.GV
    att_cols = A_Q_DIM + 2 * A_KV_DIM
    rw_src = 3 * R + 2 * DECAY_LORA + 2 * ICLR_LORA + GATE_LORA
    gla_src = 2 * GK + 2 * GV + 2 * G_GATE_RANK
    att, rwkv, gla, gate = jnp.split(w_in, np.cumsum([att_cols, rw_src, gla_src]).tolist(), axis=-1)
    a_q, a_k, a_v = att[:, :A_Q_DIM], att[:, A_Q_DIM:A_Q_DIM + A_KV_DIM], att[:, A_Q_DIM + A_KV_DIM:]
    g_q, g_k = gla[:, :GK], gla[:, GK:2 * GK]
    g_v, g_g = gla[:, 2 * GK:2 * GK + GV], gla[:, 2 * GK + GV:2 * GK + 2 * GV]
    g_ac = _pad_to(gla[:, 2 * GK + 2 * GV:], LORA_PAD, 1)
    w = jnp.concatenate([_rwkv_cols(rwkv, R), a_k, gate, g_v, g_g, g_q, g_k, a_q, a_v, g_ac], axis=-1)
    return _pad_to(w, dm.NP, 1).astype(BF16)


def _rope_tables(S):
    rows = S // GRID_W
    row = jnp.repeat(jnp.arange(rows, dtype=F32), GRID_W)
    col = jnp.tile(jnp.arange(GRID_W, dtype=F32), rows)
    n_freq = A_HEAD_DIM // 4
    inv_freq = ROPE_BASE ** (-jnp.arange(n_freq, dtype=F32) / n_freq)
    ang_r, ang_c = row[:, None] * inv_freq, col[:, None] * inv_freq
    cos = jnp.concatenate([jnp.cos(ang_r)] * 2 + [jnp.cos(ang_c)] * 2, axis=-1)
    sin = jnp.concatenate([-jnp.sin(ang_r), jnp.sin(ang_r), -jnp.sin(ang_c), jnp.sin(ang_c)], axis=-1)
    return cos, sin


def kernel(x, c, ctx, c_ctx, ada_w, ada_b, ln_g, ln_b, ffn_w_gu, ffn_w_down, w_in, attn_sink, rwkv_conv, rwkv_w0,
           rwkv_w2, rwkv_a0, rwkv_a2, rwkv_g2, rwkv_k_k, rwkv_k_a, rwkv_r_k, rwkv_ln_w, rwkv_ln_b, gla_wa2, gla_ba,
           gla_norm_w, w_branch_att, w_branch_rwkv, w_branch_gla, w_out):
    B, S, D = x.shape
    CT = ctx.shape[1]
    depth = ada_w.shape[0]
    dm = _Dims(B, S, CT, D)
    R = dm.R
    alpha = float((2 * depth) ** 0.25)
    assert B + 1 <= MOD_ROWS

    cc = _pad_to(jnp.concatenate([c, c_ctx[None, :]], axis=0), MOD_ROWS, 0)
    mods_all = _ada(cc, ada_w, ada_b)
    cos, sin = _rope_tables(S)
    head_of_lane = jnp.arange(R) // R_HEAD_DIM
    e1 = (head_of_lane[:, None] == jnp.arange(LORA_PAD)[None, :]).astype(BF16)
    e2 = e1.T

    w_gu, w_dn = ffn_w_gu.astype(BF16), ffn_w_down.astype(BF16)
    wba, wbr = w_branch_att.astype(BF16), w_branch_rwkv.astype(BF16)
    wbg, wo = w_branch_gla.astype(BF16), w_out.astype(BF16)

    xs = jnp.concatenate([ctx.reshape(dm.MC, D), x.reshape(dm.ML, D)], axis=0)
    for l in range(depth):
        last = l == depth - 1
        mods = mods_all[l].reshape(MOD_ROWS * N_MOD, 1, D)
        xs = _ffn(dm, alpha, xs, mods, 0, w_gu, w_dn, l, 0, ln_g[l, 0], ln_b[l, 0])

        z = _inproj(dm, xs, mods, _in_proj_weights(dm, w_in[l]))
        o_att = jnp.concatenate([_attn_ctx(dm, z, attn_sink[l]), _attn_latent(dm, z, attn_sink[l], cos, sin)], axis=0)

        rp = {
            'conv': _rwkv_cols(rwkv_conv[l], R),
            'g2': rwkv_g2[l].astype(BF16),
            'w0': rwkv_w0[l], 'a0': rwkv_a0[l],
            'w2': _pad_to(rwkv_w2[l], LORA_PAD, 1).astype(BF16),
            'a2': _pad_to(rwkv_a2[l], LORA_PAD, 1).astype(BF16),
            'k_k': rwkv_k_k[l].reshape(1, R), 'k_a': rwkv_k_a[l].reshape(1, R), 'r_k': rwkv_r_k[l].reshape(1, R),
            'ln_w': rwkv_ln_w[l].reshape(1, R), 'ln_b': rwkv_ln_b[l].reshape(1, R),
            'e1': e1, 'e2': e2,
        }
        r, v, kk, lw, kd, b, bonus, g = _rwkv_prep(dm, z, rp)
        g_m, h_m, q_m, y0_m = _rwkv_chunk(dm, r, v, kk, lw, kd, b)
        y_rwkv = _rwkv_scan(dm, g_m, h_m, q_m, y0_m)
        o_rwkv = _rwkv_out(dm, y_rwkv, bonus, g, rp)

        wa_p = jnp.stack([_pad_to(jnp.pad(gla_wa2[l, d], ((d * G_GATE_RANK, 0), (0, 0))), LORA_PAD, 0)
                          for d in range(2)])
        wa_hi = wa_p.astype(BF16)
        wa_split = jnp.stack([wa_hi, (wa_p - wa_hi.astype(F32)).astype(BF16)], axis=1)
        y_gla = _gla(dm, z, wa_split, gla_ba[l].reshape(2, 1, dm.GK))
        o_gla = _gla_out(dm, y_gla, z, gla_norm_w[l])

        xs = _merge(dm, alpha, xs, mods, z, o_att, o_rwkv, o_gla, wba, wbr, wbg, wo, l, ln_g[l, 1], ln_b[l, 1], last)
        xs = _ffn(dm, alpha, xs, mods, 6, w_gu, w_dn, l, 1, ln_g[l, 2], ln_b[l, 2])
    return xs.reshape(B, S, D)
```

```python
import functools
import math

import jax
import jax.numpy as jnp
import numpy as np
from jax import lax
from jax.experimental import pallas as pl
from jax.experimental.pallas import tpu as pltpu

F32 = jnp.float32
BF16 = jnp.bfloat16
HI = lax.Precision.HIGHEST

N_MOD = 9
D_FF = 5632
LN_EPS = 1e-5
FFN_RES = 0.5

A_HEADS = 8
A_KV_HEADS = 2
A_GROUP = A_HEADS // A_KV_HEADS
A_HEAD_DIM = 128
A_Q_DIM = A_HEADS * A_HEAD_DIM
A_KV_DIM = A_KV_HEADS * A_HEAD_DIM
WINDOW = 128
A_BLOCK = 128
GRID_W = 64
ROPE_BASE = 10000.0
MASK_VALUE = -1e30

R_HEAD_DIM = 64
DECAY_LORA = 96
ICLR_LORA = 96
GATE_LORA = 256
R_GN_EPS = 64e-5
R_CHUNK = 64
LORA_PAD = 128
HALO = 16

G_HEADS = 4
G_GATE_RANK = 16
G_GATE_NORM = 16.0
G_CHUNK = 64
G_EPS = 1e-5

MOD_ROWS = 16
VMEM_LIMIT = 58 * 1024 * 1024

NN = (((1,), (0,)), ((), ()))
NT = (((1,), (1,)), ((), ()))
TN = (((0,), (0,)), ((), ()))


def _dot(a, b, dims=NN, prec=None):
    return lax.dot_general(a, b, dims, precision=prec, preferred_element_type=F32)


def _cumsum_dot(tri, x):
    hi = x.astype(BF16)
    rest = x - hi.astype(F32)
    mid = rest.astype(BF16)
    lo = (rest - mid.astype(F32)).astype(BF16)
    return _dot(tri, hi) + _dot(tri, mid) + _dot(tri, lo)


def _sigmoid(x):
    return 1.0 / (1.0 + jnp.exp(-x))


def _log_sigmoid(x):
    return jnp.minimum(x, 0.0) - jnp.log(1.0 + jnp.exp(-jnp.abs(x)))


def _ln(x):
    mu = jnp.mean(x, -1, keepdims=True)
    xc = x - mu
    var = jnp.mean(xc * xc, -1, keepdims=True)
    return xc * lax.rsqrt(var + LN_EPS)


def _params(*sem):
    return pltpu.CompilerParams(dimension_semantics=sem, vmem_limit_bytes=VMEM_LIMIT)


class _Dims:
    def __init__(self, B, S, CT, D):
        self.B, self.S, self.CT, self.D = B, S, CT, D
        self.MC, self.ML = B * CT, B * S
        self.M = self.MC + self.ML
        self.R = D // 2
        self.GK = D // 2
        self.GV = D
        self.RH = self.R // R_HEAD_DIM
        self.rw_cols = 3 * self.R + GATE_LORA + 4 * LORA_PAD
        c = 0
        self.c_rwkv = c; c += self.rw_cols
        self.c_att_k = c; c += A_KV_DIM
        self.c_gate = c; c += 3 * D
        self.c_gla_v = c; c += self.GV
        self.c_gla_g = c; c += self.GV
        self.c_gla_q = c; c += self.GK
        self.c_gla_k = c; c += self.GK
        self.c_att_q = c; c += A_Q_DIM
        self.c_att_v = c; c += A_KV_DIM
        self.c_gla_ac = c; c += LORA_PAD
        self.tn_in = 1280
        self.NP = -(-c // self.tn_in) * self.tn_in
        self.tm = math.gcd(512, math.gcd(self.MC, S))
        self.tm_in = math.gcd(1024, math.gcd(self.MC, S))
        self.tmr = math.gcd(256, math.gcd(CT, S))

    def mod_row(self, i, tm):
        nct = self.MC // tm
        tpb = self.S // tm
        return jnp.where(i < nct, self.B, (i - nct) // tpb)

    def mod_spec(self, m, tm, off=0):
        return pl.BlockSpec((None, 1, self.D), lambda i, *_: (self.mod_row(i + off, tm) * N_MOD + m, 0, 0))

    def chunk_block(self, b, d, s, chunk):
        ncc, ncl = self.CT // chunk, self.S // chunk
        j = s - ncc
        ctx_blk = b * ncc + jnp.where(d == 0, s, ncc - 1 - s)
        lat_blk = self.B * ncc + b * ncl + jnp.where(d == 0, j, ncl - 1 - j)
        return jnp.where(s < ncc, ctx_blk, lat_blk)


def _ada_kernel(c_ref, w_ref, b_ref, o_ref):
    cc = c_ref[...]
    s = (cc * _sigmoid(cc)).astype(BF16)
    o_ref[...] = _dot(s, w_ref[...].astype(BF16)) + b_ref[...]


def _ada(cc, ada_w, ada_b):
    L, D, N = ada_w.shape
    tn = 2048
    return pl.pallas_call(
        _ada_kernel,
        grid=(L, N // tn),
        in_specs=[
            pl.BlockSpec((MOD_ROWS, D), lambda l, n: (0, 0)),
            pl.BlockSpec((None, D, tn), lambda l, n: (l, 0, n)),
            pl.BlockSpec((None, 1, tn), lambda l, n: (l, 0, n)),
        ],
        out_specs=pl.BlockSpec((None, MOD_ROWS, tn), lambda l, n: (l, 0, n)),
        out_shape=jax.ShapeDtypeStruct((L, MOD_ROWS, N), F32),
        compiler_params=_params("parallel", "arbitrary"),
        name="ada",
    )(cc, ada_w, ada_b.reshape(L, 1, N))


def _ffn_kernel(alpha, x_ref, sh_ref, sc_ref, gt_ref, wg_ref, wu_ref, wd_ref, g_ref, b_ref, o_ref,
                h_ref, acc_ref):
    f = pl.program_id(1)

    @pl.when(f == 0)
    def _():
        h = _ln(x_ref[...]) * (1.0 + sc_ref[...]) + sh_ref[...]
        h_ref[...] = h.astype(BF16)
        acc_ref[...] = jnp.zeros_like(acc_ref)

    h = h_ref[...]
    g = _dot(h, wg_ref[...])
    u = _dot(h, wu_ref[...])
    a = (g * _sigmoid(g) * u).astype(BF16)
    acc_ref[...] += _dot(a, wd_ref[...])

    @pl.when(f == pl.num_programs(1) - 1)
    def _():
        z = alpha * x_ref[...] + (FFN_RES * gt_ref[...]) * acc_ref[...]
        o_ref[...] = _ln(z) * g_ref[...] + b_ref[...]


def _ffn(dm, alpha, x, mods, m0, w_gu, w_down, l, j, g, b):
    D, tm, tf = dm.D, dm.tm, 512
    nf = D_FF // tf
    rows = x.shape[0]
    off = (dm.M - rows) // tm
    return pl.pallas_call(
        functools.partial(_ffn_kernel, alpha),
        grid=(rows // tm, nf),
        in_specs=[
            pl.BlockSpec((tm, D), lambda i, f: (i, 0)),
            dm.mod_spec(m0, tm, off), dm.mod_spec(m0 + 1, tm, off), dm.mod_spec(m0 + 2, tm, off),
            pl.BlockSpec((None, None, D, tf), lambda i, f: (l, j, 0, f)),
            pl.BlockSpec((None, None, D, tf), lambda i, f: (l, j, 0, f + nf)),
            pl.BlockSpec((None, None, tf, D), lambda i, f: (l, j, f, 0)),
            pl.BlockSpec((1, D), lambda i, f: (0, 0)),
            pl.BlockSpec((1, D), lambda i, f: (0, 0)),
        ],
        out_specs=pl.BlockSpec((tm, D), lambda i, f: (i, 0)),
        out_shape=jax.ShapeDtypeStruct((rows, D), F32),
        scratch_shapes=[pltpu.VMEM((tm, D), BF16), pltpu.VMEM((tm, D), F32)],
        compiler_params=_params("parallel", "arbitrary"),
        name="ffn",
    )(x, mods, mods, mods, w_gu, w_gu, w_down, g.reshape(1, D), b.reshape(1, D))


def _inproj_kernel(x_ref, sh_ref, sc_ref, w_ref, o_ref, h_ref):
    @pl.when(pl.program_id(1) == 0)
    def _():
        h = _ln(x_ref[...]) * (1.0 + sc_ref[...]) + sh_ref[...]
        h_ref[...] = h.astype(BF16)

    o_ref[...] = _dot(h_ref[...], w_ref[...]).astype(BF16)


def _inproj(dm, x, mods, w_in_p):
    D, tm, tn = dm.D, dm.tm_in, dm.tn_in
    return pl.pallas_call(
        _inproj_kernel,
        grid=(dm.M // tm, dm.NP // tn),
        in_specs=[
            pl.BlockSpec((tm, D), lambda i, n: (i, 0)),
            dm.mod_spec(3, tm), dm.mod_spec(4, tm),
            pl.BlockSpec((D, tn), lambda i, n: (0, n)),
        ],
        out_specs=pl.BlockSpec((tm, tn), lambda i, n: (i, n)),
        out_shape=jax.ShapeDtypeStruct((dm.M, dm.NP), BF16),
        scratch_shapes=[pltpu.VMEM((tm, D), BF16)],
        compiler_params=_params("parallel", "arbitrary"),
        name="inproj",
    )(x, mods, mods, w_in_p)


def _rope(x, cos, sin_signed):
    lane = lax.broadcasted_iota(jnp.int32, x.shape, 1)
    quarter = A_HEAD_DIM // 4
    swapped = jnp.where((lane % (2 * quarter)) < quarter,
                        pltpu.roll(x, A_HEAD_DIM - quarter, 1), pltpu.roll(x, quarter, 1))
    return x * cos + swapped * sin_signed


def _attn_kernel(S, nb, sink_ref, q_ref, kp_ref, kc_ref, kn_ref, vp_ref, vc_ref, vn_ref, kx_ref, vx_ref,
                 cos_ref, sin_ref, o_ref):
    n = pl.program_id(1)
    scale = A_HEAD_DIM ** -0.5

    def table(ref, blk):
        return ref[pl.ds(pl.multiple_of(blk * A_BLOCK, A_BLOCK), A_BLOCK), :]

    blk_p, blk_n = jnp.maximum(n - 1, 0), jnp.minimum(n + 1, nb - 1)
    cos_c, sin_c = table(cos_ref, n), table(sin_ref, n)
    cos_p, sin_p = table(cos_ref, blk_p), table(sin_ref, blk_p)
    cos_n, sin_n = table(cos_ref, blk_n), table(sin_ref, blk_n)

    rows, band = A_GROUP * A_BLOCK, 3 * A_BLOCK
    qpos = n * A_BLOCK + lax.broadcasted_iota(jnp.int32, (rows, band), 0) % A_BLOCK
    kpos = (n - 1) * A_BLOCK + lax.broadcasted_iota(jnp.int32, (rows, band), 1)
    valid = (jnp.abs(qpos - kpos) <= WINDOW) & (kpos >= 0) & (kpos < S)

    work = []
    for kvh in range(A_KV_HEADS):
        ks = slice(kvh * A_HEAD_DIM, (kvh + 1) * A_HEAD_DIM)
        kb = jnp.concatenate([_rope(kp_ref[:, ks].astype(F32), cos_p, sin_p),
                              _rope(kc_ref[:, ks].astype(F32), cos_c, sin_c),
                              _rope(kn_ref[:, ks].astype(F32), cos_n, sin_n)], axis=0).astype(BF16)
        heads = [kvh * A_GROUP + g for g in range(A_GROUP)]
        qs = [q_ref[:, h * A_HEAD_DIM:(h + 1) * A_HEAD_DIM] for h in heads]
        q_rope = jnp.concatenate([_rope(q.astype(F32), cos_c, sin_c) * scale for q in qs], axis=0).astype(BF16)
        q_plain = jnp.concatenate([q.astype(F32) * scale for q in qs], axis=0).astype(BF16)
        sink = jnp.concatenate([jnp.full((A_BLOCK, 1), sink_ref[h], F32) for h in heads], axis=0)
        s_loc = jnp.where(valid, _dot(q_rope, kb, NT), MASK_VALUE)
        s_ctx = _dot(q_plain, kx_ref[:, ks], NT)
        work.append(dict(ks=ks, heads=heads, sink=sink, s_loc=s_loc, s_ctx=s_ctx))
    for w in work:
        w['m'] = jnp.maximum(jnp.maximum(jnp.max(w['s_loc'], -1, keepdims=True),
                                         jnp.max(w['s_ctx'], -1, keepdims=True)), w['sink'])
    for w in work:
        w['p_loc'] = jnp.exp(w.pop('s_loc') - w['m'])
        w['p_ctx'] = jnp.exp(w.pop('s_ctx') - w['m'])
    for w in work:
        w['den'] = (jnp.sum(w['p_loc'], -1, keepdims=True) + jnp.sum(w['p_ctx'], -1, keepdims=True)
                    + jnp.exp(w['sink'] - w['m']))
    for w in work:
        ks = w['ks']
        vb = jnp.concatenate([vp_ref[:, ks], vc_ref[:, ks], vn_ref[:, ks]], axis=0)
        o = (_dot(w['p_loc'].astype(BF16), vb) + _dot(w['p_ctx'].astype(BF16), vx_ref[:, ks])) / w['den']
        for g, h in enumerate(w['heads']):
            o_ref[:, h * A_HEAD_DIM:(h + 1) * A_HEAD_DIM] = o[g * A_BLOCK:(g + 1) * A_BLOCK].astype(BF16)


def _attn_latent(dm, z, sink, cos, sin):
    B, S = dm.B, dm.S
    nb = S // A_BLOCK
    base = dm.MC // A_BLOCK
    qc = dm.c_att_q // A_Q_DIM
    kc = dm.c_att_k // A_KV_DIM
    vc = dm.c_att_v // A_KV_DIM

    def rows(shift):
        return lambda b, n: base + b * nb + jnp.clip(n + shift, 0, nb - 1)

    def kv_spec(col, shift):
        r = rows(shift)
        return pl.BlockSpec((A_BLOCK, A_KV_DIM), lambda b, n: (r(b, n), col))

    return pl.pallas_call(
        functools.partial(_attn_kernel, S, nb),
        grid=(B, nb),
        in_specs=[
            pl.BlockSpec(memory_space=pltpu.SMEM),
            pl.BlockSpec((A_BLOCK, A_Q_DIM), lambda b, n: (base + b * nb + n, qc)),
            kv_spec(kc, -1), kv_spec(kc, 0), kv_spec(kc, 1),
            kv_spec(vc, -1), kv_spec(vc, 0), kv_spec(vc, 1),
            pl.BlockSpec((dm.CT, A_KV_DIM), lambda b, n: (b, kc)),
            pl.BlockSpec((dm.CT, A_KV_DIM), lambda b, n: (b, vc)),
            pl.BlockSpec((S, A_HEAD_DIM), lambda b, n: (0, 0)),
            pl.BlockSpec((S, A_HEAD_DIM), lambda b, n: (0, 0)),
        ],
        out_specs=pl.BlockSpec((A_BLOCK, A_Q_DIM), lambda b, n: (b * nb + n, 0)),
        out_shape=jax.ShapeDtypeStruct((dm.ML, A_Q_DIM), BF16),
        compiler_params=_params("parallel", "arbitrary"),
        name="attn_latent",
    )(sink, z, z, z, z, z, z, z, z, z, cos, sin)


def _attn_ctx_kernel(CT, sink_ref, q_ref, k_ref, v_ref, o_ref):
    scale = A_HEAD_DIM ** -0.5
    for kvh in range(A_KV_HEADS):
        ks = slice(kvh * A_HEAD_DIM, (kvh + 1) * A_HEAD_DIM)
        k = k_ref[:, ks]
        v = v_ref[:, ks]
        heads = [kvh * A_GROUP + g for g in range(A_GROUP)]
        q = jnp.concatenate([q_ref[:, h * A_HEAD_DIM:(h + 1) * A_HEAD_DIM] for h in heads], axis=0)
        sink = jnp.concatenate([jnp.full((CT, 1), sink_ref[h], F32) for h in heads], axis=0)
        s = _dot(q, k, NT) * scale
        m = jnp.maximum(jnp.max(s, -1, keepdims=True), sink)
        p = jnp.exp(s - m)
        den = jnp.sum(p, -1, keepdims=True) + jnp.exp(sink - m)
        o = _dot(p.astype(BF16), v) / den
        for g, h in enumerate(heads):
            o_ref[:, h * A_HEAD_DIM:(h + 1) * A_HEAD_DIM] = o[g * CT:(g + 1) * CT].astype(BF16)


def _attn_ctx(dm, z, sink):
    CT = dm.CT
    return pl.pallas_call(
        functools.partial(_attn_ctx_kernel, CT),
        grid=(dm.B,),
        in_specs=[
            pl.BlockSpec(memory_space=pltpu.SMEM),
            pl.BlockSpec((CT, A_Q_DIM), lambda b: (b, dm.c_att_q // A_Q_DIM)),
            pl.BlockSpec((CT, A_KV_DIM), lambda b: (b, dm.c_att_k // A_KV_DIM)),
            pl.BlockSpec((CT, A_KV_DIM), lambda b: (b, dm.c_att_v // A_KV_DIM)),
        ],
        out_specs=pl.BlockSpec((CT, A_Q_DIM), lambda b: (b, 0)),
        out_shape=jax.ShapeDtypeStruct((dm.MC, A_Q_DIM), BF16),
        compiler_params=_params("parallel"),
        name="attn_ctx",
    )(sink, z, z, z)


def _head_sum(x, e1_ref, e2_ref):
    def split_dot(a, ind):
        hi = a.astype(BF16)
        lo = (a - hi.astype(F32)).astype(BF16)
        return _dot(hi, ind) + _dot(lo, ind)

    return split_dot(split_dot(x, e1_ref[...]), e2_ref[...])


def _rwkv_prep_kernel(dm, z_ref, zp_ref, zn_ref, cw_ref, g2_ref, w0_ref, w2_ref, a0_ref, a2_ref, kk_ref, ka_ref,
                      rk_ref, e1_ref, e2_ref,
                      r_out, v_out, kk_out, lw_out, kd_out, b_out, bonus_out, g_out):
    i = pl.program_id(0)
    tm, R = dm.tmr, dm.R
    nct = dm.MC // tm
    start = jnp.where(i < nct, (i * tm) % dm.CT, ((i - nct) * tm) % dm.S)
    seqlen = jnp.where(i < nct, dm.CT, dm.S)
    has_prev = (start != 0).astype(F32)
    has_next = (start + tm != seqlen).astype(F32)

    z = z_ref[...].astype(F32)
    rowid = lax.broadcasted_iota(jnp.int32, (tm, 1), 0)
    halo_prev = zp_ref[HALO - 1:HALO, :].astype(F32) * has_prev
    halo_next = zn_ref[0:1, :].astype(F32) * has_next
    z_prev = jnp.where(rowid == 0, halo_prev, pltpu.roll(z, 1, 0))
    z_next = jnp.where(rowid == tm - 1, halo_next, pltpu.roll(z, tm - 1, 0))
    zc = z_prev * cw_ref[0:1, :] + z * cw_ref[1:2, :] + z_next * cw_ref[2:3, :]

    r = zc[:, 0:R]
    k = zc[:, R:2 * R]
    v = zc[:, 2 * R:3 * R]
    c0 = 3 * R
    gc = zc[:, c0:c0 + GATE_LORA]
    c0 += GATE_LORA
    g_out[...] = _dot(_sigmoid(gc).astype(BF16), g2_ref[...])

    kkr = k * kk_ref[...]
    norm = jnp.sqrt(_head_sum(kkr * kkr, e1_ref, e2_ref))
    kk = kkr / jnp.maximum(norm, 1e-12)
    r_out[...] = r
    v_out[...] = v
    kk_out[...] = kk

    bonus = jnp.zeros_like(v)
    for d in range(2):
        wc = zc[:, c0 + d * LORA_PAD:c0 + (d + 1) * LORA_PAD]
        ac = zc[:, c0 + (2 + d) * LORA_PAD:c0 + (3 + d) * LORA_PAD]
        wl = w0_ref[d:d + 1, :] + _dot(jnp.tanh(wc).astype(BF16), w2_ref[d])
        w = _log_sigmoid(wl) - 0.5
        lw_out[d] = -jnp.exp(w)
        a = _sigmoid(a0_ref[d:d + 1, :] + _dot(ac.astype(BF16), a2_ref[d]))
        kd = k * (1.0 + (a - 1.0) * ka_ref[...])
        kd_out[d] = kd
        b_out[d] = kk * a
        bonus = bonus + _head_sum(r * kd * rk_ref[...], e1_ref, e2_ref) * v
    bonus_out[...] = bonus


def _rwkv_prep(dm, z, p):
    tm, R, M, W = dm.tmr, dm.R, dm.M, dm.rw_cols
    n_halo = M // HALO
    row = pl.BlockSpec((tm, R), lambda i: (i, 0))
    row2 = pl.BlockSpec((2, tm, R), lambda i: (0, i, 0))

    def full(a):
        return pl.BlockSpec(a.shape, lambda i: (0,) * a.ndim)

    consts = [p['conv'], p['g2'], p['w0'], p['w2'], p['a0'], p['a2'], p['k_k'], p['k_a'], p['r_k'], p['e1'], p['e2']]
    one = jax.ShapeDtypeStruct((M, R), F32)
    two = jax.ShapeDtypeStruct((2, M, R), F32)
    return pl.pallas_call(
        functools.partial(_rwkv_prep_kernel, dm),
        grid=(M // tm,),
        in_specs=[
            pl.BlockSpec((tm, W), lambda i: (i, 0)),
            pl.BlockSpec((HALO, W), lambda i: (jnp.maximum(i * (tm // HALO) - 1, 0), 0)),
            pl.BlockSpec((HALO, W), lambda i: (jnp.minimum((i + 1) * (tm // HALO), n_halo - 1), 0)),
        ] + [full(a) for a in consts],
        out_specs=[row, row, row, row2, row2, row2, row, row],
        out_shape=[one, one, one, two, two, two, one, one],
        compiler_params=_params("parallel"),
        name="rwkv_prep",
    )(z, z, z, *consts)


R_SLAB = 4 * R_HEAD_DIM


def _block_diag(x, mask):
    return jnp.where(mask, jnp.concatenate([x] * 4, axis=0), 0.0).astype(BF16)


def _diag_blocks(full, mask):
    fm = jnp.where(mask, full, 0.0)
    n = R_HEAD_DIM
    return fm[0:n] + fm[n:2 * n] + fm[2 * n:3 * n] + fm[3 * n:4 * n]


def _slab_masks():
    n, g = R_HEAD_DIM, R_SLAB
    blk = (lax.broadcasted_iota(jnp.int32, (g, g), 0) // n) == (lax.broadcasted_iota(jnp.int32, (g, g), 1) // n)
    t_idx = lax.broadcasted_iota(jnp.int32, (n, g), 0)
    s_idx = lax.broadcasted_iota(jnp.int32, (n, g), 1) % n
    return blk, t_idx, s_idx


def _rwkv_chunk_kernel(n_sub, r_ref, v_ref, kk_ref, lw_ref, kd_ref, b_ref, g_out, h_out, q_out, y0_out):
    C = R_CHUNK
    blk, t_idx, s_idx = _slab_masks()
    eye4 = (t_idx == s_idx).astype(F32)
    ti = lax.broadcasted_iota(jnp.int32, (C, C), 0)
    si = lax.broadcasted_iota(jnp.int32, (C, C), 1)

    def bd(x):
        return _block_diag(x, blk)

    def setup(c, d):
        rs = pl.ds(c * C, C)
        r, v, kk = r_ref[rs, :], v_ref[rs, :], kk_ref[rs, :]
        tri = ((ti >= si) if d == 0 else (ti <= si)).astype(BF16)
        lw = lw_ref[d, rs, :]
        cl = _cumsum_dot(tri, lw)
        tot = jnp.sum(lw, axis=0, keepdims=True)
        e_neg = jnp.exp(-cl)
        e_end = jnp.exp(tot - cl)
        kd, b = kd_ref[d, rs, :], b_ref[d, rs, :]
        ch = dict(rs=rs, d=d, tot=tot, v_b=v.astype(BF16), v_bd=bd(v),
                  incl=(t_idx >= s_idx) if d == 0 else (t_idx <= s_idx),
                  strict=(t_idx > s_idx) if d == 0 else (t_idx < s_idx),
                  kkm=kk * jnp.exp(cl - lw), rp=r * jnp.exp(cl),
                  bp=b * e_neg, kp=kd * e_neg, be=(b * e_end).astype(BF16), ke=(kd * e_end).astype(BF16))
        ch['lhs'] = jnp.concatenate([ch['kkm'], ch['rp']], axis=0).astype(BF16)
        return ch

    chains = [setup(c, d) for c in range(n_sub) for d in range(2)]
    for ch in chains:
        gram_b = _dot(ch['lhs'], bd(ch.pop('bp')), NT)
        gram_k = _dot(ch.pop('lhs'), bd(ch.pop('kp')), NT)
        lb = jnp.where(ch['strict'], gram_b[:C], 0.0)
        ch['mb'] = jnp.where(ch['incl'], gram_b[C:], 0.0).astype(BF16)
        ch['lmk'] = jnp.concatenate([jnp.where(ch['strict'], gram_k[:C], 0.0),
                                     jnp.where(ch['incl'], gram_k[C:], 0.0)], axis=0).astype(BF16)
        ch['pw'] = -lb
        ch['t_inv'] = eye4 - lb
    for ch in chains:
        ch['pw'] = _dot(ch['pw'].astype(BF16), bd(ch['pw']))
        lmkv = _dot(ch.pop('lmk'), ch.pop('v_bd'))
        ch['lkv'], ch['mkv'] = lmkv[:C], lmkv[C:]
    for _ in range(int(math.log2(C)) - 2):
        for ch in chains:
            both = _dot(jnp.concatenate([ch['pw'], ch['t_inv']], axis=0).astype(BF16), bd(ch['pw']))
            ch['pw'], ch['t_inv'] = both[:C], ch['t_inv'] + both[C:]
    for ch in chains:
        ch['t_inv'] = (ch['t_inv'] + _dot(ch['t_inv'].astype(BF16), bd(ch.pop('pw')))).astype(BF16)
    for ch in chains:
        ch['a_m'] = _dot(ch['t_inv'], bd(ch.pop('kkm')))
        ch['d_m'] = -_dot(ch.pop('t_inv'), bd(ch.pop('lkv')))
    for ch in chains:
        d, rs = ch['d'], ch['rs']
        a_m, d_m = ch['a_m'], ch['d_m']
        q_out[d, rs, :] = (ch['rp'] - _dot(ch['mb'], bd(a_m))).astype(BF16)
        y0_out[d, rs, :] = _dot(ch['mb'], bd(d_m)) + ch['mkv']
        g_cross = _dot(ch['be'], a_m.astype(BF16), TN)
        h_cross = _dot(jnp.concatenate([ch['be'], ch['ke']], axis=0),
                       jnp.concatenate([d_m.astype(BF16), ch['v_b']], axis=0), TN)
        g_out[d, rs, :] = (eye4 * jnp.exp(ch['tot']) - _diag_blocks(g_cross, blk)).astype(BF16)
        h_out[d, rs, :] = _diag_blocks(h_cross, blk)


def _rwkv_chunk(dm, r, v, kk, lw, kd, b):
    M, R = dm.M, dm.R
    rows = dm.tmr
    one = pl.BlockSpec((rows, R_SLAB), lambda i, h: (i, h))
    two = pl.BlockSpec((2, rows, R_SLAB), lambda i, h: (0, i, h))
    f32 = jax.ShapeDtypeStruct((2, M, R), F32)
    b16 = jax.ShapeDtypeStruct((2, M, R), BF16)
    return pl.pallas_call(
        functools.partial(_rwkv_chunk_kernel, rows // R_CHUNK),
        grid=(M // rows, R // R_SLAB),
        in_specs=[one, one, one, two, two, two],
        out_specs=[two, two, two, two],
        out_shape=[b16, f32, b16, f32],
        compiler_params=_params("parallel", "parallel"),
        name="rwkv_chunk",
    )(r, v, kk, lw, kd, b)


def _rwkv_scan_kernel(n_slab, n_in, g_ref, h_ref, q_ref, y0_ref, y_ref, st_ref):
    C = R_CHUNK
    d = pl.program_id(1)

    @pl.when(pl.program_id(2) == 0)
    def _():
        st_ref[...] = jnp.zeros_like(st_ref)

    blk, _, _ = _slab_masks()
    for j in range(n_in):
        cj = jnp.where(d == 0, j, n_in - 1 - j)
        rs = pl.ds(pl.multiple_of(cj * C, C), C)
        lanes = [slice(sl * R_SLAB, (sl + 1) * R_SLAB) for sl in range(n_slab)]
        st_bd = [_block_diag(st_ref[:, ls], blk) for ls in lanes]
        both = [_dot(jnp.concatenate([q_ref[rs, ls], g_ref[rs, ls]], axis=0), bd) for ls, bd in zip(lanes, st_bd)]
        for ls, res in zip(lanes, both):
            y_ref[rs, ls] = res[:C] + y0_ref[rs, ls]
            st_ref[:, ls] = res[C:] + h_ref[rs, ls]


def _rwkv_scan(dm, g_m, h_m, q_m, y0_m):
    R, rows = dm.R, dm.tmr
    steps = (dm.CT + dm.S) // rows
    spec = pl.BlockSpec((None, rows, R), lambda b, d, s: (d, dm.chunk_block(b, d, s, rows), 0))
    return pl.pallas_call(
        functools.partial(_rwkv_scan_kernel, R // R_SLAB, rows // R_CHUNK),
        grid=(dm.B, 2, steps),
        in_specs=[spec, spec, spec, spec],
        out_specs=spec,
        out_shape=jax.ShapeDtypeStruct((2, dm.M, R), F32),
        scratch_shapes=[pltpu.VMEM((R_HEAD_DIM, R), F32)],
        compiler_params=_params("parallel", "parallel", "arbitrary"),
        name="rwkv_scan",
    )(g_m, h_m, q_m, y0_m)


def _rwkv_out_kernel(y_ref, bonus_ref, g_ref, lnw_ref, lnb_ref, e1_ref, e2_ref, o_ref):
    o = y_ref[0] + y_ref[1]
    inv_n = 1.0 / R_HEAD_DIM
    mu = _head_sum(o, e1_ref, e2_ref) * inv_n
    oc = o - mu
    var = _head_sum(oc * oc, e1_ref, e2_ref) * inv_n
    o = oc * lax.rsqrt(var + R_GN_EPS) * lnw_ref[...] + lnb_ref[...]
    o_ref[...] = ((o + bonus_ref[...]) * g_ref[...]).astype(BF16)


def _rwkv_out(dm, y, bonus, g, p):
    tm, R, M = dm.tmr, dm.R, dm.M
    row = pl.BlockSpec((tm, R), lambda i: (i, 0))

    def full(a):
        return pl.BlockSpec(a.shape, lambda i: (0,) * a.ndim)

    consts = [p['ln_w'], p['ln_b'], p['e1'], p['e2']]
    return pl.pallas_call(
        _rwkv_out_kernel,
        grid=(M // tm,),
        in_specs=[pl.BlockSpec((2, tm, R), lambda i: (0, i, 0)), row, row] + [full(a) for a in consts],
        out_specs=row,
        out_shape=jax.ShapeDtypeStruct((M, R), BF16),
        compiler_params=_params("parallel"),
        name="rwkv_out",
    )(y, bonus, g, *consts)


def _gla_kernel(dm, n_in, q_ref, k_ref, v_ref, ac_ref, wa_ref, ba_ref, y_ref, st_ref):
    C = G_CHUNK
    dk, dv = dm.GK // G_HEADS, dm.GV // G_HEADS
    d = pl.program_id(1)

    @pl.when(pl.program_id(2) == 0)
    def _():
        st_ref[...] = jnp.zeros_like(st_ref)

    ti = lax.broadcasted_iota(jnp.int32, (C, C), 0)
    si = lax.broadcasted_iota(jnp.int32, (C, C), 1)
    causal = jnp.where(d == 0, ti - si, si - ti) >= 0
    tri = causal.astype(BF16)

    def body(j, carry):
        cj = jnp.where(d == 0, j, n_in - 1 - j)
        rs = pl.ds(pl.multiple_of(cj * C, C), C)
        ac = ac_ref[rs, :]
        la = _log_sigmoid(_dot(ac, wa_ref[0]) + _dot(ac, wa_ref[1]) + ba_ref[...]) * (1.0 / G_GATE_NORM)
        cb = _cumsum_dot(tri, la)
        tot = jnp.sum(la, axis=0, keepdims=True)
        q_dec = (q_ref[rs, :].astype(F32) * (dk ** -0.5) * jnp.exp(cb)).astype(BF16)
        k = k_ref[rs, :].astype(F32)
        k_inv = (k * jnp.exp(-cb)).astype(BF16)
        k_end = (k * jnp.exp(tot - cb)).astype(BF16)
        e_last = jnp.exp(tot)
        key = [slice(h * dk, (h + 1) * dk) for h in range(G_HEADS)]
        val = [slice(h * dv, (h + 1) * dv) for h in range(G_HEADS)]
        att = [jnp.where(causal, _dot(q_dec[:, ks], k_inv[:, ks], NT), 0.0).astype(BF16) for ks in key]
        inter = [_dot(q_dec[:, key[h]], st_ref[h].astype(BF16), NT) for h in range(G_HEADS)]
        upd = [_dot(v_ref[rs, val[h]], k_end[:, key[h]], TN) for h in range(G_HEADS)]
        for h in range(G_HEADS):
            y_ref[rs, val[h]] = _dot(att[h], v_ref[rs, val[h]]) + inter[h]
        for h in range(G_HEADS):
            st_ref[h] = st_ref[h] * e_last[:, key[h]] + upd[h]
        return carry

    lax.fori_loop(0, n_in, body, 0)


def _gla(dm, z, wa_p, ba):
    rows = dm.tmr
    steps = (dm.CT + dm.S) // rows
    GK, GV = dm.GK, dm.GV

    def zspec(width, col):
        return pl.BlockSpec((rows, width), lambda b, d, s: (dm.chunk_block(b, d, s, rows), col // width))

    return pl.pallas_call(
        functools.partial(_gla_kernel, dm, rows // G_CHUNK),
        grid=(dm.B, 2, steps),
        in_specs=[
            zspec(GK, dm.c_gla_q), zspec(GK, dm.c_gla_k), zspec(GV, dm.c_gla_v), zspec(LORA_PAD, dm.c_gla_ac),
            pl.BlockSpec((None, 2, LORA_PAD, GK), lambda b, d, s: (d, 0, 0, 0)),
            pl.BlockSpec((None, 1, GK), lambda b, d, s: (d, 0, 0)),
        ],
        out_specs=pl.BlockSpec((None, rows, GV), lambda b, d, s: (d, dm.chunk_block(b, d, s, rows), 0)),
        out_shape=jax.ShapeDtypeStruct((2, dm.M, GV), F32),
        scratch_shapes=[pltpu.VMEM((G_HEADS, GV // G_HEADS, GK // G_HEADS), F32)],
        compiler_params=_params("parallel", "parallel", "arbitrary"),
        name="gla",
    )(z, z, z, z, wa_p, ba)


def _gla_out_kernel(dv, y_ref, g_ref, nw_ref, o_ref):
    o = y_ref[0] + y_ref[1]
    g = g_ref[...].astype(F32)
    gate = g * _sigmoid(g)
    for h in range(G_HEADS):
        vs = slice(h * dv, (h + 1) * dv)
        oh = o[:, vs]
        oh = oh * lax.rsqrt(jnp.mean(oh * oh, -1, keepdims=True) + G_EPS) * nw_ref[...]
        o_ref[:, vs] = (oh * gate[:, vs]).astype(BF16)


def _gla_out(dm, y, z, norm_w):
    tm, GV, M = dm.tmr, dm.GV, dm.M
    dv = GV // G_HEADS
    return pl.pallas_call(
        functools.partial(_gla_out_kernel, dv),
        grid=(M // tm,),
        in_specs=[
            pl.BlockSpec((2, tm, GV), lambda i: (0, i, 0)),
            pl.BlockSpec((tm, GV), lambda i: (i, dm.c_gla_g // GV)),
            pl.BlockSpec((1, dv), lambda i: (0, 0)),
        ],
        out_specs=pl.BlockSpec((tm, GV), lambda i: (i, 0)),
        out_shape=jax.ShapeDtypeStruct((M, GV), BF16),
        compiler_params=_params("parallel"),
        name="gla_out",
    )(y, z, norm_w.reshape(1, dv))


def _merge_kernel(alpha, x_ref, mg_ref, oa_ref, or_ref, og_ref, ga_ref, gr_ref, gg_ref, wa_ref, wr_ref, wg_ref,
                  wo_ref, g_ref, b_ref, o_ref, acc_ref):
    n = pl.program_id(1)

    @pl.when(n == 0)
    def _():
        acc_ref[...] = jnp.zeros_like(acc_ref)

    merged = (_sigmoid(ga_ref[...].astype(F32)) * _dot(oa_ref[...], wa_ref[...])
              + _sigmoid(gr_ref[...].astype(F32)) * _dot(or_ref[...], wr_ref[...])
              + _sigmoid(gg_ref[...].astype(F32)) * _dot(og_ref[...], wg_ref[...]))
    acc_ref[...] += _dot(merged.astype(BF16), wo_ref[...])

    @pl.when(n == pl.num_programs(1) - 1)
    def _():
        z = alpha * x_ref[...] + mg_ref[...] * acc_ref[...]
        o_ref[...] = _ln(z) * g_ref[...] + b_ref[...]


def _merge(dm, alpha, x, mods, z, o_att, o_rwkv, o_gla, wba, wbr, wbg, w_out, l, g, b, lat_only):
    D, tm, tn = dm.D, dm.tm, 512
    gate0 = dm.c_gate // tn
    per = D // tn
    off = dm.MC // tm if lat_only else 0
    rows = dm.M - off * tm

    def gate_spec(j):
        return pl.BlockSpec((tm, tn), lambda i, n: (i + off, gate0 + j * per + n))

    def resident(width):
        return pl.BlockSpec((tm, width), lambda i, n: (i + off, 0))

    def wcol(rows_w):
        return pl.BlockSpec((None, rows_w, tn), lambda i, n: (l, 0, n))

    return pl.pallas_call(
        functools.partial(_merge_kernel, alpha),
        grid=(rows // tm, per),
        in_specs=[
            resident(D), dm.mod_spec(5, tm, off),
            resident(A_Q_DIM), resident(dm.R), resident(dm.GV),
            gate_spec(0), gate_spec(1), gate_spec(2),
            wcol(A_Q_DIM), wcol(dm.R), wcol(dm.GV),
            pl.BlockSpec((None, tn, D), lambda i, n: (l, n, 0)),
            pl.BlockSpec((1, D), lambda i, n: (0, 0)),
            pl.BlockSpec((1, D), lambda i, n: (0, 0)),
        ],
        out_specs=pl.BlockSpec((tm, D), lambda i, n: (i, 0)),
        out_shape=jax.ShapeDtypeStruct((rows, D), F32),
        scratch_shapes=[pltpu.VMEM((tm, D), F32)],
        compiler_params=_params("parallel", "arbitrary"),
        name="merge",
    )(x, mods, o_att, o_rwkv, o_gla, z, z, z, wba, wbr, wbg, w_out, g.reshape(1, D), b.reshape(1, D))


def _pad_to(a, size, axis):
    pad = [(0, 0)] * a.ndim
    pad[axis] = (0, size - a.shape[axis])
    return jnp.pad(a, pad)


def _rwkv_cols(a, R):
    c = 3 * R
    rkv = a[..., :c]
    wc0, wc1 = a[..., c:c + DECAY_LORA], a[..., c + DECAY_LORA:c + 2 * DECAY_LORA]
    c += 2 * DECAY_LORA
    ac0, ac1 = a[..., c:c + ICLR_LORA], a[..., c + ICLR_LORA:c + 2 * ICLR_LORA]
    c += 2 * ICLR_LORA
    gc = a[..., c:c + GATE_LORA]
    ax = a.ndim - 1
    return jnp.concatenate([rkv, gc] + [_pad_to(t, LORA_PAD, ax) for t in (wc0, wc1, ac0, ac1)], axis=-1)


def _in_proj_weights(dm, w_in):
    D, R, GK, GV = dm.D, dm.R, dm.GK, dm.GV
    att_cols = A_Q_DIM + 2 * A_KV_DIM
    rw_src = 3 * R + 2 * DECAY_LORA + 2 * ICLR_LORA + GATE_LORA
    gla_src = 2 * GK + 2 * GV + 2 * G_GATE_RANK
    att, rwkv, gla, gate = jnp.split(w_in, np.cumsum([att_cols, rw_src, gla_src]).tolist(), axis=-1)
    a_q, a_k, a_v = att[:, :A_Q_DIM], att[:, A_Q_DIM:A_Q_DIM + A_KV_DIM], att[:, A_Q_DIM + A_KV_DIM:]
    g_q, g_k = gla[:, :GK], gla[:, GK:2 * GK]
    g_v, g_g = gla[:, 2 * GK:2 * GK + GV], gla[:, 2 * GK + GV:2 * GK + 2 * GV]
    g_ac = _pad_to(gla[:, 2 * GK + 2 * GV:], LORA_PAD, 1)
    w = jnp.concatenate([_rwkv_cols(rwkv, R), a_k, gate, g_v, g_g, g_q, g_k, a_q, a_v, g_ac], axis=-1)
    return _pad_to(w, dm.NP, 1).astype(BF16)


def _rope_tables(S):
    rows = S // GRID_W
    row = jnp.repeat(jnp.arange(rows, dtype=F32), GRID_W)
    col = jnp.tile(jnp.arange(GRID_W, dtype=F32), rows)
    n_freq = A_HEAD_DIM // 4
    inv_freq = ROPE_BASE ** (-jnp.arange(n_freq, dtype=F32) / n_freq)
    ang_r, ang_c = row[:, None] * inv_freq, col[:, None] * inv_freq
    cos = jnp.concatenate([jnp.cos(ang_r)] * 2 + [jnp.cos(ang_c)] * 2, axis=-1)
    sin = jnp.concatenate([-jnp.sin(ang_r), jnp.sin(ang_r), -jnp.sin(ang_c), jnp.sin(ang_c)], axis=-1)
    return cos, sin


def kernel(x, c, ctx, c_ctx, ada_w, ada_b, ln_g, ln_b, ffn_w_gu, ffn_w_down, w_in, attn_sink, rwkv_conv, rwkv_w0,
           rwkv_w2, rwkv_a0, rwkv_a2, rwkv_g2, rwkv_k_k, rwkv_k_a, rwkv_r_k, rwkv_ln_w, rwkv_ln_b, gla_wa2, gla_ba,
           gla_norm_w, w_branch_att, w_branch_rwkv, w_branch_gla, w_out):
    B, S, D = x.shape
    CT = ctx.shape[1]
    depth = ada_w.shape[0]
    dm = _Dims(B, S, CT, D)
    R = dm.R
    alpha = float((2 * depth) ** 0.25)
    assert B + 1 <= MOD_ROWS

    cc = _pad_to(jnp.concatenate([c, c_ctx[None, :]], axis=0), MOD_ROWS, 0)
    mods_all = _ada(cc, ada_w, ada_b)
    cos, sin = _rope_tables(S)
    head_of_lane = jnp.arange(R) // R_HEAD_DIM
    e1 = (head_of_lane[:, None] == jnp.arange(LORA_PAD)[None, :]).astype(BF16)
    e2 = e1.T

    w_gu, w_dn = ffn_w_gu.astype(BF16), ffn_w_down.astype(BF16)
    wba, wbr = w_branch_att.astype(BF16), w_branch_rwkv.astype(BF16)
    wbg, wo = w_branch_gla.astype(BF16), w_out.astype(BF16)

    xs = jnp.concatenate([ctx.reshape(dm.MC, D), x.reshape(dm.ML, D)], axis=0)
    for l in range(depth):
        last = l == depth - 1
        mods = mods_all[l].reshape(MOD_ROWS * N_MOD, 1, D)
        xs = _ffn(dm, alpha, xs, mods, 0, w_gu, w_dn, l, 0, ln_g[l, 0], ln_b[l, 0])

        z = _inproj(dm, xs, mods, _in_proj_weights(dm, w_in[l]))
        o_att = jnp.concatenate([_attn_ctx(dm, z, attn_sink[l]), _attn_latent(dm, z, attn_sink[l], cos, sin)], axis=0)

        rp = {
            'conv': _rwkv_cols(rwkv_conv[l], R),
            'g2': rwkv_g2[l].astype(BF16),
            'w0': rwkv_w0[l], 'a0': rwkv_a0[l],
            'w2': _pad_to(rwkv_w2[l], LORA_PAD, 1).astype(BF16),
            'a2': _pad_to(rwkv_a2[l], LORA_PAD, 1).astype(BF16),
            'k_k': rwkv_k_k[l].reshape(1, R), 'k_a': rwkv_k_a[l].reshape(1, R), 'r_k': rwkv_r_k[l].reshape(1, R),
            'ln_w': rwkv_ln_w[l].reshape(1, R), 'ln_b': rwkv_ln_b[l].reshape(1, R),
            'e1': e1, 'e2': e2,
        }
        r, v, kk, lw, kd, b, bonus, g = _rwkv_prep(dm, z, rp)
        g_m, h_m, q_m, y0_m = _rwkv_chunk(dm, r, v, kk, lw, kd, b)
        y_rwkv = _rwkv_scan(dm, g_m, h_m, q_m, y0_m)
        o_rwkv = _rwkv_out(dm, y_rwkv, bonus, g, rp)

        wa_p = jnp.stack([_pad_to(jnp.pad(gla_wa2[l, d], ((d * G_GATE_RANK, 0), (0, 0))), LORA_PAD, 0)
                          for d in range(2)])
        wa_hi = wa_p.astype(BF16)
        wa_split = jnp.stack([wa_hi, (wa_p - wa_hi.astype(F32)).astype(BF16)], axis=1)
        y_gla = _gla(dm, z, wa_split, gla_ba[l].reshape(2, 1, dm.GK))
        o_gla = _gla_out(dm, y_gla, z, gla_norm_w[l])

        xs = _merge(dm, alpha, xs, mods, z, o_att, o_rwkv, o_gla, wba, wbr, wbg, wo, l, ln_g[l, 1], ln_b[l, 1], last)
        xs = _ffn(dm, alpha, xs, mods, 6, w_gu, w_dn, l, 1, ln_g[l, 2], ln_b[l, 2])
    return xs.reshape(B, S, D)
```

```python
import functools
import math

import jax
import jax.numpy as jnp
import numpy as np
from jax import lax
from jax.experimental import pallas as pl
from jax.experimental.pallas import tpu as pltpu

F32 = jnp.float32
BF16 = jnp.bfloat16
HI = lax.Precision.HIGHEST

N_MOD = 9
D_FF = 5632
LN_EPS = 1e-5
FFN_RES = 0.5

A_HEADS = 8
A_KV_HEADS = 2
A_GROUP = A_HEADS // A_KV_HEADS
A_HEAD_DIM = 128
A_Q_DIM = A_HEADS * A_HEAD_DIM
A_KV_DIM = A_KV_HEADS * A_HEAD_DIM
WINDOW = 128
A_BLOCK = 128
GRID_W = 64
ROPE_BASE = 10000.0
MASK_VALUE = -1e30

R_HEAD_DIM = 64
DECAY_LORA = 96
ICLR_LORA = 96
GATE_LORA = 256
R_GN_EPS = 64e-5
R_CHUNK = 64
LORA_PAD = 128
HALO = 16

G_HEADS = 4
G_GATE_RANK = 16
G_GATE_NORM = 16.0
G_CHUNK = 64
G_EPS = 1e-5

MOD_ROWS = 16
VMEM_LIMIT = 58 * 1024 * 1024

NN = (((1,), (0,)), ((), ()))
NT = (((1,), (1,)), ((), ()))
TN = (((0,), (0,)), ((), ()))


def _dot(a, b, dims=NN, prec=None):
    return lax.dot_general(a, b, dims, precision=prec, preferred_element_type=F32)


def _cumsum_dot(tri, x):
    hi = x.astype(BF16)
    rest = x - hi.astype(F32)
    mid = rest.astype(BF16)
    lo = (rest - mid.astype(F32)).astype(BF16)
    return _dot(tri, hi) + _dot(tri, mid) + _dot(tri, lo)


def _sigmoid(x):
    return 1.0 / (1.0 + jnp.exp(-x))


def _log_sigmoid(x):
    return jnp.minimum(x, 0.0) - jnp.log(1.0 + jnp.exp(-jnp.abs(x)))


def _ln(x):
    mu = jnp.mean(x, -1, keepdims=True)
    xc = x - mu
    var = jnp.mean(xc * xc, -1, keepdims=True)
    return xc * lax.rsqrt(var + LN_EPS)


def _params(*sem):
    return pltpu.CompilerParams(dimension_semantics=sem, vmem_limit_bytes=VMEM_LIMIT)


class _Dims:
    def __init__(self, B, S, CT, D):
        self.B, self.S, self.CT, self.D = B, S, CT, D
        self.MC, self.ML = B * CT, B * S
        self.M = self.MC + self.ML
        self.R = D // 2
        self.GK = D // 2
        self.GV = D
        self.RH = self.R // R_HEAD_DIM
        self.rw_cols = 3 * self.R + GATE_LORA + 4 * LORA_PAD
        c = 0
        self.c_rwkv = c; c += self.rw_cols
        self.c_att_k = c; c += A_KV_DIM
        self.c_gate = c; c += 3 * D
        self.c_gla_v = c; c += self.GV
        self.c_gla_g = c; c += self.GV
        self.c_gla_q = c; c += self.GK
        self.c_gla_k = c; c += self.GK
        self.c_att_q = c; c += A_Q_DIM
        self.c_att_v = c; c += A_KV_DIM
        self.c_gla_ac = c; c += LORA_PAD
        self.tn_in = 1280
        self.NP = -(-c // self.tn_in) * self.tn_in
        self.tm = math.gcd(512, math.gcd(self.MC, S))
        self.tm_in = math.gcd(1024, math.gcd(self.MC, S))
        self.tmr = math.gcd(256, math.gcd(CT, S))

    def mod_row(self, i, tm):
        nct = self.MC // tm
        tpb = self.S // tm
        return jnp.where(i < nct, self.B, (i - nct) // tpb)

    def mod_spec(self, m, tm, off=0):
        return pl.BlockSpec((None, 1, self.D), lambda i, *_: (self.mod_row(i + off, tm) * N_MOD + m, 0, 0))

    def chunk_block(self, b, d, s, chunk):
        ncc, ncl = self.CT // chunk, self.S // chunk
        j = s - ncc
        ctx_blk = b * ncc + jnp.where(d == 0, s, ncc - 1 - s)
        lat_blk = self.B * ncc + b * ncl + jnp.where(d == 0, j, ncl - 1 - j)
        return jnp.where(s < ncc, ctx_blk, lat_blk)


def _ada_kernel(c_ref, w_ref, b_ref, o_ref):
    cc = c_ref[...]
    s = (cc * _sigmoid(cc)).astype(BF16)
    o_ref[...] = _dot(s, w_ref[...].astype(BF16)) + b_ref[...]


def _ada(cc, ada_w, ada_b):
    L, D, N = ada_w.shape
    tn = 2048
    return pl.pallas_call(
        _ada_kernel,
        grid=(L, N // tn),
        in_specs=[
            pl.BlockSpec((MOD_ROWS, D), lambda l, n: (0, 0)),
            pl.BlockSpec((None, D, tn), lambda l, n: (l, 0, n)),
            pl.BlockSpec((None, 1, tn), lambda l, n: (l, 0, n)),
        ],
        out_specs=pl.BlockSpec((None, MOD_ROWS, tn), lambda l, n: (l, 0, n)),
        out_shape=jax.ShapeDtypeStruct((L, MOD_ROWS, N), F32),
        compiler_params=_params("parallel", "arbitrary"),
        name="ada",
    )(cc, ada_w, ada_b.reshape(L, 1, N))


def _ffn_kernel(alpha, x_ref, sh_ref, sc_ref, gt_ref, wg_ref, wu_ref, wd_ref, g_ref, b_ref, o_ref, h_ref):
    f = pl.program_id(1)

    @pl.when(f == 0)
    def _():
        h = _ln(x_ref[...]) * (1.0 + sc_ref[...]) + sh_ref[...]
        h_ref[...] = h.astype(BF16)
        o_ref[...] = jnp.zeros_like(o_ref)

    h = h_ref[...]
    g = _dot(h, wg_ref[...])
    u = _dot(h, wu_ref[...])
    a = (g * _sigmoid(g) * u).astype(BF16)
    o_ref[...] += _dot(a, wd_ref[...])

    @pl.when(f == pl.num_programs(1) - 1)
    def _():
        z = alpha * x_ref[...] + (FFN_RES * gt_ref[...]) * o_ref[...]
        o_ref[...] = _ln(z) * g_ref[...] + b_ref[...]


def _ffn(dm, alpha, x, mods, m0, w_gu, w_down, l, j, g, b):
    D, tm, tf = dm.D, dm.tm, 512
    nf = D_FF // tf
    rows = x.shape[0]
    off = (dm.M - rows) // tm
    return pl.pallas_call(
        functools.partial(_ffn_kernel, alpha),
        grid=(rows // tm, nf),
        in_specs=[
            pl.BlockSpec((tm, D), lambda i, f: (i, 0)),
            dm.mod_spec(m0, tm, off), dm.mod_spec(m0 + 1, tm, off), dm.mod_spec(m0 + 2, tm, off),
            pl.BlockSpec((None, None, D, tf), lambda i, f: (l, j, 0, f)),
            pl.BlockSpec((None, None, D, tf), lambda i, f: (l, j, 0, f + nf)),
            pl.BlockSpec((None, None, tf, D), lambda i, f: (l, j, f, 0)),
            pl.BlockSpec((1, D), lambda i, f: (0, 0)),
            pl.BlockSpec((1, D), lambda i, f: (0, 0)),
        ],
        out_specs=pl.BlockSpec((tm, D), lambda i, f: (i, 0)),
        out_shape=jax.ShapeDtypeStruct((rows, D), F32),
        scratch_shapes=[pltpu.VMEM((tm, D), BF16)],
        compiler_params=_params("parallel", "arbitrary"),
        name="ffn",
    )(x, mods, mods, mods, w_gu, w_gu, w_down, g.reshape(1, D), b.reshape(1, D))


def _inproj_kernel(x_ref, sh_ref, sc_ref, w_ref, o_ref, h_ref):
    @pl.when(pl.program_id(1) == 0)
    def _():
        h = _ln(x_ref[...]) * (1.0 + sc_ref[...]) + sh_ref[...]
        h_ref[...] = h.astype(BF16)

    o_ref[...] = _dot(h_ref[...], w_ref[...]).astype(BF16)


def _inproj(dm, x, mods, w_in_p):
    D, tm, tn = dm.D, dm.tm_in, dm.tn_in
    return pl.pallas_call(
        _inproj_kernel,
        grid=(dm.M // tm, dm.NP // tn),
        in_specs=[
            pl.BlockSpec((tm, D), lambda i, n: (i, 0)),
            dm.mod_spec(3, tm), dm.mod_spec(4, tm),
            pl.BlockSpec((D, tn), lambda i, n: (0, n)),
        ],
        out_specs=pl.BlockSpec((tm, tn), lambda i, n: (i, n)),
        out_shape=jax.ShapeDtypeStruct((dm.M, dm.NP), BF16),
        scratch_shapes=[pltpu.VMEM((tm, D), BF16)],
        compiler_params=_params("parallel", "arbitrary"),
        name="inproj",
    )(x, mods, mods, w_in_p)


def _rope(x, cos, sin_signed):
    lane = lax.broadcasted_iota(jnp.int32, x.shape, 1)
    quarter = A_HEAD_DIM // 4
    swapped = jnp.where((lane % (2 * quarter)) < quarter,
                        pltpu.roll(x, A_HEAD_DIM - quarter, 1), pltpu.roll(x, quarter, 1))
    return x * cos + swapped * sin_signed


def _attn_kernel(S, nb, sink_ref, q_ref, kp_ref, kc_ref, kn_ref, vp_ref, vc_ref, vn_ref, kx_ref, vx_ref,
                 cos_ref, sin_ref, o_ref):
    n = pl.program_id(1)
    scale = A_HEAD_DIM ** -0.5

    def table(ref, blk):
        return ref[pl.ds(pl.multiple_of(blk * A_BLOCK, A_BLOCK), A_BLOCK), :]

    blk_p, blk_n = jnp.maximum(n - 1, 0), jnp.minimum(n + 1, nb - 1)
    cos_c, sin_c = table(cos_ref, n), table(sin_ref, n)
    cos_p, sin_p = table(cos_ref, blk_p), table(sin_ref, blk_p)
    cos_n, sin_n = table(cos_ref, blk_n), table(sin_ref, blk_n)

    rows, band = A_GROUP * A_BLOCK, 3 * A_BLOCK
    qpos = n * A_BLOCK + lax.broadcasted_iota(jnp.int32, (rows, band), 0) % A_BLOCK
    kpos = (n - 1) * A_BLOCK + lax.broadcasted_iota(jnp.int32, (rows, band), 1)
    valid = (jnp.abs(qpos - kpos) <= WINDOW) & (kpos >= 0) & (kpos < S)

    work = []
    for kvh in range(A_KV_HEADS):
        ks = slice(kvh * A_HEAD_DIM, (kvh + 1) * A_HEAD_DIM)
        kb = jnp.concatenate([_rope(kp_ref[:, ks].astype(F32), cos_p, sin_p),
                              _rope(kc_ref[:, ks].astype(F32), cos_c, sin_c),
                              _rope(kn_ref[:, ks].astype(F32), cos_n, sin_n)], axis=0).astype(BF16)
        heads = [kvh * A_GROUP + g for g in range(A_GROUP)]
        qs = [q_ref[:, h * A_HEAD_DIM:(h + 1) * A_HEAD_DIM] for h in heads]
        q_rope = jnp.concatenate([_rope(q.astype(F32), cos_c, sin_c) * scale for q in qs], axis=0).astype(BF16)
        q_plain = jnp.concatenate([q.astype(F32) * scale for q in qs], axis=0).astype(BF16)
        sink = jnp.concatenate([jnp.full((A_BLOCK, 1), sink_ref[h], F32) for h in heads], axis=0)
        s_loc = jnp.where(valid, _dot(q_rope, kb, NT), MASK_VALUE)
        s_ctx = _dot(q_plain, kx_ref[:, ks], NT)
        work.append(dict(ks=ks, heads=heads, sink=sink, s_loc=s_loc, s_ctx=s_ctx))
    for w in work:
        w['m'] = jnp.maximum(jnp.maximum(jnp.max(w['s_loc'], -1, keepdims=True),
                                         jnp.max(w['s_ctx'], -1, keepdims=True)), w['sink'])
    for w in work:
        w['p_loc'] = jnp.exp(w.pop('s_loc') - w['m'])
        w['p_ctx'] = jnp.exp(w.pop('s_ctx') - w['m'])
    for w in work:
        w['den'] = (jnp.sum(w['p_loc'], -1, keepdims=True) + jnp.sum(w['p_ctx'], -1, keepdims=True)
                    + jnp.exp(w['sink'] - w['m']))
    for w in work:
        ks = w['ks']
        vb = jnp.concatenate([vp_ref[:, ks], vc_ref[:, ks], vn_ref[:, ks]], axis=0)
        o = (_dot(w['p_loc'].astype(BF16), vb) + _dot(w['p_ctx'].astype(BF16), vx_ref[:, ks])) / w['den']
        for g, h in enumerate(w['heads']):
            o_ref[:, h * A_HEAD_DIM:(h + 1) * A_HEAD_DIM] = o[g * A_BLOCK:(g + 1) * A_BLOCK].astype(BF16)


def _attn_latent(dm, z, sink, cos, sin):
    B, S = dm.B, dm.S
    nb = S // A_BLOCK
    base = dm.MC // A_BLOCK
    qc = dm.c_att_q // A_Q_DIM
    kc = dm.c_att_k // A_KV_DIM
    vc = dm.c_att_v // A_KV_DIM

    def rows(shift):
        return lambda b, n: base + b * nb + jnp.clip(n + shift, 0, nb - 1)

    def kv_spec(col, shift):
        r = rows(shift)
        return pl.BlockSpec((A_BLOCK, A_KV_DIM), lambda b, n: (r(b, n), col))

    return pl.pallas_call(
        functools.partial(_attn_kernel, S, nb),
        grid=(B, nb),
        in_specs=[
            pl.BlockSpec(memory_space=pltpu.SMEM),
            pl.BlockSpec((A_BLOCK, A_Q_DIM), lambda b, n: (base + b * nb + n, qc)),
            kv_spec(kc, -1), kv_spec(kc, 0), kv_spec(kc, 1),
            kv_spec(vc, -1), kv_spec(vc, 0), kv_spec(vc, 1),
            pl.BlockSpec((dm.CT, A_KV_DIM), lambda b, n: (b, kc)),
            pl.BlockSpec((dm.CT, A_KV_DIM), lambda b, n: (b, vc)),
            pl.BlockSpec((S, A_HEAD_DIM), lambda b, n: (0, 0)),
            pl.BlockSpec((S, A_HEAD_DIM), lambda b, n: (0, 0)),
        ],
        out_specs=pl.BlockSpec((A_BLOCK, A_Q_DIM), lambda b, n: (b * nb + n, 0)),
        out_shape=jax.ShapeDtypeStruct((dm.ML, A_Q_DIM), BF16),
        compiler_params=_params("parallel", "arbitrary"),
        name="attn_latent",
    )(sink, z, z, z, z, z, z, z, z, z, cos, sin)


def _attn_ctx_kernel(CT, sink_ref, q_ref, k_ref, v_ref, o_ref):
    scale = A_HEAD_DIM ** -0.5
    for kvh in range(A_KV_HEADS):
        ks = slice(kvh * A_HEAD_DIM, (kvh + 1) * A_HEAD_DIM)
        k = k_ref[:, ks]
        v = v_ref[:, ks]
        heads = [kvh * A_GROUP + g for g in range(A_GROUP)]
        q = jnp.concatenate([q_ref[:, h * A_HEAD_DIM:(h + 1) * A_HEAD_DIM] for h in heads], axis=0)
        sink = jnp.concatenate([jnp.full((CT, 1), sink_ref[h], F32) for h in heads], axis=0)
        s = _dot(q, k, NT) * scale
        m = jnp.maximum(jnp.max(s, -1, keepdims=True), sink)
        p = jnp.exp(s - m)
        den = jnp.sum(p, -1, keepdims=True) + jnp.exp(sink - m)
        o = _dot(p.astype(BF16), v) / den
        for g, h in enumerate(heads):
            o_ref[:, h * A_HEAD_DIM:(h + 1) * A_HEAD_DIM] = o[g * CT:(g + 1) * CT].astype(BF16)


def _attn_ctx(dm, z, sink):
    CT = dm.CT
    return pl.pallas_call(
        functools.partial(_attn_ctx_kernel, CT),
        grid=(dm.B,),
        in_specs=[
            pl.BlockSpec(memory_space=pltpu.SMEM),
            pl.BlockSpec((CT, A_Q_DIM), lambda b: (b, dm.c_att_q // A_Q_DIM)),
            pl.BlockSpec((CT, A_KV_DIM), lambda b: (b, dm.c_att_k // A_KV_DIM)),
            pl.BlockSpec((CT, A_KV_DIM), lambda b: (b, dm.c_att_v // A_KV_DIM)),
        ],
        out_specs=pl.BlockSpec((CT, A_Q_DIM), lambda b: (b, 0)),
        out_shape=jax.ShapeDtypeStruct((dm.MC, A_Q_DIM), BF16),
        compiler_params=_params("parallel"),
        name="attn_ctx",
    )(sink, z, z, z)


def _head_sum(x, e1_ref, e2_ref):
    def split_dot(a, ind):
        hi = a.astype(BF16)
        lo = (a - hi.astype(F32)).astype(BF16)
        return _dot(hi, ind) + _dot(lo, ind)

    return split_dot(split_dot(x, e1_ref[...]), e2_ref[...])


def _rwkv_prep_kernel(dm, z_ref, zp_ref, zn_ref, cw_ref, g2_ref, w0_ref, w2_ref, a0_ref, a2_ref, kk_ref, ka_ref,
                      rk_ref, e1_ref, e2_ref,
                      r_out, v_out, kk_out, lw_out, kd_out, b_out, bonus_out, g_out):
    i = pl.program_id(0)
    tm, R = dm.tmr, dm.R
    nct = dm.MC // tm
    start = jnp.where(i < nct, (i * tm) % dm.CT, ((i - nct) * tm) % dm.S)
    seqlen = jnp.where(i < nct, dm.CT, dm.S)
    has_prev = (start != 0).astype(F32)
    has_next = (start + tm != seqlen).astype(F32)

    z = z_ref[...].astype(F32)
    rowid = lax.broadcasted_iota(jnp.int32, (tm, 1), 0)
    halo_prev = zp_ref[HALO - 1:HALO, :].astype(F32) * has_prev
    halo_next = zn_ref[0:1, :].astype(F32) * has_next
    z_prev = jnp.where(rowid == 0, halo_prev, pltpu.roll(z, 1, 0))
    z_next = jnp.where(rowid == tm - 1, halo_next, pltpu.roll(z, tm - 1, 0))
    zc = z_prev * cw_ref[0:1, :] + z * cw_ref[1:2, :] + z_next * cw_ref[2:3, :]

    r = zc[:, 0:R]
    k = zc[:, R:2 * R]
    v = zc[:, 2 * R:3 * R]
    c0 = 3 * R
    gc = zc[:, c0:c0 + GATE_LORA]
    c0 += GATE_LORA
    g_out[...] = _dot(_sigmoid(gc).astype(BF16), g2_ref[...])

    kkr = k * kk_ref[...]
    norm = jnp.sqrt(_head_sum(kkr * kkr, e1_ref, e2_ref))
    kk = kkr / jnp.maximum(norm, 1e-12)
    r_out[...] = r
    v_out[...] = v
    kk_out[...] = kk

    bonus = jnp.zeros_like(v)
    for d in range(2):
        wc = zc[:, c0 + d * LORA_PAD:c0 + (d + 1) * LORA_PAD]
        ac = zc[:, c0 + (2 + d) * LORA_PAD:c0 + (3 + d) * LORA_PAD]
        wl = w0_ref[d:d + 1, :] + _dot(jnp.tanh(wc).astype(BF16), w2_ref[d])
        w = _log_sigmoid(wl) - 0.5
        lw_out[d] = -jnp.exp(w)
        a = _sigmoid(a0_ref[d:d + 1, :] + _dot(ac.astype(BF16), a2_ref[d]))
        kd = k * (1.0 + (a - 1.0) * ka_ref[...])
        kd_out[d] = kd
        b_out[d] = kk * a
        bonus = bonus + _head_sum(r * kd * rk_ref[...], e1_ref, e2_ref) * v
    bonus_out[...] = bonus


def _rwkv_prep(dm, z, p):
    tm, R, M, W = dm.tmr, dm.R, dm.M, dm.rw_cols
    n_halo = M // HALO
    row = pl.BlockSpec((tm, R), lambda i: (i, 0))
    row2 = pl.BlockSpec((2, tm, R), lambda i: (0, i, 0))

    def full(a):
        return pl.BlockSpec(a.shape, lambda i: (0,) * a.ndim)

    consts = [p['conv'], p['g2'], p['w0'], p['w2'], p['a0'], p['a2'], p['k_k'], p['k_a'], p['r_k'], p['e1'], p['e2']]
    one = jax.ShapeDtypeStruct((M, R), F32)
    two = jax.ShapeDtypeStruct((2, M, R), F32)
    return pl.pallas_call(
        functools.partial(_rwkv_prep_kernel, dm),
        grid=(M // tm,),
        in_specs=[
            pl.BlockSpec((tm, W), lambda i: (i, 0)),
            pl.BlockSpec((HALO, W), lambda i: (jnp.maximum(i * (tm // HALO) - 1, 0), 0)),
            pl.BlockSpec((HALO, W), lambda i: (jnp.minimum((i + 1) * (tm // HALO), n_halo - 1), 0)),
        ] + [full(a) for a in consts],
        out_specs=[row, row, row, row2, row2, row2, row, row],
        out_shape=[one, one, one, two, two, two, one, one],
        compiler_params=_params("parallel"),
        name="rwkv_prep",
    )(z, z, z, *consts)


R_SLAB = 4 * R_HEAD_DIM


def _block_diag(x, mask):
    return jnp.where(mask, jnp.concatenate([x] * 4, axis=0), 0.0).astype(BF16)


def _diag_blocks(full, mask):
    fm = jnp.where(mask, full, 0.0)
    n = R_HEAD_DIM
    return fm[0:n] + fm[n:2 * n] + fm[2 * n:3 * n] + fm[3 * n:4 * n]


def _slab_masks():
    n, g = R_HEAD_DIM, R_SLAB
    blk = (lax.broadcasted_iota(jnp.int32, (g, g), 0) // n) == (lax.broadcasted_iota(jnp.int32, (g, g), 1) // n)
    t_idx = lax.broadcasted_iota(jnp.int32, (n, g), 0)
    s_idx = lax.broadcasted_iota(jnp.int32, (n, g), 1) % n
    return blk, t_idx, s_idx


def _rwkv_chunk_kernel(n_sub, r_ref, v_ref, kk_ref, lw_ref, kd_ref, b_ref, g_out, h_out, q_out, y0_out):
    C = R_CHUNK
    blk, t_idx, s_idx = _slab_masks()
    eye4 = (t_idx == s_idx).astype(F32)
    ti = lax.broadcasted_iota(jnp.int32, (C, C), 0)
    si = lax.broadcasted_iota(jnp.int32, (C, C), 1)

    def bd(x):
        return _block_diag(x, blk)

    def setup(c, d):
        rs = pl.ds(c * C, C)
        r, v, kk = r_ref[rs, :], v_ref[rs, :], kk_ref[rs, :]
        tri = ((ti >= si) if d == 0 else (ti <= si)).astype(BF16)
        lw = lw_ref[d, rs, :]
        cl = _cumsum_dot(tri, lw)
        tot = jnp.sum(lw, axis=0, keepdims=True)
        e_neg = jnp.exp(-cl)
        e_end = jnp.exp(tot - cl)
        kd, b = kd_ref[d, rs, :], b_ref[d, rs, :]
        ch = dict(rs=rs, d=d, tot=tot, v_b=v.astype(BF16), v_bd=bd(v),
                  incl=(t_idx >= s_idx) if d == 0 else (t_idx <= s_idx),
                  strict=(t_idx > s_idx) if d == 0 else (t_idx < s_idx),
                  kkm=kk * jnp.exp(cl - lw), rp=r * jnp.exp(cl),
                  bp=b * e_neg, kp=kd * e_neg, be=(b * e_end).astype(BF16), ke=(kd * e_end).astype(BF16))
        ch['lhs'] = jnp.concatenate([ch['kkm'], ch['rp']], axis=0).astype(BF16)
        return ch

    chains = [setup(c, d) for c in range(n_sub) for d in range(2)]
    for ch in chains:
        gram_b = _dot(ch['lhs'], bd(ch.pop('bp')), NT)
        gram_k = _dot(ch.pop('lhs'), bd(ch.pop('kp')), NT)
        lb = jnp.where(ch['strict'], gram_b[:C], 0.0)
        ch['mb'] = jnp.where(ch['incl'], gram_b[C:], 0.0).astype(BF16)
        ch['lmk'] = jnp.concatenate([jnp.where(ch['strict'], gram_k[:C], 0.0),
                                     jnp.where(ch['incl'], gram_k[C:], 0.0)], axis=0).astype(BF16)
        ch['pw'] = -lb
        ch['t_inv'] = eye4 - lb
    for ch in chains:
        ch['pw'] = _dot(ch['pw'].astype(BF16), bd(ch['pw']))
        lmkv = _dot(ch.pop('lmk'), ch.pop('v_bd'))
        ch['lkv'], ch['mkv'] = lmkv[:C], lmkv[C:]
    for _ in range(int(math.log2(C)) - 2):
        for ch in chains:
            both = _dot(jnp.concatenate([ch['pw'], ch['t_inv']], axis=0).astype(BF16), bd(ch['pw']))
            ch['pw'], ch['t_inv'] = both[:C], ch['t_inv'] + both[C:]
    for ch in chains:
        ch['t_inv'] = (ch['t_inv'] + _dot(ch['t_inv'].astype(BF16), bd(ch.pop('pw')))).astype(BF16)
    for ch in chains:
        ch['a_m'] = _dot(ch['t_inv'], bd(ch.pop('kkm')))
        ch['d_m'] = -_dot(ch.pop('t_inv'), bd(ch.pop('lkv')))
    for ch in chains:
        d, rs = ch['d'], ch['rs']
        a_m, d_m = ch['a_m'], ch['d_m']
        q_out[d, rs, :] = (ch['rp'] - _dot(ch['mb'], bd(a_m))).astype(BF16)
        y0_out[d, rs, :] = _dot(ch['mb'], bd(d_m)) + ch['mkv']
        g_cross = _dot(ch['be'], a_m.astype(BF16), TN)
        h_cross = _dot(jnp.concatenate([ch['be'], ch['ke']], axis=0),
                       jnp.concatenate([d_m.astype(BF16), ch['v_b']], axis=0), TN)
        g_out[d, rs, :] = (eye4 * jnp.exp(ch['tot']) - _diag_blocks(g_cross, blk)).astype(BF16)
        h_out[d, rs, :] = _diag_blocks(h_cross, blk)


def _rwkv_chunk(dm, r, v, kk, lw, kd, b):
    M, R = dm.M, dm.R
    rows = dm.tmr
    one = pl.BlockSpec((rows, R_SLAB), lambda i, h: (i, h))
    two = pl.BlockSpec((2, rows, R_SLAB), lambda i, h: (0, i, h))
    f32 = jax.ShapeDtypeStruct((2, M, R), F32)
    b16 = jax.ShapeDtypeStruct((2, M, R), BF16)
    return pl.pallas_call(
        functools.partial(_rwkv_chunk_kernel, rows // R_CHUNK),
        grid=(M // rows, R // R_SLAB),
        in_specs=[one, one, one, two, two, two],
        out_specs=[two, two, two, two],
        out_shape=[b16, f32, b16, f32],
        compiler_params=_params("parallel", "parallel"),
        name="rwkv_chunk",
    )(r, v, kk, lw, kd, b)


def _rwkv_scan_kernel(n_slab, n_in, gf_ref, hf_ref, qf_ref, y0f_ref, gb_ref, hb_ref, qb_ref, y0b_ref,
                      yf_ref, yb_ref, st_ref):
    C = R_CHUNK
    dirs = ((gf_ref, hf_ref, qf_ref, y0f_ref, yf_ref), (gb_ref, hb_ref, qb_ref, y0b_ref, yb_ref))

    @pl.when(pl.program_id(1) == 0)
    def _():
        st_ref[...] = jnp.zeros_like(st_ref)

    blk, _, _ = _slab_masks()
    lanes = [slice(sl * R_SLAB, (sl + 1) * R_SLAB) for sl in range(n_slab)]
    for t in range(n_in):
        todo = [(d, refs, pl.ds((t, n_in - 1 - t)[d] * C, C), ls) for d, refs in enumerate(dirs) for ls in lanes]
        st_bd = [_block_diag(st_ref[d, :, ls], blk) for d, _, _, ls in todo]
        both = [_dot(jnp.concatenate([refs[2][rs, ls], refs[0][rs, ls]], axis=0), bd)
                for (d, refs, rs, ls), bd in zip(todo, st_bd)]
        for (d, refs, rs, ls), res in zip(todo, both):
            refs[4][rs, ls] = res[:C] + refs[3][rs, ls]
            st_ref[d, :, ls] = res[C:] + refs[1][rs, ls]


def _rwkv_scan(dm, g_m, h_m, q_m, y0_m):
    R, rows = dm.R, dm.tmr
    steps = (dm.CT + dm.S) // rows

    def spec(d):
        return pl.BlockSpec((None, rows, R), lambda b, s: (d, dm.chunk_block(b, d, s, rows), 0))

    def yspec(d):
        return pl.BlockSpec((rows, R), lambda b, s: (dm.chunk_block(b, d, s, rows), 0))

    y = jax.ShapeDtypeStruct((dm.M, R), F32)
    return pl.pallas_call(
        functools.partial(_rwkv_scan_kernel, R // R_SLAB, rows // R_CHUNK),
        grid=(dm.B, steps),
        in_specs=[spec(0)] * 4 + [spec(1)] * 4,
        out_specs=[yspec(0), yspec(1)],
        out_shape=[y, y],
        scratch_shapes=[pltpu.VMEM((2, R_HEAD_DIM, R), F32)],
        compiler_params=_params("parallel", "arbitrary"),
        name="rwkv_scan",
    )(g_m, h_m, q_m, y0_m, g_m, h_m, q_m, y0_m)


def _rwkv_out_kernel(yf_ref, yb_ref, bonus_ref, g_ref, lnw_ref, lnb_ref, e1_ref, e2_ref, o_ref):
    o = yf_ref[...] + yb_ref[...]
    inv_n = 1.0 / R_HEAD_DIM
    mu = _head_sum(o, e1_ref, e2_ref) * inv_n
    oc = o - mu
    var = _head_sum(oc * oc, e1_ref, e2_ref) * inv_n
    o = oc * lax.rsqrt(var + R_GN_EPS) * lnw_ref[...] + lnb_ref[...]
    o_ref[...] = ((o + bonus_ref[...]) * g_ref[...]).astype(BF16)


def _rwkv_out(dm, y_f, y_b, bonus, g, p):
    tm, R, M = dm.tmr, dm.R, dm.M
    row = pl.BlockSpec((tm, R), lambda i: (i, 0))

    def full(a):
        return pl.BlockSpec(a.shape, lambda i: (0,) * a.ndim)

    consts = [p['ln_w'], p['ln_b'], p['e1'], p['e2']]
    return pl.pallas_call(
        _rwkv_out_kernel,
        grid=(M // tm,),
        in_specs=[row, row, row, row] + [full(a) for a in consts],
        out_specs=row,
        out_shape=jax.ShapeDtypeStruct((M, R), BF16),
        compiler_params=_params("parallel"),
        name="rwkv_out",
    )(y_f, y_b, bonus, g, *consts)


def _gla_kernel(dm, n_in, qf_ref, kf_ref, vf_ref, af_ref, qb_ref, kb_ref, vb_ref, ab_ref, wa_ref, ba_ref,
                yf_ref, yb_ref, st_ref, qd_ref, ki_ref, ke_ref):
    C = G_CHUNK
    dk, dv = dm.GK // G_HEADS, dm.GV // G_HEADS
    key = [slice(h * dk, (h + 1) * dk) for h in range(G_HEADS)]
    val = [slice(h * dv, (h + 1) * dv) for h in range(G_HEADS)]
    row = [slice(c * C, (c + 1) * C) for c in range(n_in)]
    dirs = ((qf_ref, kf_ref, vf_ref, af_ref, yf_ref), (qb_ref, kb_ref, vb_ref, ab_ref, yb_ref))

    @pl.when(pl.program_id(1) == 0)
    def _():
        st_ref[...] = jnp.zeros_like(st_ref)

    ti = lax.broadcasted_iota(jnp.int32, (C, C), 0)
    si = lax.broadcasted_iota(jnp.int32, (C, C), 1)
    causal = (ti >= si, ti <= si)

    la = [_log_sigmoid(_dot(refs[3][...], wa_ref[d, 0]) + _dot(refs[3][...], wa_ref[d, 1]) + ba_ref[d])
          * (1.0 / G_GATE_NORM) for d, refs in enumerate(dirs)]
    cb = [[_cumsum_dot(causal[d].astype(BF16), la[d][r]) for r in row] for d in range(2)]
    e_last = []
    for d, refs in enumerate(dirs):
        cum = jnp.concatenate(cb[d], axis=0)
        tots = [jnp.sum(la[d][r], axis=0, keepdims=True) for r in row]
        tot = jnp.concatenate([jnp.broadcast_to(t, (C, t.shape[1])) for t in tots], axis=0)
        k = refs[1][...].astype(F32)
        qd_ref[d] = (refs[0][...].astype(F32) * (dk ** -0.5) * jnp.exp(cum)).astype(BF16)
        ki_ref[d] = (k * jnp.exp(-cum)).astype(BF16)
        ke_ref[d] = (k * jnp.exp(tot - cum)).astype(BF16)
        e_last.append([jnp.exp(t) for t in tots])
    for d, refs in enumerate(dirs):
        att = [[jnp.where(causal[d], _dot(qd_ref[d, r, ks], ki_ref[d, r, ks], NT), 0.0).astype(BF16) for ks in key]
               for r in row]
        for c, r in enumerate(row):
            for h in range(G_HEADS):
                refs[4][r, val[h]] = _dot(att[c][h], refs[2][r, val[h]])
    for t in range(n_in):
        todo = [(d, refs, (t, n_in - 1 - t)[d], h) for d, refs in enumerate(dirs) for h in range(G_HEADS)]
        inter = [_dot(qd_ref[d, row[c], key[h]], st_ref[d, h].astype(BF16), NT) for d, _, c, h in todo]
        upd = [_dot(refs[2][row[c], val[h]], ke_ref[d, row[c], key[h]], TN) for d, refs, c, h in todo]
        for (d, refs, c, h), o in zip(todo, inter):
            refs[4][row[c], val[h]] += o
        for (d, refs, c, h), u in zip(todo, upd):
            st_ref[d, h] = st_ref[d, h] * e_last[d][c][:, key[h]] + u


def _gla(dm, z, wa_split, ba):
    rows = dm.tmr
    steps = (dm.CT + dm.S) // rows
    GK, GV = dm.GK, dm.GV

    def zspec(width, col, d):
        return pl.BlockSpec((rows, width), lambda b, s: (dm.chunk_block(b, d, s, rows), col // width))

    def zspecs(d):
        return [zspec(GK, dm.c_gla_q, d), zspec(GK, dm.c_gla_k, d), zspec(GV, dm.c_gla_v, d),
                zspec(LORA_PAD, dm.c_gla_ac, d)]

    def yspec(d):
        return pl.BlockSpec((rows, GV), lambda b, s: (dm.chunk_block(b, d, s, rows), 0))

    y = jax.ShapeDtypeStruct((dm.M, GV), F32)
    return pl.pallas_call(
        functools.partial(_gla_kernel, dm, rows // G_CHUNK),
        grid=(dm.B, steps),
        in_specs=zspecs(0) + zspecs(1) + [
            pl.BlockSpec((2, 2, LORA_PAD, GK), lambda b, s: (0, 0, 0, 0)),
            pl.BlockSpec((2, 1, GK), lambda b, s: (0, 0, 0)),
        ],
        out_specs=[yspec(0), yspec(1)],
        out_shape=[y, y],
        scratch_shapes=[pltpu.VMEM((2, G_HEADS, GV // G_HEADS, GK // G_HEADS), F32),
                        pltpu.VMEM((2, rows, GK), BF16), pltpu.VMEM((2, rows, GK), BF16),
                        pltpu.VMEM((2, rows, GK), BF16)],
        compiler_params=_params("parallel", "arbitrary"),
        name="gla",
    )(z, z, z, z, z, z, z, z, wa_split, ba)


def _gla_out_kernel(dv, yf_ref, yb_ref, g_ref, nw_ref, o_ref):
    o = yf_ref[...] + yb_ref[...]
    g = g_ref[...].astype(F32)
    gate = g * _sigmoid(g)
    for h in range(G_HEADS):
        vs = slice(h * dv, (h + 1) * dv)
        oh = o[:, vs]
        oh = oh * lax.rsqrt(jnp.mean(oh * oh, -1, keepdims=True) + G_EPS) * nw_ref[...]
        o_ref[:, vs] = (oh * gate[:, vs]).astype(BF16)


def _gla_out(dm, y_f, y_b, z, norm_w):
    tm, GV, M = dm.tmr, dm.GV, dm.M
    dv = GV // G_HEADS
    return pl.pallas_call(
        functools.partial(_gla_out_kernel, dv),
        grid=(M // tm,),
        in_specs=[
            pl.BlockSpec((tm, GV), lambda i: (i, 0)),
            pl.BlockSpec((tm, GV), lambda i: (i, 0)),
            pl.BlockSpec((tm, GV), lambda i: (i, dm.c_gla_g // GV)),
            pl.BlockSpec((1, dv), lambda i: (0, 0)),
        ],
        out_specs=pl.BlockSpec((tm, GV), lambda i: (i, 0)),
        out_shape=jax.ShapeDtypeStruct((M, GV), BF16),
        compiler_params=_params("parallel"),
        name="gla_out",
    )(y_f, y_b, z, norm_w.reshape(1, dv))


def _merge_kernel(alpha, x_ref, mg_ref, oa_ref, or_ref, og_ref, ga_ref, gr_ref, gg_ref, wa_ref, wr_ref, wg_ref,
                  wo_ref, g_ref, b_ref, o_ref, acc_ref):
    n = pl.program_id(1)

    @pl.when(n == 0)
    def _():
        acc_ref[...] = jnp.zeros_like(acc_ref)

    merged = (_sigmoid(ga_ref[...].astype(F32)) * _dot(oa_ref[...], wa_ref[...])
              + _sigmoid(gr_ref[...].astype(F32)) * _dot(or_ref[...], wr_ref[...])
              + _sigmoid(gg_ref[...].astype(F32)) * _dot(og_ref[...], wg_ref[...]))
    acc_ref[...] += _dot(merged.astype(BF16), wo_ref[...])

    @pl.when(n == pl.num_programs(1) - 1)
    def _():
        z = alpha * x_ref[...] + mg_ref[...] * acc_ref[...]
        o_ref[...] = _ln(z) * g_ref[...] + b_ref[...]


def _merge(dm, alpha, x, mods, z, o_att, o_rwkv, o_gla, wba, wbr, wbg, w_out, l, g, b, lat_only):
    D, tm, tn = dm.D, dm.tm, 512
    gate0 = dm.c_gate // tn
    per = D // tn
    off = dm.MC // tm if lat_only else 0
    rows = dm.M - off * tm

    def gate_spec(j):
        return pl.BlockSpec((tm, tn), lambda i, n: (i + off, gate0 + j * per + n))

    def resident(width):
        return pl.BlockSpec((tm, width), lambda i, n: (i + off, 0))

    def wcol(rows_w):
        return pl.BlockSpec((None, rows_w, tn), lambda i, n: (l, 0, n))

    return pl.pallas_call(
        functools.partial(_merge_kernel, alpha),
        grid=(rows // tm, per),
        in_specs=[
            resident(D), dm.mod_spec(5, tm, off),
            resident(A_Q_DIM), resident(dm.R), resident(dm.GV),
            gate_spec(0), gate_spec(1), gate_spec(2),
            wcol(A_Q_DIM), wcol(dm.R), wcol(dm.GV),
            pl.BlockSpec((None, tn, D), lambda i, n: (l, n, 0)),
            pl.BlockSpec((1, D), lambda i, n: (0, 0)),
            pl.BlockSpec((1, D), lambda i, n: (0, 0)),
        ],
        out_specs=pl.BlockSpec((tm, D), lambda i, n: (i, 0)),
        out_shape=jax.ShapeDtypeStruct((rows, D), F32),
        scratch_shapes=[pltpu.VMEM((tm, D), F32)],
        compiler_params=_params("parallel", "arbitrary"),
        name="merge",
    )(x, mods, o_att, o_rwkv, o_gla, z, z, z, wba, wbr, wbg, w_out, g.reshape(1, D), b.reshape(1, D))


def _pad_to(a, size, axis):
    pad = [(0, 0)] * a.ndim
    pad[axis] = (0, size - a.shape[axis])
    return jnp.pad(a, pad)


def _rwkv_cols(a, R):
    c = 3 * R
    rkv = a[..., :c]
    wc0, wc1 = a[..., c:c + DECAY_LORA], a[..., c + DECAY_LORA:c + 2 * DECAY_LORA]
    c += 2 * DECAY_LORA
    ac0, ac1 = a[..., c:c + ICLR_LORA], a[..., c + ICLR_LORA:c + 2 * ICLR_LORA]
    c += 2 * ICLR_LORA
    gc = a[..., c:c + GATE_LORA]
    ax = a.ndim - 1
    return jnp.concatenate([rkv, gc] + [_pad_to(t, LORA_PAD, ax) for t in (wc0, wc1, ac0, ac1)], axis=-1)


def _in_proj_weights(dm, w_in):
    D, R, GK, GV = dm.D, dm.R, dm.GK, dm.GV
    att_cols = A_Q_DIM + 2 * A_KV_DIM
    rw_src = 3 * R + 2 * DECAY_LORA + 2 * ICLR_LORA + GATE_LORA
    gla_src = 2 * GK + 2 * GV + 2 * G_GATE_RANK
    att, rwkv, gla, gate = jnp.split(w_in, np.cumsum([att_cols, rw_src, gla_src]).tolist(), axis=-1)
    a_q, a_k, a_v = att[:, :A_Q_DIM], att[:, A_Q_DIM:A_Q_DIM + A_KV_DIM], att[:, A_Q_DIM + A_KV_DIM:]
    g_q, g_k = gla[:, :GK], gla[:, GK:2 * GK]
    g_v, g_g = gla[:, 2 * GK:2 * GK + GV], gla[:, 2 * GK + GV:2 * GK + 2 * GV]
    g_ac = _pad_to(gla[:, 2 * GK + 2 * GV:], LORA_PAD, 1)
    w = jnp.concatenate([_rwkv_cols(rwkv, R), a_k, gate, g_v, g_g, g_q, g_k, a_q, a_v, g_ac], axis=-1)
    return _pad_to(w, dm.NP, 1).astype(BF16)


def _rope_tables(S):
    rows = S // GRID_W
    row = jnp.repeat(jnp.arange(rows, dtype=F32), GRID_W)
    col = jnp.tile(jnp.arange(GRID_W, dtype=F32), rows)
    n_freq = A_HEAD_DIM // 4
    inv_freq = ROPE_BASE ** (-jnp.arange(n_freq, dtype=F32) / n_freq)
    ang_r, ang_c = row[:, None] * inv_freq, col[:, None] * inv_freq
    cos = jnp.concatenate([jnp.cos(ang_r)] * 2 + [jnp.cos(ang_c)] * 2, axis=-1)
    sin = jnp.concatenate([-jnp.sin(ang_r), jnp.sin(ang_r), -jnp.sin(ang_c), jnp.sin(ang_c)], axis=-1)
    return cos, sin


def kernel(x, c, ctx, c_ctx, ada_w, ada_b, ln_g, ln_b, ffn_w_gu, ffn_w_down, w_in, attn_sink, rwkv_conv, rwkv_w0,
           rwkv_w2, rwkv_a0, rwkv_a2, rwkv_g2, rwkv_k_k, rwkv_k_a, rwkv_r_k, rwkv_ln_w, rwkv_ln_b, gla_wa2, gla_ba,
           gla_norm_w, w_branch_att, w_branch_rwkv, w_branch_gla, w_out):
    B, S, D = x.shape
    CT = ctx.shape[1]
    depth = ada_w.shape[0]
    dm = _Dims(B, S, CT, D)
    R = dm.R
    alpha = float((2 * depth) ** 0.25)
    assert B + 1 <= MOD_ROWS

    cc = _pad_to(jnp.concatenate([c, c_ctx[None, :]], axis=0), MOD_ROWS, 0)
    mods_all = _ada(cc, ada_w, ada_b)
    cos, sin = _rope_tables(S)
    head_of_lane = jnp.arange(R) // R_HEAD_DIM
    e1 = (head_of_lane[:, None] == jnp.arange(LORA_PAD)[None, :]).astype(BF16)
    e2 = e1.T

    w_gu, w_dn = ffn_w_gu.astype(BF16), ffn_w_down.astype(BF16)
    wba, wbr = w_branch_att.astype(BF16), w_branch_rwkv.astype(BF16)
    wbg, wo = w_branch_gla.astype(BF16), w_out.astype(BF16)

    xs = jnp.concatenate([ctx.reshape(dm.MC, D), x.reshape(dm.ML, D)], axis=0)
    for l in range(depth):
        last = l == depth - 1
        mods = mods_all[l].reshape(MOD_ROWS * N_MOD, 1, D)
        xs = _ffn(dm, alpha, xs, mods, 0, w_gu, w_dn, l, 0, ln_g[l, 0], ln_b[l, 0])

        z = _inproj(dm, xs, mods, _in_proj_weights(dm, w_in[l]))
        o_att = jnp.concatenate([_attn_ctx(dm, z, attn_sink[l]), _attn_latent(dm, z, attn_sink[l], cos, sin)], axis=0)

        rp = {
            'conv': _rwkv_cols(rwkv_conv[l], R),
            'g2': rwkv_g2[l].astype(BF16),
            'w0': rwkv_w0[l], 'a0': rwkv_a0[l],
            'w2': _pad_to(rwkv_w2[l], LORA_PAD, 1).astype(BF16),
            'a2': _pad_to(rwkv_a2[l], LORA_PAD, 1).astype(BF16),
            'k_k': rwkv_k_k[l].reshape(1, R), 'k_a': rwkv_k_a[l].reshape(1, R), 'r_k': rwkv_r_k[l].reshape(1, R),
            'ln_w': rwkv_ln_w[l].reshape(1, R), 'ln_b': rwkv_ln_b[l].reshape(1, R),
            'e1': e1, 'e2': e2,
        }
        r, v, kk, lw, kd, b, bonus, g = _rwkv_prep(dm, z, rp)
        g_m, h_m, q_m, y0_m = _rwkv_chunk(dm, r, v, kk, lw, kd, b)
        y_rwkv_f, y_rwkv_b = _rwkv_scan(dm, g_m, h_m, q_m, y0_m)
        o_rwkv = _rwkv_out(dm, y_rwkv_f, y_rwkv_b, bonus, g, rp)

        wa_p = jnp.stack([_pad_to(jnp.pad(gla_wa2[l, d], ((d * G_GATE_RANK, 0), (0, 0))), LORA_PAD, 0)
                          for d in range(2)])
        wa_hi = wa_p.astype(BF16)
        wa_split = jnp.stack([wa_hi, (wa_p - wa_hi.astype(F32)).astype(BF16)], axis=1)
        y_gla_f, y_gla_b = _gla(dm, z, wa_split, gla_ba[l].reshape(2, 1, dm.GK))
        o_gla = _gla_out(dm, y_gla_f, y_gla_b, z, gla_norm_w[l])

        xs = _merge(dm, alpha, xs, mods, z, o_att, o_rwkv, o_gla, wba, wbr, wbg, wo, l, ln_g[l, 1], ln_b[l, 1], last)
        xs = _ffn(dm, alpha, xs, mods, 6, w_gu, w_dn, l, 1, ln_g[l, 2], ln_b[l, 2])
    return xs.reshape(B, S, D)
```

```python
import functools
import math

import jax
import jax.numpy as jnp
import numpy as np
from jax import lax
from jax.experimental import pallas as pl
from jax.experimental.pallas import tpu as pltpu

F32 = jnp.float32
BF16 = jnp.bfloat16
HI = lax.Precision.HIGHEST

N_MOD = 9
D_FF = 5632
LN_EPS = 1e-5
FFN_RES = 0.5

A_HEADS = 8
A_KV_HEADS = 2
A_GROUP = A_HEADS // A_KV_HEADS
A_HEAD_DIM = 128
A_Q_DIM = A_HEADS * A_HEAD_DIM
A_KV_DIM = A_KV_HEADS * A_HEAD_DIM
WINDOW = 128
A_BLOCK = 128
GRID_W = 64
ROPE_BASE = 10000.0
MASK_VALUE = -1e30

R_HEAD_DIM = 64
DECAY_LORA = 96
ICLR_LORA = 96
GATE_LORA = 256
R_GN_EPS = 64e-5
R_CHUNK = 64
LORA_PAD = 128
SUBLANES = 8
HALO = 16

G_HEADS = 4
G_GATE_RANK = 16
G_GATE_NORM = 16.0
G_CHUNK = 64
G_EPS = 1e-5

MOD_ROWS = 16
VMEM_LIMIT = 58 * 1024 * 1024

NN = (((1,), (0,)), ((), ()))
NT = (((1,), (1,)), ((), ()))
TN = (((0,), (0,)), ((), ()))


def _dot(a, b, dims=NN, prec=None):
    return lax.dot_general(a, b, dims, precision=prec, preferred_element_type=F32)


def _cumsum_dot(tri, x):
    hi = x.astype(BF16)
    rest = x - hi.astype(F32)
    mid = rest.astype(BF16)
    lo = (rest - mid.astype(F32)).astype(BF16)
    return _dot(tri, hi) + _dot(tri, mid) + _dot(tri, lo)


def _sigmoid(x):
    return 1.0 / (1.0 + jnp.exp(-x))


def _log_sigmoid(x):
    return jnp.minimum(x, 0.0) - jnp.log(1.0 + jnp.exp(-jnp.abs(x)))


def _ln(x):
    mu = jnp.mean(x, -1, keepdims=True)
    xc = x - mu
    var = jnp.mean(xc * xc, -1, keepdims=True)
    return xc * lax.rsqrt(var + LN_EPS)


def _params(*sem):
    return pltpu.CompilerParams(dimension_semantics=sem, vmem_limit_bytes=VMEM_LIMIT)


class _Dims:
    def __init__(self, B, S, CT, D):
        self.B, self.S, self.CT, self.D = B, S, CT, D
        self.MC, self.ML = B * CT, B * S
        self.M = self.MC + self.ML
        self.R = D // 2
        self.GK = D // 2
        self.GV = D
        self.RH = self.R // R_HEAD_DIM
        self.rw_cols = 3 * self.R + GATE_LORA + 4 * LORA_PAD
        c = 0
        self.c_rwkv = c; c += self.rw_cols
        self.c_att_k = c; c += A_KV_DIM
        self.c_gate = c; c += 3 * D
        self.c_gla_v = c; c += self.GV
        self.c_gla_g = c; c += self.GV
        self.c_gla_q = c; c += self.GK
        self.c_gla_k = c; c += self.GK
        self.c_att_q = c; c += A_Q_DIM
        self.c_att_v = c; c += A_KV_DIM
        self.c_gla_ac = c; c += LORA_PAD
        self.tn_in = 1280
        self.NP = -(-c // self.tn_in) * self.tn_in
        self.tm = math.gcd(512, math.gcd(self.MC, S))
        self.tm_in = math.gcd(1024, math.gcd(self.MC, S))
        self.tmr = math.gcd(256, math.gcd(CT, S))

    def mod_row(self, i, tm):
        nct = self.MC // tm
        tpb = self.S // tm
        return jnp.where(i < nct, self.B, (i - nct) // tpb)

    def mod_spec(self, m, tm, off=0):
        return pl.BlockSpec((None, 1, self.D), lambda i, *_: (self.mod_row(i + off, tm) * N_MOD + m, 0, 0))

    def chunk_block(self, b, d, s, chunk):
        ncc, ncl = self.CT // chunk, self.S // chunk
        j = s - ncc
        ctx_blk = b * ncc + jnp.where(d == 0, s, ncc - 1 - s)
        lat_blk = self.B * ncc + b * ncl + jnp.where(d == 0, j, ncl - 1 - j)
        return jnp.where(s < ncc, ctx_blk, lat_blk)


def _ada_kernel(c_ref, w_ref, b_ref, o_ref):
    cc = c_ref[...]
    s = (cc * _sigmoid(cc)).astype(BF16)
    o_ref[...] = _dot(s, w_ref[...].astype(BF16)) + b_ref[...]


def _ada(cc, ada_w, ada_b):
    L, D, N = ada_w.shape
    tn = 2048
    return pl.pallas_call(
        _ada_kernel,
        grid=(L, N // tn),
        in_specs=[
            pl.BlockSpec((MOD_ROWS, D), lambda l, n: (0, 0)),
            pl.BlockSpec((None, D, tn), lambda l, n: (l, 0, n)),
            pl.BlockSpec((None, 1, tn), lambda l, n: (l, 0, n)),
        ],
        out_specs=pl.BlockSpec((None, MOD_ROWS, tn), lambda l, n: (l, 0, n)),
        out_shape=jax.ShapeDtypeStruct((L, MOD_ROWS, N), F32),
        compiler_params=_params("parallel", "arbitrary"),
        name="ada",
    )(cc, ada_w, ada_b.reshape(L, 1, N))


def _ffn_kernel(alpha, x_ref, sh_ref, sc_ref, gt_ref, wg_ref, wu_ref, wd_ref, g_ref, b_ref, o_ref, h_ref):
    f = pl.program_id(1)

    @pl.when(f == 0)
    def _():
        h = _ln(x_ref[...]) * (1.0 + sc_ref[...]) + sh_ref[...]
        h_ref[...] = h.astype(BF16)
        o_ref[...] = jnp.zeros_like(o_ref)

    h = h_ref[...]
    g = _dot(h, wg_ref[...])
    u = _dot(h, wu_ref[...])
    a = (g * _sigmoid(g) * u).astype(BF16)
    o_ref[...] += _dot(a, wd_ref[...])

    @pl.when(f == pl.num_programs(1) - 1)
    def _():
        z = alpha * x_ref[...] + (FFN_RES * gt_ref[...]) * o_ref[...]
        o_ref[...] = _ln(z) * g_ref[...] + b_ref[...]


def _ffn(dm, alpha, x, mods, m0, w_gu, w_down, l, j, g, b):
    D, tm, tf = dm.D, dm.tm, 512
    nf = D_FF // tf
    rows = x.shape[0]
    off = (dm.M - rows) // tm
    return pl.pallas_call(
        functools.partial(_ffn_kernel, alpha),
        grid=(rows // tm, nf),
        in_specs=[
            pl.BlockSpec((tm, D), lambda i, f: (i, 0)),
            dm.mod_spec(m0, tm, off), dm.mod_spec(m0 + 1, tm, off), dm.mod_spec(m0 + 2, tm, off),
            pl.BlockSpec((None, None, D, tf), lambda i, f: (l, j, 0, f)),
            pl.BlockSpec((None, None, D, tf), lambda i, f: (l, j, 0, f + nf)),
            pl.BlockSpec((None, None, tf, D), lambda i, f: (l, j, f, 0)),
            pl.BlockSpec((1, D), lambda i, f: (0, 0)),
            pl.BlockSpec((1, D), lambda i, f: (0, 0)),
        ],
        out_specs=pl.BlockSpec((tm, D), lambda i, f: (i, 0)),
        out_shape=jax.ShapeDtypeStruct((rows, D), F32),
        scratch_shapes=[pltpu.VMEM((tm, D), BF16)],
        compiler_params=_params("parallel", "arbitrary"),
        name="ffn",
    )(x, mods, mods, mods, w_gu, w_gu, w_down, g.reshape(1, D), b.reshape(1, D))


def _inproj_kernel(x_ref, sh_ref, sc_ref, w_ref, o_ref, h_ref):
    @pl.when(pl.program_id(1) == 0)
    def _():
        h = _ln(x_ref[...]) * (1.0 + sc_ref[...]) + sh_ref[...]
        h_ref[...] = h.astype(BF16)

    o_ref[...] = _dot(h_ref[...], w_ref[...]).astype(BF16)


def _inproj(dm, x, mods, w_in_p):
    D, tm, tn = dm.D, dm.tm_in, dm.tn_in
    return pl.pallas_call(
        _inproj_kernel,
        grid=(dm.M // tm, dm.NP // tn),
        in_specs=[
            pl.BlockSpec((tm, D), lambda i, n: (i, 0)),
            dm.mod_spec(3, tm), dm.mod_spec(4, tm),
            pl.BlockSpec((D, tn), lambda i, n: (0, n)),
        ],
        out_specs=pl.BlockSpec((tm, tn), lambda i, n: (i, n)),
        out_shape=jax.ShapeDtypeStruct((dm.M, dm.NP), BF16),
        scratch_shapes=[pltpu.VMEM((tm, D), BF16)],
        compiler_params=_params("parallel", "arbitrary"),
        name="inproj",
    )(x, mods, mods, w_in_p)


def _rope(x, cos, sin_signed):
    lane = lax.broadcasted_iota(jnp.int32, x.shape, 1)
    quarter = A_HEAD_DIM // 4
    swapped = jnp.where((lane % (2 * quarter)) < quarter,
                        pltpu.roll(x, A_HEAD_DIM - quarter, 1), pltpu.roll(x, quarter, 1))
    return x * cos + swapped * sin_signed


def _attn_kernel(S, nb, sink_ref, q_ref, kp_ref, kc_ref, kn_ref, vp_ref, vc_ref, vn_ref, kx_ref, vx_ref,
                 cos_ref, sin_ref, o_ref):
    n = pl.program_id(1)
    scale = A_HEAD_DIM ** -0.5

    def table(ref, blk):
        return ref[pl.ds(pl.multiple_of(blk * A_BLOCK, A_BLOCK), A_BLOCK), :]

    blk_p, blk_n = jnp.maximum(n - 1, 0), jnp.minimum(n + 1, nb - 1)
    cos_c, sin_c = table(cos_ref, n), table(sin_ref, n)
    cos_p, sin_p = table(cos_ref, blk_p), table(sin_ref, blk_p)
    cos_n, sin_n = table(cos_ref, blk_n), table(sin_ref, blk_n)

    rows, band = A_GROUP * A_BLOCK, 3 * A_BLOCK
    qpos = n * A_BLOCK + lax.broadcasted_iota(jnp.int32, (rows, band), 0) % A_BLOCK
    kpos = (n - 1) * A_BLOCK + lax.broadcasted_iota(jnp.int32, (rows, band), 1)
    valid = (jnp.abs(qpos - kpos) <= WINDOW) & (kpos >= 0) & (kpos < S)

    work = []
    for kvh in range(A_KV_HEADS):
        ks = slice(kvh * A_HEAD_DIM, (kvh + 1) * A_HEAD_DIM)
        kb = jnp.concatenate([_rope(kp_ref[:, ks].astype(F32), cos_p, sin_p),
                              _rope(kc_ref[:, ks].astype(F32), cos_c, sin_c),
                              _rope(kn_ref[:, ks].astype(F32), cos_n, sin_n)], axis=0).astype(BF16)
        heads = [kvh * A_GROUP + g for g in range(A_GROUP)]
        qs = [q_ref[:, h * A_HEAD_DIM:(h + 1) * A_HEAD_DIM] for h in heads]
        q_rope = jnp.concatenate([_rope(q.astype(F32), cos_c, sin_c) * scale for q in qs], axis=0).astype(BF16)
        q_plain = jnp.concatenate([q.astype(F32) * scale for q in qs], axis=0).astype(BF16)
        sink = jnp.concatenate([jnp.full((A_BLOCK, 1), sink_ref[h], F32) for h in heads], axis=0)
        s_loc = jnp.where(valid, _dot(q_rope, kb, NT), MASK_VALUE)
        s_ctx = _dot(q_plain, kx_ref[:, ks], NT)
        work.append(dict(ks=ks, heads=heads, sink=sink, s_loc=s_loc, s_ctx=s_ctx))
    for w in work:
        w['m'] = jnp.maximum(jnp.maximum(jnp.max(w['s_loc'], -1, keepdims=True),
                                         jnp.max(w['s_ctx'], -1, keepdims=True)), w['sink'])
    for w in work:
        w['p_loc'] = jnp.exp(w.pop('s_loc') - w['m'])
        w['p_ctx'] = jnp.exp(w.pop('s_ctx') - w['m'])
    for w in work:
        w['den'] = (jnp.sum(w['p_loc'], -1, keepdims=True) + jnp.sum(w['p_ctx'], -1, keepdims=True)
                    + jnp.exp(w['sink'] - w['m']))
    for w in work:
        ks = w['ks']
        vb = jnp.concatenate([vp_ref[:, ks], vc_ref[:, ks], vn_ref[:, ks]], axis=0)
        o = (_dot(w['p_loc'].astype(BF16), vb) + _dot(w['p_ctx'].astype(BF16), vx_ref[:, ks])) / w['den']
        for g, h in enumerate(w['heads']):
            o_ref[:, h * A_HEAD_DIM:(h + 1) * A_HEAD_DIM] = o[g * A_BLOCK:(g + 1) * A_BLOCK].astype(BF16)


def _attn_latent(dm, z, sink, cos, sin):
    B, S = dm.B, dm.S
    nb = S // A_BLOCK
    base = dm.MC // A_BLOCK
    qc = dm.c_att_q // A_Q_DIM
    kc = dm.c_att_k // A_KV_DIM
    vc = dm.c_att_v // A_KV_DIM

    def rows(shift):
        return lambda b, n: base + b * nb + jnp.clip(n + shift, 0, nb - 1)

    def kv_spec(col, shift):
        r = rows(shift)
        return pl.BlockSpec((A_BLOCK, A_KV_DIM), lambda b, n: (r(b, n), col))

    return pl.pallas_call(
        functools.partial(_attn_kernel, S, nb),
        grid=(B, nb),
        in_specs=[
            pl.BlockSpec(memory_space=pltpu.SMEM),
            pl.BlockSpec((A_BLOCK, A_Q_DIM), lambda b, n: (base + b * nb + n, qc)),
            kv_spec(kc, -1), kv_spec(kc, 0), kv_spec(kc, 1),
            kv_spec(vc, -1), kv_spec(vc, 0), kv_spec(vc, 1),
            pl.BlockSpec((dm.CT, A_KV_DIM), lambda b, n: (b, kc)),
            pl.BlockSpec((dm.CT, A_KV_DIM), lambda b, n: (b, vc)),
            pl.BlockSpec((S, A_HEAD_DIM), lambda b, n: (0, 0)),
            pl.BlockSpec((S, A_HEAD_DIM), lambda b, n: (0, 0)),
        ],
        out_specs=pl.BlockSpec((A_BLOCK, A_Q_DIM), lambda b, n: (b * nb + n, 0)),
        out_shape=jax.ShapeDtypeStruct((dm.ML, A_Q_DIM), BF16),
        compiler_params=_params("parallel", "arbitrary"),
        name="attn_latent",
    )(sink, z, z, z, z, z, z, z, z, z, cos, sin)


def _attn_ctx_kernel(CT, sink_ref, q_ref, k_ref, v_ref, o_ref):
    scale = A_HEAD_DIM ** -0.5
    for kvh in range(A_KV_HEADS):
        ks = slice(kvh * A_HEAD_DIM, (kvh + 1) * A_HEAD_DIM)
        k = k_ref[:, ks]
        v = v_ref[:, ks]
        heads = [kvh * A_GROUP + g for g in range(A_GROUP)]
        q = jnp.concatenate([q_ref[:, h * A_HEAD_DIM:(h + 1) * A_HEAD_DIM] for h in heads], axis=0)
        sink = jnp.concatenate([jnp.full((CT, 1), sink_ref[h], F32) for h in heads], axis=0)
        s = _dot(q, k, NT) * scale
        m = jnp.maximum(jnp.max(s, -1, keepdims=True), sink)
        p = jnp.exp(s - m)
        den = jnp.sum(p, -1, keepdims=True) + jnp.exp(sink - m)
        o = _dot(p.astype(BF16), v) / den
        for g, h in enumerate(heads):
            o_ref[:, h * A_HEAD_DIM:(h + 1) * A_HEAD_DIM] = o[g * CT:(g + 1) * CT].astype(BF16)


def _attn_ctx(dm, z, sink):
    CT = dm.CT
    return pl.pallas_call(
        functools.partial(_attn_ctx_kernel, CT),
        grid=(dm.B,),
        in_specs=[
            pl.BlockSpec(memory_space=pltpu.SMEM),
            pl.BlockSpec((CT, A_Q_DIM), lambda b: (b, dm.c_att_q // A_Q_DIM)),
            pl.BlockSpec((CT, A_KV_DIM), lambda b: (b, dm.c_att_k // A_KV_DIM)),
            pl.BlockSpec((CT, A_KV_DIM), lambda b: (b, dm.c_att_v // A_KV_DIM)),
        ],
        out_specs=pl.BlockSpec((CT, A_Q_DIM), lambda b: (b, 0)),
        out_shape=jax.ShapeDtypeStruct((dm.MC, A_Q_DIM), BF16),
        compiler_params=_params("parallel"),
        name="attn_ctx",
    )(sink, z, z, z)


def _head_sum(x, e1_ref, e2_ref):
    def split_dot(a, ind):
        hi = a.astype(BF16)
        lo = (a - hi.astype(F32)).astype(BF16)
        return _dot(hi, ind) + _dot(lo, ind)

    return split_dot(split_dot(x, e1_ref[...]), e2_ref[...])


def _rwkv_prep_kernel(dm, z_ref, zp_ref, zn_ref, cw_ref, g2_ref, w0_ref, w2_ref, a0_ref, a2_ref, kk_ref, ka_ref,
                      rk_ref, e1_ref, e2_ref,
                      r_out, v_out, kk_out, lw_out, kd_out, b_out, bonus_out, g_out):
    i = pl.program_id(0)
    tm, R = dm.tmr, dm.R
    nct = dm.MC // tm
    start = jnp.where(i < nct, (i * tm) % dm.CT, ((i - nct) * tm) % dm.S)
    seqlen = jnp.where(i < nct, dm.CT, dm.S)
    has_prev = (start != 0).astype(F32)
    has_next = (start + tm != seqlen).astype(F32)

    z = z_ref[...].astype(F32)
    sub = lax.broadcasted_iota(jnp.int32, (SUBLANES, 1), 0)
    halo_prev = zp_ref[HALO - 1:HALO, :].astype(F32) * has_prev
    halo_next = zn_ref[0:1, :].astype(F32) * has_next
    down, up = pltpu.roll(z, 1, 0), pltpu.roll(z, tm - 1, 0)
    z_prev = jnp.concatenate([jnp.where(sub == 0, halo_prev, down[:SUBLANES]), down[SUBLANES:]], axis=0)
    z_next = jnp.concatenate([up[:tm - SUBLANES], jnp.where(sub == SUBLANES - 1, halo_next, up[tm - SUBLANES:])],
                             axis=0)
    zc = z_prev * cw_ref[0:1, :] + z * cw_ref[1:2, :] + z_next * cw_ref[2:3, :]

    r = zc[:, 0:R]
    k = zc[:, R:2 * R]
    v = zc[:, 2 * R:3 * R]
    c0 = 3 * R
    gc = zc[:, c0:c0 + GATE_LORA]
    c0 += GATE_LORA
    g_out[...] = _dot(_sigmoid(gc).astype(BF16), g2_ref[...])

    kkr = k * kk_ref[...]
    norm = jnp.sqrt(_head_sum(kkr * kkr, e1_ref, e2_ref))
    kk = kkr / jnp.maximum(norm, 1e-12)
    r_out[...] = r
    v_out[...] = v
    kk_out[...] = kk

    bonus = jnp.zeros_like(v)
    for d in range(2):
        wc = zc[:, c0 + d * LORA_PAD:c0 + (d + 1) * LORA_PAD]
        ac = zc[:, c0 + (2 + d) * LORA_PAD:c0 + (3 + d) * LORA_PAD]
        wl = w0_ref[d:d + 1, :] + _dot(jnp.tanh(wc).astype(BF16), w2_ref[d])
        lw_out[d] = _sigmoid(wl) * (-math.exp(-0.5))
        a = _sigmoid(a0_ref[d:d + 1, :] + _dot(ac.astype(BF16), a2_ref[d]))
        kd = k * (1.0 + (a - 1.0) * ka_ref[...])
        kd_out[d] = kd
        b_out[d] = kk * a
        bonus = bonus + _head_sum(r * kd * rk_ref[...], e1_ref, e2_ref) * v
    bonus_out[...] = bonus


def _rwkv_prep(dm, z, p):
    tm, R, M, W = dm.tmr, dm.R, dm.M, dm.rw_cols
    n_halo = M // HALO
    row = pl.BlockSpec((tm, R), lambda i: (i, 0))
    row2 = pl.BlockSpec((2, tm, R), lambda i: (0, i, 0))

    def full(a):
        return pl.BlockSpec(a.shape, lambda i: (0,) * a.ndim)

    consts = [p['conv'], p['g2'], p['w0'], p['w2'], p['a0'], p['a2'], p['k_k'], p['k_a'], p['r_k'], p['e1'], p['e2']]
    one = jax.ShapeDtypeStruct((M, R), F32)
    two = jax.ShapeDtypeStruct((2, M, R), F32)
    return pl.pallas_call(
        functools.partial(_rwkv_prep_kernel, dm),
        grid=(M // tm,),
        in_specs=[
            pl.BlockSpec((tm, W), lambda i: (i, 0)),
            pl.BlockSpec((HALO, W), lambda i: (jnp.maximum(i * (tm // HALO) - 1, 0), 0)),
            pl.BlockSpec((HALO, W), lambda i: (jnp.minimum((i + 1) * (tm // HALO), n_halo - 1), 0)),
        ] + [full(a) for a in consts],
        out_specs=[row, row, row, row2, row2, row2, row, row],
        out_shape=[one, one, one, two, two, two, one, one],
        compiler_params=_params("parallel"),
        name="rwkv_prep",
    )(z, z, z, *consts)


R_SLAB = 4 * R_HEAD_DIM


def _block_diag(x, mask):
    return jnp.where(mask, jnp.concatenate([x] * 4, axis=0), 0.0).astype(BF16)


def _diag_blocks(full, mask):
    fm = jnp.where(mask, full, 0.0)
    n = R_HEAD_DIM
    return fm[0:n] + fm[n:2 * n] + fm[2 * n:3 * n] + fm[3 * n:4 * n]


def _slab_masks():
    n, g = R_HEAD_DIM, R_SLAB
    blk = (lax.broadcasted_iota(jnp.int32, (g, g), 0) // n) == (lax.broadcasted_iota(jnp.int32, (g, g), 1) // n)
    t_idx = lax.broadcasted_iota(jnp.int32, (n, g), 0)
    s_idx = lax.broadcasted_iota(jnp.int32, (n, g), 1) % n
    return blk, t_idx, s_idx


def _rwkv_chunk_kernel(n_sub, r_ref, v_ref, kk_ref, lw_ref, kd_ref, b_ref, g_out, h_out, q_out, y0_out):
    C = R_CHUNK
    blk, t_idx, s_idx = _slab_masks()
    eye4 = (t_idx == s_idx).astype(F32)
    ti = lax.broadcasted_iota(jnp.int32, (C, C), 0)
    si = lax.broadcasted_iota(jnp.int32, (C, C), 1)

    def bd(x):
        return _block_diag(x, blk)

    def setup(c, d):
        rs = pl.ds(c * C, C)
        r, v, kk = r_ref[rs, :], v_ref[rs, :], kk_ref[rs, :]
        tri = ((ti >= si) if d == 0 else (ti <= si)).astype(BF16)
        lw = lw_ref[d, rs, :]
        cl = _cumsum_dot(tri, lw)
        tot = jnp.sum(lw, axis=0, keepdims=True)
        e_neg = jnp.exp(-cl)
        e_end = jnp.exp(tot - cl)
        kd, b = kd_ref[d, rs, :], b_ref[d, rs, :]
        ch = dict(rs=rs, d=d, tot=tot, v_b=v.astype(BF16), v_bd=bd(v),
                  incl=(t_idx >= s_idx) if d == 0 else (t_idx <= s_idx),
                  strict=(t_idx > s_idx) if d == 0 else (t_idx < s_idx),
                  kkm=kk * jnp.exp(cl - lw), rp=r * jnp.exp(cl),
                  bp=b * e_neg, kp=kd * e_neg, be=(b * e_end).astype(BF16), ke=(kd * e_end).astype(BF16))
        ch['lhs'] = jnp.concatenate([ch['kkm'], ch['rp']], axis=0).astype(BF16)
        return ch

    chains = [setup(c, d) for c in range(n_sub) for d in range(2)]
    for ch in chains:
        gram_b = _dot(ch['lhs'], bd(ch.pop('bp')), NT)
        gram_k = _dot(ch.pop('lhs'), bd(ch.pop('kp')), NT)
        lb = jnp.where(ch['strict'], gram_b[:C], 0.0)
        ch['mb'] = jnp.where(ch['incl'], gram_b[C:], 0.0).astype(BF16)
        ch['lmk'] = jnp.concatenate([jnp.where(ch['strict'], gram_k[:C], 0.0),
                                     jnp.where(ch['incl'], gram_k[C:], 0.0)], axis=0).astype(BF16)
        ch['pw'] = -lb
        ch['t_inv'] = eye4 - lb
    for ch in chains:
        ch['pw'] = _dot(ch['pw'].astype(BF16), bd(ch['pw']))
        lmkv = _dot(ch.pop('lmk'), ch.pop('v_bd'))
        ch['lkv'], ch['mkv'] = lmkv[:C], lmkv[C:]
    for _ in range(int(math.log2(C)) - 2):
        for ch in chains:
            both = _dot(jnp.concatenate([ch['pw'], ch['t_inv']], axis=0).astype(BF16), bd(ch['pw']))
            ch['pw'], ch['t_inv'] = both[:C], ch['t_inv'] + both[C:]
    for ch in chains:
        ch['t_inv'] = (ch['t_inv'] + _dot(ch['t_inv'].astype(BF16), bd(ch.pop('pw')))).astype(BF16)
    for ch in chains:
        ch['a_m'] = _dot(ch['t_inv'], bd(ch.pop('kkm')))
        ch['d_m'] = -_dot(ch.pop('t_inv'), bd(ch.pop('lkv')))
    for ch in chains:
        d, rs = ch['d'], ch['rs']
        a_m, d_m = ch['a_m'], ch['d_m']
        q_out[d, rs, :] = (ch['rp'] - _dot(ch['mb'], bd(a_m))).astype(BF16)
        y0_out[d, rs, :] = (_dot(ch['mb'], bd(d_m)) + ch['mkv']).astype(BF16)
        g_cross = _dot(ch['be'], a_m.astype(BF16), TN)
        h_cross = _dot(jnp.concatenate([ch['be'], ch['ke']], axis=0),
                       jnp.concatenate([d_m.astype(BF16), ch['v_b']], axis=0), TN)
        g_out[d, rs, :] = (eye4 * jnp.exp(ch['tot']) - _diag_blocks(g_cross, blk)).astype(BF16)
        h_out[d, rs, :] = _diag_blocks(h_cross, blk)


def _rwkv_chunk(dm, r, v, kk, lw, kd, b):
    M, R = dm.M, dm.R
    rows = dm.tmr
    one = pl.BlockSpec((rows, R_SLAB), lambda i, h: (i, h))
    two = pl.BlockSpec((2, rows, R_SLAB), lambda i, h: (0, i, h))
    f32 = jax.ShapeDtypeStruct((2, M, R), F32)
    b16 = jax.ShapeDtypeStruct((2, M, R), BF16)
    return pl.pallas_call(
        functools.partial(_rwkv_chunk_kernel, rows // R_CHUNK),
        grid=(M // rows, R // R_SLAB),
        in_specs=[one, one, one, two, two, two],
        out_specs=[two, two, two, two],
        out_shape=[b16, f32, b16, b16],
        compiler_params=_params("parallel", "parallel"),
        name="rwkv_chunk",
    )(r, v, kk, lw, kd, b)


def _rwkv_scan_kernel(n_slab, n_in, gf_ref, hf_ref, qf_ref, y0f_ref, gb_ref, hb_ref, qb_ref, y0b_ref,
                      yf_ref, yb_ref, st_ref):
    C = R_CHUNK
    dirs = ((gf_ref, hf_ref, qf_ref, y0f_ref, yf_ref), (gb_ref, hb_ref, qb_ref, y0b_ref, yb_ref))

    @pl.when(pl.program_id(1) == 0)
    def _():
        st_ref[...] = jnp.zeros_like(st_ref)

    blk, _, _ = _slab_masks()
    lanes = [slice(sl * R_SLAB, (sl + 1) * R_SLAB) for sl in range(n_slab)]
    for t in range(n_in):
        todo = [(d, refs, pl.ds((t, n_in - 1 - t)[d] * C, C), ls) for d, refs in enumerate(dirs) for ls in lanes]
        st_bd = [_block_diag(st_ref[d, :, ls], blk) for d, _, _, ls in todo]
        both = [_dot(jnp.concatenate([refs[2][rs, ls], refs[0][rs, ls]], axis=0), bd)
                for (d, refs, rs, ls), bd in zip(todo, st_bd)]
        for (d, refs, rs, ls), res in zip(todo, both):
            refs[4][rs, ls] = (res[:C] + refs[3][rs, ls].astype(F32)).astype(BF16)
            st_ref[d, :, ls] = res[C:] + refs[1][rs, ls]


def _rwkv_scan(dm, g_m, h_m, q_m, y0_m):
    R, rows = dm.R, dm.tmr
    steps = (dm.CT + dm.S) // rows

    def spec(d):
        return pl.BlockSpec((None, rows, R), lambda b, s: (d, dm.chunk_block(b, d, s, rows), 0))

    def yspec(d):
        return pl.BlockSpec((rows, R), lambda b, s: (dm.chunk_block(b, d, s, rows), 0))

    y = jax.ShapeDtypeStruct((dm.M, R), BF16)
    return pl.pallas_call(
        functools.partial(_rwkv_scan_kernel, R // R_SLAB, rows // R_CHUNK),
        grid=(dm.B, steps),
        in_specs=[spec(0)] * 4 + [spec(1)] * 4,
        out_specs=[yspec(0), yspec(1)],
        out_shape=[y, y],
        scratch_shapes=[pltpu.VMEM((2, R_HEAD_DIM, R), F32)],
        compiler_params=_params("parallel", "arbitrary"),
        name="rwkv_scan",
    )(g_m, h_m, q_m, y0_m, g_m, h_m, q_m, y0_m)


def _rwkv_out_kernel(yf_ref, yb_ref, bonus_ref, g_ref, lnw_ref, lnb_ref, e1_ref, e2_ref, o_ref):
    o = yf_ref[...].astype(F32) + yb_ref[...].astype(F32)
    inv_n = 1.0 / R_HEAD_DIM
    mu = _head_sum(o, e1_ref, e2_ref) * inv_n
    oc = o - mu
    var = _head_sum(oc * oc, e1_ref, e2_ref) * inv_n
    o = oc * lax.rsqrt(var + R_GN_EPS) * lnw_ref[...] + lnb_ref[...]
    o_ref[...] = ((o + bonus_ref[...]) * g_ref[...]).astype(BF16)


def _rwkv_out(dm, y_f, y_b, bonus, g, p):
    tm, R, M = dm.tmr, dm.R, dm.M
    row = pl.BlockSpec((tm, R), lambda i: (i, 0))

    def full(a):
        return pl.BlockSpec(a.shape, lambda i: (0,) * a.ndim)

    consts = [p['ln_w'], p['ln_b'], p['e1'], p['e2']]
    return pl.pallas_call(
        _rwkv_out_kernel,
        grid=(M // tm,),
        in_specs=[row, row, row, row] + [full(a) for a in consts],
        out_specs=row,
        out_shape=jax.ShapeDtypeStruct((M, R), BF16),
        compiler_params=_params("parallel"),
        name="rwkv_out",
    )(y_f, y_b, bonus, g, *consts)


def _gla_kernel(dm, n_in, qf_ref, kf_ref, vf_ref, af_ref, qb_ref, kb_ref, vb_ref, ab_ref, wa_ref, ba_ref,
                yf_ref, yb_ref, st_ref, qd_ref, ki_ref, ke_ref, y_ref):
    C = G_CHUNK
    dk, dv = dm.GK // G_HEADS, dm.GV // G_HEADS
    key = [slice(h * dk, (h + 1) * dk) for h in range(G_HEADS)]
    val = [slice(h * dv, (h + 1) * dv) for h in range(G_HEADS)]
    row = [slice(c * C, (c + 1) * C) for c in range(n_in)]
    dirs = ((qf_ref, kf_ref, vf_ref, af_ref, yf_ref), (qb_ref, kb_ref, vb_ref, ab_ref, yb_ref))

    @pl.when(pl.program_id(1) == 0)
    def _():
        st_ref[...] = jnp.zeros_like(st_ref)

    ti = lax.broadcasted_iota(jnp.int32, (C, C), 0)
    si = lax.broadcasted_iota(jnp.int32, (C, C), 1)
    causal = (ti >= si, ti <= si)

    la = [_log_sigmoid(_dot(refs[3][...], wa_ref[d, 0]) + _dot(refs[3][...], wa_ref[d, 1]) + ba_ref[d])
          * (1.0 / G_GATE_NORM) for d, refs in enumerate(dirs)]
    cb = [[_cumsum_dot(causal[d].astype(BF16), la[d][r]) for r in row] for d in range(2)]
    e_last = []
    for d, refs in enumerate(dirs):
        cum = jnp.concatenate(cb[d], axis=0)
        tots = [jnp.sum(la[d][r], axis=0, keepdims=True) for r in row]
        tot = jnp.concatenate([jnp.broadcast_to(t, (C, t.shape[1])) for t in tots], axis=0)
        k = refs[1][...].astype(F32)
        qd_ref[d] = (refs[0][...].astype(F32) * (dk ** -0.5) * jnp.exp(cum)).astype(BF16)
        ki_ref[d] = (k * jnp.exp(-cum)).astype(BF16)
        ke_ref[d] = (k * jnp.exp(tot - cum)).astype(BF16)
        e_last.append([jnp.exp(t) for t in tots])
    for d, refs in enumerate(dirs):
        att = [[jnp.where(causal[d], _dot(qd_ref[d, r, ks], ki_ref[d, r, ks], NT), 0.0).astype(BF16) for ks in key]
               for r in row]
        for c, r in enumerate(row):
            for h in range(G_HEADS):
                y_ref[d, r, val[h]] = _dot(att[c][h], refs[2][r, val[h]])
    for t in range(n_in):
        todo = [(d, refs, (t, n_in - 1 - t)[d], h) for d, refs in enumerate(dirs) for h in range(G_HEADS)]
        inter = [_dot(qd_ref[d, row[c], key[h]], st_ref[d, h].astype(BF16), NT) for d, _, c, h in todo]
        upd = [_dot(refs[2][row[c], val[h]], ke_ref[d, row[c], key[h]], TN) for d, refs, c, h in todo]
        for (d, refs, c, h), o in zip(todo, inter):
            y_ref[d, row[c], val[h]] += o
        for (d, refs, c, h), u in zip(todo, upd):
            st_ref[d, h] = st_ref[d, h] * e_last[d][c][:, key[h]] + u
    yf_ref[...] = y_ref[0].astype(BF16)
    yb_ref[...] = y_ref[1].astype(BF16)


def _gla(dm, z, wa_split, ba):
    rows = dm.tmr
    steps = (dm.CT + dm.S) // rows
    GK, GV = dm.GK, dm.GV

    def zspec(width, col, d):
        return pl.BlockSpec((rows, width), lambda b, s: (dm.chunk_block(b, d, s, rows), col // width))

    def zspecs(d):
        return [zspec(GK, dm.c_gla_q, d), zspec(GK, dm.c_gla_k, d), zspec(GV, dm.c_gla_v, d),
                zspec(LORA_PAD, dm.c_gla_ac, d)]

    def yspec(d):
        return pl.BlockSpec((rows, GV), lambda b, s: (dm.chunk_block(b, d, s, rows), 0))

    y = jax.ShapeDtypeStruct((dm.M, GV), BF16)
    return pl.pallas_call(
        functools.partial(_gla_kernel, dm, rows // G_CHUNK),
        grid=(dm.B, steps),
        in_specs=zspecs(0) + zspecs(1) + [
            pl.BlockSpec((2, 2, LORA_PAD, GK), lambda b, s: (0, 0, 0, 0)),
            pl.BlockSpec((2, 1, GK), lambda b, s: (0, 0, 0)),
        ],
        out_specs=[yspec(0), yspec(1)],
        out_shape=[y, y],
        scratch_shapes=[pltpu.VMEM((2, G_HEADS, GV // G_HEADS, GK // G_HEADS), F32),
                        pltpu.VMEM((2, rows, GK), BF16), pltpu.VMEM((2, rows, GK), BF16),
                        pltpu.VMEM((2, rows, GK), BF16), pltpu.VMEM((2, rows, GV), F32)],
        compiler_params=_params("parallel", "arbitrary"),
        name="gla",
    )(z, z, z, z, z, z, z, z, wa_split, ba)


def _gla_out_kernel(dv, yf_ref, yb_ref, g_ref, nw_ref, o_ref):
    o = yf_ref[...].astype(F32) + yb_ref[...].astype(F32)
    g = g_ref[...].astype(F32)
    gate = g * _sigmoid(g)
    for h in range(G_HEADS):
        vs = slice(h * dv, (h + 1) * dv)
        oh = o[:, vs]
        oh = oh * lax.rsqrt(jnp.mean(oh * oh, -1, keepdims=True) + G_EPS) * nw_ref[...]
        o_ref[:, vs] = (oh * gate[:, vs]).astype(BF16)


def _gla_out(dm, y_f, y_b, z, norm_w):
    tm, GV, M = dm.tmr, dm.GV, dm.M
    dv = GV // G_HEADS
    return pl.pallas_call(
        functools.partial(_gla_out_kernel, dv),
        grid=(M // tm,),
        in_specs=[
            pl.BlockSpec((tm, GV), lambda i: (i, 0)),
            pl.BlockSpec((tm, GV), lambda i: (i, 0)),
            pl.BlockSpec((tm, GV), lambda i: (i, dm.c_gla_g // GV)),
            pl.BlockSpec((1, dv), lambda i: (0, 0)),
        ],
        out_specs=pl.BlockSpec((tm, GV), lambda i: (i, 0)),
        out_shape=jax.ShapeDtypeStruct((M, GV), BF16),
        compiler_params=_params("parallel"),
        name="gla_out",
    )(y_f, y_b, z, norm_w.reshape(1, dv))


def _merge_kernel(alpha, n_chunks, x_ref, mg_ref, oa_ref, or_ref, og_ref, ga_ref, gr_ref, gg_ref, wa_ref, wr_ref,
                  wg_ref, wo_ref, g_ref, b_ref, o_ref):
    tn = wo_ref.shape[0] // n_chunks
    oa, orw, og = oa_ref[...], or_ref[...], og_ref[...]
    acc = None
    for n in range(n_chunks):
        cs = slice(n * tn, (n + 1) * tn)
        merged = (_sigmoid(ga_ref[:, cs].astype(F32)) * _dot(oa, wa_ref[:, cs])
                  + _sigmoid(gr_ref[:, cs].astype(F32)) * _dot(orw, wr_ref[:, cs])
                  + _sigmoid(gg_ref[:, cs].astype(F32)) * _dot(og, wg_ref[:, cs]))
        part = _dot(merged.astype(BF16), wo_ref[cs, :])
        acc = part if acc is None else acc + part
    z = alpha * x_ref[...] + mg_ref[...] * acc
    o_ref[...] = _ln(z) * g_ref[...] + b_ref[...]


def _merge(dm, alpha, x, mods, z, o_att, o_rwkv, o_gla, wba, wbr, wbg, w_out, l, g, b, lat_only):
    D, tm = dm.D, dm.tmr
    off = dm.MC // tm if lat_only else 0
    rows = dm.M - off * tm
    gate0 = dm.c_gate // D

    def row_spec(width, col=0):
        return pl.BlockSpec((tm, width), lambda i: (i + off, col))

    def weight(k_rows):
        return pl.BlockSpec((None, k_rows, D), lambda i: (l, 0, 0), pipeline_mode=pl.Buffered(1))

    return pl.pallas_call(
        functools.partial(_merge_kernel, alpha, 4),
        grid=(rows // tm,),
        in_specs=[
            row_spec(D), dm.mod_spec(5, tm, off),
            row_spec(A_Q_DIM), row_spec(dm.R), row_spec(dm.GV),
            row_spec(D, gate0), row_spec(D, gate0 + 1), row_spec(D, gate0 + 2),
            weight(A_Q_DIM), weight(dm.R), weight(dm.GV), weight(D),
            pl.BlockSpec((1, D), lambda i: (0, 0)),
            pl.BlockSpec((1, D), lambda i: (0, 0)),
        ],
        out_specs=pl.BlockSpec((tm, D), lambda i: (i, 0)),
        out_shape=jax.ShapeDtypeStruct((rows, D), F32),
        compiler_params=_params("parallel"),
        name="merge",
    )(x, mods, o_att, o_rwkv, o_gla, z, z, z, wba, wbr, wbg, w_out, g.reshape(1, D), b.reshape(1, D))


def _pad_to(a, size, axis):
    pad = [(0, 0)] * a.ndim
    pad[axis] = (0, size - a.shape[axis])
    return jnp.pad(a, pad)


def _rwkv_cols(a, R):
    c = 3 * R
    rkv = a[..., :c]
    wc0, wc1 = a[..., c:c + DECAY_LORA], a[..., c + DECAY_LORA:c + 2 * DECAY_LORA]
    c += 2 * DECAY_LORA
    ac0, ac1 = a[..., c:c + ICLR_LORA], a[..., c + ICLR_LORA:c + 2 * ICLR_LORA]
    c += 2 * ICLR_LORA
    gc = a[..., c:c + GATE_LORA]
    ax = a.ndim - 1
    return jnp.concatenate([rkv, gc] + [_pad_to(t, LORA_PAD, ax) for t in (wc0, wc1, ac0, ac1)], axis=-1)


def _in_proj_weights(dm, w_in):
    D, R, GK, GV = dm.D, dm.R, dm.GK, dm.GV
    att_cols = A_Q_DIM + 2 * A_KV_DIM
    rw_src = 3 * R + 2 * DECAY_LORA + 2 * ICLR_LORA + GATE_LORA
    gla_src = 2 * GK + 2 * GV + 2 * G_GATE_RANK
    att, rwkv, gla, gate = jnp.split(w_in, np.cumsum([att_cols, rw_src, gla_src]).tolist(), axis=-1)
    a_q, a_k, a_v = att[:, :A_Q_DIM], att[:, A_Q_DIM:A_Q_DIM + A_KV_DIM], att[:, A_Q_DIM + A_KV_DIM:]
    g_q, g_k = gla[:, :GK], gla[:, GK:2 * GK]
    g_v, g_g = gla[:, 2 * GK:2 * GK + GV], gla[:, 2 * GK + GV:2 * GK + 2 * GV]
    g_ac = _pad_to(gla[:, 2 * GK + 2 * GV:], LORA_PAD, 1)
    w = jnp.concatenate([_rwkv_cols(rwkv, R), a_k, gate, g_v, g_g, g_q, g_k, a_q, a_v, g_ac], axis=-1)
    return _pad_to(w, dm.NP, 1).astype(BF16)


def _rope_tables(S):
    rows = S // GRID_W
    row = jnp.repeat(jnp.arange(rows, dtype=F32), GRID_W)
    col = jnp.tile(jnp.arange(GRID_W, dtype=F32), rows)
    n_freq = A_HEAD_DIM // 4
    inv_freq = ROPE_BASE ** (-jnp.arange(n_freq, dtype=F32) / n_freq)
    ang_r, ang_c = row[:, None] * inv_freq, col[:, None] * inv_freq
    cos = jnp.concatenate([jnp.cos(ang_r)] * 2 + [jnp.cos(ang_c)] * 2, axis=-1)
    sin = jnp.concatenate([-jnp.sin(ang_r), jnp.sin(ang_r), -jnp.sin(ang_c), jnp.sin(ang_c)], axis=-1)
    return cos, sin


def kernel(x, c, ctx, c_ctx, ada_w, ada_b, ln_g, ln_b, ffn_w_gu, ffn_w_down, w_in, attn_sink, rwkv_conv, rwkv_w0,
           rwkv_w2, rwkv_a0, rwkv_a2, rwkv_g2, rwkv_k_k, rwkv_k_a, rwkv_r_k, rwkv_ln_w, rwkv_ln_b, gla_wa2, gla_ba,
           gla_norm_w, w_branch_att, w_branch_rwkv, w_branch_gla, w_out):
    B, S, D = x.shape
    CT = ctx.shape[1]
    depth = ada_w.shape[0]
    dm = _Dims(B, S, CT, D)
    R = dm.R
    alpha = float((2 * depth) ** 0.25)
    assert B + 1 <= MOD_ROWS

    cc = _pad_to(jnp.concatenate([c, c_ctx[None, :]], axis=0), MOD_ROWS, 0)
    mods_all = _ada(cc, ada_w, ada_b)
    cos, sin = _rope_tables(S)
    head_of_lane = jnp.arange(R) // R_HEAD_DIM
    e1 = (head_of_lane[:, None] == jnp.arange(LORA_PAD)[None, :]).astype(BF16)
    e2 = e1.T

    w_gu, w_dn = ffn_w_gu.astype(BF16), ffn_w_down.astype(BF16)
    wba, wbr = w_branch_att.astype(BF16), w_branch_rwkv.astype(BF16)
    wbg, wo = w_branch_gla.astype(BF16), w_out.astype(BF16)

    xs = jnp.concatenate([ctx.reshape(dm.MC, D), x.reshape(dm.ML, D)], axis=0)
    for l in range(depth):
        last = l == depth - 1
        mods = mods_all[l].reshape(MOD_ROWS * N_MOD, 1, D)
        xs = _ffn(dm, alpha, xs, mods, 0, w_gu, w_dn, l, 0, ln_g[l, 0], ln_b[l, 0])

        z = _inproj(dm, xs, mods, _in_proj_weights(dm, w_in[l]))
        o_att = jnp.concatenate([_attn_ctx(dm, z, attn_sink[l]), _attn_latent(dm, z, attn_sink[l], cos, sin)], axis=0)

        rp = {
            'conv': _rwkv_cols(rwkv_conv[l], R),
            'g2': rwkv_g2[l].astype(BF16),
            'w0': rwkv_w0[l], 'a0': rwkv_a0[l],
            'w2': _pad_to(rwkv_w2[l], LORA_PAD, 1).astype(BF16),
            'a2': _pad_to(rwkv_a2[l], LORA_PAD, 1).astype(BF16),
            'k_k': rwkv_k_k[l].reshape(1, R), 'k_a': rwkv_k_a[l].reshape(1, R), 'r_k': rwkv_r_k[l].reshape(1, R),
            'ln_w': rwkv_ln_w[l].reshape(1, R), 'ln_b': rwkv_ln_b[l].reshape(1, R),
            'e1': e1, 'e2': e2,
        }
        r, v, kk, lw, kd, b, bonus, g = _rwkv_prep(dm, z, rp)
        g_m, h_m, q_m, y0_m = _rwkv_chunk(dm, r, v, kk, lw, kd, b)
        y_rwkv_f, y_rwkv_b = _rwkv_scan(dm, g_m, h_m, q_m, y0_m)
        o_rwkv = _rwkv_out(dm, y_rwkv_f, y_rwkv_b, bonus, g, rp)

        wa_p = jnp.stack([_pad_to(jnp.pad(gla_wa2[l, d], ((d * G_GATE_RANK, 0), (0, 0))), LORA_PAD, 0)
                          for d in range(2)])
        wa_hi = wa_p.astype(BF16)
        wa_split = jnp.stack([wa_hi, (wa_p - wa_hi.astype(F32)).astype(BF16)], axis=1)
        y_gla_f, y_gla_b = _gla(dm, z, wa_split, gla_ba[l].reshape(2, 1, dm.GK))
        o_gla = _gla_out(dm, y_gla_f, y_gla_b, z, gla_norm_w[l])

        xs = _merge(dm, alpha, xs, mods, z, o_att, o_rwkv, o_gla, wba, wbr, wbg, wo, l, ln_g[l, 1], ln_b[l, 1], last)
        xs = _ffn(dm, alpha, xs, mods, 6, w_gu, w_dn, l, 1, ln_g[l, 2], ln_b[l, 2])
    return xs.reshape(B, S, D)
```

```python
import functools
import math

import jax
import jax.numpy as jnp
import numpy as np
from jax import lax
from jax.experimental import pallas as pl
from jax.experimental.pallas import tpu as pltpu

F32 = jnp.float32
BF16 = jnp.bfloat16
HI = lax.Precision.HIGHEST

N_MOD = 9
D_FF = 5632
LN_EPS = 1e-5
FFN_RES = 0.5

A_HEADS = 8
A_KV_HEADS = 2
A_GROUP = A_HEADS // A_KV_HEADS
A_HEAD_DIM = 128
A_Q_DIM = A_HEADS * A_HEAD_DIM
A_KV_DIM = A_KV_HEADS * A_HEAD_DIM
WINDOW = 128
A_BLOCK = 128
GRID_W = 64
ROPE_BASE = 10000.0
MASK_VALUE = -1e30

R_HEAD_DIM = 64
DECAY_LORA = 96
ICLR_LORA = 96
GATE_LORA = 256
R_GN_EPS = 64e-5
R_CHUNK = 64
LORA_PAD = 128
SUBLANES = 8
HALO = 16

G_HEADS = 4
G_GATE_RANK = 16
G_GATE_NORM = 16.0
G_CHUNK = 64
G_EPS = 1e-5

MOD_ROWS = 16
VMEM_LIMIT = 58 * 1024 * 1024
VMEM_LIMIT_FFN = 63 * 1024 * 1024

NN = (((1,), (0,)), ((), ()))
NT = (((1,), (1,)), ((), ()))
TN = (((0,), (0,)), ((), ()))


def _dot(a, b, dims=NN, prec=None):
    return lax.dot_general(a, b, dims, precision=prec, preferred_element_type=F32)


def _cumsum_dot(tri, x):
    hi = x.astype(BF16)
    rest = x - hi.astype(F32)
    mid = rest.astype(BF16)
    lo = (rest - mid.astype(F32)).astype(BF16)
    return _dot(tri, hi) + _dot(tri, mid) + _dot(tri, lo)


def _sigmoid(x):
    return 1.0 / (1.0 + jnp.exp(-x))


def _log_sigmoid(x):
    return jnp.minimum(x, 0.0) - jnp.log(1.0 + jnp.exp(-jnp.abs(x)))


def _ln(x):
    mu = jnp.mean(x, -1, keepdims=True)
    xc = x - mu
    var = jnp.mean(xc * xc, -1, keepdims=True)
    return xc * lax.rsqrt(var + LN_EPS)


def _params(*sem, vmem=VMEM_LIMIT):
    return pltpu.CompilerParams(dimension_semantics=sem, vmem_limit_bytes=vmem)


class _Dims:
    def __init__(self, B, S, CT, D):
        self.B, self.S, self.CT, self.D = B, S, CT, D
        self.MC, self.ML = B * CT, B * S
        self.M = self.MC + self.ML
        self.R = D // 2
        self.GK = D // 2
        self.GV = D
        self.RH = self.R // R_HEAD_DIM
        self.rw_cols = 3 * self.R + GATE_LORA + 4 * LORA_PAD
        c = 0
        self.c_rwkv = c; c += self.rw_cols
        self.c_att_k = c; c += A_KV_DIM
        self.c_gate = c; c += 3 * D
        self.c_gla_v = c; c += self.GV
        self.c_gla_g = c; c += self.GV
        self.c_gla_q = c; c += self.GK
        self.c_gla_k = c; c += self.GK
        self.c_att_q = c; c += A_Q_DIM
        self.c_att_v = c; c += A_KV_DIM
        self.c_gla_ac = c; c += LORA_PAD
        self.tn_in = 1280
        self.NP = -(-c // self.tn_in) * self.tn_in
        self.tm = math.gcd(512, math.gcd(self.MC, S))
        self.tm_in = math.gcd(1024, math.gcd(self.MC, S))
        self.tmr = math.gcd(256, math.gcd(CT, S))

    def mod_row(self, i, tm):
        nct = self.MC // tm
        tpb = self.S // tm
        return jnp.where(i < nct, self.B, (i - nct) // tpb)

    def mod_spec(self, m, tm, off=0):
        return pl.BlockSpec((None, 1, self.D), lambda i, *_: (self.mod_row(i + off, tm) * N_MOD + m, 0, 0))

    def chunk_block(self, b, d, s, chunk):
        ncc, ncl = self.CT // chunk, self.S // chunk
        j = s - ncc
        ctx_blk = b * ncc + jnp.where(d == 0, s, ncc - 1 - s)
        lat_blk = self.B * ncc + b * ncl + jnp.where(d == 0, j, ncl - 1 - j)
        return jnp.where(s < ncc, ctx_blk, lat_blk)


def _ada_kernel(c_ref, w_ref, b_ref, o_ref):
    cc = c_ref[...]
    s = (cc * _sigmoid(cc)).astype(BF16)
    o_ref[...] = _dot(s, w_ref[...].astype(BF16)) + b_ref[...]


def _ada(cc, ada_w, ada_b):
    L, D, N = ada_w.shape
    tn = 2048
    return pl.pallas_call(
        _ada_kernel,
        grid=(L, N // tn),
        in_specs=[
            pl.BlockSpec((MOD_ROWS, D), lambda l, n: (0, 0)),
            pl.BlockSpec((None, D, tn), lambda l, n: (l, 0, n)),
            pl.BlockSpec((None, 1, tn), lambda l, n: (l, 0, n)),
        ],
        out_specs=pl.BlockSpec((None, MOD_ROWS, tn), lambda l, n: (l, 0, n)),
        out_shape=jax.ShapeDtypeStruct((L, MOD_ROWS, N), F32),
        compiler_params=_params("parallel", "arbitrary"),
        name="ada",
    )(cc, ada_w, ada_b.reshape(L, 1, N))


def _ffn_kernel(alpha, x_ref, sh_ref, sc_ref, gt_ref, wg_ref, wu_ref, wd_ref, g_ref, b_ref, o_ref, h_ref):
    f = pl.program_id(1)

    @pl.when(f == 0)
    def _():
        h = _ln(x_ref[...]) * (1.0 + sc_ref[...]) + sh_ref[...]
        h_ref[...] = h.astype(BF16)
        o_ref[...] = jnp.zeros_like(o_ref)

    h = h_ref[...]
    g = _dot(h, wg_ref[...])
    u = _dot(h, wu_ref[...])
    a = (g * _sigmoid(g) * u).astype(BF16)
    o_ref[...] += _dot(a, wd_ref[...])

    @pl.when(f == pl.num_programs(1) - 1)
    def _():
        z = alpha * x_ref[...] + (FFN_RES * gt_ref[...]) * o_ref[...]
        o_ref[...] = _ln(z) * g_ref[...] + b_ref[...]


def _ffn(dm, alpha, x, mods, m0, w_gu, w_down, l, j, g, b):
    D, tm, tf = dm.D, dm.tm_in, 512
    nf = D_FF // tf
    rows = x.shape[0]
    off = (dm.M - rows) // tm
    return pl.pallas_call(
        functools.partial(_ffn_kernel, alpha),
        grid=(rows // tm, nf),
        in_specs=[
            pl.BlockSpec((tm, D), lambda i, f: (i, 0)),
            dm.mod_spec(m0, tm, off), dm.mod_spec(m0 + 1, tm, off), dm.mod_spec(m0 + 2, tm, off),
            pl.BlockSpec((None, None, D, tf), lambda i, f: (l, j, 0, f)),
            pl.BlockSpec((None, None, D, tf), lambda i, f: (l, j, 0, f + nf)),
            pl.BlockSpec((None, None, tf, D), lambda i, f: (l, j, f, 0)),
            pl.BlockSpec((1, D), lambda i, f: (0, 0)),
            pl.BlockSpec((1, D), lambda i, f: (0, 0)),
        ],
        out_specs=pl.BlockSpec((tm, D), lambda i, f: (i, 0)),
        out_shape=jax.ShapeDtypeStruct((rows, D), F32),
        scratch_shapes=[pltpu.VMEM((tm, D), BF16)],
        compiler_params=_params("parallel", "arbitrary", vmem=VMEM_LIMIT_FFN),
        name="ffn",
    )(x, mods, mods, mods, w_gu, w_gu, w_down, g.reshape(1, D), b.reshape(1, D))


def _inproj_kernel(x_ref, sh_ref, sc_ref, w_ref, o_ref, h_ref):
    @pl.when(pl.program_id(1) == 0)
    def _():
        h = _ln(x_ref[...]) * (1.0 + sc_ref[...]) + sh_ref[...]
        h_ref[...] = h.astype(BF16)

    o_ref[...] = _dot(h_ref[...], w_ref[...]).astype(BF16)


def _inproj(dm, x, mods, w_in_p):
    D, tm, tn = dm.D, dm.tm_in, dm.tn_in
    return pl.pallas_call(
        _inproj_kernel,
        grid=(dm.M // tm, dm.NP // tn),
        in_specs=[
            pl.BlockSpec((tm, D), lambda i, n: (i, 0)),
            dm.mod_spec(3, tm), dm.mod_spec(4, tm),
            pl.BlockSpec((D, tn), lambda i, n: (0, n)),
        ],
        out_specs=pl.BlockSpec((tm, tn), lambda i, n: (i, n)),
        out_shape=jax.ShapeDtypeStruct((dm.M, dm.NP), BF16),
        scratch_shapes=[pltpu.VMEM((tm, D), BF16)],
        compiler_params=_params("parallel", "arbitrary"),
        name="inproj",
    )(x, mods, mods, w_in_p)


def _rope(x, cos, sin_signed):
    lane = lax.broadcasted_iota(jnp.int32, x.shape, 1)
    quarter = A_HEAD_DIM // 4
    swapped = jnp.where((lane % (2 * quarter)) < quarter,
                        pltpu.roll(x, A_HEAD_DIM - quarter, 1), pltpu.roll(x, quarter, 1))
    return x * cos + swapped * sin_signed


def _attn_kernel(S, nb, sink_ref, q_ref, kp_ref, kc_ref, kn_ref, vp_ref, vc_ref, vn_ref, kx_ref, vx_ref,
                 cos_ref, sin_ref, o_ref):
    n = pl.program_id(1)
    scale = A_HEAD_DIM ** -0.5

    def table(ref, blk):
        return ref[pl.ds(pl.multiple_of(blk * A_BLOCK, A_BLOCK), A_BLOCK), :]

    blk_p, blk_n = jnp.maximum(n - 1, 0), jnp.minimum(n + 1, nb - 1)
    cos_c, sin_c = table(cos_ref, n), table(sin_ref, n)
    cos_p, sin_p = table(cos_ref, blk_p), table(sin_ref, blk_p)
    cos_n, sin_n = table(cos_ref, blk_n), table(sin_ref, blk_n)

    rows, band = A_GROUP * A_BLOCK, 3 * A_BLOCK
    qpos = n * A_BLOCK + lax.broadcasted_iota(jnp.int32, (rows, band), 0) % A_BLOCK
    kpos = (n - 1) * A_BLOCK + lax.broadcasted_iota(jnp.int32, (rows, band), 1)
    valid = (jnp.abs(qpos - kpos) <= WINDOW) & (kpos >= 0) & (kpos < S)

    work = []
    for kvh in range(A_KV_HEADS):
        ks = slice(kvh * A_HEAD_DIM, (kvh + 1) * A_HEAD_DIM)
        kb = jnp.concatenate([_rope(kp_ref[:, ks].astype(F32), cos_p, sin_p),
                              _rope(kc_ref[:, ks].astype(F32), cos_c, sin_c),
                              _rope(kn_ref[:, ks].astype(F32), cos_n, sin_n)], axis=0).astype(BF16)
        heads = [kvh * A_GROUP + g for g in range(A_GROUP)]
        qs = [q_ref[:, h * A_HEAD_DIM:(h + 1) * A_HEAD_DIM] for h in heads]
        q_rope = jnp.concatenate([_rope(q.astype(F32), cos_c, sin_c) * scale for q in qs], axis=0).astype(BF16)
        q_plain = jnp.concatenate([q.astype(F32) * scale for q in qs], axis=0).astype(BF16)
        sink = jnp.concatenate([jnp.full((A_BLOCK, 1), sink_ref[h], F32) for h in heads], axis=0)
        s_loc = jnp.where(valid, _dot(q_rope, kb, NT), MASK_VALUE)
        s_ctx = _dot(q_plain, kx_ref[:, ks], NT)
        work.append(dict(ks=ks, heads=heads, sink=sink, s_loc=s_loc, s_ctx=s_ctx))
    for w in work:
        w['m'] = jnp.maximum(jnp.maximum(jnp.max(w['s_loc'], -1, keepdims=True),
                                         jnp.max(w['s_ctx'], -1, keepdims=True)), w['sink'])
    for w in work:
        w['p_loc'] = jnp.exp(w.pop('s_loc') - w['m'])
        w['p_ctx'] = jnp.exp(w.pop('s_ctx') - w['m'])
    for w in work:
        w['den'] = (jnp.sum(w['p_loc'], -1, keepdims=True) + jnp.sum(w['p_ctx'], -1, keepdims=True)
                    + jnp.exp(w['sink'] - w['m']))
    for w in work:
        ks = w['ks']
        vb = jnp.concatenate([vp_ref[:, ks], vc_ref[:, ks], vn_ref[:, ks]], axis=0)
        o = (_dot(w['p_loc'].astype(BF16), vb) + _dot(w['p_ctx'].astype(BF16), vx_ref[:, ks])) / w['den']
        for g, h in enumerate(w['heads']):
            o_ref[:, h * A_HEAD_DIM:(h + 1) * A_HEAD_DIM] = o[g * A_BLOCK:(g + 1) * A_BLOCK].astype(BF16)


def _attn_latent(dm, z, sink, cos, sin):
    B, S = dm.B, dm.S
    nb = S // A_BLOCK
    base = dm.MC // A_BLOCK
    qc = dm.c_att_q // A_Q_DIM
    kc = dm.c_att_k // A_KV_DIM
    vc = dm.c_att_v // A_KV_DIM

    def rows(shift):
        return lambda b, n: base + b * nb + jnp.clip(n + shift, 0, nb - 1)

    def kv_spec(col, shift):
        r = rows(shift)
        return pl.BlockSpec((A_BLOCK, A_KV_DIM), lambda b, n: (r(b, n), col))

    return pl.pallas_call(
        functools.partial(_attn_kernel, S, nb),
        grid=(B, nb),
        in_specs=[
            pl.BlockSpec(memory_space=pltpu.SMEM),
            pl.BlockSpec((A_BLOCK, A_Q_DIM), lambda b, n: (base + b * nb + n, qc)),
            kv_spec(kc, -1), kv_spec(kc, 0), kv_spec(kc, 1),
            kv_spec(vc, -1), kv_spec(vc, 0), kv_spec(vc, 1),
            pl.BlockSpec((dm.CT, A_KV_DIM), lambda b, n: (b, kc)),
            pl.BlockSpec((dm.CT, A_KV_DIM), lambda b, n: (b, vc)),
            pl.BlockSpec((S, A_HEAD_DIM), lambda b, n: (0, 0)),
            pl.BlockSpec((S, A_HEAD_DIM), lambda b, n: (0, 0)),
        ],
        out_specs=pl.BlockSpec((A_BLOCK, A_Q_DIM), lambda b, n: (b * nb + n, 0)),
        out_shape=jax.ShapeDtypeStruct((dm.ML, A_Q_DIM), BF16),
        compiler_params=_params("parallel", "arbitrary"),
        name="attn_latent",
    )(sink, z, z, z, z, z, z, z, z, z, cos, sin)


def _attn_ctx_kernel(CT, sink_ref, q_ref, k_ref, v_ref, o_ref):
    scale = A_HEAD_DIM ** -0.5
    for kvh in range(A_KV_HEADS):
        ks = slice(kvh * A_HEAD_DIM, (kvh + 1) * A_HEAD_DIM)
        k = k_ref[:, ks]
        v = v_ref[:, ks]
        heads = [kvh * A_GROUP + g for g in range(A_GROUP)]
        q = jnp.concatenate([q_ref[:, h * A_HEAD_DIM:(h + 1) * A_HEAD_DIM] for h in heads], axis=0)
        sink = jnp.concatenate([jnp.full((CT, 1), sink_ref[h], F32) for h in heads], axis=0)
        s = _dot(q, k, NT) * scale
        m = jnp.maximum(jnp.max(s, -1, keepdims=True), sink)
        p = jnp.exp(s - m)
        den = jnp.sum(p, -1, keepdims=True) + jnp.exp(sink - m)
        o = _dot(p.astype(BF16), v) / den
        for g, h in enumerate(heads):
            o_ref[:, h * A_HEAD_DIM:(h + 1) * A_HEAD_DIM] = o[g * CT:(g + 1) * CT].astype(BF16)


def _attn_ctx(dm, z, sink):
    CT = dm.CT
    return pl.pallas_call(
        functools.partial(_attn_ctx_kernel, CT),
        grid=(dm.B,),
        in_specs=[
            pl.BlockSpec(memory_space=pltpu.SMEM),
            pl.BlockSpec((CT, A_Q_DIM), lambda b: (b, dm.c_att_q // A_Q_DIM)),
            pl.BlockSpec((CT, A_KV_DIM), lambda b: (b, dm.c_att_k // A_KV_DIM)),
            pl.BlockSpec((CT, A_KV_DIM), lambda b: (b, dm.c_att_v // A_KV_DIM)),
        ],
        out_specs=pl.BlockSpec((CT, A_Q_DIM), lambda b: (b, 0)),
        out_shape=jax.ShapeDtypeStruct((dm.MC, A_Q_DIM), BF16),
        compiler_params=_params("parallel"),
        name="attn_ctx",
    )(sink, z, z, z)


def _head_sum(x, e1_ref, e2_ref):
    def split_dot(a, ind):
        hi = a.astype(BF16)
        lo = (a - hi.astype(F32)).astype(BF16)
        return _dot(hi, ind) + _dot(lo, ind)

    return split_dot(split_dot(x, e1_ref[...]), e2_ref[...])


def _rwkv_prep_kernel(dm, z_ref, zp_ref, zn_ref, cw_ref, g2_ref, w0_ref, w2_ref, a0_ref, a2_ref, kk_ref, ka_ref,
                      rk_ref, e1_ref, e2_ref,
                      r_out, v_out, kk_out, lw_out, kd_out, b_out, bonus_out, g_out):
    i = pl.program_id(0)
    tm, R = dm.tmr, dm.R
    nct = dm.MC // tm
    start = jnp.where(i < nct, (i * tm) % dm.CT, ((i - nct) * tm) % dm.S)
    seqlen = jnp.where(i < nct, dm.CT, dm.S)
    has_prev = (start != 0).astype(F32)
    has_next = (start + tm != seqlen).astype(F32)

    z = z_ref[...].astype(F32)
    sub = lax.broadcasted_iota(jnp.int32, (SUBLANES, 1), 0)
    halo_prev = zp_ref[HALO - 1:HALO, :].astype(F32) * has_prev
    halo_next = zn_ref[0:1, :].astype(F32) * has_next
    down, up = pltpu.roll(z, 1, 0), pltpu.roll(z, tm - 1, 0)
    z_prev = jnp.concatenate([jnp.where(sub == 0, halo_prev, down[:SUBLANES]), down[SUBLANES:]], axis=0)
    z_next = jnp.concatenate([up[:tm - SUBLANES], jnp.where(sub == SUBLANES - 1, halo_next, up[tm - SUBLANES:])],
                             axis=0)
    zc = z_prev * cw_ref[0:1, :] + z * cw_ref[1:2, :] + z_next * cw_ref[2:3, :]

    r = zc[:, 0:R]
    k = zc[:, R:2 * R]
    v = zc[:, 2 * R:3 * R]
    c0 = 3 * R
    gc = zc[:, c0:c0 + GATE_LORA]
    c0 += GATE_LORA
    g_out[...] = _dot(_sigmoid(gc).astype(BF16), g2_ref[...])

    kkr = k * kk_ref[...]
    norm = jnp.sqrt(_head_sum(kkr * kkr, e1_ref, e2_ref))
    kk = kkr / jnp.maximum(norm, 1e-12)
    r_out[...] = r
    v_out[...] = v
    kk_out[...] = kk

    bonus = jnp.zeros_like(v)
    for d in range(2):
        wc = zc[:, c0 + d * LORA_PAD:c0 + (d + 1) * LORA_PAD]
        ac = zc[:, c0 + (2 + d) * LORA_PAD:c0 + (3 + d) * LORA_PAD]
        wl = w0_ref[d:d + 1, :] + _dot(jnp.tanh(wc).astype(BF16), w2_ref[d])
        lw_out[d] = _sigmoid(wl) * (-math.exp(-0.5))
        a = _sigmoid(a0_ref[d:d + 1, :] + _dot(ac.astype(BF16), a2_ref[d]))
        kd = k * (1.0 + (a - 1.0) * ka_ref[...])
        kd_out[d] = kd
        b_out[d] = kk * a
        bonus = bonus + _head_sum(r * kd * rk_ref[...], e1_ref, e2_ref) * v
    bonus_out[...] = bonus


def _rwkv_prep(dm, z, p):
    tm, R, M, W = dm.tmr, dm.R, dm.M, dm.rw_cols
    n_halo = M // HALO
    row = pl.BlockSpec((tm, R), lambda i: (i, 0))
    row2 = pl.BlockSpec((2, tm, R), lambda i: (0, i, 0))

    def full(a):
        return pl.BlockSpec(a.shape, lambda i: (0,) * a.ndim)

    consts = [p['conv'], p['g2'], p['w0'], p['w2'], p['a0'], p['a2'], p['k_k'], p['k_a'], p['r_k'], p['e1'], p['e2']]
    one = jax.ShapeDtypeStruct((M, R), F32)
    two = jax.ShapeDtypeStruct((2, M, R), F32)
    return pl.pallas_call(
        functools.partial(_rwkv_prep_kernel, dm),
        grid=(M // tm,),
        in_specs=[
            pl.BlockSpec((tm, W), lambda i: (i, 0)),
            pl.BlockSpec((HALO, W), lambda i: (jnp.maximum(i * (tm // HALO) - 1, 0), 0)),
            pl.BlockSpec((HALO, W), lambda i: (jnp.minimum((i + 1) * (tm // HALO), n_halo - 1), 0)),
        ] + [full(a) for a in consts],
        out_specs=[row, row, row, row2, row2, row2, row, row],
        out_shape=[one, one, one, two, two, two, one, one],
        compiler_params=_params("parallel"),
        name="rwkv_prep",
    )(z, z, z, *consts)


R_SLAB = 4 * R_HEAD_DIM


def _block_diag(x, mask):
    return jnp.where(mask, jnp.concatenate([x] * 4, axis=0), 0.0).astype(BF16)


def _diag_blocks(full, mask):
    fm = jnp.where(mask, full, 0.0)
    n = R_HEAD_DIM
    return fm[0:n] + fm[n:2 * n] + fm[2 * n:3 * n] + fm[3 * n:4 * n]


def _slab_masks():
    n, g = R_HEAD_DIM, R_SLAB
    blk = (lax.broadcasted_iota(jnp.int32, (g, g), 0) // n) == (lax.broadcasted_iota(jnp.int32, (g, g), 1) // n)
    t_idx = lax.broadcasted_iota(jnp.int32, (n, g), 0)
    s_idx = lax.broadcasted_iota(jnp.int32, (n, g), 1) % n
    return blk, t_idx, s_idx


def _rwkv_chunk_kernel(n_sub, r_ref, v_ref, kk_ref, lw_ref, kd_ref, b_ref, g_out, h_out, q_out, y0_out):
    C = R_CHUNK
    blk, t_idx, s_idx = _slab_masks()
    eye4 = (t_idx == s_idx).astype(F32)
    ti = lax.broadcasted_iota(jnp.int32, (C, C), 0)
    si = lax.broadcasted_iota(jnp.int32, (C, C), 1)

    def bd(x):
        return _block_diag(x, blk)

    def setup(c, d):
        rs = pl.ds(c * C, C)
        r, v, kk = r_ref[rs, :], v_ref[rs, :], kk_ref[rs, :]
        tri = ((ti >= si) if d == 0 else (ti <= si)).astype(BF16)
        lw = lw_ref[d, rs, :]
        cl = _cumsum_dot(tri, lw)
        tot = jnp.sum(lw, axis=0, keepdims=True)
        e_neg = jnp.exp(-cl)
        e_end = jnp.exp(tot - cl)
        kd, b = kd_ref[d, rs, :], b_ref[d, rs, :]
        ch = dict(rs=rs, d=d, tot=tot, v_b=v.astype(BF16), v_bd=bd(v),
                  incl=(t_idx >= s_idx) if d == 0 else (t_idx <= s_idx),
                  strict=(t_idx > s_idx) if d == 0 else (t_idx < s_idx),
                  kkm=kk * jnp.exp(cl - lw), rp=r * jnp.exp(cl),
                  bp=b * e_neg, kp=kd * e_neg, be=(b * e_end).astype(BF16), ke=(kd * e_end).astype(BF16))
        ch['lhs'] = jnp.concatenate([ch['kkm'], ch['rp']], axis=0).astype(BF16)
        return ch

    chains = [setup(c, d) for c in range(n_sub) for d in range(2)]
    for ch in chains:
        gram_b = _dot(ch['lhs'], bd(ch.pop('bp')), NT)
        gram_k = _dot(ch.pop('lhs'), bd(ch.pop('kp')), NT)
        lb = jnp.where(ch['strict'], gram_b[:C], 0.0)
        ch['mb'] = jnp.where(ch['incl'], gram_b[C:], 0.0).astype(BF16)
        ch['lmk'] = jnp.concatenate([jnp.where(ch['strict'], gram_k[:C], 0.0),
                                     jnp.where(ch['incl'], gram_k[C:], 0.0)], axis=0).astype(BF16)
        ch['pw'] = -lb
        ch['t_inv'] = eye4 - lb
    for ch in chains:
        ch['pw'] = _dot(ch['pw'].astype(BF16), bd(ch['pw']))
        lmkv = _dot(ch.pop('lmk'), ch.pop('v_bd'))
        ch['lkv'], ch['mkv'] = lmkv[:C], lmkv[C:]
    for _ in range(int(math.log2(C)) - 2):
        for ch in chains:
            both = _dot(jnp.concatenate([ch['pw'], ch['t_inv']], axis=0).astype(BF16), bd(ch['pw']))
            ch['pw'], ch['t_inv'] = both[:C], ch['t_inv'] + both[C:]
    for ch in chains:
        ch['t_inv'] = (ch['t_inv'] + _dot(ch['t_inv'].astype(BF16), bd(ch.pop('pw')))).astype(BF16)
    for ch in chains:
        ch['a_m'] = _dot(ch['t_inv'], bd(ch.pop('kkm')))
        ch['d_m'] = -_dot(ch.pop('t_inv'), bd(ch.pop('lkv')))
    for ch in chains:
        d, rs = ch['d'], ch['rs']
        a_m, d_m = ch['a_m'], ch['d_m']
        q_out[d, rs, :] = (ch['rp'] - _dot(ch['mb'], bd(a_m))).astype(BF16)
        y0_out[d, rs, :] = (_dot(ch['mb'], bd(d_m)) + ch['mkv']).astype(BF16)
        g_cross = _dot(ch['be'], a_m.astype(BF16), TN)
        h_cross = _dot(jnp.concatenate([ch['be'], ch['ke']], axis=0),
                       jnp.concatenate([d_m.astype(BF16), ch['v_b']], axis=0), TN)
        g_out[d, rs, :] = (eye4 * jnp.exp(ch['tot']) - _diag_blocks(g_cross, blk)).astype(BF16)
        h_out[d, rs, :] = _diag_blocks(h_cross, blk)


def _rwkv_chunk(dm, r, v, kk, lw, kd, b):
    M, R = dm.M, dm.R
    rows = dm.tmr
    one = pl.BlockSpec((rows, R_SLAB), lambda i, h: (i, h))
    two = pl.BlockSpec((2, rows, R_SLAB), lambda i, h: (0, i, h))
    f32 = jax.ShapeDtypeStruct((2, M, R), F32)
    b16 = jax.ShapeDtypeStruct((2, M, R), BF16)
    return pl.pallas_call(
        functools.partial(_rwkv_chunk_kernel, rows // R_CHUNK),
        grid=(M // rows, R // R_SLAB),
        in_specs=[one, one, one, two, two, two],
        out_specs=[two, two, two, two],
        out_shape=[b16, f32, b16, b16],
        compiler_params=_params("parallel", "parallel"),
        name="rwkv_chunk",
    )(r, v, kk, lw, kd, b)


def _rwkv_scan_kernel(n_slab, n_in, gf_ref, hf_ref, qf_ref, y0f_ref, gb_ref, hb_ref, qb_ref, y0b_ref,
                      yf_ref, yb_ref, st_ref):
    C = R_CHUNK
    dirs = ((gf_ref, hf_ref, qf_ref, y0f_ref, yf_ref), (gb_ref, hb_ref, qb_ref, y0b_ref, yb_ref))

    @pl.when(pl.program_id(1) == 0)
    def _():
        st_ref[...] = jnp.zeros_like(st_ref)

    blk, _, _ = _slab_masks()
    lanes = [slice(sl * R_SLAB, (sl + 1) * R_SLAB) for sl in range(n_slab)]
    for t in range(n_in):
        todo = [(d, refs, pl.ds((t, n_in - 1 - t)[d] * C, C), ls) for d, refs in enumerate(dirs) for ls in lanes]
        st_bd = [_block_diag(st_ref[d, :, ls], blk) for d, _, _, ls in todo]
        both = [_dot(jnp.concatenate([refs[2][rs, ls], refs[0][rs, ls]], axis=0), bd)
                for (d, refs, rs, ls), bd in zip(todo, st_bd)]
        for (d, refs, rs, ls), res in zip(todo, both):
            refs[4][rs, ls] = (res[:C] + refs[3][rs, ls].astype(F32)).astype(BF16)
            st_ref[d, :, ls] = res[C:] + refs[1][rs, ls]


def _rwkv_scan(dm, g_m, h_m, q_m, y0_m):
    R, rows = dm.R, dm.tmr
    steps = (dm.CT + dm.S) // rows

    def spec(d):
        return pl.BlockSpec((None, rows, R), lambda b, s: (d, dm.chunk_block(b, d, s, rows), 0))

    def yspec(d):
        return pl.BlockSpec((rows, R), lambda b, s: (dm.chunk_block(b, d, s, rows), 0))

    y = jax.ShapeDtypeStruct((dm.M, R), BF16)
    return pl.pallas_call(
        functools.partial(_rwkv_scan_kernel, R // R_SLAB, rows // R_CHUNK),
        grid=(dm.B, steps),
        in_specs=[spec(0)] * 4 + [spec(1)] * 4,
        out_specs=[yspec(0), yspec(1)],
        out_shape=[y, y],
        scratch_shapes=[pltpu.VMEM((2, R_HEAD_DIM, R), F32)],
        compiler_params=_params("parallel", "arbitrary"),
        name="rwkv_scan",
    )(g_m, h_m, q_m, y0_m, g_m, h_m, q_m, y0_m)


def _rwkv_out_kernel(yf_ref, yb_ref, bonus_ref, g_ref, lnw_ref, lnb_ref, e1_ref, e2_ref, o_ref):
    o = yf_ref[...].astype(F32) + yb_ref[...].astype(F32)
    inv_n = 1.0 / R_HEAD_DIM
    mu = _head_sum(o, e1_ref, e2_ref) * inv_n
    oc = o - mu
    var = _head_sum(oc * oc, e1_ref, e2_ref) * inv_n
    o = oc * lax.rsqrt(var + R_GN_EPS) * lnw_ref[...] + lnb_ref[...]
    o_ref[...] = ((o + bonus_ref[...]) * g_ref[...]).astype(BF16)


def _rwkv_out(dm, y_f, y_b, bonus, g, p):
    tm, R, M = dm.tmr, dm.R, dm.M
    row = pl.BlockSpec((tm, R), lambda i: (i, 0))

    def full(a):
        return pl.BlockSpec(a.shape, lambda i: (0,) * a.ndim)

    consts = [p['ln_w'], p['ln_b'], p['e1'], p['e2']]
    return pl.pallas_call(
        _rwkv_out_kernel,
        grid=(M // tm,),
        in_specs=[row, row, row, row] + [full(a) for a in consts],
        out_specs=row,
        out_shape=jax.ShapeDtypeStruct((M, R), BF16),
        compiler_params=_params("parallel"),
        name="rwkv_out",
    )(y_f, y_b, bonus, g, *consts)


def _gla_kernel(dm, n_in, qf_ref, kf_ref, vf_ref, af_ref, qb_ref, kb_ref, vb_ref, ab_ref, wa_ref, ba_ref,
                yf_ref, yb_ref, st_ref, qd_ref, ki_ref, ke_ref, y_ref):
    C = G_CHUNK
    dk, dv = dm.GK // G_HEADS, dm.GV // G_HEADS
    key = [slice(h * dk, (h + 1) * dk) for h in range(G_HEADS)]
    val = [slice(h * dv, (h + 1) * dv) for h in range(G_HEADS)]
    row = [slice(c * C, (c + 1) * C) for c in range(n_in)]
    dirs = ((qf_ref, kf_ref, vf_ref, af_ref, yf_ref), (qb_ref, kb_ref, vb_ref, ab_ref, yb_ref))

    @pl.when(pl.program_id(1) == 0)
    def _():
        st_ref[...] = jnp.zeros_like(st_ref)

    ti = lax.broadcasted_iota(jnp.int32, (C, C), 0)
    si = lax.broadcasted_iota(jnp.int32, (C, C), 1)
    causal = (ti >= si, ti <= si)

    la = [_log_sigmoid(_dot(refs[3][...], wa_ref[d, 0]) + _dot(refs[3][...], wa_ref[d, 1]) + ba_ref[d])
          * (1.0 / G_GATE_NORM) for d, refs in enumerate(dirs)]
    cb = [[_cumsum_dot(causal[d].astype(BF16), la[d][r]) for r in row] for d in range(2)]
    e_last = []
    for d, refs in enumerate(dirs):
        cum = jnp.concatenate(cb[d], axis=0)
        tots = [jnp.sum(la[d][r], axis=0, keepdims=True) for r in row]
        tot = jnp.concatenate([jnp.broadcast_to(t, (C, t.shape[1])) for t in tots], axis=0)
        k = refs[1][...].astype(F32)
        qd_ref[d] = (refs[0][...].astype(F32) * (dk ** -0.5) * jnp.exp(cum)).astype(BF16)
        ki_ref[d] = (k * jnp.exp(-cum)).astype(BF16)
        ke_ref[d] = (k * jnp.exp(tot - cum)).astype(BF16)
        e_last.append([jnp.exp(t) for t in tots])
    for d, refs in enumerate(dirs):
        att = [[jnp.where(causal[d], _dot(qd_ref[d, r, ks], ki_ref[d, r, ks], NT), 0.0).astype(BF16) for ks in key]
               for r in row]
        for c, r in enumerate(row):
            for h in range(G_HEADS):
                y_ref[d, r, val[h]] = _dot(att[c][h], refs[2][r, val[h]])
    for t in range(n_in):
        todo = [(d, refs, (t, n_in - 1 - t)[d], h) for d, refs in enumerate(dirs) for h in range(G_HEADS)]
        inter = [_dot(qd_ref[d, row[c], key[h]], st_ref[d, h].astype(BF16), NT) for d, _, c, h in todo]
        upd = [_dot(refs[2][row[c], val[h]], ke_ref[d, row[c], key[h]], TN) for d, refs, c, h in todo]
        for (d, refs, c, h), o in zip(todo, inter):
            y_ref[d, row[c], val[h]] += o
        for (d, refs, c, h), u in zip(todo, upd):
            st_ref[d, h] = st_ref[d, h] * e_last[d][c][:, key[h]] + u
    yf_ref[...] = y_ref[0].astype(BF16)
    yb_ref[...] = y_ref[1].astype(BF16)


def _gla(dm, z, wa_split, ba):
    rows = dm.tmr
    steps = (dm.CT + dm.S) // rows
    GK, GV = dm.GK, dm.GV

    def zspec(width, col, d):
        return pl.BlockSpec((rows, width), lambda b, s: (dm.chunk_block(b, d, s, rows), col // width))

    def zspecs(d):
        return [zspec(GK, dm.c_gla_q, d), zspec(GK, dm.c_gla_k, d), zspec(GV, dm.c_gla_v, d),
                zspec(LORA_PAD, dm.c_gla_ac, d)]

    def yspec(d):
        return pl.BlockSpec((rows, GV), lambda b, s: (dm.chunk_block(b, d, s, rows), 0))

    y = jax.ShapeDtypeStruct((dm.M, GV), BF16)
    return pl.pallas_call(
        functools.partial(_gla_kernel, dm, rows // G_CHUNK),
        grid=(dm.B, steps),
        in_specs=zspecs(0) + zspecs(1) + [
            pl.BlockSpec((2, 2, LORA_PAD, GK), lambda b, s: (0, 0, 0, 0)),
            pl.BlockSpec((2, 1, GK), lambda b, s: (0, 0, 0)),
        ],
        out_specs=[yspec(0), yspec(1)],
        out_shape=[y, y],
        scratch_shapes=[pltpu.VMEM((2, G_HEADS, GV // G_HEADS, GK // G_HEADS), F32),
                        pltpu.VMEM((2, rows, GK), BF16), pltpu.VMEM((2, rows, GK), BF16),
                        pltpu.VMEM((2, rows, GK), BF16), pltpu.VMEM((2, rows, GV), F32)],
        compiler_params=_params("parallel", "arbitrary"),
        name="gla",
    )(z, z, z, z, z, z, z, z, wa_split, ba)


def _gla_out_kernel(dv, yf_ref, yb_ref, g_ref, nw_ref, o_ref):
    o = yf_ref[...].astype(F32) + yb_ref[...].astype(F32)
    g = g_ref[...].astype(F32)
    gate = g * _sigmoid(g)
    for h in range(G_HEADS):
        vs = slice(h * dv, (h + 1) * dv)
        oh = o[:, vs]
        oh = oh * lax.rsqrt(jnp.mean(oh * oh, -1, keepdims=True) + G_EPS) * nw_ref[...]
        o_ref[:, vs] = (oh * gate[:, vs]).astype(BF16)


def _gla_out(dm, y_f, y_b, z, norm_w):
    tm, GV, M = dm.tmr, dm.GV, dm.M
    dv = GV // G_HEADS
    return pl.pallas_call(
        functools.partial(_gla_out_kernel, dv),
        grid=(M // tm,),
        in_specs=[
            pl.BlockSpec((tm, GV), lambda i: (i, 0)),
            pl.BlockSpec((tm, GV), lambda i: (i, 0)),
            pl.BlockSpec((tm, GV), lambda i: (i, dm.c_gla_g // GV)),
            pl.BlockSpec((1, dv), lambda i: (0, 0)),
        ],
        out_specs=pl.BlockSpec((tm, GV), lambda i: (i, 0)),
        out_shape=jax.ShapeDtypeStruct((M, GV), BF16),
        compiler_params=_params("parallel"),
        name="gla_out",
    )(y_f, y_b, z, norm_w.reshape(1, dv))


def _merge_kernel(alpha, n_chunks, x_ref, mg_ref, oa_ref, or_ref, og_ref, ga_ref, gr_ref, gg_ref, wa_ref, wr_ref,
                  wg_ref, wo_ref, g_ref, b_ref, o_ref):
    tn = wo_ref.shape[0] // n_chunks
    oa, orw, og = oa_ref[...], or_ref[...], og_ref[...]
    acc = None
    for n in range(n_chunks):
        cs = slice(n * tn, (n + 1) * tn)
        merged = (_sigmoid(ga_ref[:, cs].astype(F32)) * _dot(oa, wa_ref[:, cs])
                  + _sigmoid(gr_ref[:, cs].astype(F32)) * _dot(orw, wr_ref[:, cs])
                  + _sigmoid(gg_ref[:, cs].astype(F32)) * _dot(og, wg_ref[:, cs]))
        part = _dot(merged.astype(BF16), wo_ref[cs, :])
        acc = part if acc is None else acc + part
    z = alpha * x_ref[...] + mg_ref[...] * acc
    o_ref[...] = _ln(z) * g_ref[...] + b_ref[...]


def _merge(dm, alpha, x, mods, z, o_att, o_rwkv, o_gla, wba, wbr, wbg, w_out, l, g, b, lat_only):
    D, tm = dm.D, dm.tmr
    off = dm.MC // tm if lat_only else 0
    rows = dm.M - off * tm
    gate0 = dm.c_gate // D

    def row_spec(width, col=0):
        return pl.BlockSpec((tm, width), lambda i: (i + off, col))

    def weight(k_rows):
        return pl.BlockSpec((None, k_rows, D), lambda i: (l, 0, 0), pipeline_mode=pl.Buffered(1))

    return pl.pallas_call(
        functools.partial(_merge_kernel, alpha, 4),
        grid=(rows // tm,),
        in_specs=[
            row_spec(D), dm.mod_spec(5, tm, off),
            row_spec(A_Q_DIM), row_spec(dm.R), row_spec(dm.GV),
            row_spec(D, gate0), row_spec(D, gate0 + 1), row_spec(D, gate0 + 2),
            weight(A_Q_DIM), weight(dm.R), weight(dm.GV), weight(D),
            pl.BlockSpec((1, D), lambda i: (0, 0)),
            pl.BlockSpec((1, D), lambda i: (0, 0)),
        ],
        out_specs=pl.BlockSpec((tm, D), lambda i: (i, 0)),
        out_shape=jax.ShapeDtypeStruct((rows, D), F32),
        compiler_params=_params("parallel"),
        name="merge",
    )(x, mods, o_att, o_rwkv, o_gla, z, z, z, wba, wbr, wbg, w_out, g.reshape(1, D), b.reshape(1, D))


def _pad_to(a, size, axis):
    pad = [(0, 0)] * a.ndim
    pad[axis] = (0, size - a.shape[axis])
    return jnp.pad(a, pad)


def _rwkv_cols(a, R):
    c = 3 * R
    rkv = a[..., :c]
    wc0, wc1 = a[..., c:c + DECAY_LORA], a[..., c + DECAY_LORA:c + 2 * DECAY_LORA]
    c += 2 * DECAY_LORA
    ac0, ac1 = a[..., c:c + ICLR_LORA], a[..., c + ICLR_LORA:c + 2 * ICLR_LORA]
    c += 2 * ICLR_LORA
    gc = a[..., c:c + GATE_LORA]
    ax = a.ndim - 1
    return jnp.concatenate([rkv, gc] + [_pad_to(t, LORA_PAD, ax) for t in (wc0, wc1, ac0, ac1)], axis=-1)


def _in_proj_weights(dm, w_in):
    D, R, GK, GV = dm.D, dm.R, dm.GK, dm.GV
    att_cols = A_Q_DIM + 2 * A_KV_DIM
    rw_src = 3 * R + 2 * DECAY_LORA + 2 * ICLR_LORA + GATE_LORA
    gla_src = 2 * GK + 2 * GV + 2 * G_GATE_RANK
    att, rwkv, gla, gate = jnp.split(w_in, np.cumsum([att_cols, rw_src, gla_src]).tolist(), axis=-1)
    a_q, a_k, a_v = att[:, :A_Q_DIM], att[:, A_Q_DIM:A_Q_DIM + A_KV_DIM], att[:, A_Q_DIM + A_KV_DIM:]
    g_q, g_k = gla[:, :GK], gla[:, GK:2 * GK]
    g_v, g_g = gla[:, 2 * GK:2 * GK + GV], gla[:, 2 * GK + GV:2 * GK + 2 * GV]
    g_ac = _pad_to(gla[:, 2 * GK + 2 * GV:], LORA_PAD, 1)
    w = jnp.concatenate([_rwkv_cols(rwkv, R), a_k, gate, g_v, g_g, g_q, g_k, a_q, a_v, g_ac], axis=-1)
    return _pad_to(w, dm.NP, 1).astype(BF16)


def _rope_tables(S):
    rows = S // GRID_W
    row = jnp.repeat(jnp.arange(rows, dtype=F32), GRID_W)
    col = jnp.tile(jnp.arange(GRID_W, dtype=F32), rows)
    n_freq = A_HEAD_DIM // 4
    inv_freq = ROPE_BASE ** (-jnp.arange(n_freq, dtype=F32) / n_freq)
    ang_r, ang_c = row[:, None] * inv_freq, col[:, None] * inv_freq
    cos = jnp.concatenate([jnp.cos(ang_r)] * 2 + [jnp.cos(ang_c)] * 2, axis=-1)
    sin = jnp.concatenate([-jnp.sin(ang_r), jnp.sin(ang_r), -jnp.sin(ang_c), jnp.sin(ang_c)], axis=-1)
    return cos, sin


def kernel(x, c, ctx, c_ctx, ada_w, ada_b, ln_g, ln_b, ffn_w_gu, ffn_w_down, w_in, attn_sink, rwkv_conv, rwkv_w0,
           rwkv_w2, rwkv_a0, rwkv_a2, rwkv_g2, rwkv_k_k, rwkv_k_a, rwkv_r_k, rwkv_ln_w, rwkv_ln_b, gla_wa2, gla_ba,
           gla_norm_w, w_branch_att, w_branch_rwkv, w_branch_gla, w_out):
    B, S, D = x.shape
    CT = ctx.shape[1]
    depth = ada_w.shape[0]
    dm = _Dims(B, S, CT, D)
    R = dm.R
    alpha = float((2 * depth) ** 0.25)
    assert B + 1 <= MOD_ROWS

    cc = _pad_to(jnp.concatenate([c, c_ctx[None, :]], axis=0), MOD_ROWS, 0)
    mods_all = _ada(cc, ada_w, ada_b)
    cos, sin = _rope_tables(S)
    head_of_lane = jnp.arange(R) // R_HEAD_DIM
    e1 = (head_of_lane[:, None] == jnp.arange(LORA_PAD)[None, :]).astype(BF16)
    e2 = e1.T

    w_gu, w_dn = ffn_w_gu.astype(BF16), ffn_w_down.astype(BF16)
    wba, wbr = w_branch_att.astype(BF16), w_branch_rwkv.astype(BF16)
    wbg, wo = w_branch_gla.astype(BF16), w_out.astype(BF16)

    xs = jnp.concatenate([ctx.reshape(dm.MC, D), x.reshape(dm.ML, D)], axis=0)
    for l in range(depth):
        last = l == depth - 1
        mods = mods_all[l].reshape(MOD_ROWS * N_MOD, 1, D)
        xs = _ffn(dm, alpha, xs, mods, 0, w_gu, w_dn, l, 0, ln_g[l, 0], ln_b[l, 0])

        z = _inproj(dm, xs, mods, _in_proj_weights(dm, w_in[l]))
        o_att = jnp.concatenate([_attn_ctx(dm, z, attn_sink[l]), _attn_latent(dm, z, attn_sink[l], cos, sin)], axis=0)

        rp = {
            'conv': _rwkv_cols(rwkv_conv[l], R),
            'g2': rwkv_g2[l].astype(BF16),
            'w0': rwkv_w0[l], 'a0': rwkv_a0[l],
            'w2': _pad_to(rwkv_w2[l], LORA_PAD, 1).astype(BF16),
            'a2': _pad_to(rwkv_a2[l], LORA_PAD, 1).astype(BF16),
            'k_k': rwkv_k_k[l].reshape(1, R), 'k_a': rwkv_k_a[l].reshape(1, R), 'r_k': rwkv_r_k[l].reshape(1, R),
            'ln_w': rwkv_ln_w[l].reshape(1, R), 'ln_b': rwkv_ln_b[l].reshape(1, R),
            'e1': e1, 'e2': e2,
        }
        r, v, kk, lw, kd, b, bonus, g = _rwkv_prep(dm, z, rp)
        g_m, h_m, q_m, y0_m = _rwkv_chunk(dm, r, v, kk, lw, kd, b)
        y_rwkv_f, y_rwkv_b = _rwkv_scan(dm, g_m, h_m, q_m, y0_m)
        o_rwkv = _rwkv_out(dm, y_rwkv_f, y_rwkv_b, bonus, g, rp)

        wa_p = jnp.stack([_pad_to(jnp.pad(gla_wa2[l, d], ((d * G_GATE_RANK, 0), (0, 0))), LORA_PAD, 0)
                          for d in range(2)])
        wa_hi = wa_p.astype(BF16)
        wa_split = jnp.stack([wa_hi, (wa_p - wa_hi.astype(F32)).astype(BF16)], axis=1)
        y_gla_f, y_gla_b = _gla(dm, z, wa_split, gla_ba[l].reshape(2, 1, dm.GK))
        o_gla = _gla_out(dm, y_gla_f, y_gla_b, z, gla_norm_w[l])

        xs = _merge(dm, alpha, xs, mods, z, o_att, o_rwkv, o_gla, wba, wbr, wbg, wo, l, ln_g[l, 1], ln_b[l, 1], last)
        xs = _ffn(dm, alpha, xs, mods, 6, w_gu, w_dn, l, 1, ln_g[l, 2], ln_b[l, 2])
    return xs.reshape(B, S, D)
```

```python
import functools
import math

import jax
import jax.numpy as jnp
import numpy as np
from jax import lax
from jax.experimental import pallas as pl
from jax.experimental.pallas import tpu as pltpu

F32 = jnp.float32
BF16 = jnp.bfloat16
HI = lax.Precision.HIGHEST

N_MOD = 9
D_FF = 5632
LN_EPS = 1e-5
FFN_RES = 0.5

A_HEADS = 8
A_KV_HEADS = 2
A_GROUP = A_HEADS // A_KV_HEADS
A_HEAD_DIM = 128
A_Q_DIM = A_HEADS * A_HEAD_DIM
A_KV_DIM = A_KV_HEADS * A_HEAD_DIM
WINDOW = 128
A_BLOCK = 128
GRID_W = 64
ROPE_BASE = 10000.0
MASK_VALUE = -1e30

R_HEAD_DIM = 64
DECAY_LORA = 96
ICLR_LORA = 96
GATE_LORA = 256
R_GN_EPS = 64e-5
R_CHUNK = 64
LORA_PAD = 128
SUBLANES = 8
HALO = 16

G_HEADS = 4
G_GATE_RANK = 16
G_GATE_NORM = 16.0
G_CHUNK = 64
G_EPS = 1e-5

MOD_ROWS = 16
VMEM_LIMIT = 58 * 1024 * 1024
VMEM_LIMIT_FFN = 63 * 1024 * 1024

NN = (((1,), (0,)), ((), ()))
NT = (((1,), (1,)), ((), ()))
TN = (((0,), (0,)), ((), ()))


def _dot(a, b, dims=NN, prec=None):
    return lax.dot_general(a, b, dims, precision=prec, preferred_element_type=F32)


def _cumsum_dot(tri, x):
    hi = x.astype(BF16)
    rest = x - hi.astype(F32)
    mid = rest.astype(BF16)
    lo = (rest - mid.astype(F32)).astype(BF16)
    return _dot(tri, hi) + _dot(tri, mid) + _dot(tri, lo)


def _sigmoid(x):
    return 1.0 / (1.0 + jnp.exp(-x))


def _log_sigmoid(x):
    return jnp.minimum(x, 0.0) - jnp.log(1.0 + jnp.exp(-jnp.abs(x)))


def _ln(x):
    mu = jnp.mean(x, -1, keepdims=True)
    xc = x - mu
    var = jnp.mean(xc * xc, -1, keepdims=True)
    return xc * lax.rsqrt(var + LN_EPS)


def _params(*sem, vmem=VMEM_LIMIT):
    return pltpu.CompilerParams(dimension_semantics=sem, vmem_limit_bytes=vmem)


class _Dims:
    def __init__(self, B, S, CT, D):
        self.B, self.S, self.CT, self.D = B, S, CT, D
        self.MC, self.ML = B * CT, B * S
        self.M = self.MC + self.ML
        self.R = D // 2
        self.GK = D // 2
        self.GV = D
        self.RH = self.R // R_HEAD_DIM
        self.rw_cols = 3 * self.R + GATE_LORA + 4 * LORA_PAD
        c = 0
        self.c_rwkv = c; c += self.rw_cols
        self.c_att_k = c; c += A_KV_DIM
        self.c_gate = c; c += 3 * D
        self.c_gla_v = c; c += self.GV
        self.c_gla_g = c; c += self.GV
        self.c_gla_q = c; c += self.GK
        self.c_gla_k = c; c += self.GK
        self.c_att_q = c; c += A_Q_DIM
        self.c_att_v = c; c += A_KV_DIM
        self.c_gla_ac = c; c += LORA_PAD
        self.tn_in = 1280
        self.NP = -(-c // self.tn_in) * self.tn_in
        self.tm = math.gcd(512, math.gcd(self.MC, S))
        self.tm_in = math.gcd(1024, math.gcd(self.MC, S))
        self.tmr = math.gcd(256, math.gcd(CT, S))

    def mod_row(self, i, tm):
        nct = self.MC // tm
        tpb = self.S // tm
        return jnp.where(i < nct, self.B, (i - nct) // tpb)

    def mod_spec(self, m, tm, off=0):
        return pl.BlockSpec((None, 1, self.D), lambda i, *_: (self.mod_row(i + off, tm) * N_MOD + m, 0, 0))

    def chunk_block(self, b, d, s, chunk):
        ncc, ncl = self.CT // chunk, self.S // chunk
        j = s - ncc
        ctx_blk = b * ncc + jnp.where(d == 0, s, ncc - 1 - s)
        lat_blk = self.B * ncc + b * ncl + jnp.where(d == 0, j, ncl - 1 - j)
        return jnp.where(s < ncc, ctx_blk, lat_blk)


def _ada_kernel(c_ref, w_ref, b_ref, o_ref):
    cc = c_ref[...]
    s = (cc * _sigmoid(cc)).astype(BF16)
    o_ref[...] = _dot(s, w_ref[...].astype(BF16)) + b_ref[...]


def _ada(cc, ada_w, ada_b):
    L, D, N = ada_w.shape
    tn = 2048
    return pl.pallas_call(
        _ada_kernel,
        grid=(L, N // tn),
        in_specs=[
            pl.BlockSpec((MOD_ROWS, D), lambda l, n: (0, 0)),
            pl.BlockSpec((None, D, tn), lambda l, n: (l, 0, n)),
            pl.BlockSpec((None, 1, tn), lambda l, n: (l, 0, n)),
        ],
        out_specs=pl.BlockSpec((None, MOD_ROWS, tn), lambda l, n: (l, 0, n)),
        out_shape=jax.ShapeDtypeStruct((L, MOD_ROWS, N), F32),
        compiler_params=_params("parallel", "arbitrary"),
        name="ada",
    )(cc, ada_w, ada_b.reshape(L, 1, N))


def _ffn_kernel(alpha, x_ref, sh_ref, sc_ref, gt_ref, wg_ref, wu_ref, wd_ref, g_ref, b_ref, o_ref, h_ref):
    f = pl.program_id(1)

    @pl.when(f == 0)
    def _():
        h = _ln(x_ref[...]) * (1.0 + sc_ref[...]) + sh_ref[...]
        h_ref[...] = h.astype(BF16)
        o_ref[...] = jnp.zeros_like(o_ref)

    h = h_ref[...]
    g = _dot(h, wg_ref[...])
    u = _dot(h, wu_ref[...])
    a = (g * _sigmoid(g) * u).astype(BF16)
    o_ref[...] += _dot(a, wd_ref[...])

    @pl.when(f == pl.num_programs(1) - 1)
    def _():
        z = alpha * x_ref[...] + (FFN_RES * gt_ref[...]) * o_ref[...]
        o_ref[...] = _ln(z) * g_ref[...] + b_ref[...]


def _ffn(dm, alpha, x, mods, m0, w_gu, w_down, l, j, g, b):
    D, tm, tf = dm.D, dm.tm_in, 512
    nf = D_FF // tf
    rows = x.shape[0]
    off = (dm.M - rows) // tm
    return pl.pallas_call(
        functools.partial(_ffn_kernel, alpha),
        grid=(rows // tm, nf),
        in_specs=[
            pl.BlockSpec((tm, D), lambda i, f: (i, 0)),
            dm.mod_spec(m0, tm, off), dm.mod_spec(m0 + 1, tm, off), dm.mod_spec(m0 + 2, tm, off),
            pl.BlockSpec((None, None, D, tf), lambda i, f: (l, j, 0, f)),
            pl.BlockSpec((None, None, D, tf), lambda i, f: (l, j, 0, f + nf)),
            pl.BlockSpec((None, None, tf, D), lambda i, f: (l, j, f, 0)),
            pl.BlockSpec((1, D), lambda i, f: (0, 0)),
            pl.BlockSpec((1, D), lambda i, f: (0, 0)),
        ],
        out_specs=pl.BlockSpec((tm, D), lambda i, f: (i, 0)),
        out_shape=jax.ShapeDtypeStruct((rows, D), F32),
        scratch_shapes=[pltpu.VMEM((tm, D), BF16)],
        compiler_params=_params("parallel", "arbitrary", vmem=VMEM_LIMIT_FFN),
        name="ffn",
    )(x, mods, mods, mods, w_gu, w_gu, w_down, g.reshape(1, D), b.reshape(1, D))


def _inproj_kernel(x_ref, sh_ref, sc_ref, w_ref, o_ref, h_ref):
    @pl.when(pl.program_id(1) == 0)
    def _():
        h = _ln(x_ref[...]) * (1.0 + sc_ref[...]) + sh_ref[...]
        h_ref[...] = h.astype(BF16)

    o_ref[...] = _dot(h_ref[...], w_ref[...]).astype(BF16)


def _inproj(dm, x, mods, w_in_p):
    D, tm, tn = dm.D, dm.tm_in, dm.tn_in
    return pl.pallas_call(
        _inproj_kernel,
        grid=(dm.M // tm, dm.NP // tn),
        in_specs=[
            pl.BlockSpec((tm, D), lambda i, n: (i, 0)),
            dm.mod_spec(3, tm), dm.mod_spec(4, tm),
            pl.BlockSpec((D, tn), lambda i, n: (0, n)),
        ],
        out_specs=pl.BlockSpec((tm, tn), lambda i, n: (i, n)),
        out_shape=jax.ShapeDtypeStruct((dm.M, dm.NP), BF16),
        scratch_shapes=[pltpu.VMEM((tm, D), BF16)],
        compiler_params=_params("parallel", "arbitrary"),
        name="inproj",
    )(x, mods, mods, w_in_p)


def _rope(x, cos, sin_signed):
    lane = lax.broadcasted_iota(jnp.int32, x.shape, 1)
    quarter = A_HEAD_DIM // 4
    swapped = jnp.where((lane % (2 * quarter)) < quarter,
                        pltpu.roll(x, A_HEAD_DIM - quarter, 1), pltpu.roll(x, quarter, 1))
    return x * cos + swapped * sin_signed


def _attn_kernel(S, nb, sink_ref, q_ref, kp_ref, kc_ref, kn_ref, vp_ref, vc_ref, vn_ref, kx_ref, vx_ref,
                 cos_ref, sin_ref, o_ref):
    n = pl.program_id(1)
    scale = A_HEAD_DIM ** -0.5

    def table(ref, blk):
        return ref[pl.ds(pl.multiple_of(blk * A_BLOCK, A_BLOCK), A_BLOCK), :]

    blk_p, blk_n = jnp.maximum(n - 1, 0), jnp.minimum(n + 1, nb - 1)
    cos_c, sin_c = table(cos_ref, n), table(sin_ref, n)
    cos_p, sin_p = table(cos_ref, blk_p), table(sin_ref, blk_p)
    cos_n, sin_n = table(cos_ref, blk_n), table(sin_ref, blk_n)

    rows, band = A_GROUP * A_BLOCK, 3 * A_BLOCK
    qpos = n * A_BLOCK + lax.broadcasted_iota(jnp.int32, (rows, band), 0) % A_BLOCK
    kpos = (n - 1) * A_BLOCK + lax.broadcasted_iota(jnp.int32, (rows, band), 1)
    valid = (jnp.abs(qpos - kpos) <= WINDOW) & (kpos >= 0) & (kpos < S)

    work = []
    for kvh in range(A_KV_HEADS):
        ks = slice(kvh * A_HEAD_DIM, (kvh + 1) * A_HEAD_DIM)
        kb = jnp.concatenate([_rope(kp_ref[:, ks].astype(F32), cos_p, sin_p),
                              _rope(kc_ref[:, ks].astype(F32), cos_c, sin_c),
                              _rope(kn_ref[:, ks].astype(F32), cos_n, sin_n)], axis=0).astype(BF16)
        heads = [kvh * A_GROUP + g for g in range(A_GROUP)]
        qs = [q_ref[:, h * A_HEAD_DIM:(h + 1) * A_HEAD_DIM] for h in heads]
        q_rope = jnp.concatenate([_rope(q.astype(F32), cos_c, sin_c) * scale for q in qs], axis=0).astype(BF16)
        q_plain = jnp.concatenate([q.astype(F32) * scale for q in qs], axis=0).astype(BF16)
        sink = jnp.concatenate([jnp.full((A_BLOCK, 1), sink_ref[h], F32) for h in heads], axis=0)
        s_loc = jnp.where(valid, _dot(q_rope, kb, NT), MASK_VALUE)
        s_ctx = _dot(q_plain, kx_ref[:, ks], NT)
        work.append(dict(ks=ks, heads=heads, sink=sink, s_loc=s_loc, s_ctx=s_ctx))
    for w in work:
        w['m'] = jnp.maximum(jnp.maximum(jnp.max(w['s_loc'], -1, keepdims=True),
                                         jnp.max(w['s_ctx'], -1, keepdims=True)), w['sink'])
    for w in work:
        w['p_loc'] = jnp.exp(w.pop('s_loc') - w['m'])
        w['p_ctx'] = jnp.exp(w.pop('s_ctx') - w['m'])
    for w in work:
        w['den'] = (jnp.sum(w['p_loc'], -1, keepdims=True) + jnp.sum(w['p_ctx'], -1, keepdims=True)
                    + jnp.exp(w['sink'] - w['m']))
    for w in work:
        ks = w['ks']
        vb = jnp.concatenate([vp_ref[:, ks], vc_ref[:, ks], vn_ref[:, ks]], axis=0)
        o = (_dot(w['p_loc'].astype(BF16), vb) + _dot(w['p_ctx'].astype(BF16), vx_ref[:, ks])) / w['den']
        for g, h in enumerate(w['heads']):
            o_ref[:, h * A_HEAD_DIM:(h + 1) * A_HEAD_DIM] = o[g * A_BLOCK:(g + 1) * A_BLOCK].astype(BF16)


def _attn_latent(dm, z, sink, cos, sin):
    B, S = dm.B, dm.S
    nb = S // A_BLOCK
    base = dm.MC // A_BLOCK
    qc = dm.c_att_q // A_Q_DIM
    kc = dm.c_att_k // A_KV_DIM
    vc = dm.c_att_v // A_KV_DIM

    def rows(shift):
        return lambda b, n: base + b * nb + jnp.clip(n + shift, 0, nb - 1)

    def kv_spec(col, shift):
        r = rows(shift)
        return pl.BlockSpec((A_BLOCK, A_KV_DIM), lambda b, n: (r(b, n), col))

    return pl.pallas_call(
        functools.partial(_attn_kernel, S, nb),
        grid=(B, nb),
        in_specs=[
            pl.BlockSpec(memory_space=pltpu.SMEM),
            pl.BlockSpec((A_BLOCK, A_Q_DIM), lambda b, n: (base + b * nb + n, qc)),
            kv_spec(kc, -1), kv_spec(kc, 0), kv_spec(kc, 1),
            kv_spec(vc, -1), kv_spec(vc, 0), kv_spec(vc, 1),
            pl.BlockSpec((dm.CT, A_KV_DIM), lambda b, n: (b, kc)),
            pl.BlockSpec((dm.CT, A_KV_DIM), lambda b, n: (b, vc)),
            pl.BlockSpec((S, A_HEAD_DIM), lambda b, n: (0, 0)),
            pl.BlockSpec((S, A_HEAD_DIM), lambda b, n: (0, 0)),
        ],
        out_specs=pl.BlockSpec((A_BLOCK, A_Q_DIM), lambda b, n: (b * nb + n, 0)),
        out_shape=jax.ShapeDtypeStruct((dm.ML, A_Q_DIM), BF16),
        compiler_params=_params("parallel", "arbitrary"),
        name="attn_latent",
    )(sink, z, z, z, z, z, z, z, z, z, cos, sin)


def _attn_ctx_kernel(CT, sink_ref, q_ref, k_ref, v_ref, o_ref):
    scale = A_HEAD_DIM ** -0.5
    for kvh in range(A_KV_HEADS):
        ks = slice(kvh * A_HEAD_DIM, (kvh + 1) * A_HEAD_DIM)
        k = k_ref[:, ks]
        v = v_ref[:, ks]
        heads = [kvh * A_GROUP + g for g in range(A_GROUP)]
        q = jnp.concatenate([q_ref[:, h * A_HEAD_DIM:(h + 1) * A_HEAD_DIM] for h in heads], axis=0)
        sink = jnp.concatenate([jnp.full((CT, 1), sink_ref[h], F32) for h in heads], axis=0)
        s = _dot(q, k, NT) * scale
        m = jnp.maximum(jnp.max(s, -1, keepdims=True), sink)
        p = jnp.exp(s - m)
        den = jnp.sum(p, -1, keepdims=True) + jnp.exp(sink - m)
        o = _dot(p.astype(BF16), v) / den
        for g, h in enumerate(heads):
            o_ref[:, h * A_HEAD_DIM:(h + 1) * A_HEAD_DIM] = o[g * CT:(g + 1) * CT].astype(BF16)


def _attn_ctx(dm, z, sink):
    CT = dm.CT
    return pl.pallas_call(
        functools.partial(_attn_ctx_kernel, CT),
        grid=(dm.B,),
        in_specs=[
            pl.BlockSpec(memory_space=pltpu.SMEM),
            pl.BlockSpec((CT, A_Q_DIM), lambda b: (b, dm.c_att_q // A_Q_DIM)),
            pl.BlockSpec((CT, A_KV_DIM), lambda b: (b, dm.c_att_k // A_KV_DIM)),
            pl.BlockSpec((CT, A_KV_DIM), lambda b: (b, dm.c_att_v // A_KV_DIM)),
        ],
        out_specs=pl.BlockSpec((CT, A_Q_DIM), lambda b: (b, 0)),
        out_shape=jax.ShapeDtypeStruct((dm.MC, A_Q_DIM), BF16),
        compiler_params=_params("parallel"),
        name="attn_ctx",
    )(sink, z, z, z)


def _head_sum(x, e1_ref, e2_ref):
    def split_dot(a, ind):
        hi = a.astype(BF16)
        lo = (a - hi.astype(F32)).astype(BF16)
        return _dot(hi, ind) + _dot(lo, ind)

    return split_dot(split_dot(x, e1_ref[...]), e2_ref[...])


def _rwkv_prep_kernel(dm, z_ref, zp_ref, zn_ref, cw_ref, g2_ref, w0_ref, w2_ref, a0_ref, a2_ref, kk_ref, ka_ref,
                      rk_ref, e1_ref, e2_ref,
                      r_out, v_out, kk_out, lw_out, kd_out, b_out, bonus_out, g_out):
    i = pl.program_id(0)
    tm, R = dm.tmr, dm.R
    nct = dm.MC // tm
    start = jnp.where(i < nct, (i * tm) % dm.CT, ((i - nct) * tm) % dm.S)
    seqlen = jnp.where(i < nct, dm.CT, dm.S)
    has_prev = (start != 0).astype(F32)
    has_next = (start + tm != seqlen).astype(F32)

    z = z_ref[...].astype(F32)
    sub = lax.broadcasted_iota(jnp.int32, (SUBLANES, 1), 0)
    halo_prev = zp_ref[HALO - 1:HALO, :].astype(F32) * has_prev
    halo_next = zn_ref[0:1, :].astype(F32) * has_next
    down, up = pltpu.roll(z, 1, 0), pltpu.roll(z, tm - 1, 0)
    z_prev = jnp.concatenate([jnp.where(sub == 0, halo_prev, down[:SUBLANES]), down[SUBLANES:]], axis=0)
    z_next = jnp.concatenate([up[:tm - SUBLANES], jnp.where(sub == SUBLANES - 1, halo_next, up[tm - SUBLANES:])],
                             axis=0)
    zc = z_prev * cw_ref[0:1, :] + z * cw_ref[1:2, :] + z_next * cw_ref[2:3, :]

    r = zc[:, 0:R]
    k = zc[:, R:2 * R]
    v = zc[:, 2 * R:3 * R]
    c0 = 3 * R
    gc = zc[:, c0:c0 + GATE_LORA]
    c0 += GATE_LORA
    g_out[...] = _dot(_sigmoid(gc).astype(BF16), g2_ref[...])

    kkr = k * kk_ref[...]
    norm = jnp.sqrt(_head_sum(kkr * kkr, e1_ref, e2_ref))
    kk = kkr / jnp.maximum(norm, 1e-12)
    r_out[...] = r
    v_out[...] = v
    kk_out[...] = kk

    bonus = jnp.zeros_like(v)
    for d in range(2):
        wc = zc[:, c0 + d * LORA_PAD:c0 + (d + 1) * LORA_PAD]
        ac = zc[:, c0 + (2 + d) * LORA_PAD:c0 + (3 + d) * LORA_PAD]
        wl = w0_ref[d:d + 1, :] + _dot(jnp.tanh(wc).astype(BF16), w2_ref[d])
        lw_out[d] = _sigmoid(wl) * (-math.exp(-0.5))
        a = _sigmoid(a0_ref[d:d + 1, :] + _dot(ac.astype(BF16), a2_ref[d]))
        kd = k * (1.0 + (a - 1.0) * ka_ref[...])
        kd_out[d] = kd
        b_out[d] = kk * a
        bonus = bonus + _head_sum(r * kd * rk_ref[...], e1_ref, e2_ref) * v
    bonus_out[...] = bonus


def _rwkv_prep(dm, z, p):
    tm, R, M, W = dm.tmr, dm.R, dm.M, dm.rw_cols
    n_halo = M // HALO
    row = pl.BlockSpec((tm, R), lambda i: (i, 0))
    row2 = pl.BlockSpec((2, tm, R), lambda i: (0, i, 0))

    def full(a):
        return pl.BlockSpec(a.shape, lambda i: (0,) * a.ndim)

    consts = [p['conv'], p['g2'], p['w0'], p['w2'], p['a0'], p['a2'], p['k_k'], p['k_a'], p['r_k'], p['e1'], p['e2']]
    one = jax.ShapeDtypeStruct((M, R), F32)
    two = jax.ShapeDtypeStruct((2, M, R), F32)
    return pl.pallas_call(
        functools.partial(_rwkv_prep_kernel, dm),
        grid=(M // tm,),
        in_specs=[
            pl.BlockSpec((tm, W), lambda i: (i, 0)),
            pl.BlockSpec((HALO, W), lambda i: (jnp.maximum(i * (tm // HALO) - 1, 0), 0)),
            pl.BlockSpec((HALO, W), lambda i: (jnp.minimum((i + 1) * (tm // HALO), n_halo - 1), 0)),
        ] + [full(a) for a in consts],
        out_specs=[row, row, row, row2, row2, row2, row, row],
        out_shape=[one, one, one, two, two, two, one, one],
        compiler_params=_params("parallel"),
        name="rwkv_prep",
    )(z, z, z, *consts)


R_SLAB = 4 * R_HEAD_DIM


def _block_diag(x, mask):
    return jnp.where(mask, jnp.concatenate([x] * 4, axis=0), 0.0).astype(BF16)


def _diag_blocks(full, mask):
    fm = jnp.where(mask, full, 0.0)
    n = R_HEAD_DIM
    return fm[0:n] + fm[n:2 * n] + fm[2 * n:3 * n] + fm[3 * n:4 * n]


def _slab_masks():
    n, g = R_HEAD_DIM, R_SLAB
    blk = (lax.broadcasted_iota(jnp.int32, (g, g), 0) // n) == (lax.broadcasted_iota(jnp.int32, (g, g), 1) // n)
    t_idx = lax.broadcasted_iota(jnp.int32, (n, g), 0)
    s_idx = lax.broadcasted_iota(jnp.int32, (n, g), 1) % n
    return blk, t_idx, s_idx


def _rwkv_chunk_kernel(n_sub, r_ref, v_ref, kk_ref, lw_ref, kd_ref, b_ref, g_out, h_out, q_out, y0_out):
    C = R_CHUNK
    blk, t_idx, s_idx = _slab_masks()
    eye4 = (t_idx == s_idx).astype(F32)
    ti = lax.broadcasted_iota(jnp.int32, (C, C), 0)
    si = lax.broadcasted_iota(jnp.int32, (C, C), 1)

    def bd(x):
        return _block_diag(x, blk)

    def setup(c, d):
        rs = pl.ds(c * C, C)
        r, v, kk = r_ref[rs, :], v_ref[rs, :], kk_ref[rs, :]
        tri = ((ti >= si) if d == 0 else (ti <= si)).astype(BF16)
        lw = lw_ref[d, rs, :]
        cl = _cumsum_dot(tri, lw)
        tot = jnp.sum(lw, axis=0, keepdims=True)
        e_neg = jnp.exp(-cl)
        e_end = jnp.exp(tot - cl)
        kd, b = kd_ref[d, rs, :], b_ref[d, rs, :]
        ch = dict(rs=rs, d=d, tot=tot, v_b=v.astype(BF16), v_bd=bd(v),
                  incl=(t_idx >= s_idx) if d == 0 else (t_idx <= s_idx),
                  strict=(t_idx > s_idx) if d == 0 else (t_idx < s_idx),
                  kkm=kk * jnp.exp(cl - lw), rp=r * jnp.exp(cl),
                  bp=b * e_neg, kp=kd * e_neg, be=(b * e_end).astype(BF16), ke=(kd * e_end).astype(BF16))
        ch['lhs'] = jnp.concatenate([ch['kkm'], ch['rp']], axis=0).astype(BF16)
        return ch

    chains = [setup(c, d) for c in range(n_sub) for d in range(2)]
    for ch in chains:
        gram_b = _dot(ch['lhs'], bd(ch.pop('bp')), NT)
        gram_k = _dot(ch.pop('lhs'), bd(ch.pop('kp')), NT)
        lb = jnp.where(ch['strict'], gram_b[:C], 0.0)
        ch['mb'] = jnp.where(ch['incl'], gram_b[C:], 0.0).astype(BF16)
        ch['lmk'] = jnp.concatenate([jnp.where(ch['strict'], gram_k[:C], 0.0),
                                     jnp.where(ch['incl'], gram_k[C:], 0.0)], axis=0).astype(BF16)
        ch['pw'] = -lb
        ch['t_inv'] = eye4 - lb
    for ch in chains:
        ch['pw'] = _dot(ch['pw'].astype(BF16), bd(ch['pw']))
        lmkv = _dot(ch.pop('lmk'), ch.pop('v_bd'))
        ch['lkv'], ch['mkv'] = lmkv[:C], lmkv[C:]
    for _ in range(int(math.log2(C)) - 2):
        for ch in chains:
            both = _dot(jnp.concatenate([ch['pw'], ch['t_inv']], axis=0).astype(BF16), bd(ch['pw']))
            ch['pw'], ch['t_inv'] = both[:C], ch['t_inv'] + both[C:]
    for ch in chains:
        ch['t_inv'] = (ch['t_inv'] + _dot(ch['t_inv'].astype(BF16), bd(ch.pop('pw')))).astype(BF16)
    for ch in chains:
        ch['a_m'] = _dot(ch['t_inv'], bd(ch.pop('kkm')))
        ch['d_m'] = -_dot(ch.pop('t_inv'), bd(ch.pop('lkv')))
    for ch in chains:
        d, rs = ch['d'], ch['rs']
        a_m, d_m = ch['a_m'], ch['d_m']
        q_out[d, rs, :] = (ch['rp'] - _dot(ch['mb'], bd(a_m))).astype(BF16)
        y0_out[d, rs, :] = (_dot(ch['mb'], bd(d_m)) + ch['mkv']).astype(BF16)
        g_cross = _dot(ch['be'], a_m.astype(BF16), TN)
        h_cross = _dot(jnp.concatenate([ch['be'], ch['ke']], axis=0),
                       jnp.concatenate([d_m.astype(BF16), ch['v_b']], axis=0), TN)
        g_out[d, rs, :] = (eye4 * jnp.exp(ch['tot']) - _diag_blocks(g_cross, blk)).astype(BF16)
        h_out[d, rs, :] = _diag_blocks(h_cross, blk)


def _rwkv_chunk(dm, r, v, kk, lw, kd, b):
    M, R = dm.M, dm.R
    rows = dm.tmr
    one = pl.BlockSpec((rows, R_SLAB), lambda i, h: (i, h))
    two = pl.BlockSpec((2, rows, R_SLAB), lambda i, h: (0, i, h))
    f32 = jax.ShapeDtypeStruct((2, M, R), F32)
    b16 = jax.ShapeDtypeStruct((2, M, R), BF16)
    return pl.pallas_call(
        functools.partial(_rwkv_chunk_kernel, rows // R_CHUNK),
        grid=(M // rows, R // R_SLAB),
        in_specs=[one, one, one, two, two, two],
        out_specs=[two, two, two, two],
        out_shape=[b16, f32, b16, b16],
        compiler_params=_params("parallel", "parallel"),
        name="rwkv_chunk",
    )(r, v, kk, lw, kd, b)


def _rwkv_scan_kernel(n_slab, n_in, gf_ref, hf_ref, qf_ref, y0f_ref, gb_ref, hb_ref, qb_ref, y0b_ref,
                      yf_ref, yb_ref, st_ref):
    C = R_CHUNK
    dirs = ((gf_ref, hf_ref, qf_ref, y0f_ref, yf_ref), (gb_ref, hb_ref, qb_ref, y0b_ref, yb_ref))

    @pl.when(pl.program_id(1) == 0)
    def _():
        st_ref[...] = jnp.zeros_like(st_ref)

    blk, _, _ = _slab_masks()
    lanes = [slice(sl * R_SLAB, (sl + 1) * R_SLAB) for sl in range(n_slab)]
    for t in range(n_in):
        todo = [(d, refs, pl.ds((t, n_in - 1 - t)[d] * C, C), ls) for d, refs in enumerate(dirs) for ls in lanes]
        st_bd = [_block_diag(st_ref[d, :, ls], blk) for d, _, _, ls in todo]
        both = [_dot(jnp.concatenate([refs[2][rs, ls], refs[0][rs, ls]], axis=0), bd)
                for (d, refs, rs, ls), bd in zip(todo, st_bd)]
        for (d, refs, rs, ls), res in zip(todo, both):
            refs[4][rs, ls] = (res[:C] + refs[3][rs, ls].astype(F32)).astype(BF16)
            st_ref[d, :, ls] = res[C:] + refs[1][rs, ls]


def _rwkv_scan(dm, g_m, h_m, q_m, y0_m):
    R, rows = dm.R, dm.tmr
    steps = (dm.CT + dm.S) // rows

    def spec(d):
        return pl.BlockSpec((None, rows, R), lambda b, s: (d, dm.chunk_block(b, d, s, rows), 0))

    def yspec(d):
        return pl.BlockSpec((rows, R), lambda b, s: (dm.chunk_block(b, d, s, rows), 0))

    y = jax.ShapeDtypeStruct((dm.M, R), BF16)
    return pl.pallas_call(
        functools.partial(_rwkv_scan_kernel, R // R_SLAB, rows // R_CHUNK),
        grid=(dm.B, steps),
        in_specs=[spec(0)] * 4 + [spec(1)] * 4,
        out_specs=[yspec(0), yspec(1)],
        out_shape=[y, y],
        scratch_shapes=[pltpu.VMEM((2, R_HEAD_DIM, R), F32)],
        compiler_params=_params("parallel", "arbitrary"),
        name="rwkv_scan",
    )(g_m, h_m, q_m, y0_m, g_m, h_m, q_m, y0_m)


def _rwkv_branch_out(y_f, y_b, bonus, g, lnw, lnb, e1_ref, e2_ref):
    o = y_f.astype(F32) + y_b.astype(F32)
    inv_n = 1.0 / R_HEAD_DIM
    mu = _head_sum(o, e1_ref, e2_ref) * inv_n
    oc = o - mu
    var = _head_sum(oc * oc, e1_ref, e2_ref) * inv_n
    o = oc * lax.rsqrt(var + R_GN_EPS) * lnw + lnb
    return ((o + bonus) * g).astype(BF16)


def _gla_kernel(dm, n_in, qf_ref, kf_ref, vf_ref, af_ref, qb_ref, kb_ref, vb_ref, ab_ref, wa_ref, ba_ref,
                yf_ref, yb_ref, st_ref, qd_ref, ki_ref, ke_ref, y_ref):
    C = G_CHUNK
    dk, dv = dm.GK // G_HEADS, dm.GV // G_HEADS
    key = [slice(h * dk, (h + 1) * dk) for h in range(G_HEADS)]
    val = [slice(h * dv, (h + 1) * dv) for h in range(G_HEADS)]
    row = [slice(c * C, (c + 1) * C) for c in range(n_in)]
    dirs = ((qf_ref, kf_ref, vf_ref, af_ref, yf_ref), (qb_ref, kb_ref, vb_ref, ab_ref, yb_ref))

    @pl.when(pl.program_id(1) == 0)
    def _():
        st_ref[...] = jnp.zeros_like(st_ref)

    ti = lax.broadcasted_iota(jnp.int32, (C, C), 0)
    si = lax.broadcasted_iota(jnp.int32, (C, C), 1)
    causal = (ti >= si, ti <= si)

    la = [_log_sigmoid(_dot(refs[3][...], wa_ref[d, 0]) + _dot(refs[3][...], wa_ref[d, 1]) + ba_ref[d])
          * (1.0 / G_GATE_NORM) for d, refs in enumerate(dirs)]
    cb = [[_cumsum_dot(causal[d].astype(BF16), la[d][r]) for r in row] for d in range(2)]
    e_last = []
    for d, refs in enumerate(dirs):
        cum = jnp.concatenate(cb[d], axis=0)
        tots = [jnp.sum(la[d][r], axis=0, keepdims=True) for r in row]
        tot = jnp.concatenate([jnp.broadcast_to(t, (C, t.shape[1])) for t in tots], axis=0)
        k = refs[1][...].astype(F32)
        qd_ref[d] = (refs[0][...].astype(F32) * (dk ** -0.5) * jnp.exp(cum)).astype(BF16)
        ki_ref[d] = (k * jnp.exp(-cum)).astype(BF16)
        ke_ref[d] = (k * jnp.exp(tot - cum)).astype(BF16)
        e_last.append([jnp.exp(t) for t in tots])
    for d, refs in enumerate(dirs):
        att = [[jnp.where(causal[d], _dot(qd_ref[d, r, ks], ki_ref[d, r, ks], NT), 0.0).astype(BF16) for ks in key]
               for r in row]
        for c, r in enumerate(row):
            for h in range(G_HEADS):
                y_ref[d, r, val[h]] = _dot(att[c][h], refs[2][r, val[h]])
    for t in range(n_in):
        todo = [(d, refs, (t, n_in - 1 - t)[d], h) for d, refs in enumerate(dirs) for h in range(G_HEADS)]
        inter = [_dot(qd_ref[d, row[c], key[h]], st_ref[d, h].astype(BF16), NT) for d, _, c, h in todo]
        upd = [_dot(refs[2][row[c], val[h]], ke_ref[d, row[c], key[h]], TN) for d, refs, c, h in todo]
        for (d, refs, c, h), o in zip(todo, inter):
            y_ref[d, row[c], val[h]] += o
        for (d, refs, c, h), u in zip(todo, upd):
            st_ref[d, h] = st_ref[d, h] * e_last[d][c][:, key[h]] + u
    yf_ref[...] = y_ref[0].astype(BF16)
    yb_ref[...] = y_ref[1].astype(BF16)


def _gla(dm, z, wa_split, ba):
    rows = dm.tmr
    steps = (dm.CT + dm.S) // rows
    GK, GV = dm.GK, dm.GV

    def zspec(width, col, d):
        return pl.BlockSpec((rows, width), lambda b, s: (dm.chunk_block(b, d, s, rows), col // width))

    def zspecs(d):
        return [zspec(GK, dm.c_gla_q, d), zspec(GK, dm.c_gla_k, d), zspec(GV, dm.c_gla_v, d),
                zspec(LORA_PAD, dm.c_gla_ac, d)]

    def yspec(d):
        return pl.BlockSpec((rows, GV), lambda b, s: (dm.chunk_block(b, d, s, rows), 0))

    y = jax.ShapeDtypeStruct((dm.M, GV), BF16)
    return pl.pallas_call(
        functools.partial(_gla_kernel, dm, rows // G_CHUNK),
        grid=(dm.B, steps),
        in_specs=zspecs(0) + zspecs(1) + [
            pl.BlockSpec((2, 2, LORA_PAD, GK), lambda b, s: (0, 0, 0, 0)),
            pl.BlockSpec((2, 1, GK), lambda b, s: (0, 0, 0)),
        ],
        out_specs=[yspec(0), yspec(1)],
        out_shape=[y, y],
        scratch_shapes=[pltpu.VMEM((2, G_HEADS, GV // G_HEADS, GK // G_HEADS), F32),
                        pltpu.VMEM((2, rows, GK), BF16), pltpu.VMEM((2, rows, GK), BF16),
                        pltpu.VMEM((2, rows, GK), BF16), pltpu.VMEM((2, rows, GV), F32)],
        compiler_params=_params("parallel", "arbitrary"),
        name="gla",
    )(z, z, z, z, z, z, z, z, wa_split, ba)


def _gla_branch_out(y_f, y_b, g, norm_w):
    o = y_f.astype(F32) + y_b.astype(F32)
    g = g.astype(F32)
    gate = g * _sigmoid(g)
    dv = norm_w.shape[1]
    heads = []
    for h in range(G_HEADS):
        vs = slice(h * dv, (h + 1) * dv)
        oh = o[:, vs]
        oh = oh * lax.rsqrt(jnp.mean(oh * oh, -1, keepdims=True) + G_EPS) * norm_w
        heads.append((oh * gate[:, vs]).astype(BF16))
    return jnp.concatenate(heads, axis=1)


def _merge_kernel(alpha, n_chunks, x_ref, mg_ref, oa_ref, ryf_ref, ryb_ref, bonus_ref, rg_ref, gyf_ref, gyb_ref,
                  gg_in_ref, ga_ref, gr_ref, gg_ref, wa_ref, wr_ref, wg_ref, wo_ref, lnw_ref, lnb_ref, e1_ref, e2_ref,
                  nw_ref, g_ref, b_ref, o_ref):
    tn = wo_ref.shape[0] // n_chunks
    oa = oa_ref[...]
    orw = _rwkv_branch_out(ryf_ref[...], ryb_ref[...], bonus_ref[...], rg_ref[...], lnw_ref[...], lnb_ref[...],
                           e1_ref, e2_ref)
    og = _gla_branch_out(gyf_ref[...], gyb_ref[...], gg_in_ref[...], nw_ref[...])
    acc = None
    for n in range(n_chunks):
        cs = slice(n * tn, (n + 1) * tn)
        merged = (_sigmoid(ga_ref[:, cs].astype(F32)) * _dot(oa, wa_ref[:, cs])
                  + _sigmoid(gr_ref[:, cs].astype(F32)) * _dot(orw, wr_ref[:, cs])
                  + _sigmoid(gg_ref[:, cs].astype(F32)) * _dot(og, wg_ref[:, cs]))
        part = _dot(merged.astype(BF16), wo_ref[cs, :])
        acc = part if acc is None else acc + part
    z = alpha * x_ref[...] + mg_ref[...] * acc
    o_ref[...] = _ln(z) * g_ref[...] + b_ref[...]


def _merge(dm, alpha, x, mods, z, o_att, rwkv, gla, wba, wbr, wbg, w_out, l, g, b, lat_only):
    D, tm = dm.D, dm.tmr
    off = dm.MC // tm if lat_only else 0
    rows = dm.M - off * tm
    gate0 = dm.c_gate // D
    ry_f, ry_b, bonus, r_gate, rp = rwkv
    gy_f, gy_b, norm_w = gla
    dv = dm.GV // G_HEADS

    def row_spec(width, col=0):
        return pl.BlockSpec((tm, width), lambda i: (i + off, col))

    def weight(k_rows):
        return pl.BlockSpec((None, k_rows, D), lambda i: (l, 0, 0), pipeline_mode=pl.Buffered(1))

    def full(a):
        return pl.BlockSpec(a.shape, lambda i: (0,) * a.ndim)

    consts = [rp['ln_w'], rp['ln_b'], rp['e1'], rp['e2'], norm_w.reshape(1, dv), g.reshape(1, D), b.reshape(1, D)]
    return pl.pallas_call(
        functools.partial(_merge_kernel, alpha, 4),
        grid=(rows // tm,),
        in_specs=[
            row_spec(D), dm.mod_spec(5, tm, off),
            row_spec(A_Q_DIM),
            row_spec(dm.R), row_spec(dm.R), row_spec(dm.R), row_spec(dm.R),
            row_spec(dm.GV), row_spec(dm.GV), row_spec(dm.GV, dm.c_gla_g // dm.GV),
            row_spec(D, gate0), row_spec(D, gate0 + 1), row_spec(D, gate0 + 2),
            weight(A_Q_DIM), weight(dm.R), weight(dm.GV), weight(D),
        ] + [full(a) for a in consts],
        out_specs=pl.BlockSpec((tm, D), lambda i: (i, 0)),
        out_shape=jax.ShapeDtypeStruct((rows, D), F32),
        compiler_params=_params("parallel"),
        name="merge",
    )(x, mods, o_att, ry_f, ry_b, bonus, r_gate, gy_f, gy_b, z, z, z, z, wba, wbr, wbg, w_out, *consts)


def _pad_to(a, size, axis):
    pad = [(0, 0)] * a.ndim
    pad[axis] = (0, size - a.shape[axis])
    return jnp.pad(a, pad)


def _rwkv_cols(a, R):
    c = 3 * R
    rkv = a[..., :c]
    wc0, wc1 = a[..., c:c + DECAY_LORA], a[..., c + DECAY_LORA:c + 2 * DECAY_LORA]
    c += 2 * DECAY_LORA
    ac0, ac1 = a[..., c:c + ICLR_LORA], a[..., c + ICLR_LORA:c + 2 * ICLR_LORA]
    c += 2 * ICLR_LORA
    gc = a[..., c:c + GATE_LORA]
    ax = a.ndim - 1
    return jnp.concatenate([rkv, gc] + [_pad_to(t, LORA_PAD, ax) for t in (wc0, wc1, ac0, ac1)], axis=-1)


def _in_proj_weights(dm, w_in):
    D, R, GK, GV = dm.D, dm.R, dm.GK, dm.GV
    att_cols = A_Q_DIM + 2 * A_KV_DIM
    rw_src = 3 * R + 2 * DECAY_LORA + 2 * ICLR_LORA + GATE_LORA
    gla_src = 2 * GK + 2 * GV + 2 * G_GATE_RANK
    att, rwkv, gla, gate = jnp.split(w_in, np.cumsum([att_cols, rw_src, gla_src]).tolist(), axis=-1)
    a_q, a_k, a_v = att[:, :A_Q_DIM], att[:, A_Q_DIM:A_Q_DIM + A_KV_DIM], att[:, A_Q_DIM + A_KV_DIM:]
    g_q, g_k = gla[:, :GK], gla[:, GK:2 * GK]
    g_v, g_g = gla[:, 2 * GK:2 * GK + GV], gla[:, 2 * GK + GV:2 * GK + 2 * GV]
    g_ac = _pad_to(gla[:, 2 * GK + 2 * GV:], LORA_PAD, 1)
    w = jnp.concatenate([_rwkv_cols(rwkv, R), a_k, gate, g_v, g_g, g_q, g_k, a_q, a_v, g_ac], axis=-1)
    return _pad_to(w, dm.NP, 1).astype(BF16)


def _rope_tables(S):
    rows = S // GRID_W
    row = jnp.repeat(jnp.arange(rows, dtype=F32), GRID_W)
    col = jnp.tile(jnp.arange(GRID_W, dtype=F32), rows)
    n_freq = A_HEAD_DIM // 4
    inv_freq = ROPE_BASE ** (-jnp.arange(n_freq, dtype=F32) / n_freq)
    ang_r, ang_c = row[:, None] * inv_freq, col[:, None] * inv_freq
    cos = jnp.concatenate([jnp.cos(ang_r)] * 2 + [jnp.cos(ang_c)] * 2, axis=-1)
    sin = jnp.concatenate([-jnp.sin(ang_r), jnp.sin(ang_r), -jnp.sin(ang_c), jnp.sin(ang_c)], axis=-1)
    return cos, sin


def kernel(x, c, ctx, c_ctx, ada_w, ada_b, ln_g, ln_b, ffn_w_gu, ffn_w_down, w_in, attn_sink, rwkv_conv, rwkv_w0,
           rwkv_w2, rwkv_a0, rwkv_a2, rwkv_g2, rwkv_k_k, rwkv_k_a, rwkv_r_k, rwkv_ln_w, rwkv_ln_b, gla_wa2, gla_ba,
           gla_norm_w, w_branch_att, w_branch_rwkv, w_branch_gla, w_out):
    B, S, D = x.shape
    CT = ctx.shape[1]
    depth = ada_w.shape[0]
    dm = _Dims(B, S, CT, D)
    R = dm.R
    alpha = float((2 * depth) ** 0.25)
    assert B + 1 <= MOD_ROWS

    cc = _pad_to(jnp.concatenate([c, c_ctx[None, :]], axis=0), MOD_ROWS, 0)
    mods_all = _ada(cc, ada_w, ada_b)
    cos, sin = _rope_tables(S)
    head_of_lane = jnp.arange(R) // R_HEAD_DIM
    e1 = (head_of_lane[:, None] == jnp.arange(LORA_PAD)[None, :]).astype(BF16)
    e2 = e1.T

    w_gu, w_dn = ffn_w_gu.astype(BF16), ffn_w_down.astype(BF16)
    wba, wbr = w_branch_att.astype(BF16), w_branch_rwkv.astype(BF16)
    wbg, wo = w_branch_gla.astype(BF16), w_out.astype(BF16)

    xs = jnp.concatenate([ctx.reshape(dm.MC, D), x.reshape(dm.ML, D)], axis=0)
    for l in range(depth):
        last = l == depth - 1
        mods = mods_all[l].reshape(MOD_ROWS * N_MOD, 1, D)
        xs = _ffn(dm, alpha, xs, mods, 0, w_gu, w_dn, l, 0, ln_g[l, 0], ln_b[l, 0])

        z = _inproj(dm, xs, mods, _in_proj_weights(dm, w_in[l]))
        o_att = jnp.concatenate([_attn_ctx(dm, z, attn_sink[l]), _attn_latent(dm, z, attn_sink[l], cos, sin)], axis=0)

        rp = {
            'conv': _rwkv_cols(rwkv_conv[l], R),
            'g2': rwkv_g2[l].astype(BF16),
            'w0': rwkv_w0[l], 'a0': rwkv_a0[l],
            'w2': _pad_to(rwkv_w2[l], LORA_PAD, 1).astype(BF16),
            'a2': _pad_to(rwkv_a2[l], LORA_PAD, 1).astype(BF16),
            'k_k': rwkv_k_k[l].reshape(1, R), 'k_a': rwkv_k_a[l].reshape(1, R), 'r_k': rwkv_r_k[l].reshape(1, R),
            'ln_w': rwkv_ln_w[l].reshape(1, R), 'ln_b': rwkv_ln_b[l].reshape(1, R),
            'e1': e1, 'e2': e2,
        }
        r, v, kk, lw, kd, b, bonus, g = _rwkv_prep(dm, z, rp)
        g_m, h_m, q_m, y0_m = _rwkv_chunk(dm, r, v, kk, lw, kd, b)
        y_rwkv_f, y_rwkv_b = _rwkv_scan(dm, g_m, h_m, q_m, y0_m)

        wa_p = jnp.stack([_pad_to(jnp.pad(gla_wa2[l, d], ((d * G_GATE_RANK, 0), (0, 0))), LORA_PAD, 0)
                          for d in range(2)])
        wa_hi = wa_p.astype(BF16)
        wa_split = jnp.stack([wa_hi, (wa_p - wa_hi.astype(F32)).astype(BF16)], axis=1)
        y_gla_f, y_gla_b = _gla(dm, z, wa_split, gla_ba[l].reshape(2, 1, dm.GK))

        xs = _merge(dm, alpha, xs, mods, z, o_att, (y_rwkv_f, y_rwkv_b, bonus, g, rp),
                    (y_gla_f, y_gla_b, gla_norm_w[l]), wba, wbr, wbg, wo, l, ln_g[l, 1], ln_b[l, 1], last)
        xs = _ffn(dm, alpha, xs, mods, 6, w_gu, w_dn, l, 1, ln_g[l, 2], ln_b[l, 2])
    return xs.reshape(B, S, D)
```

```python
import functools
import math

import jax
import jax.numpy as jnp
import numpy as np
from jax import lax
from jax.experimental import pallas as pl
from jax.experimental.pallas import tpu as pltpu

F32 = jnp.float32
BF16 = jnp.bfloat16
HI = lax.Precision.HIGHEST

N_MOD = 9
D_FF = 5632
LN_EPS = 1e-5
FFN_RES = 0.5

A_HEADS = 8
A_KV_HEADS = 2
A_GROUP = A_HEADS // A_KV_HEADS
A_HEAD_DIM = 128
A_Q_DIM = A_HEADS * A_HEAD_DIM
A_KV_DIM = A_KV_HEADS * A_HEAD_DIM
WINDOW = 128
A_BLOCK = 128
GRID_W = 64
ROPE_BASE = 10000.0
MASK_VALUE = -1e30

R_HEAD_DIM = 64
DECAY_LORA = 96
ICLR_LORA = 96
GATE_LORA = 256
R_GN_EPS = 64e-5
R_CHUNK = 64
LORA_PAD = 128
SUBLANES = 8
HALO = 16

G_HEADS = 4
G_GATE_RANK = 16
G_GATE_NORM = 16.0
G_CHUNK = 64
G_EPS = 1e-5

MOD_ROWS = 16
VMEM_LIMIT = 58 * 1024 * 1024
VMEM_LIMIT_FFN = 63 * 1024 * 1024

NN = (((1,), (0,)), ((), ()))
NT = (((1,), (1,)), ((), ()))
TN = (((0,), (0,)), ((), ()))


def _dot(a, b, dims=NN, prec=None):
    return lax.dot_general(a, b, dims, precision=prec, preferred_element_type=F32)


def _cumsum_dot(tri, x):
    hi = x.astype(BF16)
    rest = x - hi.astype(F32)
    mid = rest.astype(BF16)
    lo = (rest - mid.astype(F32)).astype(BF16)
    return _dot(tri, hi) + _dot(tri, mid) + _dot(tri, lo)


def _sigmoid(x):
    return 1.0 / (1.0 + jnp.exp(-x))


def _log_sigmoid(x):
    return jnp.minimum(x, 0.0) - jnp.log(1.0 + jnp.exp(-jnp.abs(x)))


def _ln(x):
    mu = jnp.mean(x, -1, keepdims=True)
    xc = x - mu
    var = jnp.mean(xc * xc, -1, keepdims=True)
    return xc * lax.rsqrt(var + LN_EPS)


def _params(*sem, vmem=VMEM_LIMIT):
    return pltpu.CompilerParams(dimension_semantics=sem, vmem_limit_bytes=vmem)


class _Dims:
    def __init__(self, B, S, CT, D):
        self.B, self.S, self.CT, self.D = B, S, CT, D
        self.MC, self.ML = B * CT, B * S
        self.M = self.MC + self.ML
        self.R = D // 2
        self.GK = D // 2
        self.GV = D
        self.RH = self.R // R_HEAD_DIM
        self.rw_cols = 3 * self.R + GATE_LORA + 4 * LORA_PAD
        c = 0
        self.c_rwkv = c; c += self.rw_cols
        self.c_att_k = c; c += A_KV_DIM
        self.c_gate = c; c += 3 * D
        self.c_gla_v = c; c += self.GV
        self.c_gla_g = c; c += self.GV
        self.c_gla_q = c; c += self.GK
        self.c_gla_k = c; c += self.GK
        self.c_att_q = c; c += A_Q_DIM
        self.c_att_v = c; c += A_KV_DIM
        self.c_gla_ac = c; c += LORA_PAD
        self.tn_in = 1280
        self.NP = -(-c // self.tn_in) * self.tn_in
        self.tm = math.gcd(512, math.gcd(self.MC, S))
        self.tm_in = math.gcd(1024, math.gcd(self.MC, S))
        self.tmr = math.gcd(256, math.gcd(CT, S))

    def mod_row(self, i, tm):
        nct = self.MC // tm
        tpb = self.S // tm
        return jnp.where(i < nct, self.B, (i - nct) // tpb)

    def mod_spec(self, m, tm, off=0):
        return pl.BlockSpec((None, 1, self.D), lambda i, *_: (self.mod_row(i + off, tm) * N_MOD + m, 0, 0))

    def chunk_block(self, b, d, s, chunk):
        ncc, ncl = self.CT // chunk, self.S // chunk
        j = s - ncc
        ctx_blk = b * ncc + jnp.where(d == 0, s, ncc - 1 - s)
        lat_blk = self.B * ncc + b * ncl + jnp.where(d == 0, j, ncl - 1 - j)
        return jnp.where(s < ncc, ctx_blk, lat_blk)


def _ada_kernel(c_ref, w_ref, b_ref, o_ref):
    @pl.when(pl.program_id(1) == 0)
    def _():
        o_ref[...] = jnp.broadcast_to(b_ref[...], o_ref.shape)

    cc = c_ref[...]
    s = (cc * _sigmoid(cc)).astype(BF16)
    o_ref[...] += _dot(s, w_ref[...].astype(BF16))


def _ada(cc, ada_w, ada_b):
    L, D, N = ada_w.shape
    tk = 256
    cc_k = cc.reshape(MOD_ROWS, D // tk, tk).transpose(1, 0, 2)
    return pl.pallas_call(
        _ada_kernel,
        grid=(L, D // tk),
        in_specs=[
            pl.BlockSpec((None, MOD_ROWS, tk), lambda l, k: (k, 0, 0)),
            pl.BlockSpec((None, tk, N), lambda l, k: (l, k, 0)),
            pl.BlockSpec((None, 1, N), lambda l, k: (l, 0, 0)),
        ],
        out_specs=pl.BlockSpec((None, MOD_ROWS, N), lambda l, k: (l, 0, 0)),
        out_shape=jax.ShapeDtypeStruct((L, MOD_ROWS, N), F32),
        compiler_params=_params("parallel", "arbitrary"),
        name="ada",
    )(cc_k, ada_w, ada_b.reshape(L, 1, N))


def _ffn_kernel(alpha, x_ref, sh_ref, sc_ref, gt_ref, wg_ref, wu_ref, wd_ref, g_ref, b_ref, o_ref, h_ref):
    f = pl.program_id(1)

    @pl.when(f == 0)
    def _():
        h = _ln(x_ref[...]) * (1.0 + sc_ref[...]) + sh_ref[...]
        h_ref[...] = h.astype(BF16)
        o_ref[...] = jnp.zeros_like(o_ref)

    h = h_ref[...]
    g = _dot(h, wg_ref[...])
    u = _dot(h, wu_ref[...])
    a = (g * _sigmoid(g) * u).astype(BF16)
    o_ref[...] += _dot(a, wd_ref[...])

    @pl.when(f == pl.num_programs(1) - 1)
    def _():
        z = alpha * x_ref[...] + (FFN_RES * gt_ref[...]) * o_ref[...]
        o_ref[...] = _ln(z) * g_ref[...] + b_ref[...]


def _ffn(dm, alpha, x, mods, m0, w_gu, w_down, l, j, g, b):
    D, tm, tf = dm.D, dm.tm_in, 512
    nf = D_FF // tf
    rows = x.shape[0]
    off = (dm.M - rows) // tm
    return pl.pallas_call(
        functools.partial(_ffn_kernel, alpha),
        grid=(rows // tm, nf),
        in_specs=[
            pl.BlockSpec((tm, D), lambda i, f: (i, 0)),
            dm.mod_spec(m0, tm, off), dm.mod_spec(m0 + 1, tm, off), dm.mod_spec(m0 + 2, tm, off),
            pl.BlockSpec((None, None, D, tf), lambda i, f: (l, j, 0, f)),
            pl.BlockSpec((None, None, D, tf), lambda i, f: (l, j, 0, f + nf)),
            pl.BlockSpec((None, None, tf, D), lambda i, f: (l, j, f, 0)),
            pl.BlockSpec((1, D), lambda i, f: (0, 0)),
            pl.BlockSpec((1, D), lambda i, f: (0, 0)),
        ],
        out_specs=pl.BlockSpec((tm, D), lambda i, f: (i, 0)),
        out_shape=jax.ShapeDtypeStruct((rows, D), F32),
        scratch_shapes=[pltpu.VMEM((tm, D), BF16)],
        compiler_params=_params("parallel", "arbitrary", vmem=VMEM_LIMIT_FFN),
        name="ffn",
    )(x, mods, mods, mods, w_gu, w_gu, w_down, g.reshape(1, D), b.reshape(1, D))


def _inproj_kernel(x_ref, sh_ref, sc_ref, w_ref, o_ref, h_ref):
    @pl.when(pl.program_id(1) == 0)
    def _():
        h = _ln(x_ref[...]) * (1.0 + sc_ref[...]) + sh_ref[...]
        h_ref[...] = h.astype(BF16)

    o_ref[...] = _dot(h_ref[...], w_ref[...]).astype(BF16)


def _inproj(dm, x, mods, w_in_p):
    D, tm, tn = dm.D, dm.tm_in, dm.tn_in
    return pl.pallas_call(
        _inproj_kernel,
        grid=(dm.M // tm, dm.NP // tn),
        in_specs=[
            pl.BlockSpec((tm, D), lambda i, n: (i, 0)),
            dm.mod_spec(3, tm), dm.mod_spec(4, tm),
            pl.BlockSpec((D, tn), lambda i, n: (0, n)),
        ],
        out_specs=pl.BlockSpec((tm, tn), lambda i, n: (i, n)),
        out_shape=jax.ShapeDtypeStruct((dm.M, dm.NP), BF16),
        scratch_shapes=[pltpu.VMEM((tm, D), BF16)],
        compiler_params=_params("parallel", "arbitrary"),
        name="inproj",
    )(x, mods, mods, w_in_p)


def _rope(x, cos, sin_signed):
    lane = lax.broadcasted_iota(jnp.int32, x.shape, 1)
    quarter = A_HEAD_DIM // 4
    swapped = jnp.where((lane % (2 * quarter)) < quarter,
                        pltpu.roll(x, A_HEAD_DIM - quarter, 1), pltpu.roll(x, quarter, 1))
    return x * cos + swapped * sin_signed


def _attn_kernel(S, nb, sink_ref, q_ref, kp_ref, kc_ref, kn_ref, vp_ref, vc_ref, vn_ref, kx_ref, vx_ref,
                 cos_ref, sin_ref, o_ref):
    n = pl.program_id(1)
    scale = A_HEAD_DIM ** -0.5

    def table(ref, blk):
        return ref[pl.ds(pl.multiple_of(blk * A_BLOCK, A_BLOCK), A_BLOCK), :]

    blk_p, blk_n = jnp.maximum(n - 1, 0), jnp.minimum(n + 1, nb - 1)
    cos_c, sin_c = table(cos_ref, n), table(sin_ref, n)
    cos_p, sin_p = table(cos_ref, blk_p), table(sin_ref, blk_p)
    cos_n, sin_n = table(cos_ref, blk_n), table(sin_ref, blk_n)

    rows, band = A_GROUP * A_BLOCK, 3 * A_BLOCK
    qpos = n * A_BLOCK + lax.broadcasted_iota(jnp.int32, (rows, band), 0) % A_BLOCK
    kpos = (n - 1) * A_BLOCK + lax.broadcasted_iota(jnp.int32, (rows, band), 1)
    valid = (jnp.abs(qpos - kpos) <= WINDOW) & (kpos >= 0) & (kpos < S)

    work = []
    for kvh in range(A_KV_HEADS):
        ks = slice(kvh * A_HEAD_DIM, (kvh + 1) * A_HEAD_DIM)
        kb = jnp.concatenate([_rope(kp_ref[:, ks].astype(F32), cos_p, sin_p),
                              _rope(kc_ref[:, ks].astype(F32), cos_c, sin_c),
                              _rope(kn_ref[:, ks].astype(F32), cos_n, sin_n)], axis=0).astype(BF16)
        heads = [kvh * A_GROUP + g for g in range(A_GROUP)]
        qs = [q_ref[:, h * A_HEAD_DIM:(h + 1) * A_HEAD_DIM] for h in heads]
        q_rope = jnp.concatenate([_rope(q.astype(F32), cos_c, sin_c) * scale for q in qs], axis=0).astype(BF16)
        q_plain = jnp.concatenate([q.astype(F32) * scale for q in qs], axis=0).astype(BF16)
        sink = jnp.concatenate([jnp.full((A_BLOCK, 1), sink_ref[h], F32) for h in heads], axis=0)
        s_loc = jnp.where(valid, _dot(q_rope, kb, NT), MASK_VALUE)
        s_ctx = _dot(q_plain, kx_ref[:, ks], NT)
        work.append(dict(ks=ks, heads=heads, sink=sink, s_loc=s_loc, s_ctx=s_ctx))
    for w in work:
        w['m'] = jnp.maximum(jnp.maximum(jnp.max(w['s_loc'], -1, keepdims=True),
                                         jnp.max(w['s_ctx'], -1, keepdims=True)), w['sink'])
    for w in work:
        w['p_loc'] = jnp.exp(w.pop('s_loc') - w['m'])
        w['p_ctx'] = jnp.exp(w.pop('s_ctx') - w['m'])
    for w in work:
        w['den'] = (jnp.sum(w['p_loc'], -1, keepdims=True) + jnp.sum(w['p_ctx'], -1, keepdims=True)
                    + jnp.exp(w['sink'] - w['m']))
    for w in work:
        ks = w['ks']
        vb = jnp.concatenate([vp_ref[:, ks], vc_ref[:, ks], vn_ref[:, ks]], axis=0)
        o = (_dot(w['p_loc'].astype(BF16), vb) + _dot(w['p_ctx'].astype(BF16), vx_ref[:, ks])) / w['den']
        for g, h in enumerate(w['heads']):
            o_ref[:, h * A_HEAD_DIM:(h + 1) * A_HEAD_DIM] = o[g * A_BLOCK:(g + 1) * A_BLOCK].astype(BF16)


def _attn_latent(dm, z, sink, cos, sin):
    B, S = dm.B, dm.S
    nb = S // A_BLOCK
    base = dm.MC // A_BLOCK
    qc = dm.c_att_q // A_Q_DIM
    kc = dm.c_att_k // A_KV_DIM
    vc = dm.c_att_v // A_KV_DIM

    def rows(shift):
        return lambda b, n: base + b * nb + jnp.clip(n + shift, 0, nb - 1)

    def kv_spec(col, shift):
        r = rows(shift)
        return pl.BlockSpec((A_BLOCK, A_KV_DIM), lambda b, n: (r(b, n), col))

    return pl.pallas_call(
        functools.partial(_attn_kernel, S, nb),
        grid=(B, nb),
        in_specs=[
            pl.BlockSpec(memory_space=pltpu.SMEM),
            pl.BlockSpec((A_BLOCK, A_Q_DIM), lambda b, n: (base + b * nb + n, qc)),
            kv_spec(kc, -1), kv_spec(kc, 0), kv_spec(kc, 1),
            kv_spec(vc, -1), kv_spec(vc, 0), kv_spec(vc, 1),
            pl.BlockSpec((dm.CT, A_KV_DIM), lambda b, n: (b, kc)),
            pl.BlockSpec((dm.CT, A_KV_DIM), lambda b, n: (b, vc)),
            pl.BlockSpec((S, A_HEAD_DIM), lambda b, n: (0, 0)),
            pl.BlockSpec((S, A_HEAD_DIM), lambda b, n: (0, 0)),
        ],
        out_specs=pl.BlockSpec((A_BLOCK, A_Q_DIM), lambda b, n: (b * nb + n, 0)),
        out_shape=jax.ShapeDtypeStruct((dm.ML, A_Q_DIM), BF16),
        compiler_params=_params("parallel", "arbitrary"),
        name="attn_latent",
    )(sink, z, z, z, z, z, z, z, z, z, cos, sin)


def _attn_ctx_kernel(CT, sink_ref, q_ref, k_ref, v_ref, o_ref):
    scale = A_HEAD_DIM ** -0.5
    for kvh in range(A_KV_HEADS):
        ks = slice(kvh * A_HEAD_DIM, (kvh + 1) * A_HEAD_DIM)
        k = k_ref[:, ks]
        v = v_ref[:, ks]
        heads = [kvh * A_GROUP + g for g in range(A_GROUP)]
        q = jnp.concatenate([q_ref[:, h * A_HEAD_DIM:(h + 1) * A_HEAD_DIM] for h in heads], axis=0)
        sink = jnp.concatenate([jnp.full((CT, 1), sink_ref[h], F32) for h in heads], axis=0)
        s = _dot(q, k, NT) * scale
        m = jnp.maximum(jnp.max(s, -1, keepdims=True), sink)
        p = jnp.exp(s - m)
        den = jnp.sum(p, -1, keepdims=True) + jnp.exp(sink - m)
        o = _dot(p.astype(BF16), v) / den
        for g, h in enumerate(heads):
            o_ref[:, h * A_HEAD_DIM:(h + 1) * A_HEAD_DIM] = o[g * CT:(g + 1) * CT].astype(BF16)


def _attn_ctx(dm, z, sink):
    CT = dm.CT
    return pl.pallas_call(
        functools.partial(_attn_ctx_kernel, CT),
        grid=(dm.B,),
        in_specs=[
            pl.BlockSpec(memory_space=pltpu.SMEM),
            pl.BlockSpec((CT, A_Q_DIM), lambda b: (b, dm.c_att_q // A_Q_DIM)),
            pl.BlockSpec((CT, A_KV_DIM), lambda b: (b, dm.c_att_k // A_KV_DIM)),
            pl.BlockSpec((CT, A_KV_DIM), lambda b: (b, dm.c_att_v // A_KV_DIM)),
        ],
        out_specs=pl.BlockSpec((CT, A_Q_DIM), lambda b: (b, 0)),
        out_shape=jax.ShapeDtypeStruct((dm.MC, A_Q_DIM), BF16),
        compiler_params=_params("parallel"),
        name="attn_ctx",
    )(sink, z, z, z)


def _head_sum(x, e1_ref, e2_ref):
    def split_dot(a, ind):
        hi = a.astype(BF16)
        lo = (a - hi.astype(F32)).astype(BF16)
        return _dot(hi, ind) + _dot(lo, ind)

    return split_dot(split_dot(x, e1_ref[...]), e2_ref[...])


def _rwkv_prep_kernel(dm, z_ref, zp_ref, zn_ref, cw_ref, g2_ref, w0_ref, w2_ref, a0_ref, a2_ref, kk_ref, ka_ref,
                      rk_ref, e1_ref, e2_ref,
                      r_out, v_out, kk_out, lw_out, kd_out, b_out, bonus_out, g_out):
    i = pl.program_id(0)
    tm, R = dm.tmr, dm.R
    nct = dm.MC // tm
    start = jnp.where(i < nct, (i * tm) % dm.CT, ((i - nct) * tm) % dm.S)
    seqlen = jnp.where(i < nct, dm.CT, dm.S)
    has_prev = (start != 0).astype(F32)
    has_next = (start + tm != seqlen).astype(F32)

    z = z_ref[...].astype(F32)
    sub = lax.broadcasted_iota(jnp.int32, (SUBLANES, 1), 0)
    halo_prev = zp_ref[HALO - 1:HALO, :].astype(F32) * has_prev
    halo_next = zn_ref[0:1, :].astype(F32) * has_next
    down, up = pltpu.roll(z, 1, 0), pltpu.roll(z, tm - 1, 0)
    z_prev = jnp.concatenate([jnp.where(sub == 0, halo_prev, down[:SUBLANES]), down[SUBLANES:]], axis=0)
    z_next = jnp.concatenate([up[:tm - SUBLANES], jnp.where(sub == SUBLANES - 1, halo_next, up[tm - SUBLANES:])],
                             axis=0)
    zc = z_prev * cw_ref[0:1, :] + z * cw_ref[1:2, :] + z_next * cw_ref[2:3, :]

    r = zc[:, 0:R]
    k = zc[:, R:2 * R]
    v = zc[:, 2 * R:3 * R]
    c0 = 3 * R
    gc = zc[:, c0:c0 + GATE_LORA]
    c0 += GATE_LORA
    g_out[...] = _dot(_sigmoid(gc).astype(BF16), g2_ref[...])

    kkr = k * kk_ref[...]
    norm = jnp.sqrt(_head_sum(kkr * kkr, e1_ref, e2_ref))
    kk = kkr / jnp.maximum(norm, 1e-12)
    r_out[...] = r
    v_out[...] = v
    kk_out[...] = kk

    bonus = jnp.zeros_like(v)
    for d in range(2):
        wc = zc[:, c0 + d * LORA_PAD:c0 + (d + 1) * LORA_PAD]
        ac = zc[:, c0 + (2 + d) * LORA_PAD:c0 + (3 + d) * LORA_PAD]
        wl = w0_ref[d:d + 1, :] + _dot(jnp.tanh(wc).astype(BF16), w2_ref[d])
        lw_out[d] = _sigmoid(wl) * (-math.exp(-0.5))
        a = _sigmoid(a0_ref[d:d + 1, :] + _dot(ac.astype(BF16), a2_ref[d]))
        kd = k * (1.0 + (a - 1.0) * ka_ref[...])
        kd_out[d] = kd
        b_out[d] = kk * a
        bonus = bonus + _head_sum(r * kd * rk_ref[...], e1_ref, e2_ref) * v
    bonus_out[...] = bonus


def _rwkv_prep(dm, z, p):
    tm, R, M, W = dm.tmr, dm.R, dm.M, dm.rw_cols
    n_halo = M // HALO
    row = pl.BlockSpec((tm, R), lambda i: (i, 0))
    row2 = pl.BlockSpec((2, tm, R), lambda i: (0, i, 0))

    def full(a):
        return pl.BlockSpec(a.shape, lambda i: (0,) * a.ndim)

    consts = [p['conv'], p['g2'], p['w0'], p['w2'], p['a0'], p['a2'], p['k_k'], p['k_a'], p['r_k'], p['e1'], p['e2']]
    one = jax.ShapeDtypeStruct((M, R), F32)
    two = jax.ShapeDtypeStruct((2, M, R), F32)
    return pl.pallas_call(
        functools.partial(_rwkv_prep_kernel, dm),
        grid=(M // tm,),
        in_specs=[
            pl.BlockSpec((tm, W), lambda i: (i, 0)),
            pl.BlockSpec((HALO, W), lambda i: (jnp.maximum(i * (tm // HALO) - 1, 0), 0)),
            pl.BlockSpec((HALO, W), lambda i: (jnp.minimum((i + 1) * (tm // HALO), n_halo - 1), 0)),
        ] + [full(a) for a in consts],
        out_specs=[row, row, row, row2, row2, row2, row, row],
        out_shape=[one, one, one, two, two, two, one, one],
        compiler_params=_params("parallel"),
        name="rwkv_prep",
    )(z, z, z, *consts)


R_SLAB = 4 * R_HEAD_DIM


def _block_diag(x, mask):
    return jnp.where(mask, jnp.concatenate([x] * 4, axis=0), 0.0).astype(BF16)


def _diag_blocks(full, mask):
    fm = jnp.where(mask, full, 0.0)
    n = R_HEAD_DIM
    return fm[0:n] + fm[n:2 * n] + fm[2 * n:3 * n] + fm[3 * n:4 * n]


def _slab_masks():
    n, g = R_HEAD_DIM, R_SLAB
    blk = (lax.broadcasted_iota(jnp.int32, (g, g), 0) // n) == (lax.broadcasted_iota(jnp.int32, (g, g), 1) // n)
    t_idx = lax.broadcasted_iota(jnp.int32, (n, g), 0)
    s_idx = lax.broadcasted_iota(jnp.int32, (n, g), 1) % n
    return blk, t_idx, s_idx


def _rwkv_chunk_kernel(n_sub, r_ref, v_ref, kk_ref, lw_ref, kd_ref, b_ref, g_out, h_out, q_out, y0_out):
    C = R_CHUNK
    blk, t_idx, s_idx = _slab_masks()
    eye4 = (t_idx == s_idx).astype(F32)
    ti = lax.broadcasted_iota(jnp.int32, (C, C), 0)
    si = lax.broadcasted_iota(jnp.int32, (C, C), 1)

    def bd(x):
        return _block_diag(x, blk)

    def setup(c, d):
        rs = pl.ds(c * C, C)
        r, v, kk = r_ref[rs, :], v_ref[rs, :], kk_ref[rs, :]
        tri = ((ti >= si) if d == 0 else (ti <= si)).astype(BF16)
        lw = lw_ref[d, rs, :]
        cl = _cumsum_dot(tri, lw)
        tot = jnp.sum(lw, axis=0, keepdims=True)
        e_neg = jnp.exp(-cl)
        e_end = jnp.exp(tot - cl)
        kd, b = kd_ref[d, rs, :], b_ref[d, rs, :]
        ch = dict(rs=rs, d=d, tot=tot, v_b=v.astype(BF16), v_bd=bd(v),
                  incl=(t_idx >= s_idx) if d == 0 else (t_idx <= s_idx),
                  strict=(t_idx > s_idx) if d == 0 else (t_idx < s_idx),
                  kkm=kk * jnp.exp(cl - lw), rp=r * jnp.exp(cl),
                  bp=b * e_neg, kp=kd * e_neg, be=(b * e_end).astype(BF16), ke=(kd * e_end).astype(BF16))
        ch['lhs'] = jnp.concatenate([ch['kkm'], ch['rp']], axis=0).astype(BF16)
        return ch

    chains = [setup(c, d) for c in range(n_sub) for d in range(2)]
    for ch in chains:
        gram_b = _dot(ch['lhs'], bd(ch.pop('bp')), NT)
        gram_k = _dot(ch.pop('lhs'), bd(ch.pop('kp')), NT)
        lb = jnp.where(ch['strict'], gram_b[:C], 0.0)
        ch['mb'] = jnp.where(ch['incl'], gram_b[C:], 0.0).astype(BF16)
        ch['lmk'] = jnp.concatenate([jnp.where(ch['strict'], gram_k[:C], 0.0),
                                     jnp.where(ch['incl'], gram_k[C:], 0.0)], axis=0).astype(BF16)
        ch['pw'] = -lb
        ch['t_inv'] = eye4 - lb
    for ch in chains:
        ch['pw'] = _dot(ch['pw'].astype(BF16), bd(ch['pw']))
        lmkv = _dot(ch.pop('lmk'), ch.pop('v_bd'))
        ch['lkv'], ch['mkv'] = lmkv[:C], lmkv[C:]
    for _ in range(int(math.log2(C)) - 2):
        for ch in chains:
            both = _dot(jnp.concatenate([ch['pw'], ch['t_inv']], axis=0).astype(BF16), bd(ch['pw']))
            ch['pw'], ch['t_inv'] = both[:C], ch['t_inv'] + both[C:]
    for ch in chains:
        ch['t_inv'] = (ch['t_inv'] + _dot(ch['t_inv'].astype(BF16), bd(ch.pop('pw')))).astype(BF16)
    for ch in chains:
        ch['a_m'] = _dot(ch['t_inv'], bd(ch.pop('kkm')))
        ch['d_m'] = -_dot(ch.pop('t_inv'), bd(ch.pop('lkv')))
    for ch in chains:
        d, rs = ch['d'], ch['rs']
        a_m, d_m = ch['a_m'], ch['d_m']
        q_out[d, rs, :] = (ch['rp'] - _dot(ch['mb'], bd(a_m))).astype(BF16)
        y0_out[d, rs, :] = (_dot(ch['mb'], bd(d_m)) + ch['mkv']).astype(BF16)
        g_cross = _dot(ch['be'], a_m.astype(BF16), TN)
        h_cross = _dot(jnp.concatenate([ch['be'], ch['ke']], axis=0),
                       jnp.concatenate([d_m.astype(BF16), ch['v_b']], axis=0), TN)
        g_out[d, rs, :] = (eye4 * jnp.exp(ch['tot']) - _diag_blocks(g_cross, blk)).astype(BF16)
        h_out[d, rs, :] = _diag_blocks(h_cross, blk)


def _rwkv_chunk(dm, r, v, kk, lw, kd, b):
    M, R = dm.M, dm.R
    rows = dm.tmr
    one = pl.BlockSpec((rows, R_SLAB), lambda i, h: (i, h))
    two = pl.BlockSpec((2, rows, R_SLAB), lambda i, h: (0, i, h))
    f32 = jax.ShapeDtypeStruct((2, M, R), F32)
    b16 = jax.ShapeDtypeStruct((2, M, R), BF16)
    return pl.pallas_call(
        functools.partial(_rwkv_chunk_kernel, rows // R_CHUNK),
        grid=(M // rows, R // R_SLAB),
        in_specs=[one, one, one, two, two, two],
        out_specs=[two, two, two, two],
        out_shape=[b16, f32, b16, b16],
        compiler_params=_params("parallel", "parallel"),
        name="rwkv_chunk",
    )(r, v, kk, lw, kd, b)


def _rwkv_scan_kernel(n_slab, n_in, gf_ref, hf_ref, qf_ref, y0f_ref, gb_ref, hb_ref, qb_ref, y0b_ref,
                      yf_ref, yb_ref, st_ref):
    C = R_CHUNK
    dirs = ((gf_ref, hf_ref, qf_ref, y0f_ref, yf_ref), (gb_ref, hb_ref, qb_ref, y0b_ref, yb_ref))

    @pl.when(pl.program_id(1) == 0)
    def _():
        st_ref[...] = jnp.zeros_like(st_ref)

    blk, _, _ = _slab_masks()
    lanes = [slice(sl * R_SLAB, (sl + 1) * R_SLAB) for sl in range(n_slab)]
    for t in range(n_in):
        todo = [(d, refs, pl.ds((t, n_in - 1 - t)[d] * C, C), ls) for d, refs in enumerate(dirs) for ls in lanes]
        st_bd = [_block_diag(st_ref[d, :, ls], blk) for d, _, _, ls in todo]
        both = [_dot(jnp.concatenate([refs[2][rs, ls], refs[0][rs, ls]], axis=0), bd)
                for (d, refs, rs, ls), bd in zip(todo, st_bd)]
        for (d, refs, rs, ls), res in zip(todo, both):
            refs[4][rs, ls] = (res[:C] + refs[3][rs, ls].astype(F32)).astype(BF16)
            st_ref[d, :, ls] = res[C:] + refs[1][rs, ls]


def _rwkv_scan(dm, g_m, h_m, q_m, y0_m):
    R, rows = dm.R, dm.tmr
    steps = (dm.CT + dm.S) // rows

    def spec(d):
        return pl.BlockSpec((None, rows, R), lambda b, s: (d, dm.chunk_block(b, d, s, rows), 0))

    def yspec(d):
        return pl.BlockSpec((rows, R), lambda b, s: (dm.chunk_block(b, d, s, rows), 0))

    y = jax.ShapeDtypeStruct((dm.M, R), BF16)
    return pl.pallas_call(
        functools.partial(_rwkv_scan_kernel, R // R_SLAB, rows // R_CHUNK),
        grid=(dm.B, steps),
        in_specs=[spec(0)] * 4 + [spec(1)] * 4,
        out_specs=[yspec(0), yspec(1)],
        out_shape=[y, y],
        scratch_shapes=[pltpu.VMEM((2, R_HEAD_DIM, R), F32)],
        compiler_params=_params("parallel", "arbitrary"),
        name="rwkv_scan",
    )(g_m, h_m, q_m, y0_m, g_m, h_m, q_m, y0_m)


def _rwkv_branch_out(y_f, y_b, bonus, g, lnw, lnb, e1_ref, e2_ref):
    o = y_f.astype(F32) + y_b.astype(F32)
    inv_n = 1.0 / R_HEAD_DIM
    mu = _head_sum(o, e1_ref, e2_ref) * inv_n
    oc = o - mu
    var = _head_sum(oc * oc, e1_ref, e2_ref) * inv_n
    o = oc * lax.rsqrt(var + R_GN_EPS) * lnw + lnb
    return ((o + bonus) * g).astype(BF16)


def _gla_kernel(dm, n_in, qf_ref, kf_ref, vf_ref, af_ref, qb_ref, kb_ref, vb_ref, ab_ref, wa_ref, ba_ref,
                yf_ref, yb_ref, st_ref, qd_ref, ki_ref, ke_ref, y_ref):
    C = G_CHUNK
    dk, dv = dm.GK // G_HEADS, dm.GV // G_HEADS
    key = [slice(h * dk, (h + 1) * dk) for h in range(G_HEADS)]
    val = [slice(h * dv, (h + 1) * dv) for h in range(G_HEADS)]
    row = [slice(c * C, (c + 1) * C) for c in range(n_in)]
    dirs = ((qf_ref, kf_ref, vf_ref, af_ref, yf_ref), (qb_ref, kb_ref, vb_ref, ab_ref, yb_ref))

    @pl.when(pl.program_id(1) == 0)
    def _():
        st_ref[...] = jnp.zeros_like(st_ref)

    ti = lax.broadcasted_iota(jnp.int32, (C, C), 0)
    si = lax.broadcasted_iota(jnp.int32, (C, C), 1)
    causal = (ti >= si, ti <= si)

    la = [_log_sigmoid(_dot(refs[3][...], wa_ref[d, 0]) + _dot(refs[3][...], wa_ref[d, 1]) + ba_ref[d])
          * (1.0 / G_GATE_NORM) for d, refs in enumerate(dirs)]
    cb = [[_cumsum_dot(causal[d].astype(BF16), la[d][r]) for r in row] for d in range(2)]
    e_last = []
    for d, refs in enumerate(dirs):
        cum = jnp.concatenate(cb[d], axis=0)
        tots = [jnp.sum(la[d][r], axis=0, keepdims=True) for r in row]
        tot = jnp.concatenate([jnp.broadcast_to(t, (C, t.shape[1])) for t in tots], axis=0)
        k = refs[1][...].astype(F32)
        qd_ref[d] = (refs[0][...].astype(F32) * (dk ** -0.5) * jnp.exp(cum)).astype(BF16)
        ki_ref[d] = (k * jnp.exp(-cum)).astype(BF16)
        ke_ref[d] = (k * jnp.exp(tot - cum)).astype(BF16)
        e_last.append([jnp.exp(t) for t in tots])
    for d, refs in enumerate(dirs):
        att = [[jnp.where(causal[d], _dot(qd_ref[d, r, ks], ki_ref[d, r, ks], NT), 0.0).astype(BF16) for ks in key]
               for r in row]
        for c, r in enumerate(row):
            for h in range(G_HEADS):
                y_ref[d, r, val[h]] = _dot(att[c][h], refs[2][r, val[h]])
    for t in range(n_in):
        todo = [(d, refs, (t, n_in - 1 - t)[d], h) for d, refs in enumerate(dirs) for h in range(G_HEADS)]
        inter = [_dot(qd_ref[d, row[c], key[h]], st_ref[d, h].astype(BF16), NT) for d, _, c, h in todo]
        upd = [_dot(refs[2][row[c], val[h]], ke_ref[d, row[c], key[h]], TN) for d, refs, c, h in todo]
        for (d, refs, c, h), o in zip(todo, inter):
            y_ref[d, row[c], val[h]] += o
        for (d, refs, c, h), u in zip(todo, upd):
            st_ref[d, h] = st_ref[d, h] * e_last[d][c][:, key[h]] + u
    yf_ref[...] = y_ref[0].astype(BF16)
    yb_ref[...] = y_ref[1].astype(BF16)


def _gla(dm, z, wa_split, ba):
    rows = dm.tmr
    steps = (dm.CT + dm.S) // rows
    GK, GV = dm.GK, dm.GV

    def zspec(width, col, d):
        return pl.BlockSpec((rows, width), lambda b, s: (dm.chunk_block(b, d, s, rows), col // width))

    def zspecs(d):
        return [zspec(GK, dm.c_gla_q, d), zspec(GK, dm.c_gla_k, d), zspec(GV, dm.c_gla_v, d),
                zspec(LORA_PAD, dm.c_gla_ac, d)]

    def yspec(d):
        return pl.BlockSpec((rows, GV), lambda b, s: (dm.chunk_block(b, d, s, rows), 0))

    y = jax.ShapeDtypeStruct((dm.M, GV), BF16)
    return pl.pallas_call(
        functools.partial(_gla_kernel, dm, rows // G_CHUNK),
        grid=(dm.B, steps),
        in_specs=zspecs(0) + zspecs(1) + [
            pl.BlockSpec((2, 2, LORA_PAD, GK), lambda b, s: (0, 0, 0, 0)),
            pl.BlockSpec((2, 1, GK), lambda b, s: (0, 0, 0)),
        ],
        out_specs=[yspec(0), yspec(1)],
        out_shape=[y, y],
        scratch_shapes=[pltpu.VMEM((2, G_HEADS, GV // G_HEADS, GK // G_HEADS), F32),
                        pltpu.VMEM((2, rows, GK), BF16), pltpu.VMEM((2, rows, GK), BF16),
                        pltpu.VMEM((2, rows, GK), BF16), pltpu.VMEM((2, rows, GV), F32)],
        compiler_params=_params("parallel", "arbitrary"),
        name="gla",
    )(z, z, z, z, z, z, z, z, wa_split, ba)


def _gla_branch_out(y_f, y_b, g, norm_w):
    o = y_f.astype(F32) + y_b.astype(F32)
    g = g.astype(F32)
    gate = g * _sigmoid(g)
    dv = norm_w.shape[1]
    heads = []
    for h in range(G_HEADS):
        vs = slice(h * dv, (h + 1) * dv)
        oh = o[:, vs]
        oh = oh * lax.rsqrt(jnp.mean(oh * oh, -1, keepdims=True) + G_EPS) * norm_w
        heads.append((oh * gate[:, vs]).astype(BF16))
    return jnp.concatenate(heads, axis=1)


def _merge_kernel(alpha, n_chunks, ctx_tiles, x_ref, mg_ref, oac_ref, oal_ref, ryf_ref, ryb_ref, bonus_ref, rg_ref, gyf_ref, gyb_ref,
                  gg_in_ref, ga_ref, gr_ref, gg_ref, wa_ref, wr_ref, wg_ref, wo_ref, lnw_ref, lnb_ref, e1_ref, e2_ref,
                  nw_ref, g_ref, b_ref, o_ref):
    tn = wo_ref.shape[0] // n_chunks
    oa = jnp.where(pl.program_id(0) < ctx_tiles, oac_ref[...], oal_ref[...])
    orw = _rwkv_branch_out(ryf_ref[...], ryb_ref[...], bonus_ref[...], rg_ref[...], lnw_ref[...], lnb_ref[...],
                           e1_ref, e2_ref)
    og = _gla_branch_out(gyf_ref[...], gyb_ref[...], gg_in_ref[...], nw_ref[...])
    acc = None
    for n in range(n_chunks):
        cs = slice(n * tn, (n + 1) * tn)
        merged = (_sigmoid(ga_ref[:, cs].astype(F32)) * _dot(oa, wa_ref[:, cs])
                  + _sigmoid(gr_ref[:, cs].astype(F32)) * _dot(orw, wr_ref[:, cs])
                  + _sigmoid(gg_ref[:, cs].astype(F32)) * _dot(og, wg_ref[:, cs]))
        part = _dot(merged.astype(BF16), wo_ref[cs, :])
        acc = part if acc is None else acc + part
    z = alpha * x_ref[...] + mg_ref[...] * acc
    o_ref[...] = _ln(z) * g_ref[...] + b_ref[...]


def _merge(dm, alpha, x, mods, z, o_att_ctx, o_att_lat, rwkv, gla, wba, wbr, wbg, w_out, l, g, b):
    D, tm = dm.D, dm.tmr
    nct = dm.MC // tm
    off = nct if o_att_ctx is None else 0
    rows = dm.M - off * tm
    if o_att_ctx is None:
        o_att_ctx = o_att_lat
    gate0 = dm.c_gate // D
    ry_f, ry_b, bonus, r_gate, rp = rwkv
    gy_f, gy_b, norm_w = gla
    dv = dm.GV // G_HEADS

    def row_spec(width, col=0):
        return pl.BlockSpec((tm, width), lambda i: (i + off, col))

    def weight(k_rows):
        return pl.BlockSpec((None, k_rows, D), lambda i: (l, 0, 0), pipeline_mode=pl.Buffered(1))

    def full(a):
        return pl.BlockSpec(a.shape, lambda i: (0,) * a.ndim)

    consts = [rp['ln_w'], rp['ln_b'], rp['e1'], rp['e2'], norm_w.reshape(1, dv), g.reshape(1, D), b.reshape(1, D)]
    return pl.pallas_call(
        functools.partial(_merge_kernel, alpha, 4, nct - off),
        grid=(rows // tm,),
        in_specs=[
            row_spec(D), dm.mod_spec(5, tm, off),
            pl.BlockSpec((tm, A_Q_DIM), lambda i: (jnp.minimum(i + off, nct - 1), 0)),
            pl.BlockSpec((tm, A_Q_DIM), lambda i: (jnp.maximum(i + off - nct, 0), 0)),
            row_spec(dm.R), row_spec(dm.R), row_spec(dm.R), row_spec(dm.R),
            row_spec(dm.GV), row_spec(dm.GV), row_spec(dm.GV, dm.c_gla_g // dm.GV),
            row_spec(D, gate0), row_spec(D, gate0 + 1), row_spec(D, gate0 + 2),
            weight(A_Q_DIM), weight(dm.R), weight(dm.GV), weight(D),
        ] + [full(a) for a in consts],
        out_specs=pl.BlockSpec((tm, D), lambda i: (i, 0)),
        out_shape=jax.ShapeDtypeStruct((rows, D), F32),
        compiler_params=_params("parallel"),
        name="merge",
    )(x, mods, o_att_ctx, o_att_lat, ry_f, ry_b, bonus, r_gate, gy_f, gy_b, z, z, z, z, wba, wbr, wbg, w_out, *consts)


def _pad_to(a, size, axis):
    pad = [(0, 0)] * a.ndim
    pad[axis] = (0, size - a.shape[axis])
    return jnp.pad(a, pad)


def _rwkv_cols(a, R):
    c = 3 * R
    rkv = a[..., :c]
    wc0, wc1 = a[..., c:c + DECAY_LORA], a[..., c + DECAY_LORA:c + 2 * DECAY_LORA]
    c += 2 * DECAY_LORA
    ac0, ac1 = a[..., c:c + ICLR_LORA], a[..., c + ICLR_LORA:c + 2 * ICLR_LORA]
    c += 2 * ICLR_LORA
    gc = a[..., c:c + GATE_LORA]
    ax = a.ndim - 1
    return jnp.concatenate([rkv, gc] + [_pad_to(t, LORA_PAD, ax) for t in (wc0, wc1, ac0, ac1)], axis=-1)


def _in_proj_weights(dm, w_in):
    D, R, GK, GV = dm.D, dm.R, dm.GK, dm.GV
    att_cols = A_Q_DIM + 2 * A_KV_DIM
    rw_src = 3 * R + 2 * DECAY_LORA + 2 * ICLR_LORA + GATE_LORA
    gla_src = 2 * GK + 2 * GV + 2 * G_GATE_RANK
    att, rwkv, gla, gate = jnp.split(w_in, np.cumsum([att_cols, rw_src, gla_src]).tolist(), axis=-1)
    a_q, a_k, a_v = att[:, :A_Q_DIM], att[:, A_Q_DIM:A_Q_DIM + A_KV_DIM], att[:, A_Q_DIM + A_KV_DIM:]
    g_q, g_k = gla[:, :GK], gla[:, GK:2 * GK]
    g_v, g_g = gla[:, 2 * GK:2 * GK + GV], gla[:, 2 * GK + GV:2 * GK + 2 * GV]
    g_ac = _pad_to(gla[:, 2 * GK + 2 * GV:], LORA_PAD, 1)
    w = jnp.concatenate([_rwkv_cols(rwkv, R), a_k, gate, g_v, g_g, g_q, g_k, a_q, a_v, g_ac], axis=-1)
    return _pad_to(w, dm.NP, 1).astype(BF16)


def _rope_tables(S):
    rows = S // GRID_W
    row = jnp.repeat(jnp.arange(rows, dtype=F32), GRID_W)
    col = jnp.tile(jnp.arange(GRID_W, dtype=F32), rows)
    n_freq = A_HEAD_DIM // 4
    inv_freq = ROPE_BASE ** (-jnp.arange(n_freq, dtype=F32) / n_freq)
    ang_r, ang_c = row[:, None] * inv_freq, col[:, None] * inv_freq
    cos = jnp.concatenate([jnp.cos(ang_r)] * 2 + [jnp.cos(ang_c)] * 2, axis=-1)
    sin = jnp.concatenate([-jnp.sin(ang_r), jnp.sin(ang_r), -jnp.sin(ang_c), jnp.sin(ang_c)], axis=-1)
    return cos, sin


def kernel(x, c, ctx, c_ctx, ada_w, ada_b, ln_g, ln_b, ffn_w_gu, ffn_w_down, w_in, attn_sink, rwkv_conv, rwkv_w0,
           rwkv_w2, rwkv_a0, rwkv_a2, rwkv_g2, rwkv_k_k, rwkv_k_a, rwkv_r_k, rwkv_ln_w, rwkv_ln_b, gla_wa2, gla_ba,
           gla_norm_w, w_branch_att, w_branch_rwkv, w_branch_gla, w_out):
    B, S, D = x.shape
    CT = ctx.shape[1]
    depth = ada_w.shape[0]
    dm = _Dims(B, S, CT, D)
    R = dm.R
    alpha = float((2 * depth) ** 0.25)
    assert B + 1 <= MOD_ROWS

    cc = _pad_to(jnp.concatenate([c, c_ctx[None, :]], axis=0), MOD_ROWS, 0)
    mods_all = _ada(cc, ada_w, ada_b)
    cos, sin = _rope_tables(S)
    head_of_lane = jnp.arange(R) // R_HEAD_DIM
    e1 = (head_of_lane[:, None] == jnp.arange(LORA_PAD)[None, :]).astype(BF16)
    e2 = e1.T

    w_gu, w_dn = ffn_w_gu.astype(BF16), ffn_w_down.astype(BF16)
    wba, wbr = w_branch_att.astype(BF16), w_branch_rwkv.astype(BF16)
    wbg, wo = w_branch_gla.astype(BF16), w_out.astype(BF16)

    xs = jnp.concatenate([ctx.reshape(dm.MC, D), x.reshape(dm.ML, D)], axis=0)
    for l in range(depth):
        last = l == depth - 1
        mods = mods_all[l].reshape(MOD_ROWS * N_MOD, 1, D)
        xs = _ffn(dm, alpha, xs, mods, 0, w_gu, w_dn, l, 0, ln_g[l, 0], ln_b[l, 0])

        z = _inproj(dm, xs, mods, _in_proj_weights(dm, w_in[l]))
        o_att_lat = _attn_latent(dm, z, attn_sink[l], cos, sin)
        o_att_ctx = None if last else _attn_ctx(dm, z, attn_sink[l])

        rp = {
            'conv': _rwkv_cols(rwkv_conv[l], R),
            'g2': rwkv_g2[l].astype(BF16),
            'w0': rwkv_w0[l], 'a0': rwkv_a0[l],
            'w2': _pad_to(rwkv_w2[l], LORA_PAD, 1).astype(BF16),
            'a2': _pad_to(rwkv_a2[l], LORA_PAD, 1).astype(BF16),
            'k_k': rwkv_k_k[l].reshape(1, R), 'k_a': rwkv_k_a[l].reshape(1, R), 'r_k': rwkv_r_k[l].reshape(1, R),
            'ln_w': rwkv_ln_w[l].reshape(1, R), 'ln_b': rwkv_ln_b[l].reshape(1, R),
            'e1': e1, 'e2': e2,
        }
        r, v, kk, lw, kd, b, bonus, g = _rwkv_prep(dm, z, rp)
        g_m, h_m, q_m, y0_m = _rwkv_chunk(dm, r, v, kk, lw, kd, b)
        y_rwkv_f, y_rwkv_b = _rwkv_scan(dm, g_m, h_m, q_m, y0_m)

        wa_p = jnp.stack([_pad_to(jnp.pad(gla_wa2[l, d], ((d * G_GATE_RANK, 0), (0, 0))), LORA_PAD, 0)
                          for d in range(2)])
        wa_hi = wa_p.astype(BF16)
        wa_split = jnp.stack([wa_hi, (wa_p - wa_hi.astype(F32)).astype(BF16)], axis=1)
        y_gla_f, y_gla_b = _gla(dm, z, wa_split, gla_ba[l].reshape(2, 1, dm.GK))

        xs = _merge(dm, alpha, xs, mods, z, o_att_ctx, o_att_lat, (y_rwkv_f, y_rwkv_b, bonus, g, rp),
                    (y_gla_f, y_gla_b, gla_norm_w[l]), wba, wbr, wbg, wo, l, ln_g[l, 1], ln_b[l, 1])
        xs = _ffn(dm, alpha, xs, mods, 6, w_gu, w_dn, l, 1, ln_g[l, 2], ln_b[l, 2])
    return xs.reshape(B, S, D)
```

```python
import functools
import math

import jax
import jax.numpy as jnp
import numpy as np
from jax import lax
from jax.experimental import pallas as pl
from jax.experimental.pallas import tpu as pltpu

F32 = jnp.float32
BF16 = jnp.bfloat16
HI = lax.Precision.HIGHEST

N_MOD = 9
D_FF = 5632
LN_EPS = 1e-5
FFN_RES = 0.5

A_HEADS = 8
A_KV_HEADS = 2
A_GROUP = A_HEADS // A_KV_HEADS
A_HEAD_DIM = 128
A_Q_DIM = A_HEADS * A_HEAD_DIM
A_KV_DIM = A_KV_HEADS * A_HEAD_DIM
WINDOW = 128
A_BLOCK = 128
GRID_W = 64
ROPE_BASE = 10000.0
MASK_VALUE = -1e30

R_HEAD_DIM = 64
DECAY_LORA = 96
ICLR_LORA = 96
GATE_LORA = 256
R_GN_EPS = 64e-5
R_CHUNK = 64
LORA_PAD = 128
SUBLANES = 8
HALO = 16

G_HEADS = 4
G_GATE_RANK = 16
G_GATE_NORM = 16.0
G_CHUNK = 64
G_EPS = 1e-5

ADA_STREAMS = 3
MOD_ROWS = 16
VMEM_LIMIT = 58 * 1024 * 1024
VMEM_LIMIT_FFN = 63 * 1024 * 1024

NN = (((1,), (0,)), ((), ()))
NT = (((1,), (1,)), ((), ()))
TN = (((0,), (0,)), ((), ()))


def _dot(a, b, dims=NN, prec=None):
    return lax.dot_general(a, b, dims, precision=prec, preferred_element_type=F32)


def _cumsum_dot(tri, x):
    hi = x.astype(BF16)
    rest = x - hi.astype(F32)
    mid = rest.astype(BF16)
    lo = (rest - mid.astype(F32)).astype(BF16)
    return _dot(tri, hi) + _dot(tri, mid) + _dot(tri, lo)


def _sigmoid(x):
    return 1.0 / (1.0 + jnp.exp(-x))


def _log_sigmoid(x):
    return jnp.minimum(x, 0.0) - jnp.log(1.0 + jnp.exp(-jnp.abs(x)))


def _ln(x):
    mu = jnp.mean(x, -1, keepdims=True)
    xc = x - mu
    var = jnp.mean(xc * xc, -1, keepdims=True)
    return xc * lax.rsqrt(var + LN_EPS)


def _params(*sem, vmem=VMEM_LIMIT):
    return pltpu.CompilerParams(dimension_semantics=sem, vmem_limit_bytes=vmem)


class _Dims:
    def __init__(self, B, S, CT, D):
        self.B, self.S, self.CT, self.D = B, S, CT, D
        self.MC, self.ML = B * CT, B * S
        self.M = self.MC + self.ML
        self.R = D // 2
        self.GK = D // 2
        self.GV = D
        self.RH = self.R // R_HEAD_DIM
        self.rw_cols = 3 * self.R + GATE_LORA + 4 * LORA_PAD
        c = 0
        self.c_rwkv = c; c += self.rw_cols
        self.c_att_k = c; c += A_KV_DIM
        self.c_gate = c; c += 3 * D
        self.c_gla_v = c; c += self.GV
        self.c_gla_g = c; c += self.GV
        self.c_gla_q = c; c += self.GK
        self.c_gla_k = c; c += self.GK
        self.c_att_q = c; c += A_Q_DIM
        self.c_att_v = c; c += A_KV_DIM
        self.c_gla_ac = c; c += LORA_PAD
        self.tn_in = 1280
        self.NP = -(-c // self.tn_in) * self.tn_in
        self.tm = math.gcd(512, math.gcd(self.MC, S))
        self.tm_in = math.gcd(1024, math.gcd(self.MC, S))
        self.tmr = math.gcd(256, math.gcd(CT, S))

    def mod_row(self, i, tm):
        nct = self.MC // tm
        tpb = self.S // tm
        return jnp.where(i < nct, self.B, (i - nct) // tpb)

    def mod_spec(self, m, tm, off=0):
        return pl.BlockSpec((None, 1, self.D), lambda i, *_: (self.mod_row(i + off, tm) * N_MOD + m, 0, 0))

    def chunk_block(self, b, d, s, chunk):
        ncc, ncl = self.CT // chunk, self.S // chunk
        j = s - ncc
        ctx_blk = b * ncc + jnp.where(d == 0, s, ncc - 1 - s)
        lat_blk = self.B * ncc + b * ncl + jnp.where(d == 0, j, ncl - 1 - j)
        return jnp.where(s < ncc, ctx_blk, lat_blk)


def _ada_kernel(c_ref, *refs):
    w_refs, b_ref, o_ref = refs[:-2], refs[-2], refs[-1]

    @pl.when(pl.program_id(1) == 0)
    def _():
        o_ref[...] = jnp.broadcast_to(b_ref[...], o_ref.shape)

    cc = c_ref[...]
    s = (cc * _sigmoid(cc)).astype(BF16)
    wide = o_ref.shape[1] // len(w_refs)
    for j, w_ref in enumerate(w_refs):
        o_ref[:, j * wide:(j + 1) * wide] += _dot(s, w_ref[...].astype(BF16))


def _ada(cc, ada_w, ada_b):
    L, D, N = ada_w.shape
    tk = 256
    wide = N // ADA_STREAMS
    cc_k = cc.reshape(MOD_ROWS, D // tk, tk).transpose(1, 0, 2)
    return pl.pallas_call(
        _ada_kernel,
        grid=(L, D // tk),
        in_specs=[
            pl.BlockSpec((None, MOD_ROWS, tk), lambda l, k: (k, 0, 0)),
        ] + [pl.BlockSpec((None, tk, wide), lambda l, k, j=j: (l, k, j)) for j in range(ADA_STREAMS)] + [
            pl.BlockSpec((None, 1, N), lambda l, k: (l, 0, 0)),
        ],
        out_specs=pl.BlockSpec((None, MOD_ROWS, N), lambda l, k: (l, 0, 0)),
        out_shape=jax.ShapeDtypeStruct((L, MOD_ROWS, N), F32),
        compiler_params=_params("parallel", "arbitrary"),
        name="ada",
    )(cc_k, *([ada_w] * ADA_STREAMS), ada_b.reshape(L, 1, N))


def _ffn_kernel(alpha, x_ref, sh_ref, sc_ref, gt_ref, wg_ref, wu_ref, wd_ref, g_ref, b_ref, o_ref, h_ref):
    f = pl.program_id(1)

    @pl.when(f == 0)
    def _():
        h = _ln(x_ref[...]) * (1.0 + sc_ref[...]) + sh_ref[...]
        h_ref[...] = h.astype(BF16)
        o_ref[...] = jnp.zeros_like(o_ref)

    h = h_ref[...]
    g = _dot(h, wg_ref[...])
    u = _dot(h, wu_ref[...])
    a = (g * _sigmoid(g) * u).astype(BF16)
    o_ref[...] += _dot(a, wd_ref[...])

    @pl.when(f == pl.num_programs(1) - 1)
    def _():
        z = alpha * x_ref[...] + (FFN_RES * gt_ref[...]) * o_ref[...]
        o_ref[...] = _ln(z) * g_ref[...] + b_ref[...]


def _ffn(dm, alpha, x, mods, m0, w_gu, w_down, l, j, g, b):
    D, tm, tf = dm.D, dm.tm_in, 512
    nf = D_FF // tf
    rows = x.shape[0]
    off = (dm.M - rows) // tm
    return pl.pallas_call(
        functools.partial(_ffn_kernel, alpha),
        grid=(rows // tm, nf),
        in_specs=[
            pl.BlockSpec((tm, D), lambda i, f: (i, 0)),
            dm.mod_spec(m0, tm, off), dm.mod_spec(m0 + 1, tm, off), dm.mod_spec(m0 + 2, tm, off),
            pl.BlockSpec((None, None, D, tf), lambda i, f: (l, j, 0, f)),
            pl.BlockSpec((None, None, D, tf), lambda i, f: (l, j, 0, f + nf)),
            pl.BlockSpec((None, None, tf, D), lambda i, f: (l, j, f, 0)),
            pl.BlockSpec((1, D), lambda i, f: (0, 0)),
            pl.BlockSpec((1, D), lambda i, f: (0, 0)),
        ],
        out_specs=pl.BlockSpec((tm, D), lambda i, f: (i, 0)),
        out_shape=jax.ShapeDtypeStruct((rows, D), F32),
        scratch_shapes=[pltpu.VMEM((tm, D), BF16)],
        compiler_params=_params("parallel", "arbitrary", vmem=VMEM_LIMIT_FFN),
        name="ffn",
    )(x, mods, mods, mods, w_gu, w_gu, w_down, g.reshape(1, D), b.reshape(1, D))


def _inproj_kernel(x_ref, sh_ref, sc_ref, w_ref, o_ref, h_ref):
    @pl.when(pl.program_id(1) == 0)
    def _():
        h = _ln(x_ref[...]) * (1.0 + sc_ref[...]) + sh_ref[...]
        h_ref[...] = h.astype(BF16)

    o_ref[...] = _dot(h_ref[...], w_ref[...]).astype(BF16)


def _inproj(dm, x, mods, w_in_p):
    D, tm, tn = dm.D, dm.tm_in, dm.tn_in
    return pl.pallas_call(
        _inproj_kernel,
        grid=(dm.M // tm, dm.NP // tn),
        in_specs=[
            pl.BlockSpec((tm, D), lambda i, n: (i, 0)),
            dm.mod_spec(3, tm), dm.mod_spec(4, tm),
            pl.BlockSpec((D, tn), lambda i, n: (0, n)),
        ],
        out_specs=pl.BlockSpec((tm, tn), lambda i, n: (i, n)),
        out_shape=jax.ShapeDtypeStruct((dm.M, dm.NP), BF16),
        scratch_shapes=[pltpu.VMEM((tm, D), BF16)],
        compiler_params=_params("parallel", "arbitrary"),
        name="inproj",
    )(x, mods, mods, w_in_p)


def _rope(x, cos, sin_signed):
    lane = lax.broadcasted_iota(jnp.int32, x.shape, 1)
    quarter = A_HEAD_DIM // 4
    swapped = jnp.where((lane % (2 * quarter)) < quarter,
                        pltpu.roll(x, A_HEAD_DIM - quarter, 1), pltpu.roll(x, quarter, 1))
    return x * cos + swapped * sin_signed


def _attn_kernel(S, nb, sink_ref, q_ref, kp_ref, kc_ref, kn_ref, vp_ref, vc_ref, vn_ref, kx_ref, vx_ref,
                 cos_ref, sin_ref, o_ref):
    n = pl.program_id(1)
    scale = A_HEAD_DIM ** -0.5

    def table(ref, blk):
        return ref[pl.ds(pl.multiple_of(blk * A_BLOCK, A_BLOCK), A_BLOCK), :]

    blk_p, blk_n = jnp.maximum(n - 1, 0), jnp.minimum(n + 1, nb - 1)
    cos_c, sin_c = table(cos_ref, n), table(sin_ref, n)
    cos_p, sin_p = table(cos_ref, blk_p), table(sin_ref, blk_p)
    cos_n, sin_n = table(cos_ref, blk_n), table(sin_ref, blk_n)

    rows, band = A_GROUP * A_BLOCK, 3 * A_BLOCK
    qpos = n * A_BLOCK + lax.broadcasted_iota(jnp.int32, (rows, band), 0) % A_BLOCK
    kpos = (n - 1) * A_BLOCK + lax.broadcasted_iota(jnp.int32, (rows, band), 1)
    valid = (jnp.abs(qpos - kpos) <= WINDOW) & (kpos >= 0) & (kpos < S)

    work = []
    for kvh in range(A_KV_HEADS):
        ks = slice(kvh * A_HEAD_DIM, (kvh + 1) * A_HEAD_DIM)
        kb = jnp.concatenate([_rope(kp_ref[:, ks].astype(F32), cos_p, sin_p),
                              _rope(kc_ref[:, ks].astype(F32), cos_c, sin_c),
                              _rope(kn_ref[:, ks].astype(F32), cos_n, sin_n)], axis=0).astype(BF16)
        heads = [kvh * A_GROUP + g for g in range(A_GROUP)]
        qs = [q_ref[:, h * A_HEAD_DIM:(h + 1) * A_HEAD_DIM] for h in heads]
        q_rope = jnp.concatenate([_rope(q.astype(F32), cos_c, sin_c) * scale for q in qs], axis=0).astype(BF16)
        q_plain = jnp.concatenate([q.astype(F32) * scale for q in qs], axis=0).astype(BF16)
        sink = jnp.concatenate([jnp.full((A_BLOCK, 1), sink_ref[h], F32) for h in heads], axis=0)
        s_loc = jnp.where(valid, _dot(q_rope, kb, NT), MASK_VALUE)
        s_ctx = _dot(q_plain, kx_ref[:, ks], NT)
        work.append(dict(ks=ks, heads=heads, sink=sink, s_loc=s_loc, s_ctx=s_ctx))
    for w in work:
        w['m'] = jnp.maximum(jnp.maximum(jnp.max(w['s_loc'], -1, keepdims=True),
                                         jnp.max(w['s_ctx'], -1, keepdims=True)), w['sink'])
    for w in work:
        w['p_loc'] = jnp.exp(w.pop('s_loc') - w['m'])
        w['p_ctx'] = jnp.exp(w.pop('s_ctx') - w['m'])
    for w in work:
        w['den'] = (jnp.sum(w['p_loc'], -1, keepdims=True) + jnp.sum(w['p_ctx'], -1, keepdims=True)
                    + jnp.exp(w['sink'] - w['m']))
    for w in work:
        ks = w['ks']
        vb = jnp.concatenate([vp_ref[:, ks], vc_ref[:, ks], vn_ref[:, ks]], axis=0)
        o = (_dot(w['p_loc'].astype(BF16), vb) + _dot(w['p_ctx'].astype(BF16), vx_ref[:, ks])) / w['den']
        for g, h in enumerate(w['heads']):
            o_ref[:, h * A_HEAD_DIM:(h + 1) * A_HEAD_DIM] = o[g * A_BLOCK:(g + 1) * A_BLOCK].astype(BF16)


def _attn_latent(dm, z, sink, cos, sin):
    B, S = dm.B, dm.S
    nb = S // A_BLOCK
    base = dm.MC // A_BLOCK
    qc = dm.c_att_q // A_Q_DIM
    kc = dm.c_att_k // A_KV_DIM
    vc = dm.c_att_v // A_KV_DIM

    def rows(shift):
        return lambda b, n: base + b * nb + jnp.clip(n + shift, 0, nb - 1)

    def kv_spec(col, shift):
        r = rows(shift)
        return pl.BlockSpec((A_BLOCK, A_KV_DIM), lambda b, n: (r(b, n), col))

    return pl.pallas_call(
        functools.partial(_attn_kernel, S, nb),
        grid=(B, nb),
        in_specs=[
            pl.BlockSpec(memory_space=pltpu.SMEM),
            pl.BlockSpec((A_BLOCK, A_Q_DIM), lambda b, n: (base + b * nb + n, qc)),
            kv_spec(kc, -1), kv_spec(kc, 0), kv_spec(kc, 1),
            kv_spec(vc, -1), kv_spec(vc, 0), kv_spec(vc, 1),
            pl.BlockSpec((dm.CT, A_KV_DIM), lambda b, n: (b, kc)),
            pl.BlockSpec((dm.CT, A_KV_DIM), lambda b, n: (b, vc)),
            pl.BlockSpec((S, A_HEAD_DIM), lambda b, n: (0, 0)),
            pl.BlockSpec((S, A_HEAD_DIM), lambda b, n: (0, 0)),
        ],
        out_specs=pl.BlockSpec((A_BLOCK, A_Q_DIM), lambda b, n: (b * nb + n, 0)),
        out_shape=jax.ShapeDtypeStruct((dm.ML, A_Q_DIM), BF16),
        compiler_params=_params("parallel", "arbitrary"),
        name="attn_latent",
    )(sink, z, z, z, z, z, z, z, z, z, cos, sin)


def _attn_ctx_kernel(CT, sink_ref, q_ref, k_ref, v_ref, o_ref):
    scale = A_HEAD_DIM ** -0.5
    for kvh in range(A_KV_HEADS):
        ks = slice(kvh * A_HEAD_DIM, (kvh + 1) * A_HEAD_DIM)
        k = k_ref[:, ks]
        v = v_ref[:, ks]
        heads = [kvh * A_GROUP + g for g in range(A_GROUP)]
        q = jnp.concatenate([q_ref[:, h * A_HEAD_DIM:(h + 1) * A_HEAD_DIM] for h in heads], axis=0)
        sink = jnp.concatenate([jnp.full((CT, 1), sink_ref[h], F32) for h in heads], axis=0)
        s = _dot(q, k, NT) * scale
        m = jnp.maximum(jnp.max(s, -1, keepdims=True), sink)
        p = jnp.exp(s - m)
        den = jnp.sum(p, -1, keepdims=True) + jnp.exp(sink - m)
        o = _dot(p.astype(BF16), v) / den
        for g, h in enumerate(heads):
            o_ref[:, h * A_HEAD_DIM:(h + 1) * A_HEAD_DIM] = o[g * CT:(g + 1) * CT].astype(BF16)


def _attn_ctx(dm, z, sink):
    CT = dm.CT
    return pl.pallas_call(
        functools.partial(_attn_ctx_kernel, CT),
        grid=(dm.B,),
        in_specs=[
            pl.BlockSpec(memory_space=pltpu.SMEM),
            pl.BlockSpec((CT, A_Q_DIM), lambda b: (b, dm.c_att_q // A_Q_DIM)),
            pl.BlockSpec((CT, A_KV_DIM), lambda b: (b, dm.c_att_k // A_KV_DIM)),
            pl.BlockSpec((CT, A_KV_DIM), lambda b: (b, dm.c_att_v // A_KV_DIM)),
        ],
        out_specs=pl.BlockSpec((CT, A_Q_DIM), lambda b: (b, 0)),
        out_shape=jax.ShapeDtypeStruct((dm.MC, A_Q_DIM), BF16),
        compiler_params=_params("parallel"),
        name="attn_ctx",
    )(sink, z, z, z)


def _head_sum(x, e1_ref, e2_ref):
    def split_dot(a, ind):
        hi = a.astype(BF16)
        lo = (a - hi.astype(F32)).astype(BF16)
        return _dot(hi, ind) + _dot(lo, ind)

    return split_dot(split_dot(x, e1_ref[...]), e2_ref[...])


def _rwkv_prep_kernel(dm, z_ref, zp_ref, zn_ref, cw_ref, g2_ref, w0_ref, w2_ref, a0_ref, a2_ref, kk_ref, ka_ref,
                      rk_ref, e1_ref, e2_ref,
                      r_out, v_out, kk_out, lw_out, kd_out, b_out, bonus_out, g_out):
    i = pl.program_id(0)
    tm, R = dm.tmr, dm.R
    nct = dm.MC // tm
    start = jnp.where(i < nct, (i * tm) % dm.CT, ((i - nct) * tm) % dm.S)
    seqlen = jnp.where(i < nct, dm.CT, dm.S)
    has_prev = (start != 0).astype(F32)
    has_next = (start + tm != seqlen).astype(F32)

    z = z_ref[...].astype(F32)
    sub = lax.broadcasted_iota(jnp.int32, (SUBLANES, 1), 0)
    halo_prev = zp_ref[HALO - 1:HALO, :].astype(F32) * has_prev
    halo_next = zn_ref[0:1, :].astype(F32) * has_next
    down, up = pltpu.roll(z, 1, 0), pltpu.roll(z, tm - 1, 0)
    z_prev = jnp.concatenate([jnp.where(sub == 0, halo_prev, down[:SUBLANES]), down[SUBLANES:]], axis=0)
    z_next = jnp.concatenate([up[:tm - SUBLANES], jnp.where(sub == SUBLANES - 1, halo_next, up[tm - SUBLANES:])],
                             axis=0)
    zc = z_prev * cw_ref[0:1, :] + z * cw_ref[1:2, :] + z_next * cw_ref[2:3, :]

    r = zc[:, 0:R]
    k = zc[:, R:2 * R]
    v = zc[:, 2 * R:3 * R]
    c0 = 3 * R
    gc = zc[:, c0:c0 + GATE_LORA]
    c0 += GATE_LORA
    g_out[...] = _dot(_sigmoid(gc).astype(BF16), g2_ref[...])

    kkr = k * kk_ref[...]
    norm = jnp.sqrt(_head_sum(kkr * kkr, e1_ref, e2_ref))
    kk = kkr / jnp.maximum(norm, 1e-12)
    r_out[...] = r
    v_out[...] = v
    kk_out[...] = kk

    bonus = jnp.zeros_like(v)
    for d in range(2):
        wc = zc[:, c0 + d * LORA_PAD:c0 + (d + 1) * LORA_PAD]
        ac = zc[:, c0 + (2 + d) * LORA_PAD:c0 + (3 + d) * LORA_PAD]
        wl = w0_ref[d:d + 1, :] + _dot(jnp.tanh(wc).astype(BF16), w2_ref[d])
        lw_out[d] = _sigmoid(wl) * (-math.exp(-0.5))
        a = _sigmoid(a0_ref[d:d + 1, :] + _dot(ac.astype(BF16), a2_ref[d]))
        kd = k * (1.0 + (a - 1.0) * ka_ref[...])
        kd_out[d] = kd
        b_out[d] = kk * a
        bonus = bonus + _head_sum(r * kd * rk_ref[...], e1_ref, e2_ref) * v
    bonus_out[...] = bonus


def _rwkv_prep(dm, z, p):
    tm, R, M, W = dm.tmr, dm.R, dm.M, dm.rw_cols
    n_halo = M // HALO
    row = pl.BlockSpec((tm, R), lambda i: (i, 0))
    row2 = pl.BlockSpec((2, tm, R), lambda i: (0, i, 0))

    def full(a):
        return pl.BlockSpec(a.shape, lambda i: (0,) * a.ndim)

    consts = [p['conv'], p['g2'], p['w0'], p['w2'], p['a0'], p['a2'], p['k_k'], p['k_a'], p['r_k'], p['e1'], p['e2']]
    one = jax.ShapeDtypeStruct((M, R), F32)
    two = jax.ShapeDtypeStruct((2, M, R), F32)
    return pl.pallas_call(
        functools.partial(_rwkv_prep_kernel, dm),
        grid=(M // tm,),
        in_specs=[
            pl.BlockSpec((tm, W), lambda i: (i, 0)),
            pl.BlockSpec((HALO, W), lambda i: (jnp.maximum(i * (tm // HALO) - 1, 0), 0)),
            pl.BlockSpec((HALO, W), lambda i: (jnp.minimum((i + 1) * (tm // HALO), n_halo - 1), 0)),
        ] + [full(a) for a in consts],
        out_specs=[row, row, row, row2, row2, row2, row, row],
        out_shape=[one, one, one, two, two, two, one, one],
        compiler_params=_params("parallel"),
        name="rwkv_prep",
    )(z, z, z, *consts)


R_SLAB = 4 * R_HEAD_DIM


def _block_diag(x, mask):
    return jnp.where(mask, jnp.concatenate([x] * 4, axis=0), 0.0).astype(BF16)


def _diag_blocks(full, mask):
    fm = jnp.where(mask, full, 0.0)
    n = R_HEAD_DIM
    return fm[0:n] + fm[n:2 * n] + fm[2 * n:3 * n] + fm[3 * n:4 * n]


def _slab_masks():
    n, g = R_HEAD_DIM, R_SLAB
    blk = (lax.broadcasted_iota(jnp.int32, (g, g), 0) // n) == (lax.broadcasted_iota(jnp.int32, (g, g), 1) // n)
    t_idx = lax.broadcasted_iota(jnp.int32, (n, g), 0)
    s_idx = lax.broadcasted_iota(jnp.int32, (n, g), 1) % n
    return blk, t_idx, s_idx


def _rwkv_chunk_kernel(n_sub, r_ref, v_ref, kk_ref, lw_ref, kd_ref, b_ref, g_out, h_out, q_out, y0_out):
    C = R_CHUNK
    blk, t_idx, s_idx = _slab_masks()
    eye4 = (t_idx == s_idx).astype(F32)
    ti = lax.broadcasted_iota(jnp.int32, (C, C), 0)
    si = lax.broadcasted_iota(jnp.int32, (C, C), 1)

    def bd(x):
        return _block_diag(x, blk)

    def setup(c, d):
        rs = pl.ds(c * C, C)
        r, v, kk = r_ref[rs, :], v_ref[rs, :], kk_ref[rs, :]
        tri = ((ti >= si) if d == 0 else (ti <= si)).astype(BF16)
        lw = lw_ref[d, rs, :]
        cl = _cumsum_dot(tri, lw)
        tot = jnp.sum(lw, axis=0, keepdims=True)
        e_neg = jnp.exp(-cl)
        e_end = jnp.exp(tot - cl)
        kd, b = kd_ref[d, rs, :], b_ref[d, rs, :]
        ch = dict(rs=rs, d=d, tot=tot, v_b=v.astype(BF16), v_bd=bd(v),
                  incl=(t_idx >= s_idx) if d == 0 else (t_idx <= s_idx),
                  strict=(t_idx > s_idx) if d == 0 else (t_idx < s_idx),
                  kkm=kk * jnp.exp(cl - lw), rp=r * jnp.exp(cl),
                  bp=b * e_neg, kp=kd * e_neg, be=(b * e_end).astype(BF16), ke=(kd * e_end).astype(BF16))
        ch['lhs'] = jnp.concatenate([ch['kkm'], ch['rp']], axis=0).astype(BF16)
        return ch

    chains = [setup(c, d) for c in range(n_sub) for d in range(2)]
    for ch in chains:
        gram_b = _dot(ch['lhs'], bd(ch.pop('bp')), NT)
        gram_k = _dot(ch.pop('lhs'), bd(ch.pop('kp')), NT)
        lb = jnp.where(ch['strict'], gram_b[:C], 0.0)
        ch['mb'] = jnp.where(ch['incl'], gram_b[C:], 0.0).astype(BF16)
        ch['lmk'] = jnp.concatenate([jnp.where(ch['strict'], gram_k[:C], 0.0),
                                     jnp.where(ch['incl'], gram_k[C:], 0.0)], axis=0).astype(BF16)
        ch['pw'] = -lb
        ch['t_inv'] = eye4 - lb
    for ch in chains:
        ch['pw'] = _dot(ch['pw'].astype(BF16), bd(ch['pw']))
        lmkv = _dot(ch.pop('lmk'), ch.pop('v_bd'))
        ch['lkv'], ch['mkv'] = lmkv[:C], lmkv[C:]
    for _ in range(int(math.log2(C)) - 2):
        for ch in chains:
            both = _dot(jnp.concatenate([ch['pw'], ch['t_inv']], axis=0).astype(BF16), bd(ch['pw']))
            ch['pw'], ch['t_inv'] = both[:C], ch['t_inv'] + both[C:]
    for ch in chains:
        ch['t_inv'] = (ch['t_inv'] + _dot(ch['t_inv'].astype(BF16), bd(ch.pop('pw')))).astype(BF16)
    for ch in chains:
        ch['a_m'] = _dot(ch['t_inv'], bd(ch.pop('kkm')))
        ch['d_m'] = -_dot(ch.pop('t_inv'), bd(ch.pop('lkv')))
    for ch in chains:
        d, rs = ch['d'], ch['rs']
        a_m, d_m = ch['a_m'], ch['d_m']
        q_out[d, rs, :] = (ch['rp'] - _dot(ch['mb'], bd(a_m))).astype(BF16)
        y0_out[d, rs, :] = (_dot(ch['mb'], bd(d_m)) + ch['mkv']).astype(BF16)
        g_cross = _dot(ch['be'], a_m.astype(BF16), TN)
        h_cross = _dot(jnp.concatenate([ch['be'], ch['ke']], axis=0),
                       jnp.concatenate([d_m.astype(BF16), ch['v_b']], axis=0), TN)
        g_out[d, rs, :] = (eye4 * jnp.exp(ch['tot']) - _diag_blocks(g_cross, blk)).astype(BF16)
        h_out[d, rs, :] = _diag_blocks(h_cross, blk)


def _rwkv_chunk(dm, r, v, kk, lw, kd, b):
    M, R = dm.M, dm.R
    rows = dm.tmr
    one = pl.BlockSpec((rows, R_SLAB), lambda i, h: (i, h))
    two = pl.BlockSpec((2, rows, R_SLAB), lambda i, h: (0, i, h))
    f32 = jax.ShapeDtypeStruct((2, M, R), F32)
    b16 = jax.ShapeDtypeStruct((2, M, R), BF16)
    return pl.pallas_call(
        functools.partial(_rwkv_chunk_kernel, rows // R_CHUNK),
        grid=(M // rows, R // R_SLAB),
        in_specs=[one, one, one, two, two, two],
        out_specs=[two, two, two, two],
        out_shape=[b16, f32, b16, b16],
        compiler_params=_params("parallel", "parallel"),
        name="rwkv_chunk",
    )(r, v, kk, lw, kd, b)


def _rwkv_scan_kernel(n_slab, n_in, gf_ref, hf_ref, qf_ref, y0f_ref, gb_ref, hb_ref, qb_ref, y0b_ref,
                      yf_ref, yb_ref, st_ref):
    C = R_CHUNK
    dirs = ((gf_ref, hf_ref, qf_ref, y0f_ref, yf_ref), (gb_ref, hb_ref, qb_ref, y0b_ref, yb_ref))

    @pl.when(pl.program_id(1) == 0)
    def _():
        st_ref[...] = jnp.zeros_like(st_ref)

    blk, _, _ = _slab_masks()
    lanes = [slice(sl * R_SLAB, (sl + 1) * R_SLAB) for sl in range(n_slab)]
    for t in range(n_in):
        todo = [(d, refs, pl.ds((t, n_in - 1 - t)[d] * C, C), ls) for d, refs in enumerate(dirs) for ls in lanes]
        st_bd = [_block_diag(st_ref[d, :, ls], blk) for d, _, _, ls in todo]
        both = [_dot(jnp.concatenate([refs[2][rs, ls], refs[0][rs, ls]], axis=0), bd)
                for (d, refs, rs, ls), bd in zip(todo, st_bd)]
        for (d, refs, rs, ls), res in zip(todo, both):
            refs[4][rs, ls] = (res[:C] + refs[3][rs, ls].astype(F32)).astype(BF16)
            st_ref[d, :, ls] = res[C:] + refs[1][rs, ls]


def _rwkv_scan(dm, g_m, h_m, q_m, y0_m):
    R, rows = dm.R, dm.tmr
    steps = (dm.CT + dm.S) // rows

    def spec(d):
        return pl.BlockSpec((None, rows, R), lambda b, s: (d, dm.chunk_block(b, d, s, rows), 0))

    def yspec(d):
        return pl.BlockSpec((rows, R), lambda b, s: (dm.chunk_block(b, d, s, rows), 0))

    y = jax.ShapeDtypeStruct((dm.M, R), BF16)
    return pl.pallas_call(
        functools.partial(_rwkv_scan_kernel, R // R_SLAB, rows // R_CHUNK),
        grid=(dm.B, steps),
        in_specs=[spec(0)] * 4 + [spec(1)] * 4,
        out_specs=[yspec(0), yspec(1)],
        out_shape=[y, y],
        scratch_shapes=[pltpu.VMEM((2, R_HEAD_DIM, R), F32)],
        compiler_params=_params("parallel", "arbitrary"),
        name="rwkv_scan",
    )(g_m, h_m, q_m, y0_m, g_m, h_m, q_m, y0_m)


def _rwkv_branch_out(y_f, y_b, bonus, g, lnw, lnb, e1_ref, e2_ref):
    o = y_f.astype(F32) + y_b.astype(F32)
    inv_n = 1.0 / R_HEAD_DIM
    mu = _head_sum(o, e1_ref, e2_ref) * inv_n
    oc = o - mu
    var = _head_sum(oc * oc, e1_ref, e2_ref) * inv_n
    o = oc * lax.rsqrt(var + R_GN_EPS) * lnw + lnb
    return ((o + bonus) * g).astype(BF16)


def _gla_kernel(dm, n_in, qf_ref, kf_ref, vf_ref, af_ref, qb_ref, kb_ref, vb_ref, ab_ref, wa_ref, ba_ref,
                yf_ref, yb_ref, st_ref, qd_ref, ki_ref, ke_ref, y_ref):
    C = G_CHUNK
    dk, dv = dm.GK // G_HEADS, dm.GV // G_HEADS
    key = [slice(h * dk, (h + 1) * dk) for h in range(G_HEADS)]
    val = [slice(h * dv, (h + 1) * dv) for h in range(G_HEADS)]
    row = [slice(c * C, (c + 1) * C) for c in range(n_in)]
    dirs = ((qf_ref, kf_ref, vf_ref, af_ref, yf_ref), (qb_ref, kb_ref, vb_ref, ab_ref, yb_ref))

    @pl.when(pl.program_id(1) == 0)
    def _():
        st_ref[...] = jnp.zeros_like(st_ref)

    ti = lax.broadcasted_iota(jnp.int32, (C, C), 0)
    si = lax.broadcasted_iota(jnp.int32, (C, C), 1)
    causal = (ti >= si, ti <= si)

    la = [_log_sigmoid(_dot(refs[3][...], wa_ref[d, 0]) + _dot(refs[3][...], wa_ref[d, 1]) + ba_ref[d])
          * (1.0 / G_GATE_NORM) for d, refs in enumerate(dirs)]
    cb = [[_cumsum_dot(causal[d].astype(BF16), la[d][r]) for r in row] for d in range(2)]
    e_last = []
    for d, refs in enumerate(dirs):
        cum = jnp.concatenate(cb[d], axis=0)
        tots = [jnp.sum(la[d][r], axis=0, keepdims=True) for r in row]
        tot = jnp.concatenate([jnp.broadcast_to(t, (C, t.shape[1])) for t in tots], axis=0)
        k = refs[1][...].astype(F32)
        qd_ref[d] = (refs[0][...].astype(F32) * (dk ** -0.5) * jnp.exp(cum)).astype(BF16)
        ki_ref[d] = (k * jnp.exp(-cum)).astype(BF16)
        ke_ref[d] = (k * jnp.exp(tot - cum)).astype(BF16)
        e_last.append([jnp.exp(t) for t in tots])
    for d, refs in enumerate(dirs):
        att = [[jnp.where(causal[d], _dot(qd_ref[d, r, ks], ki_ref[d, r, ks], NT), 0.0).astype(BF16) for ks in key]
               for r in row]
        for c, r in enumerate(row):
            for h in range(G_HEADS):
                y_ref[d, r, val[h]] = _dot(att[c][h], refs[2][r, val[h]])
    for t in range(n_in):
        todo = [(d, refs, (t, n_in - 1 - t)[d], h) for d, refs in enumerate(dirs) for h in range(G_HEADS)]
        inter = [_dot(qd_ref[d, row[c], key[h]], st_ref[d, h].astype(BF16), NT) for d, _, c, h in todo]
        upd = [_dot(refs[2][row[c], val[h]], ke_ref[d, row[c], key[h]], TN) for d, refs, c, h in todo]
        for (d, refs, c, h), o in zip(todo, inter):
            y_ref[d, row[c], val[h]] += o
        for (d, refs, c, h), u in zip(todo, upd):
            st_ref[d, h] = st_ref[d, h] * e_last[d][c][:, key[h]] + u
    yf_ref[...] = y_ref[0].astype(BF16)
    yb_ref[...] = y_ref[1].astype(BF16)


def _gla(dm, z, wa_split, ba):
    rows = dm.tmr
    steps = (dm.CT + dm.S) // rows
    GK, GV = dm.GK, dm.GV

    def zspec(width, col, d):
        return pl.BlockSpec((rows, width), lambda b, s: (dm.chunk_block(b, d, s, rows), col // width))

    def zspecs(d):
        return [zspec(GK, dm.c_gla_q, d), zspec(GK, dm.c_gla_k, d), zspec(GV, dm.c_gla_v, d),
                zspec(LORA_PAD, dm.c_gla_ac, d)]

    def yspec(d):
        return pl.BlockSpec((rows, GV), lambda b, s: (dm.chunk_block(b, d, s, rows), 0))

    y = jax.ShapeDtypeStruct((dm.M, GV), BF16)
    return pl.pallas_call(
        functools.partial(_gla_kernel, dm, rows // G_CHUNK),
        grid=(dm.B, steps),
        in_specs=zspecs(0) + zspecs(1) + [
            pl.BlockSpec((2, 2, LORA_PAD, GK), lambda b, s: (0, 0, 0, 0)),
            pl.BlockSpec((2, 1, GK), lambda b, s: (0, 0, 0)),
        ],
        out_specs=[yspec(0), yspec(1)],
        out_shape=[y, y],
        scratch_shapes=[pltpu.VMEM((2, G_HEADS, GV // G_HEADS, GK // G_HEADS), F32),
                        pltpu.VMEM((2, rows, GK), BF16), pltpu.VMEM((2, rows, GK), BF16),
                        pltpu.VMEM((2, rows, GK), BF16), pltpu.VMEM((2, rows, GV), F32)],
        compiler_params=_params("parallel", "arbitrary"),
        name="gla",
    )(z, z, z, z, z, z, z, z, wa_split, ba)


def _gla_branch_out(y_f, y_b, g, norm_w):
    o = y_f.astype(F32) + y_b.astype(F32)
    g = g.astype(F32)
    gate = g * _sigmoid(g)
    dv = norm_w.shape[1]
    heads = []
    for h in range(G_HEADS):
        vs = slice(h * dv, (h + 1) * dv)
        oh = o[:, vs]
        oh = oh * lax.rsqrt(jnp.mean(oh * oh, -1, keepdims=True) + G_EPS) * norm_w
        heads.append((oh * gate[:, vs]).astype(BF16))
    return jnp.concatenate(heads, axis=1)


def _merge_kernel(alpha, n_chunks, ctx_tiles, x_ref, mg_ref, oac_ref, oal_ref, ryf_ref, ryb_ref, bonus_ref, rg_ref, gyf_ref, gyb_ref,
                  gg_in_ref, ga_ref, gr_ref, gg_ref, wa_ref, wr_ref, wg_ref, wo_ref, lnw_ref, lnb_ref, e1_ref, e2_ref,
                  nw_ref, g_ref, b_ref, o_ref):
    tn = wo_ref.shape[0] // n_chunks
    oa = jnp.where(pl.program_id(0) < ctx_tiles, oac_ref[...], oal_ref[...])
    orw = _rwkv_branch_out(ryf_ref[...], ryb_ref[...], bonus_ref[...], rg_ref[...], lnw_ref[...], lnb_ref[...],
                           e1_ref, e2_ref)
    og = _gla_branch_out(gyf_ref[...], gyb_ref[...], gg_in_ref[...], nw_ref[...])
    acc = None
    for n in range(n_chunks):
        cs = slice(n * tn, (n + 1) * tn)
        merged = (_sigmoid(ga_ref[:, cs].astype(F32)) * _dot(oa, wa_ref[:, cs])
                  + _sigmoid(gr_ref[:, cs].astype(F32)) * _dot(orw, wr_ref[:, cs])
                  + _sigmoid(gg_ref[:, cs].astype(F32)) * _dot(og, wg_ref[:, cs]))
        part = _dot(merged.astype(BF16), wo_ref[cs, :])
        acc = part if acc is None else acc + part
    z = alpha * x_ref[...] + mg_ref[...] * acc
    o_ref[...] = _ln(z) * g_ref[...] + b_ref[...]


def _merge(dm, alpha, x, mods, z, o_att_ctx, o_att_lat, rwkv, gla, wba, wbr, wbg, w_out, l, g, b):
    D, tm = dm.D, dm.tmr
    nct = dm.MC // tm
    off = nct if o_att_ctx is None else 0
    rows = dm.M - off * tm
    if o_att_ctx is None:
        o_att_ctx = o_att_lat
    gate0 = dm.c_gate // D
    ry_f, ry_b, bonus, r_gate, rp = rwkv
    gy_f, gy_b, norm_w = gla
    dv = dm.GV // G_HEADS

    def row_spec(width, col=0):
        return pl.BlockSpec((tm, width), lambda i: (i + off, col))

    def weight(k_rows):
        return pl.BlockSpec((None, k_rows, D), lambda i: (l, 0, 0), pipeline_mode=pl.Buffered(1))

    def full(a):
        return pl.BlockSpec(a.shape, lambda i: (0,) * a.ndim)

    consts = [rp['ln_w'], rp['ln_b'], rp['e1'], rp['e2'], norm_w.reshape(1, dv), g.reshape(1, D), b.reshape(1, D)]
    return pl.pallas_call(
        functools.partial(_merge_kernel, alpha, 4, nct - off),
        grid=(rows // tm,),
        in_specs=[
            row_spec(D), dm.mod_spec(5, tm, off),
            pl.BlockSpec((tm, A_Q_DIM), lambda i: (jnp.minimum(i + off, nct - 1), 0)),
            pl.BlockSpec((tm, A_Q_DIM), lambda i: (jnp.maximum(i + off - nct, 0), 0)),
            row_spec(dm.R), row_spec(dm.R), row_spec(dm.R), row_spec(dm.R),
            row_spec(dm.GV), row_spec(dm.GV), row_spec(dm.GV, dm.c_gla_g // dm.GV),
            row_spec(D, gate0), row_spec(D, gate0 + 1), row_spec(D, gate0 + 2),
            weight(A_Q_DIM), weight(dm.R), weight(dm.GV), weight(D),
        ] + [full(a) for a in consts],
        out_specs=pl.BlockSpec((tm, D), lambda i: (i, 0)),
        out_shape=jax.ShapeDtypeStruct((rows, D), F32),
        compiler_params=_params("parallel"),
        name="merge",
    )(x, mods, o_att_ctx, o_att_lat, ry_f, ry_b, bonus, r_gate, gy_f, gy_b, z, z, z, z, wba, wbr, wbg, w_out, *consts)


def _pad_to(a, size, axis):
    pad = [(0, 0)] * a.ndim
    pad[axis] = (0, size - a.shape[axis])
    return jnp.pad(a, pad)


def _rwkv_cols(a, R):
    c = 3 * R
    rkv = a[..., :c]
    wc0, wc1 = a[..., c:c + DECAY_LORA], a[..., c + DECAY_LORA:c + 2 * DECAY_LORA]
    c += 2 * DECAY_LORA
    ac0, ac1 = a[..., c:c + ICLR_LORA], a[..., c + ICLR_LORA:c + 2 * ICLR_LORA]
    c += 2 * ICLR_LORA
    gc = a[..., c:c + GATE_LORA]
    ax = a.ndim - 1
    return jnp.concatenate([rkv, gc] + [_pad_to(t, LORA_PAD, ax) for t in (wc0, wc1, ac0, ac1)], axis=-1)


def _in_proj_weights(dm, w_in):
    D, R, GK, GV = dm.D, dm.R, dm.GK, dm.GV
    att_cols = A_Q_DIM + 2 * A_KV_DIM
    rw_src = 3 * R + 2 * DECAY_LORA + 2 * ICLR_LORA + GATE_LORA
    gla_src = 2 * GK + 2 * GV + 2 * G_GATE_RANK
    att, rwkv, gla, gate = jnp.split(w_in, np.cumsum([att_cols, rw_src, gla_src]).tolist(), axis=-1)
    a_q, a_k, a_v = att[:, :A_Q_DIM], att[:, A_Q_DIM:A_Q_DIM + A_KV_DIM], att[:, A_Q_DIM + A_KV_DIM:]
    g_q, g_k = gla[:, :GK], gla[:, GK:2 * GK]
    g_v, g_g = gla[:, 2 * GK:2 * GK + GV], gla[:, 2 * GK + GV:2 * GK + 2 * GV]
    g_ac = _pad_to(gla[:, 2 * GK + 2 * GV:], LORA_PAD, 1)
    parts = [_rwkv_cols(rwkv, R), a_k, gate, g_v, g_g, g_q, g_k, a_q, a_v, g_ac]
    parts.append(jnp.zeros((D, dm.NP - sum(p.shape[1] for p in parts)), w_in.dtype))
    return jnp.concatenate([p.astype(BF16) for p in parts], axis=-1)


def _rope_tables(S):
    rows = S // GRID_W
    row = jnp.repeat(jnp.arange(rows, dtype=F32), GRID_W)
    col = jnp.tile(jnp.arange(GRID_W, dtype=F32), rows)
    n_freq = A_HEAD_DIM // 4
    inv_freq = ROPE_BASE ** (-jnp.arange(n_freq, dtype=F32) / n_freq)
    ang_r, ang_c = row[:, None] * inv_freq, col[:, None] * inv_freq
    cos = jnp.concatenate([jnp.cos(ang_r)] * 2 + [jnp.cos(ang_c)] * 2, axis=-1)
    sin = jnp.concatenate([-jnp.sin(ang_r), jnp.sin(ang_r), -jnp.sin(ang_c), jnp.sin(ang_c)], axis=-1)
    return cos, sin


def kernel(x, c, ctx, c_ctx, ada_w, ada_b, ln_g, ln_b, ffn_w_gu, ffn_w_down, w_in, attn_sink, rwkv_conv, rwkv_w0,
           rwkv_w2, rwkv_a0, rwkv_a2, rwkv_g2, rwkv_k_k, rwkv_k_a, rwkv_r_k, rwkv_ln_w, rwkv_ln_b, gla_wa2, gla_ba,
           gla_norm_w, w_branch_att, w_branch_rwkv, w_branch_gla, w_out):
    B, S, D = x.shape
    CT = ctx.shape[1]
    depth = ada_w.shape[0]
    dm = _Dims(B, S, CT, D)
    R = dm.R
    alpha = float((2 * depth) ** 0.25)
    assert B + 1 <= MOD_ROWS

    cc = _pad_to(jnp.concatenate([c, c_ctx[None, :]], axis=0), MOD_ROWS, 0)
    mods_all = _ada(cc, ada_w, ada_b)
    cos, sin = _rope_tables(S)
    head_of_lane = jnp.arange(R) // R_HEAD_DIM
    e1 = (head_of_lane[:, None] == jnp.arange(LORA_PAD)[None, :]).astype(BF16)
    e2 = e1.T

    w_gu, w_dn = ffn_w_gu.astype(BF16), ffn_w_down.astype(BF16)
    wba, wbr = w_branch_att.astype(BF16), w_branch_rwkv.astype(BF16)
    wbg, wo = w_branch_gla.astype(BF16), w_out.astype(BF16)

    xs = jnp.concatenate([ctx.reshape(dm.MC, D), x.reshape(dm.ML, D)], axis=0)
    for l in range(depth):
        last = l == depth - 1
        mods = mods_all[l].reshape(MOD_ROWS * N_MOD, 1, D)
        xs = _ffn(dm, alpha, xs, mods, 0, w_gu, w_dn, l, 0, ln_g[l, 0], ln_b[l, 0])

        z = _inproj(dm, xs, mods, _in_proj_weights(dm, w_in[l]))
        o_att_lat = _attn_latent(dm, z, attn_sink[l], cos, sin)
        o_att_ctx = None if last else _attn_ctx(dm, z, attn_sink[l])

        rp = {
            'conv': _rwkv_cols(rwkv_conv[l], R),
            'g2': rwkv_g2[l].astype(BF16),
            'w0': rwkv_w0[l], 'a0': rwkv_a0[l],
            'w2': _pad_to(rwkv_w2[l], LORA_PAD, 1).astype(BF16),
            'a2': _pad_to(rwkv_a2[l], LORA_PAD, 1).astype(BF16),
            'k_k': rwkv_k_k[l].reshape(1, R), 'k_a': rwkv_k_a[l].reshape(1, R), 'r_k': rwkv_r_k[l].reshape(1, R),
            'ln_w': rwkv_ln_w[l].reshape(1, R), 'ln_b': rwkv_ln_b[l].reshape(1, R),
            'e1': e1, 'e2': e2,
        }
        r, v, kk, lw, kd, b, bonus, g = _rwkv_prep(dm, z, rp)
        g_m, h_m, q_m, y0_m = _rwkv_chunk(dm, r, v, kk, lw, kd, b)
        y_rwkv_f, y_rwkv_b = _rwkv_scan(dm, g_m, h_m, q_m, y0_m)

        wa_p = jnp.stack([_pad_to(jnp.pad(gla_wa2[l, d], ((d * G_GATE_RANK, 0), (0, 0))), LORA_PAD, 0)
                          for d in range(2)])
        wa_hi = wa_p.astype(BF16)
        wa_split = jnp.stack([wa_hi, (wa_p - wa_hi.astype(F32)).astype(BF16)], axis=1)
        y_gla_f, y_gla_b = _gla(dm, z, wa_split, gla_ba[l].reshape(2, 1, dm.GK))

        xs = _merge(dm, alpha, xs, mods, z, o_att_ctx, o_att_lat, (y_rwkv_f, y_rwkv_b, bonus, g, rp),
                    (y_gla_f, y_gla_b, gla_norm_w[l]), wba, wbr, wbg, wo, l, ln_g[l, 1], ln_b[l, 1])
        xs = _ffn(dm, alpha, xs, mods, 6, w_gu, w_dn, l, 1, ln_g[l, 2], ln_b[l, 2])
    return xs.reshape(B, S, D)
```

```python
import functools
import math

import jax
import jax.numpy as jnp
import numpy as np
from jax import lax
from jax.experimental import pallas as pl
from jax.experimental.pallas import tpu as pltpu

F32 = jnp.float32
BF16 = jnp.bfloat16

N_MOD = 9
D_FF = 5632
LN_EPS = 1e-5
FFN_RES = 0.5

A_HEADS = 8
A_KV_HEADS = 2
A_GROUP = A_HEADS // A_KV_HEADS
A_HEAD_DIM = 128
A_Q_DIM = A_HEADS * A_HEAD_DIM
A_KV_DIM = A_KV_HEADS * A_HEAD_DIM
WINDOW = 128
A_BLOCK = 128
GRID_W = 64
ROPE_BASE = 10000.0
MASK_VALUE = -1e30

R_HEAD_DIM = 64
DECAY_LORA = 96
ICLR_LORA = 96
GATE_LORA = 256
R_GN_EPS = 64e-5
R_CHUNK = 64
LORA_PAD = 128
SUBLANES = 8
HALO = 16

G_HEADS = 4
G_GATE_RANK = 16
G_GATE_NORM = 16.0
G_CHUNK = 64
G_EPS = 1e-5

ADA_STREAMS = 3
ROW_TILE = 512
ROW_TILE_WIDE = 1024
SEQ_TILE = 256
IN_PROJ_COLS = 1280
FFN_HIDDEN_TILE = 512
ADA_K_TILE = 256
MERGE_CHUNKS = 4
MOD_ROWS = 16
VMEM_LIMIT = 58 * 1024 * 1024
VMEM_LIMIT_FFN = 63 * 1024 * 1024

NN = (((1,), (0,)), ((), ()))
NT = (((1,), (1,)), ((), ()))
TN = (((0,), (0,)), ((), ()))


def _dot(a, b, dims=NN):
    return lax.dot_general(a, b, dims, preferred_element_type=F32)


def _cumsum_dot(tri, x):
    hi = x.astype(BF16)
    rest = x - hi.astype(F32)
    mid = rest.astype(BF16)
    lo = (rest - mid.astype(F32)).astype(BF16)
    return _dot(tri, hi) + _dot(tri, mid) + _dot(tri, lo)


def _sigmoid(x):
    return 1.0 / (1.0 + jnp.exp(-x))


def _log_sigmoid(x):
    return jnp.minimum(x, 0.0) - jnp.log(1.0 + jnp.exp(-jnp.abs(x)))


def _ln(x):
    mu = jnp.mean(x, -1, keepdims=True)
    xc = x - mu
    var = jnp.mean(xc * xc, -1, keepdims=True)
    return xc * lax.rsqrt(var + LN_EPS)


def _params(*sem, vmem=VMEM_LIMIT):
    return pltpu.CompilerParams(dimension_semantics=sem, vmem_limit_bytes=vmem)


class _Dims:
    def __init__(self, B, S, CT, D):
        self.B, self.S, self.CT, self.D = B, S, CT, D
        self.MC, self.ML = B * CT, B * S
        self.M = self.MC + self.ML
        self.R = D // 2
        self.GK = D // 2
        self.GV = D
        self.rw_cols = 3 * self.R + GATE_LORA + 4 * LORA_PAD
        c = 0
        self.c_rwkv = c; c += self.rw_cols
        self.c_att_k = c; c += A_KV_DIM
        self.c_gate = c; c += 3 * D
        self.c_gla_v = c; c += self.GV
        self.c_gla_g = c; c += self.GV
        self.c_gla_q = c; c += self.GK
        self.c_gla_k = c; c += self.GK
        self.c_att_q = c; c += A_Q_DIM
        self.c_att_v = c; c += A_KV_DIM
        self.c_gla_ac = c; c += LORA_PAD
        self.tn_in = IN_PROJ_COLS
        self.NP = -(-c // self.tn_in) * self.tn_in
        self.tm = math.gcd(ROW_TILE, math.gcd(self.MC, S))
        self.tm_in = math.gcd(ROW_TILE_WIDE, math.gcd(self.MC, S))
        self.tmr = math.gcd(SEQ_TILE, math.gcd(CT, S))

    def mod_row(self, i, tm):
        nct = self.MC // tm
        tpb = self.S // tm
        return jnp.where(i < nct, self.B, (i - nct) // tpb)

    def mod_spec(self, m, tm, off=0):
        return pl.BlockSpec((None, 1, self.D), lambda i, *_: (self.mod_row(i + off, tm) * N_MOD + m, 0, 0))

    def chunk_block(self, b, d, s, chunk):
        ncc, ncl = self.CT // chunk, self.S // chunk
        j = s - ncc
        ctx_blk = b * ncc + jnp.where(d == 0, s, ncc - 1 - s)
        lat_blk = self.B * ncc + b * ncl + jnp.where(d == 0, j, ncl - 1 - j)
        return jnp.where(s < ncc, ctx_blk, lat_blk)


def _ada_kernel(c_ref, *refs):
    w_refs, b_ref, o_ref = refs[:-2], refs[-2], refs[-1]

    @pl.when(pl.program_id(1) == 0)
    def _():
        o_ref[...] = jnp.broadcast_to(b_ref[...], o_ref.shape)

    cc = c_ref[...]
    s = (cc * _sigmoid(cc)).astype(BF16)
    wide = o_ref.shape[1] // len(w_refs)
    for j, w_ref in enumerate(w_refs):
        o_ref[:, j * wide:(j + 1) * wide] += _dot(s, w_ref[...].astype(BF16))


def _ada(cc, ada_w, ada_b):
    L, D, N = ada_w.shape
    tk = ADA_K_TILE
    wide = N // ADA_STREAMS
    cc_k = cc.reshape(MOD_ROWS, D // tk, tk).transpose(1, 0, 2)
    return pl.pallas_call(
        _ada_kernel,
        grid=(L, D // tk),
        in_specs=[
            pl.BlockSpec((None, MOD_ROWS, tk), lambda l, k: (k, 0, 0)),
        ] + [pl.BlockSpec((None, tk, wide), lambda l, k, j=j: (l, k, j)) for j in range(ADA_STREAMS)] + [
            pl.BlockSpec((None, 1, N), lambda l, k: (l, 0, 0)),
        ],
        out_specs=pl.BlockSpec((None, MOD_ROWS, N), lambda l, k: (l, 0, 0)),
        out_shape=jax.ShapeDtypeStruct((L, MOD_ROWS, N), F32),
        compiler_params=_params("parallel", "arbitrary"),
        name="ada",
    )(cc_k, *([ada_w] * ADA_STREAMS), ada_b.reshape(L, 1, N))


def _ffn_kernel(alpha, x_ref, sh_ref, sc_ref, gt_ref, wg_ref, wu_ref, wd_ref, g_ref, b_ref, o_ref, h_ref):
    f = pl.program_id(1)

    @pl.when(f == 0)
    def _():
        h = _ln(x_ref[...]) * (1.0 + sc_ref[...]) + sh_ref[...]
        h_ref[...] = h.astype(BF16)
        o_ref[...] = jnp.zeros_like(o_ref)

    h = h_ref[...]
    g = _dot(h, wg_ref[...])
    u = _dot(h, wu_ref[...])
    a = (g * _sigmoid(g) * u).astype(BF16)
    o_ref[...] += _dot(a, wd_ref[...])

    @pl.when(f == pl.num_programs(1) - 1)
    def _():
        z = alpha * x_ref[...] + (FFN_RES * gt_ref[...]) * o_ref[...]
        o_ref[...] = _ln(z) * g_ref[...] + b_ref[...]


def _ffn(dm, alpha, x, mods, m0, w_gu, w_down, l, j, g, b):
    D, tm, tf = dm.D, dm.tm_in, FFN_HIDDEN_TILE
    nf = D_FF // tf
    rows = x.shape[0]
    off = (dm.M - rows) // tm
    return pl.pallas_call(
        functools.partial(_ffn_kernel, alpha),
        grid=(rows // tm, nf),
        in_specs=[
            pl.BlockSpec((tm, D), lambda i, f: (i, 0)),
            dm.mod_spec(m0, tm, off), dm.mod_spec(m0 + 1, tm, off), dm.mod_spec(m0 + 2, tm, off),
            pl.BlockSpec((None, None, D, tf), lambda i, f: (l, j, 0, f)),
            pl.BlockSpec((None, None, D, tf), lambda i, f: (l, j, 0, f + nf)),
            pl.BlockSpec((None, None, tf, D), lambda i, f: (l, j, f, 0)),
            pl.BlockSpec((1, D), lambda i, f: (0, 0)),
            pl.BlockSpec((1, D), lambda i, f: (0, 0)),
        ],
        out_specs=pl.BlockSpec((tm, D), lambda i, f: (i, 0)),
        out_shape=jax.ShapeDtypeStruct((rows, D), F32),
        scratch_shapes=[pltpu.VMEM((tm, D), BF16)],
        compiler_params=_params("parallel", "arbitrary", vmem=VMEM_LIMIT_FFN),
        name="ffn",
    )(x, mods, mods, mods, w_gu, w_gu, w_down, g.reshape(1, D), b.reshape(1, D))


def _inproj_kernel(x_ref, sh_ref, sc_ref, w_ref, o_ref, h_ref):
    @pl.when(pl.program_id(1) == 0)
    def _():
        h = _ln(x_ref[...]) * (1.0 + sc_ref[...]) + sh_ref[...]
        h_ref[...] = h.astype(BF16)

    o_ref[...] = _dot(h_ref[...], w_ref[...]).astype(BF16)


def _inproj(dm, x, mods, w_in_p):
    D, tm, tn = dm.D, dm.tm_in, dm.tn_in
    return pl.pallas_call(
        _inproj_kernel,
        grid=(dm.M // tm, dm.NP // tn),
        in_specs=[
            pl.BlockSpec((tm, D), lambda i, n: (i, 0)),
            dm.mod_spec(3, tm), dm.mod_spec(4, tm),
            pl.BlockSpec((D, tn), lambda i, n: (0, n)),
        ],
        out_specs=pl.BlockSpec((tm, tn), lambda i, n: (i, n)),
        out_shape=jax.ShapeDtypeStruct((dm.M, dm.NP), BF16),
        scratch_shapes=[pltpu.VMEM((tm, D), BF16)],
        compiler_params=_params("parallel", "arbitrary"),
        name="inproj",
    )(x, mods, mods, w_in_p)


def _rope(x, cos, sin_signed):
    lane = lax.broadcasted_iota(jnp.int32, x.shape, 1)
    quarter = A_HEAD_DIM // 4
    swapped = jnp.where((lane % (2 * quarter)) < quarter,
                        pltpu.roll(x, A_HEAD_DIM - quarter, 1), pltpu.roll(x, quarter, 1))
    return x * cos + swapped * sin_signed


def _attn_kernel(S, nb, sink_ref, q_ref, kp_ref, kc_ref, kn_ref, vp_ref, vc_ref, vn_ref, kx_ref, vx_ref,
                 cos_ref, sin_ref, o_ref):
    n = pl.program_id(1)
    scale = A_HEAD_DIM ** -0.5

    def table(ref, blk):
        return ref[pl.ds(pl.multiple_of(blk * A_BLOCK, A_BLOCK), A_BLOCK), :]

    blk_p, blk_n = jnp.maximum(n - 1, 0), jnp.minimum(n + 1, nb - 1)
    cos_c, sin_c = table(cos_ref, n), table(sin_ref, n)
    cos_p, sin_p = table(cos_ref, blk_p), table(sin_ref, blk_p)
    cos_n, sin_n = table(cos_ref, blk_n), table(sin_ref, blk_n)

    rows, band = A_GROUP * A_BLOCK, 3 * A_BLOCK
    qpos = n * A_BLOCK + lax.broadcasted_iota(jnp.int32, (rows, band), 0) % A_BLOCK
    kpos = (n - 1) * A_BLOCK + lax.broadcasted_iota(jnp.int32, (rows, band), 1)
    valid = (jnp.abs(qpos - kpos) <= WINDOW) & (kpos >= 0) & (kpos < S)

    work = []
    for kvh in range(A_KV_HEADS):
        ks = slice(kvh * A_HEAD_DIM, (kvh + 1) * A_HEAD_DIM)
        kb = jnp.concatenate([_rope(kp_ref[:, ks].astype(F32), cos_p, sin_p),
                              _rope(kc_ref[:, ks].astype(F32), cos_c, sin_c),
                              _rope(kn_ref[:, ks].astype(F32), cos_n, sin_n)], axis=0).astype(BF16)
        heads = [kvh * A_GROUP + g for g in range(A_GROUP)]
        qs = [q_ref[:, h * A_HEAD_DIM:(h + 1) * A_HEAD_DIM] for h in heads]
        q_rope = jnp.concatenate([_rope(q.astype(F32), cos_c, sin_c) * scale for q in qs], axis=0).astype(BF16)
        q_plain = jnp.concatenate([q.astype(F32) * scale for q in qs], axis=0).astype(BF16)
        sink = jnp.concatenate([jnp.full((A_BLOCK, 1), sink_ref[h], F32) for h in heads], axis=0)
        s_loc = jnp.where(valid, _dot(q_rope, kb, NT), MASK_VALUE)
        s_ctx = _dot(q_plain, kx_ref[:, ks], NT)
        work.append(dict(ks=ks, heads=heads, sink=sink, s_loc=s_loc, s_ctx=s_ctx))
    for w in work:
        w['m'] = jnp.maximum(jnp.maximum(jnp.max(w['s_loc'], -1, keepdims=True),
                                         jnp.max(w['s_ctx'], -1, keepdims=True)), w['sink'])
    for w in work:
        w['p_loc'] = jnp.exp(w.pop('s_loc') - w['m'])
        w['p_ctx'] = jnp.exp(w.pop('s_ctx') - w['m'])
    for w in work:
        w['den'] = (jnp.sum(w['p_loc'], -1, keepdims=True) + jnp.sum(w['p_ctx'], -1, keepdims=True)
                    + jnp.exp(w['sink'] - w['m']))
    for w in work:
        ks = w['ks']
        vb = jnp.concatenate([vp_ref[:, ks], vc_ref[:, ks], vn_ref[:, ks]], axis=0)
        o = (_dot(w['p_loc'].astype(BF16), vb) + _dot(w['p_ctx'].astype(BF16), vx_ref[:, ks])) / w['den']
        for g, h in enumerate(w['heads']):
            o_ref[:, h * A_HEAD_DIM:(h + 1) * A_HEAD_DIM] = o[g * A_BLOCK:(g + 1) * A_BLOCK].astype(BF16)


def _attn_latent(dm, z, sink, cos, sin):
    B, S = dm.B, dm.S
    nb = S // A_BLOCK
    base = dm.MC // A_BLOCK
    qc = dm.c_att_q // A_Q_DIM
    kc = dm.c_att_k // A_KV_DIM
    vc = dm.c_att_v // A_KV_DIM

    def rows(shift):
        return lambda b, n: base + b * nb + jnp.clip(n + shift, 0, nb - 1)

    def kv_spec(col, shift):
        r = rows(shift)
        return pl.BlockSpec((A_BLOCK, A_KV_DIM), lambda b, n: (r(b, n), col))

    return pl.pallas_call(
        functools.partial(_attn_kernel, S, nb),
        grid=(B, nb),
        in_specs=[
            pl.BlockSpec(memory_space=pltpu.SMEM),
            pl.BlockSpec((A_BLOCK, A_Q_DIM), lambda b, n: (base + b * nb + n, qc)),
            kv_spec(kc, -1), kv_spec(kc, 0), kv_spec(kc, 1),
            kv_spec(vc, -1), kv_spec(vc, 0), kv_spec(vc, 1),
            pl.BlockSpec((dm.CT, A_KV_DIM), lambda b, n: (b, kc)),
            pl.BlockSpec((dm.CT, A_KV_DIM), lambda b, n: (b, vc)),
            pl.BlockSpec((S, A_HEAD_DIM), lambda b, n: (0, 0)),
            pl.BlockSpec((S, A_HEAD_DIM), lambda b, n: (0, 0)),
        ],
        out_specs=pl.BlockSpec((A_BLOCK, A_Q_DIM), lambda b, n: (b * nb + n, 0)),
        out_shape=jax.ShapeDtypeStruct((dm.ML, A_Q_DIM), BF16),
        compiler_params=_params("parallel", "arbitrary"),
        name="attn_latent",
    )(sink, z, z, z, z, z, z, z, z, z, cos, sin)


def _attn_ctx_kernel(CT, sink_ref, q_ref, k_ref, v_ref, o_ref):
    scale = A_HEAD_DIM ** -0.5
    for kvh in range(A_KV_HEADS):
        ks = slice(kvh * A_HEAD_DIM, (kvh + 1) * A_HEAD_DIM)
        k = k_ref[:, ks]
        v = v_ref[:, ks]
        heads = [kvh * A_GROUP + g for g in range(A_GROUP)]
        q = jnp.concatenate([q_ref[:, h * A_HEAD_DIM:(h + 1) * A_HEAD_DIM] for h in heads], axis=0)
        sink = jnp.concatenate([jnp.full((CT, 1), sink_ref[h], F32) for h in heads], axis=0)
        s = _dot(q, k, NT) * scale
        m = jnp.maximum(jnp.max(s, -1, keepdims=True), sink)
        p = jnp.exp(s - m)
        den = jnp.sum(p, -1, keepdims=True) + jnp.exp(sink - m)
        o = _dot(p.astype(BF16), v) / den
        for g, h in enumerate(heads):
            o_ref[:, h * A_HEAD_DIM:(h + 1) * A_HEAD_DIM] = o[g * CT:(g + 1) * CT].astype(BF16)


def _attn_ctx(dm, z, sink):
    CT = dm.CT
    return pl.pallas_call(
        functools.partial(_attn_ctx_kernel, CT),
        grid=(dm.B,),
        in_specs=[
            pl.BlockSpec(memory_space=pltpu.SMEM),
            pl.BlockSpec((CT, A_Q_DIM), lambda b: (b, dm.c_att_q // A_Q_DIM)),
            pl.BlockSpec((CT, A_KV_DIM), lambda b: (b, dm.c_att_k // A_KV_DIM)),
            pl.BlockSpec((CT, A_KV_DIM), lambda b: (b, dm.c_att_v // A_KV_DIM)),
        ],
        out_specs=pl.BlockSpec((CT, A_Q_DIM), lambda b: (b, 0)),
        out_shape=jax.ShapeDtypeStruct((dm.MC, A_Q_DIM), BF16),
        compiler_params=_params("parallel"),
        name="attn_ctx",
    )(sink, z, z, z)


def _head_sum(x, e1_ref, e2_ref):
    def split_dot(a, ind):
        hi = a.astype(BF16)
        lo = (a - hi.astype(F32)).astype(BF16)
        return _dot(hi, ind) + _dot(lo, ind)

    return split_dot(split_dot(x, e1_ref[...]), e2_ref[...])


def _rwkv_prep_kernel(dm, z_ref, zp_ref, zn_ref, cw_ref, g2_ref, w0_ref, w2_ref, a0_ref, a2_ref, kk_ref, ka_ref,
                      rk_ref, e1_ref, e2_ref,
                      r_out, v_out, kk_out, lw_out, kd_out, b_out, bonus_out, g_out):
    i = pl.program_id(0)
    tm, R = dm.tmr, dm.R
    nct = dm.MC // tm
    start = jnp.where(i < nct, (i * tm) % dm.CT, ((i - nct) * tm) % dm.S)
    seqlen = jnp.where(i < nct, dm.CT, dm.S)
    has_prev = (start != 0).astype(F32)
    has_next = (start + tm != seqlen).astype(F32)

    z = z_ref[...].astype(F32)
    sub = lax.broadcasted_iota(jnp.int32, (SUBLANES, 1), 0)
    halo_prev = zp_ref[HALO - 1:HALO, :].astype(F32) * has_prev
    halo_next = zn_ref[0:1, :].astype(F32) * has_next
    down, up = pltpu.roll(z, 1, 0), pltpu.roll(z, tm - 1, 0)
    z_prev = jnp.concatenate([jnp.where(sub == 0, halo_prev, down[:SUBLANES]), down[SUBLANES:]], axis=0)
    z_next = jnp.concatenate([up[:tm - SUBLANES], jnp.where(sub == SUBLANES - 1, halo_next, up[tm - SUBLANES:])],
                             axis=0)
    zc = z_prev * cw_ref[0:1, :] + z * cw_ref[1:2, :] + z_next * cw_ref[2:3, :]

    r = zc[:, 0:R]
    k = zc[:, R:2 * R]
    v = zc[:, 2 * R:3 * R]
    c0 = 3 * R
    gc = zc[:, c0:c0 + GATE_LORA]
    c0 += GATE_LORA
    g_out[...] = _dot(_sigmoid(gc).astype(BF16), g2_ref[...])

    kkr = k * kk_ref[...]
    norm = jnp.sqrt(_head_sum(kkr * kkr, e1_ref, e2_ref))
    kk = kkr / jnp.maximum(norm, 1e-12)
    r_out[...] = r
    v_out[...] = v
    kk_out[...] = kk

    bonus = jnp.zeros_like(v)
    for d in range(2):
        wc = zc[:, c0 + d * LORA_PAD:c0 + (d + 1) * LORA_PAD]
        ac = zc[:, c0 + (2 + d) * LORA_PAD:c0 + (3 + d) * LORA_PAD]
        wl = w0_ref[d:d + 1, :] + _dot(jnp.tanh(wc).astype(BF16), w2_ref[d])
        lw_out[d] = _sigmoid(wl) * (-math.exp(-0.5))
        a = _sigmoid(a0_ref[d:d + 1, :] + _dot(ac.astype(BF16), a2_ref[d]))
        kd = k * (1.0 + (a - 1.0) * ka_ref[...])
        kd_out[d] = kd
        b_out[d] = kk * a
        bonus = bonus + _head_sum(r * kd * rk_ref[...], e1_ref, e2_ref) * v
    bonus_out[...] = bonus


def _rwkv_prep(dm, z, p):
    tm, R, M, W = dm.tmr, dm.R, dm.M, dm.rw_cols
    n_halo = M // HALO
    col = dm.c_rwkv // W
    row = pl.BlockSpec((tm, R), lambda i: (i, 0))
    row2 = pl.BlockSpec((2, tm, R), lambda i: (0, i, 0))

    def full(a):
        return pl.BlockSpec(a.shape, lambda i: (0,) * a.ndim)

    consts = [p['conv'], p['g2'], p['w0'], p['w2'], p['a0'], p['a2'], p['k_k'], p['k_a'], p['r_k'], p['e1'], p['e2']]
    one = jax.ShapeDtypeStruct((M, R), F32)
    two = jax.ShapeDtypeStruct((2, M, R), F32)
    return pl.pallas_call(
        functools.partial(_rwkv_prep_kernel, dm),
        grid=(M // tm,),
        in_specs=[
            pl.BlockSpec((tm, W), lambda i: (i, col)),
            pl.BlockSpec((HALO, W), lambda i: (jnp.maximum(i * (tm // HALO) - 1, 0), col)),
            pl.BlockSpec((HALO, W), lambda i: (jnp.minimum((i + 1) * (tm // HALO), n_halo - 1), col)),
        ] + [full(a) for a in consts],
        out_specs=[row, row, row, row2, row2, row2, row, row],
        out_shape=[one, one, one, two, two, two, one, one],
        compiler_params=_params("parallel"),
        name="rwkv_prep",
    )(z, z, z, *consts)


R_SLAB = 4 * R_HEAD_DIM
SLAB_HEADS = R_SLAB // R_HEAD_DIM


def _block_diag(x, mask):
    return jnp.where(mask, jnp.concatenate([x] * SLAB_HEADS, axis=0), 0.0).astype(BF16)


def _diag_blocks(full, mask):
    fm = jnp.where(mask, full, 0.0)
    n = R_HEAD_DIM
    return functools.reduce(lambda acc, j: acc + fm[j * n:(j + 1) * n], range(1, SLAB_HEADS), fm[0:n])


def _slab_masks():
    n, g = R_HEAD_DIM, R_SLAB
    blk = (lax.broadcasted_iota(jnp.int32, (g, g), 0) // n) == (lax.broadcasted_iota(jnp.int32, (g, g), 1) // n)
    t_idx = lax.broadcasted_iota(jnp.int32, (n, g), 0)
    s_idx = lax.broadcasted_iota(jnp.int32, (n, g), 1) % n
    return blk, t_idx, s_idx


def _rwkv_chunk_kernel(n_sub, r_ref, v_ref, kk_ref, lw_ref, kd_ref, b_ref, g_out, h_out, q_out, y0_out):
    C = R_CHUNK
    blk, t_idx, s_idx = _slab_masks()
    eye4 = (t_idx == s_idx).astype(F32)
    ti = lax.broadcasted_iota(jnp.int32, (C, C), 0)
    si = lax.broadcasted_iota(jnp.int32, (C, C), 1)

    def bd(x):
        return _block_diag(x, blk)

    def setup(c, d):
        rs = pl.ds(c * C, C)
        r, v, kk = r_ref[rs, :], v_ref[rs, :], kk_ref[rs, :]
        tri = ((ti >= si) if d == 0 else (ti <= si)).astype(BF16)
        lw = lw_ref[d, rs, :]
        cl = _cumsum_dot(tri, lw)
        tot = jnp.sum(lw, axis=0, keepdims=True)
        e_neg = jnp.exp(-cl)
        e_end = jnp.exp(tot - cl)
        kd, b = kd_ref[d, rs, :], b_ref[d, rs, :]
        ch = dict(rs=rs, d=d, tot=tot, v_b=v.astype(BF16), v_bd=bd(v),
                  incl=(t_idx >= s_idx) if d == 0 else (t_idx <= s_idx),
                  strict=(t_idx > s_idx) if d == 0 else (t_idx < s_idx),
                  kkm=kk * jnp.exp(cl - lw), rp=r * jnp.exp(cl),
                  bp=b * e_neg, kp=kd * e_neg, be=(b * e_end).astype(BF16), ke=(kd * e_end).astype(BF16))
        ch['lhs'] = jnp.concatenate([ch['kkm'], ch['rp']], axis=0).astype(BF16)
        return ch

    chains = [setup(c, d) for c in range(n_sub) for d in range(2)]
    for ch in chains:
        gram_b = _dot(ch['lhs'], bd(ch.pop('bp')), NT)
        gram_k = _dot(ch.pop('lhs'), bd(ch.pop('kp')), NT)
        lb = jnp.where(ch['strict'], gram_b[:C], 0.0)
        ch['mb'] = jnp.where(ch['incl'], gram_b[C:], 0.0).astype(BF16)
        ch['lmk'] = jnp.concatenate([jnp.where(ch['strict'], gram_k[:C], 0.0),
                                     jnp.where(ch['incl'], gram_k[C:], 0.0)], axis=0).astype(BF16)
        ch['pw'] = -lb
        ch['t_inv'] = eye4 - lb
    for ch in chains:
        ch['pw'] = _dot(ch['pw'].astype(BF16), bd(ch['pw']))
        lmkv = _dot(ch.pop('lmk'), ch.pop('v_bd'))
        ch['lkv'], ch['mkv'] = lmkv[:C], lmkv[C:]
    for _ in range(int(math.log2(C)) - 2):
        for ch in chains:
            both = _dot(jnp.concatenate([ch['pw'], ch['t_inv']], axis=0).astype(BF16), bd(ch['pw']))
            ch['pw'], ch['t_inv'] = both[:C], ch['t_inv'] + both[C:]
    for ch in chains:
        ch['t_inv'] = (ch['t_inv'] + _dot(ch['t_inv'].astype(BF16), bd(ch.pop('pw')))).astype(BF16)
    for ch in chains:
        ch['a_m'] = _dot(ch['t_inv'], bd(ch.pop('kkm')))
        ch['d_m'] = -_dot(ch.pop('t_inv'), bd(ch.pop('lkv')))
    for ch in chains:
        d, rs = ch['d'], ch['rs']
        a_m, d_m = ch['a_m'], ch['d_m']
        q_out[d, rs, :] = (ch['rp'] - _dot(ch['mb'], bd(a_m))).astype(BF16)
        y0_out[d, rs, :] = (_dot(ch['mb'], bd(d_m)) + ch['mkv']).astype(BF16)
        g_cross = _dot(ch['be'], a_m.astype(BF16), TN)
        h_cross = _dot(jnp.concatenate([ch['be'], ch['ke']], axis=0),
                       jnp.concatenate([d_m.astype(BF16), ch['v_b']], axis=0), TN)
        g_out[d, rs, :] = (eye4 * jnp.exp(ch['tot']) - _diag_blocks(g_cross, blk)).astype(BF16)
        h_out[d, rs, :] = _diag_blocks(h_cross, blk)


def _rwkv_chunk(dm, r, v, kk, lw, kd, b):
    M, R = dm.M, dm.R
    rows = dm.tmr
    one = pl.BlockSpec((rows, R_SLAB), lambda i, h: (i, h))
    two = pl.BlockSpec((2, rows, R_SLAB), lambda i, h: (0, i, h))
    f32 = jax.ShapeDtypeStruct((2, M, R), F32)
    b16 = jax.ShapeDtypeStruct((2, M, R), BF16)
    return pl.pallas_call(
        functools.partial(_rwkv_chunk_kernel, rows // R_CHUNK),
        grid=(M // rows, R // R_SLAB),
        in_specs=[one, one, one, two, two, two],
        out_specs=[two, two, two, two],
        out_shape=[b16, f32, b16, b16],
        compiler_params=_params("parallel", "parallel"),
        name="rwkv_chunk",
    )(r, v, kk, lw, kd, b)


def _rwkv_scan_kernel(n_slab, n_in, gf_ref, hf_ref, qf_ref, y0f_ref, gb_ref, hb_ref, qb_ref, y0b_ref,
                      yf_ref, yb_ref, st_ref):
    C = R_CHUNK
    dirs = ((gf_ref, hf_ref, qf_ref, y0f_ref, yf_ref), (gb_ref, hb_ref, qb_ref, y0b_ref, yb_ref))

    @pl.when(pl.program_id(1) == 0)
    def _():
        st_ref[...] = jnp.zeros_like(st_ref)

    blk, _, _ = _slab_masks()
    lanes = [slice(sl * R_SLAB, (sl + 1) * R_SLAB) for sl in range(n_slab)]
    for t in range(n_in):
        todo = [(d, refs, pl.ds((t, n_in - 1 - t)[d] * C, C), ls) for d, refs in enumerate(dirs) for ls in lanes]
        st_bd = [_block_diag(st_ref[d, :, ls], blk) for d, _, _, ls in todo]
        both = [_dot(jnp.concatenate([refs[2][rs, ls], refs[0][rs, ls]], axis=0), bd)
                for (d, refs, rs, ls), bd in zip(todo, st_bd)]
        for (d, refs, rs, ls), res in zip(todo, both):
            refs[4][rs, ls] = (res[:C] + refs[3][rs, ls].astype(F32)).astype(BF16)
            st_ref[d, :, ls] = res[C:] + refs[1][rs, ls]


def _rwkv_scan(dm, g_m, h_m, q_m, y0_m):
    R, rows = dm.R, dm.tmr
    steps = (dm.CT + dm.S) // rows

    def spec(d):
        return pl.BlockSpec((None, rows, R), lambda b, s: (d, dm.chunk_block(b, d, s, rows), 0))

    def yspec(d):
        return pl.BlockSpec((rows, R), lambda b, s: (dm.chunk_block(b, d, s, rows), 0))

    y = jax.ShapeDtypeStruct((dm.M, R), BF16)
    return pl.pallas_call(
        functools.partial(_rwkv_scan_kernel, R // R_SLAB, rows // R_CHUNK),
        grid=(dm.B, steps),
        in_specs=[spec(0)] * 4 + [spec(1)] * 4,
        out_specs=[yspec(0), yspec(1)],
        out_shape=[y, y],
        scratch_shapes=[pltpu.VMEM((2, R_HEAD_DIM, R), F32)],
        compiler_params=_params("parallel", "arbitrary"),
        name="rwkv_scan",
    )(g_m, h_m, q_m, y0_m, g_m, h_m, q_m, y0_m)


def _rwkv_branch_out(y_f, y_b, bonus, g, lnw, lnb, e1_ref, e2_ref):
    o = y_f.astype(F32) + y_b.astype(F32)
    inv_n = 1.0 / R_HEAD_DIM
    mu = _head_sum(o, e1_ref, e2_ref) * inv_n
    oc = o - mu
    var = _head_sum(oc * oc, e1_ref, e2_ref) * inv_n
    o = oc * lax.rsqrt(var + R_GN_EPS) * lnw + lnb
    return ((o + bonus) * g).astype(BF16)


def _gla_kernel(dm, n_in, qf_ref, kf_ref, vf_ref, af_ref, qb_ref, kb_ref, vb_ref, ab_ref, wa_ref, ba_ref,
                yf_ref, yb_ref, st_ref, qd_ref, ki_ref, ke_ref, y_ref):
    C = G_CHUNK
    dk, dv = dm.GK // G_HEADS, dm.GV // G_HEADS
    key = [slice(h * dk, (h + 1) * dk) for h in range(G_HEADS)]
    val = [slice(h * dv, (h + 1) * dv) for h in range(G_HEADS)]
    row = [slice(c * C, (c + 1) * C) for c in range(n_in)]
    dirs = ((qf_ref, kf_ref, vf_ref, af_ref, yf_ref), (qb_ref, kb_ref, vb_ref, ab_ref, yb_ref))

    @pl.when(pl.program_id(1) == 0)
    def _():
        st_ref[...] = jnp.zeros_like(st_ref)

    ti = lax.broadcasted_iota(jnp.int32, (C, C), 0)
    si = lax.broadcasted_iota(jnp.int32, (C, C), 1)
    causal = (ti >= si, ti <= si)

    la = [_log_sigmoid(_dot(refs[3][...], wa_ref[d, 0]) + _dot(refs[3][...], wa_ref[d, 1]) + ba_ref[d])
          * (1.0 / G_GATE_NORM) for d, refs in enumerate(dirs)]
    cb = [[_cumsum_dot(causal[d].astype(BF16), la[d][r]) for r in row] for d in range(2)]
    e_last = []
    for d, refs in enumerate(dirs):
        cum = jnp.concatenate(cb[d], axis=0)
        tots = [jnp.sum(la[d][r], axis=0, keepdims=True) for r in row]
        tot = jnp.concatenate([jnp.broadcast_to(t, (C, t.shape[1])) for t in tots], axis=0)
        k = refs[1][...].astype(F32)
        qd_ref[d] = (refs[0][...].astype(F32) * (dk ** -0.5) * jnp.exp(cum)).astype(BF16)
        ki_ref[d] = (k * jnp.exp(-cum)).astype(BF16)
        ke_ref[d] = (k * jnp.exp(tot - cum)).astype(BF16)
        e_last.append([jnp.exp(t) for t in tots])
    for d, refs in enumerate(dirs):
        att = [[jnp.where(causal[d], _dot(qd_ref[d, r, ks], ki_ref[d, r, ks], NT), 0.0).astype(BF16) for ks in key]
               for r in row]
        for c, r in enumerate(row):
            for h in range(G_HEADS):
                y_ref[d, r, val[h]] = _dot(att[c][h], refs[2][r, val[h]])
    for t in range(n_in):
        todo = [(d, refs, (t, n_in - 1 - t)[d], h) for d, refs in enumerate(dirs) for h in range(G_HEADS)]
        inter = [_dot(qd_ref[d, row[c], key[h]], st_ref[d, h].astype(BF16), NT) for d, _, c, h in todo]
        upd = [_dot(refs[2][row[c], val[h]], ke_ref[d, row[c], key[h]], TN) for d, refs, c, h in todo]
        for (d, refs, c, h), o in zip(todo, inter):
            y_ref[d, row[c], val[h]] += o
        for (d, refs, c, h), u in zip(todo, upd):
            st_ref[d, h] = st_ref[d, h] * e_last[d][c][:, key[h]] + u
    yf_ref[...] = y_ref[0].astype(BF16)
    yb_ref[...] = y_ref[1].astype(BF16)


def _gla(dm, z, wa_split, ba):
    rows = dm.tmr
    steps = (dm.CT + dm.S) // rows
    GK, GV = dm.GK, dm.GV

    def zspec(width, col, d):
        return pl.BlockSpec((rows, width), lambda b, s: (dm.chunk_block(b, d, s, rows), col // width))

    def zspecs(d):
        return [zspec(GK, dm.c_gla_q, d), zspec(GK, dm.c_gla_k, d), zspec(GV, dm.c_gla_v, d),
                zspec(LORA_PAD, dm.c_gla_ac, d)]

    def yspec(d):
        return pl.BlockSpec((rows, GV), lambda b, s: (dm.chunk_block(b, d, s, rows), 0))

    y = jax.ShapeDtypeStruct((dm.M, GV), BF16)
    return pl.pallas_call(
        functools.partial(_gla_kernel, dm, rows // G_CHUNK),
        grid=(dm.B, steps),
        in_specs=zspecs(0) + zspecs(1) + [
            pl.BlockSpec((2, 2, LORA_PAD, GK), lambda b, s: (0, 0, 0, 0)),
            pl.BlockSpec((2, 1, GK), lambda b, s: (0, 0, 0)),
        ],
        out_specs=[yspec(0), yspec(1)],
        out_shape=[y, y],
        scratch_shapes=[pltpu.VMEM((2, G_HEADS, GV // G_HEADS, GK // G_HEADS), F32),
                        pltpu.VMEM((2, rows, GK), BF16), pltpu.VMEM((2, rows, GK), BF16),
                        pltpu.VMEM((2, rows, GK), BF16), pltpu.VMEM((2, rows, GV), F32)],
        compiler_params=_params("parallel", "arbitrary"),
        name="gla",
    )(z, z, z, z, z, z, z, z, wa_split, ba)


def _gla_branch_out(y_f, y_b, g, norm_w):
    o = y_f.astype(F32) + y_b.astype(F32)
    g = g.astype(F32)
    gate = g * _sigmoid(g)
    dv = norm_w.shape[1]
    heads = []
    for h in range(G_HEADS):
        vs = slice(h * dv, (h + 1) * dv)
        oh = o[:, vs]
        oh = oh * lax.rsqrt(jnp.mean(oh * oh, -1, keepdims=True) + G_EPS) * norm_w
        heads.append((oh * gate[:, vs]).astype(BF16))
    return jnp.concatenate(heads, axis=1)


def _merge_kernel(alpha, n_chunks, ctx_tiles, x_ref, mg_ref, oac_ref, oal_ref, ryf_ref, ryb_ref, bonus_ref, rg_ref, gyf_ref, gyb_ref,
                  gg_in_ref, ga_ref, gr_ref, gg_ref, wa_ref, wr_ref, wg_ref, wo_ref, lnw_ref, lnb_ref, e1_ref, e2_ref,
                  nw_ref, g_ref, b_ref, o_ref):
    tn = wo_ref.shape[0] // n_chunks
    oa = jnp.where(pl.program_id(0) < ctx_tiles, oac_ref[...], oal_ref[...])
    orw = _rwkv_branch_out(ryf_ref[...], ryb_ref[...], bonus_ref[...], rg_ref[...], lnw_ref[...], lnb_ref[...],
                           e1_ref, e2_ref)
    og = _gla_branch_out(gyf_ref[...], gyb_ref[...], gg_in_ref[...], nw_ref[...])
    acc = None
    for n in range(n_chunks):
        cs = slice(n * tn, (n + 1) * tn)
        merged = (_sigmoid(ga_ref[:, cs].astype(F32)) * _dot(oa, wa_ref[:, cs])
                  + _sigmoid(gr_ref[:, cs].astype(F32)) * _dot(orw, wr_ref[:, cs])
                  + _sigmoid(gg_ref[:, cs].astype(F32)) * _dot(og, wg_ref[:, cs]))
        part = _dot(merged.astype(BF16), wo_ref[cs, :])
        acc = part if acc is None else acc + part
    z = alpha * x_ref[...] + mg_ref[...] * acc
    o_ref[...] = _ln(z) * g_ref[...] + b_ref[...]


def _merge(dm, alpha, x, mods, z, o_att_ctx, o_att_lat, rwkv, gla, wba, wbr, wbg, w_out, l, g, b):
    D, tm = dm.D, dm.tmr
    nct = dm.MC // tm
    off = nct if o_att_ctx is None else 0
    rows = dm.M - off * tm
    if o_att_ctx is None:
        o_att_ctx = o_att_lat
    gate0 = dm.c_gate // D
    ry_f, ry_b, bonus, r_gate, rp = rwkv
    gy_f, gy_b, norm_w = gla
    dv = dm.GV // G_HEADS

    def row_spec(width, col=0):
        return pl.BlockSpec((tm, width), lambda i: (i + off, col))

    def weight(k_rows):
        return pl.BlockSpec((None, k_rows, D), lambda i: (l, 0, 0), pipeline_mode=pl.Buffered(1))

    def full(a):
        return pl.BlockSpec(a.shape, lambda i: (0,) * a.ndim)

    consts = [rp['ln_w'], rp['ln_b'], rp['e1'], rp['e2'], norm_w.reshape(1, dv), g.reshape(1, D), b.reshape(1, D)]
    return pl.pallas_call(
        functools.partial(_merge_kernel, alpha, MERGE_CHUNKS, nct - off),
        grid=(rows // tm,),
        in_specs=[
            row_spec(D), dm.mod_spec(5, tm, off),
            pl.BlockSpec((tm, A_Q_DIM), lambda i: (jnp.minimum(i + off, nct - 1), 0)),
            pl.BlockSpec((tm, A_Q_DIM), lambda i: (jnp.maximum(i + off - nct, 0), 0)),
            row_spec(dm.R), row_spec(dm.R), row_spec(dm.R), row_spec(dm.R),
            row_spec(dm.GV), row_spec(dm.GV), row_spec(dm.GV, dm.c_gla_g // dm.GV),
            row_spec(D, gate0), row_spec(D, gate0 + 1), row_spec(D, gate0 + 2),
            weight(A_Q_DIM), weight(dm.R), weight(dm.GV), weight(D),
        ] + [full(a) for a in consts],
        out_specs=pl.BlockSpec((tm, D), lambda i: (i, 0)),
        out_shape=jax.ShapeDtypeStruct((rows, D), F32),
        compiler_params=_params("parallel"),
        name="merge",
    )(x, mods, o_att_ctx, o_att_lat, ry_f, ry_b, bonus, r_gate, gy_f, gy_b, z, z, z, z, wba, wbr, wbg, w_out, *consts)


def _pad_to(a, size, axis):
    pad = [(0, 0)] * a.ndim
    pad[axis] = (0, size - a.shape[axis])
    return jnp.pad(a, pad)


def _rwkv_cols(a, R):
    c = 3 * R
    rkv = a[..., :c]
    wc0, wc1 = a[..., c:c + DECAY_LORA], a[..., c + DECAY_LORA:c + 2 * DECAY_LORA]
    c += 2 * DECAY_LORA
    ac0, ac1 = a[..., c:c + ICLR_LORA], a[..., c + ICLR_LORA:c + 2 * ICLR_LORA]
    c += 2 * ICLR_LORA
    gc = a[..., c:c + GATE_LORA]
    ax = a.ndim - 1
    return jnp.concatenate([rkv, gc] + [_pad_to(t, LORA_PAD, ax) for t in (wc0, wc1, ac0, ac1)], axis=-1)


def _in_proj_weights(dm, w_in):
    D, R, GK, GV = dm.D, dm.R, dm.GK, dm.GV
    att_cols = A_Q_DIM + 2 * A_KV_DIM
    rw_src = 3 * R + 2 * DECAY_LORA + 2 * ICLR_LORA + GATE_LORA
    gla_src = 2 * GK + 2 * GV + 2 * G_GATE_RANK
    att, rwkv, gla, gate = jnp.split(w_in, np.cumsum([att_cols, rw_src, gla_src]).tolist(), axis=-1)
    a_q, a_k, a_v = att[:, :A_Q_DIM], att[:, A_Q_DIM:A_Q_DIM + A_KV_DIM], att[:, A_Q_DIM + A_KV_DIM:]
    g_q, g_k = gla[:, :GK], gla[:, GK:2 * GK]
    g_v, g_g = gla[:, 2 * GK:2 * GK + GV], gla[:, 2 * GK + GV:2 * GK + 2 * GV]
    g_ac = _pad_to(gla[:, 2 * GK + 2 * GV:], LORA_PAD, 1)
    parts = [_rwkv_cols(rwkv, R), a_k, gate, g_v, g_g, g_q, g_k, a_q, a_v, g_ac]
    parts.append(jnp.zeros((D, dm.NP - sum(p.shape[1] for p in parts)), w_in.dtype))
    return jnp.concatenate([p.astype(BF16) for p in parts], axis=-1)


def _rope_tables(S):
    rows = S // GRID_W
    row = jnp.repeat(jnp.arange(rows, dtype=F32), GRID_W)
    col = jnp.tile(jnp.arange(GRID_W, dtype=F32), rows)
    n_freq = A_HEAD_DIM // 4
    inv_freq = ROPE_BASE ** (-jnp.arange(n_freq, dtype=F32) / n_freq)
    ang_r, ang_c = row[:, None] * inv_freq, col[:, None] * inv_freq
    cos = jnp.concatenate([jnp.cos(ang_r)] * 2 + [jnp.cos(ang_c)] * 2, axis=-1)
    sin = jnp.concatenate([-jnp.sin(ang_r), jnp.sin(ang_r), -jnp.sin(ang_c), jnp.sin(ang_c)], axis=-1)
    return cos, sin


def kernel(x, c, ctx, c_ctx, ada_w, ada_b, ln_g, ln_b, ffn_w_gu, ffn_w_down, w_in, attn_sink, rwkv_conv, rwkv_w0,
           rwkv_w2, rwkv_a0, rwkv_a2, rwkv_g2, rwkv_k_k, rwkv_k_a, rwkv_r_k, rwkv_ln_w, rwkv_ln_b, gla_wa2, gla_ba,
           gla_norm_w, w_branch_att, w_branch_rwkv, w_branch_gla, w_out):
    B, S, D = x.shape
    CT = ctx.shape[1]
    depth = ada_w.shape[0]
    dm = _Dims(B, S, CT, D)
    R = dm.R
    alpha = float((2 * depth) ** 0.25)
    assert B + 1 <= MOD_ROWS

    cc = _pad_to(jnp.concatenate([c, c_ctx[None, :]], axis=0), MOD_ROWS, 0)
    mods_all = _ada(cc, ada_w, ada_b)
    cos, sin = _rope_tables(S)
    head_of_lane = jnp.arange(R) // R_HEAD_DIM
    e1 = (head_of_lane[:, None] == jnp.arange(LORA_PAD)[None, :]).astype(BF16)
    e2 = e1.T

    w_gu, w_dn = ffn_w_gu.astype(BF16), ffn_w_down.astype(BF16)
    wba, wbr = w_branch_att.astype(BF16), w_branch_rwkv.astype(BF16)
    wbg, wo = w_branch_gla.astype(BF16), w_out.astype(BF16)

    xs = jnp.concatenate([ctx.reshape(dm.MC, D), x.reshape(dm.ML, D)], axis=0)
    for l in range(depth):
        last = l == depth - 1
        mods = mods_all[l].reshape(MOD_ROWS * N_MOD, 1, D)
        xs = _ffn(dm, alpha, xs, mods, 0, w_gu, w_dn, l, 0, ln_g[l, 0], ln_b[l, 0])

        z = _inproj(dm, xs, mods, _in_proj_weights(dm, w_in[l]))
        o_att_lat = _attn_latent(dm, z, attn_sink[l], cos, sin)
        o_att_ctx = None if last else _attn_ctx(dm, z, attn_sink[l])

        rp = {
            'conv': _rwkv_cols(rwkv_conv[l], R),
            'g2': rwkv_g2[l].astype(BF16),
            'w0': rwkv_w0[l], 'a0': rwkv_a0[l],
            'w2': _pad_to(rwkv_w2[l], LORA_PAD, 1).astype(BF16),
            'a2': _pad_to(rwkv_a2[l], LORA_PAD, 1).astype(BF16),
            'k_k': rwkv_k_k[l].reshape(1, R), 'k_a': rwkv_k_a[l].reshape(1, R), 'r_k': rwkv_r_k[l].reshape(1, R),
            'ln_w': rwkv_ln_w[l].reshape(1, R), 'ln_b': rwkv_ln_b[l].reshape(1, R),
            'e1': e1, 'e2': e2,
        }
        r, v, kk, lw, kd, b, bonus, g = _rwkv_prep(dm, z, rp)
        g_m, h_m, q_m, y0_m = _rwkv_chunk(dm, r, v, kk, lw, kd, b)
        y_rwkv_f, y_rwkv_b = _rwkv_scan(dm, g_m, h_m, q_m, y0_m)

        wa_p = jnp.stack([_pad_to(jnp.pad(gla_wa2[l, d], ((d * G_GATE_RANK, 0), (0, 0))), LORA_PAD, 0)
                          for d in range(2)])
        wa_hi = wa_p.astype(BF16)
        wa_split = jnp.stack([wa_hi, (wa_p - wa_hi.astype(F32)).astype(BF16)], axis=1)
        y_gla_f, y_gla_b = _gla(dm, z, wa_split, gla_ba[l].reshape(2, 1, dm.GK))

        xs = _merge(dm, alpha, xs, mods, z, o_att_ctx, o_att_lat, (y_rwkv_f, y_rwkv_b, bonus, g, rp),
                    (y_gla_f, y_gla_b, gla_norm_w[l]), wba, wbr, wbg, wo, l, ln_g[l, 1], ln_b[l, 1])
        xs = _ffn(dm, alpha, xs, mods, 6, w_gu, w_dn, l, 1, ln_g[l, 2], ln_b[l, 2])
    return xs.reshape(B, S, D)
```

```python
import functools
import math

import jax
import jax.numpy as jnp
import numpy as np
from jax import lax
from jax.experimental import pallas as pl
from jax.experimental.pallas import tpu as pltpu

F32 = jnp.float32
BF16 = jnp.bfloat16

N_MOD = 9
D_FF = 5632
LN_EPS = 1e-5
FFN_RES = 0.5

A_HEADS = 8
A_KV_HEADS = 2
A_GROUP = A_HEADS // A_KV_HEADS
A_HEAD_DIM = 128
A_Q_DIM = A_HEADS * A_HEAD_DIM
A_KV_DIM = A_KV_HEADS * A_HEAD_DIM
WINDOW = 128
A_BLOCK = 128
GRID_W = 64
ROPE_BASE = 10000.0
MASK_VALUE = -1e30

R_HEAD_DIM = 64
DECAY_LORA = 96
ICLR_LORA = 96
GATE_LORA = 256
R_GN_EPS = 64e-5
R_CHUNK = 64
LORA_PAD = 128
SUBLANES = 8
HALO = 16

G_HEADS = 4
G_GATE_RANK = 16
G_GATE_NORM = 16.0
G_CHUNK = 64
G_EPS = 1e-5

ADA_STREAMS = 3
ROW_TILE = 512
ROW_TILE_WIDE = 1024
SEQ_TILE = 256
IN_PROJ_COLS = 1280
FFN_HIDDEN_TILE = 512
ADA_K_TILE = 256
MERGE_CHUNKS = 4
MOD_ROWS = 16
VMEM_LIMIT = 58 * 1024 * 1024
VMEM_LIMIT_FFN = 63 * 1024 * 1024

NN = (((1,), (0,)), ((), ()))
NT = (((1,), (1,)), ((), ()))
TN = (((0,), (0,)), ((), ()))


def _dot(a, b, dims=NN):
    return lax.dot_general(a, b, dims, preferred_element_type=F32)


def _cumsum_dot(tri, x):
    hi = x.astype(BF16)
    rest = x - hi.astype(F32)
    mid = rest.astype(BF16)
    lo = (rest - mid.astype(F32)).astype(BF16)
    return _dot(tri, hi) + _dot(tri, mid) + _dot(tri, lo)


def _sigmoid(x):
    return 1.0 / (1.0 + jnp.exp(-x))


def _log_sigmoid(x):
    return jnp.minimum(x, 0.0) - jnp.log(1.0 + jnp.exp(-jnp.abs(x)))


def _ln(x):
    mu = jnp.mean(x, -1, keepdims=True)
    xc = x - mu
    var = jnp.mean(xc * xc, -1, keepdims=True)
    return xc * lax.rsqrt(var + LN_EPS)


def _params(*sem, vmem=VMEM_LIMIT):
    return pltpu.CompilerParams(dimension_semantics=sem, vmem_limit_bytes=vmem)


class _Dims:
    def __init__(self, B, S, CT, D):
        self.B, self.S, self.CT, self.D = B, S, CT, D
        self.MC, self.ML = B * CT, B * S
        self.M = self.MC + self.ML
        self.R = D // 2
        self.GK = D // 2
        self.GV = D
        self.rw_cols = 3 * self.R + GATE_LORA + 4 * LORA_PAD
        c = 0
        self.c_rwkv = c; c += self.rw_cols
        self.c_att_k = c; c += A_KV_DIM
        self.c_gate = c; c += 3 * D
        self.c_gla_v = c; c += self.GV
        self.c_gla_g = c; c += self.GV
        self.c_gla_q = c; c += self.GK
        self.c_gla_k = c; c += self.GK
        self.c_att_q = c; c += A_Q_DIM
        self.c_att_v = c; c += A_KV_DIM
        self.c_gla_ac = c; c += LORA_PAD
        self.tn_in = IN_PROJ_COLS
        self.NP = -(-c // self.tn_in) * self.tn_in
        self.tm = math.gcd(ROW_TILE, math.gcd(self.MC, S))
        self.tm_in = math.gcd(ROW_TILE_WIDE, math.gcd(self.MC, S))
        self.tmr = math.gcd(SEQ_TILE, math.gcd(CT, S))

    def mod_row(self, i, tm):
        nct = self.MC // tm
        tpb = self.S // tm
        return jnp.where(i < nct, self.B, (i - nct) // tpb)

    def mod_spec(self, m, tm, off=0):
        return pl.BlockSpec((None, 1, self.D), lambda i, *_: (self.mod_row(i + off, tm) * N_MOD + m, 0, 0))

    def chunk_block(self, b, d, s, chunk):
        ncc, ncl = self.CT // chunk, self.S // chunk
        j = s - ncc
        ctx_blk = b * ncc + jnp.where(d == 0, s, ncc - 1 - s)
        lat_blk = self.B * ncc + b * ncl + jnp.where(d == 0, j, ncl - 1 - j)
        return jnp.where(s < ncc, ctx_blk, lat_blk)


def _ada_kernel(c_ref, *refs):
    w_refs, b_ref, o_ref = refs[:-2], refs[-2], refs[-1]

    @pl.when(pl.program_id(1) == 0)
    def _():
        o_ref[...] = jnp.broadcast_to(b_ref[...], o_ref.shape)

    cc = c_ref[...]
    s = (cc * _sigmoid(cc)).astype(BF16)
    wide = o_ref.shape[1] // len(w_refs)
    for j, w_ref in enumerate(w_refs):
        o_ref[:, j * wide:(j + 1) * wide] += _dot(s, w_ref[...].astype(BF16))


def _ada(cc, ada_w, ada_b):
    L, D, N = ada_w.shape
    tk = ADA_K_TILE
    wide = N // ADA_STREAMS
    cc_k = cc.reshape(MOD_ROWS, D // tk, tk).transpose(1, 0, 2)
    return pl.pallas_call(
        _ada_kernel,
        grid=(L, D // tk),
        in_specs=[
            pl.BlockSpec((None, MOD_ROWS, tk), lambda l, k: (k, 0, 0)),
        ] + [pl.BlockSpec((None, tk, wide), lambda l, k, j=j: (l, k, j)) for j in range(ADA_STREAMS)] + [
            pl.BlockSpec((None, 1, N), lambda l, k: (l, 0, 0)),
        ],
        out_specs=pl.BlockSpec((None, MOD_ROWS, N), lambda l, k: (l, 0, 0)),
        out_shape=jax.ShapeDtypeStruct((L, MOD_ROWS, N), F32),
        compiler_params=_params("parallel", "arbitrary"),
        name="ada",
    )(cc_k, *([ada_w] * ADA_STREAMS), ada_b.reshape(L, 1, N))


def _ffn_kernel(alpha, x_ref, sh_ref, sc_ref, gt_ref, wg_ref, wu_ref, wd_ref, g_ref, b_ref, o_ref, h_ref):
    f = pl.program_id(1)

    @pl.when(f == 0)
    def _():
        h = _ln(x_ref[...]) * (1.0 + sc_ref[...]) + sh_ref[...]
        h_ref[...] = h.astype(BF16)
        o_ref[...] = jnp.zeros_like(o_ref)

    h = h_ref[...]
    g = _dot(h, wg_ref[...])
    u = _dot(h, wu_ref[...])
    a = (g * _sigmoid(g) * u).astype(BF16)
    o_ref[...] += _dot(a, wd_ref[...])

    @pl.when(f == pl.num_programs(1) - 1)
    def _():
        z = alpha * x_ref[...] + (FFN_RES * gt_ref[...]) * o_ref[...]
        o_ref[...] = _ln(z) * g_ref[...] + b_ref[...]


def _ffn(dm, alpha, x, mods, m0, w_gu, w_down, l, j, g, b):
    D, tm, tf = dm.D, dm.tm_in, FFN_HIDDEN_TILE
    nf = D_FF // tf
    rows = x.shape[0]
    off = (dm.M - rows) // tm
    return pl.pallas_call(
        functools.partial(_ffn_kernel, alpha),
        grid=(rows // tm, nf),
        in_specs=[
            pl.BlockSpec((tm, D), lambda i, f: (i, 0)),
            dm.mod_spec(m0, tm, off), dm.mod_spec(m0 + 1, tm, off), dm.mod_spec(m0 + 2, tm, off),
            pl.BlockSpec((None, None, D, tf), lambda i, f: (l, j, 0, f)),
            pl.BlockSpec((None, None, D, tf), lambda i, f: (l, j, 0, f + nf)),
            pl.BlockSpec((None, None, tf, D), lambda i, f: (l, j, f, 0)),
            pl.BlockSpec((1, D), lambda i, f: (0, 0)),
            pl.BlockSpec((1, D), lambda i, f: (0, 0)),
        ],
        out_specs=pl.BlockSpec((tm, D), lambda i, f: (i, 0)),
        out_shape=jax.ShapeDtypeStruct((rows, D), F32),
        scratch_shapes=[pltpu.VMEM((tm, D), BF16)],
        compiler_params=_params("parallel", "arbitrary", vmem=VMEM_LIMIT_FFN),
        name="ffn",
    )(x, mods, mods, mods, w_gu, w_gu, w_down, g.reshape(1, D), b.reshape(1, D))


def _inproj_kernel(x_ref, sh_ref, sc_ref, w_ref, o_ref, h_ref):
    @pl.when(pl.program_id(1) == 0)
    def _():
        h = _ln(x_ref[...]) * (1.0 + sc_ref[...]) + sh_ref[...]
        h_ref[...] = h.astype(BF16)

    o_ref[...] = _dot(h_ref[...], w_ref[...]).astype(BF16)


def _inproj(dm, x, mods, w_in_p):
    D, tm, tn = dm.D, dm.tm_in, dm.tn_in
    return pl.pallas_call(
        _inproj_kernel,
        grid=(dm.M // tm, dm.NP // tn),
        in_specs=[
            pl.BlockSpec((tm, D), lambda i, n: (i, 0)),
            dm.mod_spec(3, tm), dm.mod_spec(4, tm),
            pl.BlockSpec((D, tn), lambda i, n: (0, n)),
        ],
        out_specs=pl.BlockSpec((tm, tn), lambda i, n: (i, n)),
        out_shape=jax.ShapeDtypeStruct((dm.M, dm.NP), BF16),
        scratch_shapes=[pltpu.VMEM((tm, D), BF16)],
        compiler_params=_params("parallel", "arbitrary"),
        name="inproj",
    )(x, mods, mods, w_in_p)


def _rope(x, cos, sin_signed):
    lane = lax.broadcasted_iota(jnp.int32, x.shape, 1)
    quarter = A_HEAD_DIM // 4
    swapped = jnp.where((lane % (2 * quarter)) < quarter,
                        pltpu.roll(x, A_HEAD_DIM - quarter, 1), pltpu.roll(x, quarter, 1))
    return x * cos + swapped * sin_signed


def _attn_kernel(S, nb, sink_ref, q_ref, kp_ref, kc_ref, kn_ref, vp_ref, vc_ref, vn_ref, kx_ref, vx_ref,
                 cos_ref, sin_ref, o_ref):
    n = pl.program_id(1)
    scale = A_HEAD_DIM ** -0.5

    def table(ref, blk):
        return ref[pl.ds(pl.multiple_of(blk * A_BLOCK, A_BLOCK), A_BLOCK), :]

    blk_p, blk_n = jnp.maximum(n - 1, 0), jnp.minimum(n + 1, nb - 1)
    cos_c, sin_c = table(cos_ref, n), table(sin_ref, n)
    cos_p, sin_p = table(cos_ref, blk_p), table(sin_ref, blk_p)
    cos_n, sin_n = table(cos_ref, blk_n), table(sin_ref, blk_n)

    rows, band = A_GROUP * A_BLOCK, 3 * A_BLOCK
    qpos = n * A_BLOCK + lax.broadcasted_iota(jnp.int32, (rows, band), 0) % A_BLOCK
    kpos = (n - 1) * A_BLOCK + lax.broadcasted_iota(jnp.int32, (rows, band), 1)
    valid = (jnp.abs(qpos - kpos) <= WINDOW) & (kpos >= 0) & (kpos < S)

    work = []
    for kvh in range(A_KV_HEADS):
        ks = slice(kvh * A_HEAD_DIM, (kvh + 1) * A_HEAD_DIM)
        kb = jnp.concatenate([_rope(kp_ref[:, ks].astype(F32), cos_p, sin_p),
                              _rope(kc_ref[:, ks].astype(F32), cos_c, sin_c),
                              _rope(kn_ref[:, ks].astype(F32), cos_n, sin_n)], axis=0).astype(BF16)
        heads = [kvh * A_GROUP + g for g in range(A_GROUP)]
        qs = [q_ref[:, h * A_HEAD_DIM:(h + 1) * A_HEAD_DIM] for h in heads]
        q_rope = jnp.concatenate([_rope(q.astype(F32), cos_c, sin_c) * scale for q in qs], axis=0).astype(BF16)
        q_plain = jnp.concatenate([q.astype(F32) * scale for q in qs], axis=0).astype(BF16)
        sink = jnp.concatenate([jnp.full((A_BLOCK, 1), sink_ref[h], F32) for h in heads], axis=0)
        s_loc = jnp.where(valid, _dot(q_rope, kb, NT), MASK_VALUE)
        s_ctx = _dot(q_plain, kx_ref[:, ks], NT)
        work.append(dict(ks=ks, heads=heads, sink=sink, s_loc=s_loc, s_ctx=s_ctx))
    for w in work:
        w['m'] = jnp.maximum(jnp.maximum(jnp.max(w['s_loc'], -1, keepdims=True),
                                         jnp.max(w['s_ctx'], -1, keepdims=True)), w['sink'])
    for w in work:
        w['p_loc'] = jnp.exp(w.pop('s_loc') - w['m'])
        w['p_ctx'] = jnp.exp(w.pop('s_ctx') - w['m'])
    for w in work:
        w['den'] = (jnp.sum(w['p_loc'], -1, keepdims=True) + jnp.sum(w['p_ctx'], -1, keepdims=True)
                    + jnp.exp(w['sink'] - w['m']))
    for w in work:
        ks = w['ks']
        vb = jnp.concatenate([vp_ref[:, ks], vc_ref[:, ks], vn_ref[:, ks]], axis=0)
        o = (_dot(w['p_loc'].astype(BF16), vb) + _dot(w['p_ctx'].astype(BF16), vx_ref[:, ks])) / w['den']
        for g, h in enumerate(w['heads']):
            o_ref[:, h * A_HEAD_DIM:(h + 1) * A_HEAD_DIM] = o[g * A_BLOCK:(g + 1) * A_BLOCK].astype(BF16)


def _attn_latent(dm, z, sink, cos, sin):
    B, S = dm.B, dm.S
    nb = S // A_BLOCK
    base = dm.MC // A_BLOCK
    qc = dm.c_att_q // A_Q_DIM
    kc = dm.c_att_k // A_KV_DIM
    vc = dm.c_att_v // A_KV_DIM

    def rows(shift):
        return lambda b, n: base + b * nb + jnp.clip(n + shift, 0, nb - 1)

    def kv_spec(col, shift):
        r = rows(shift)
        return pl.BlockSpec((A_BLOCK, A_KV_DIM), lambda b, n: (r(b, n), col))

    return pl.pallas_call(
        functools.partial(_attn_kernel, S, nb),
        grid=(B, nb),
        in_specs=[
            pl.BlockSpec(memory_space=pltpu.SMEM),
            pl.BlockSpec((A_BLOCK, A_Q_DIM), lambda b, n: (base + b * nb + n, qc)),
            kv_spec(kc, -1), kv_spec(kc, 0), kv_spec(kc, 1),
            kv_spec(vc, -1), kv_spec(vc, 0), kv_spec(vc, 1),
            pl.BlockSpec((dm.CT, A_KV_DIM), lambda b, n: (b, kc)),
            pl.BlockSpec((dm.CT, A_KV_DIM), lambda b, n: (b, vc)),
            pl.BlockSpec((S, A_HEAD_DIM), lambda b, n: (0, 0)),
            pl.BlockSpec((S, A_HEAD_DIM), lambda b, n: (0, 0)),
        ],
        out_specs=pl.BlockSpec((A_BLOCK, A_Q_DIM), lambda b, n: (b * nb + n, 0)),
        out_shape=jax.ShapeDtypeStruct((dm.ML, A_Q_DIM), BF16),
        compiler_params=_params("parallel", "arbitrary"),
        name="attn_latent",
    )(sink, z, z, z, z, z, z, z, z, z, cos, sin)


def _attn_ctx_kernel(CT, sink_ref, q_ref, k_ref, v_ref, o_ref):
    scale = A_HEAD_DIM ** -0.5
    for kvh in range(A_KV_HEADS):
        ks = slice(kvh * A_HEAD_DIM, (kvh + 1) * A_HEAD_DIM)
        k = k_ref[:, ks]
        v = v_ref[:, ks]
        heads = [kvh * A_GROUP + g for g in range(A_GROUP)]
        q = jnp.concatenate([q_ref[:, h * A_HEAD_DIM:(h + 1) * A_HEAD_DIM] for h in heads], axis=0)
        sink = jnp.concatenate([jnp.full((CT, 1), sink_ref[h], F32) for h in heads], axis=0)
        s = _dot(q, k, NT) * scale
        m = jnp.maximum(jnp.max(s, -1, keepdims=True), sink)
        p = jnp.exp(s - m)
        den = jnp.sum(p, -1, keepdims=True) + jnp.exp(sink - m)
        o = _dot(p.astype(BF16), v) / den
        for g, h in enumerate(heads):
            o_ref[:, h * A_HEAD_DIM:(h + 1) * A_HEAD_DIM] = o[g * CT:(g + 1) * CT].astype(BF16)


def _attn_ctx(dm, z, sink):
    CT = dm.CT
    return pl.pallas_call(
        functools.partial(_attn_ctx_kernel, CT),
        grid=(dm.B,),
        in_specs=[
            pl.BlockSpec(memory_space=pltpu.SMEM),
            pl.BlockSpec((CT, A_Q_DIM), lambda b: (b, dm.c_att_q // A_Q_DIM)),
            pl.BlockSpec((CT, A_KV_DIM), lambda b: (b, dm.c_att_k // A_KV_DIM)),
            pl.BlockSpec((CT, A_KV_DIM), lambda b: (b, dm.c_att_v // A_KV_DIM)),
        ],
        out_specs=pl.BlockSpec((CT, A_Q_DIM), lambda b: (b, 0)),
        out_shape=jax.ShapeDtypeStruct((dm.MC, A_Q_DIM), BF16),
        compiler_params=_params("parallel"),
        name="attn_ctx",
    )(sink, z, z, z)


def _head_sum(x, e1_ref, e2_ref):
    def split_dot(a, ind):
        hi = a.astype(BF16)
        lo = (a - hi.astype(F32)).astype(BF16)
        return _dot(hi, ind) + _dot(lo, ind)

    return split_dot(split_dot(x, e1_ref[...]), e2_ref[...])


def _rwkv_prep_kernel(dm, z_ref, zp_ref, zn_ref, cw_ref, g2_ref, w0_ref, w2_ref, a0_ref, a2_ref, kk_ref, ka_ref,
                      rk_ref, e1_ref, e2_ref,
                      r_out, v_out, kk_out, lw_out, kd_out, b_out, bonus_out, g_out):
    i = pl.program_id(0)
    tm, R = dm.tmr, dm.R
    nct = dm.MC // tm
    start = jnp.where(i < nct, (i * tm) % dm.CT, ((i - nct) * tm) % dm.S)
    seqlen = jnp.where(i < nct, dm.CT, dm.S)
    has_prev = (start != 0).astype(F32)
    has_next = (start + tm != seqlen).astype(F32)

    z = z_ref[...].astype(F32)
    sub = lax.broadcasted_iota(jnp.int32, (SUBLANES, 1), 0)
    halo_prev = zp_ref[HALO - 1:HALO, :].astype(F32) * has_prev
    halo_next = zn_ref[0:1, :].astype(F32) * has_next
    down, up = pltpu.roll(z, 1, 0), pltpu.roll(z, tm - 1, 0)
    z_prev = jnp.concatenate([jnp.where(sub == 0, halo_prev, down[:SUBLANES]), down[SUBLANES:]], axis=0)
    z_next = jnp.concatenate([up[:tm - SUBLANES], jnp.where(sub == SUBLANES - 1, halo_next, up[tm - SUBLANES:])],
                             axis=0)
    zc = z_prev * cw_ref[0:1, :] + z * cw_ref[1:2, :] + z_next * cw_ref[2:3, :]

    r = zc[:, 0:R]
    k = zc[:, R:2 * R]
    v = zc[:, 2 * R:3 * R]
    c0 = 3 * R
    gc = zc[:, c0:c0 + GATE_LORA]
    c0 += GATE_LORA
    g_out[...] = _dot(_sigmoid(gc).astype(BF16), g2_ref[...])

    kkr = k * kk_ref[...]
    norm = jnp.sqrt(_head_sum(kkr * kkr, e1_ref, e2_ref))
    kk = kkr / jnp.maximum(norm, 1e-12)
    r_out[...] = r
    v_out[...] = v
    kk_out[...] = kk

    bonus = jnp.zeros_like(v)
    for d in range(2):
        wc = zc[:, c0 + d * LORA_PAD:c0 + (d + 1) * LORA_PAD]
        ac = zc[:, c0 + (2 + d) * LORA_PAD:c0 + (3 + d) * LORA_PAD]
        wl = w0_ref[d:d + 1, :] + _dot(jnp.tanh(wc).astype(BF16), w2_ref[d])
        lw_out[d] = _sigmoid(wl) * (-math.exp(-0.5))
        a = _sigmoid(a0_ref[d:d + 1, :] + _dot(ac.astype(BF16), a2_ref[d]))
        kd = k * (1.0 + (a - 1.0) * ka_ref[...])
        kd_out[d] = kd
        b_out[d] = kk * a
        bonus = bonus + _head_sum(r * kd * rk_ref[...], e1_ref, e2_ref) * v
    bonus_out[...] = bonus


def _rwkv_prep(dm, z, p):
    tm, R, M, W = dm.tmr, dm.R, dm.M, dm.rw_cols
    n_halo = M // HALO
    col = dm.c_rwkv // W
    row = pl.BlockSpec((tm, R), lambda i: (i, 0))
    row2 = pl.BlockSpec((2, tm, R), lambda i: (0, i, 0))

    def full(a):
        return pl.BlockSpec(a.shape, lambda i: (0,) * a.ndim)

    consts = [p['conv'], p['g2'], p['w0'], p['w2'], p['a0'], p['a2'], p['k_k'], p['k_a'], p['r_k'], p['e1'], p['e2']]
    one = jax.ShapeDtypeStruct((M, R), F32)
    two = jax.ShapeDtypeStruct((2, M, R), F32)
    return pl.pallas_call(
        functools.partial(_rwkv_prep_kernel, dm),
        grid=(M // tm,),
        in_specs=[
            pl.BlockSpec((tm, W), lambda i: (i, col)),
            pl.BlockSpec((HALO, W), lambda i: (jnp.maximum(i * (tm // HALO) - 1, 0), col)),
            pl.BlockSpec((HALO, W), lambda i: (jnp.minimum((i + 1) * (tm // HALO), n_halo - 1), col)),
        ] + [full(a) for a in consts],
        out_specs=[row, row, row, row2, row2, row2, row, row],
        out_shape=[one, one, one, two, two, two, one, one],
        compiler_params=_params("parallel"),
        name="rwkv_prep",
    )(z, z, z, *consts)


R_SLAB = 2 * R_HEAD_DIM
CHUNK_SLABS = 4
SLAB_HEADS = R_SLAB // R_HEAD_DIM


def _block_diag(x, mask):
    return jnp.where(mask, jnp.concatenate([x] * SLAB_HEADS, axis=0), 0.0).astype(BF16)


def _diag_blocks(full, mask):
    fm = jnp.where(mask, full, 0.0)
    n = R_HEAD_DIM
    return functools.reduce(lambda acc, j: acc + fm[j * n:(j + 1) * n], range(1, SLAB_HEADS), fm[0:n])


def _slab_masks():
    n, g = R_HEAD_DIM, R_SLAB
    blk = (lax.broadcasted_iota(jnp.int32, (g, g), 0) // n) == (lax.broadcasted_iota(jnp.int32, (g, g), 1) // n)
    t_idx = lax.broadcasted_iota(jnp.int32, (n, g), 0)
    s_idx = lax.broadcasted_iota(jnp.int32, (n, g), 1) % n
    return blk, t_idx, s_idx


def _rwkv_chunk_kernel(n_sub, n_slab, r_ref, v_ref, kk_ref, lw_ref, kd_ref, b_ref, g_out, h_out, q_out, y0_out):
    C = R_CHUNK
    blk, t_idx, s_idx = _slab_masks()
    eye4 = (t_idx == s_idx).astype(F32)
    ti = lax.broadcasted_iota(jnp.int32, (C, C), 0)
    si = lax.broadcasted_iota(jnp.int32, (C, C), 1)

    def bd(x):
        return _block_diag(x, blk)

    def setup(c, d, sl):
        rs = pl.ds(c * C, C)
        ls = pl.ds(sl * R_SLAB, R_SLAB)
        r, v, kk = r_ref[rs, ls], v_ref[rs, ls], kk_ref[rs, ls]
        tri = ((ti >= si) if d == 0 else (ti <= si)).astype(BF16)
        lw = lw_ref[d, rs, ls]
        cl = _cumsum_dot(tri, lw)
        tot = jnp.sum(lw, axis=0, keepdims=True)
        e_neg = jnp.exp(-cl)
        e_end = jnp.exp(tot - cl)
        kd, b = kd_ref[d, rs, ls], b_ref[d, rs, ls]
        ch = dict(rs=rs, ls=ls, d=d, tot=tot, v_b=v.astype(BF16), v_bd=bd(v),
                  incl=(t_idx >= s_idx) if d == 0 else (t_idx <= s_idx),
                  strict=(t_idx > s_idx) if d == 0 else (t_idx < s_idx),
                  kkm=kk * jnp.exp(cl - lw), rp=r * jnp.exp(cl),
                  bp=b * e_neg, kp=kd * e_neg, be=(b * e_end).astype(BF16), ke=(kd * e_end).astype(BF16))
        ch['lhs'] = jnp.concatenate([ch['kkm'], ch['rp']], axis=0).astype(BF16)
        return ch

    chains = [setup(c, d, sl) for c in range(n_sub) for d in range(2) for sl in range(n_slab)]
    for ch in chains:
        gram_b = _dot(ch['lhs'], bd(ch.pop('bp')), NT)
        gram_k = _dot(ch.pop('lhs'), bd(ch.pop('kp')), NT)
        lb = jnp.where(ch['strict'], gram_b[:C], 0.0)
        ch['mb'] = jnp.where(ch['incl'], gram_b[C:], 0.0).astype(BF16)
        ch['lmk'] = jnp.concatenate([jnp.where(ch['strict'], gram_k[:C], 0.0),
                                     jnp.where(ch['incl'], gram_k[C:], 0.0)], axis=0).astype(BF16)
        ch['pw'] = -lb
        ch['t_inv'] = eye4 - lb
    for ch in chains:
        ch['pw'] = _dot(ch['pw'].astype(BF16), bd(ch['pw']))
        lmkv = _dot(ch.pop('lmk'), ch.pop('v_bd'))
        ch['lkv'], ch['mkv'] = lmkv[:C], lmkv[C:]
    for _ in range(int(math.log2(C)) - 2):
        for ch in chains:
            both = _dot(jnp.concatenate([ch['pw'], ch['t_inv']], axis=0).astype(BF16), bd(ch['pw']))
            ch['pw'], ch['t_inv'] = both[:C], ch['t_inv'] + both[C:]
    for ch in chains:
        ch['t_inv'] = (ch['t_inv'] + _dot(ch['t_inv'].astype(BF16), bd(ch.pop('pw')))).astype(BF16)
    for ch in chains:
        ch['a_m'] = _dot(ch['t_inv'], bd(ch.pop('kkm')))
        ch['d_m'] = -_dot(ch.pop('t_inv'), bd(ch.pop('lkv')))
    for ch in chains:
        d, rs, ls = ch['d'], ch['rs'], ch['ls']
        a_m, d_m = ch['a_m'], ch['d_m']
        q_out[d, rs, ls] = (ch['rp'] - _dot(ch['mb'], bd(a_m))).astype(BF16)
        y0_out[d, rs, ls] = (_dot(ch['mb'], bd(d_m)) + ch['mkv']).astype(BF16)
        g_cross = _dot(ch['be'], a_m.astype(BF16), TN)
        h_cross = _dot(jnp.concatenate([ch['be'], ch['ke']], axis=0),
                       jnp.concatenate([d_m.astype(BF16), ch['v_b']], axis=0), TN)
        g_out[d, rs, ls] = (eye4 * jnp.exp(ch['tot']) - _diag_blocks(g_cross, blk)).astype(BF16)
        h_out[d, rs, ls] = _diag_blocks(h_cross, blk)


def _rwkv_chunk(dm, r, v, kk, lw, kd, b):
    M, R = dm.M, dm.R
    rows = dm.tmr
    lanes = CHUNK_SLABS * R_SLAB
    one = pl.BlockSpec((rows, lanes), lambda i, h: (i, h))
    two = pl.BlockSpec((2, rows, lanes), lambda i, h: (0, i, h))
    f32 = jax.ShapeDtypeStruct((2, M, R), F32)
    b16 = jax.ShapeDtypeStruct((2, M, R), BF16)
    return pl.pallas_call(
        functools.partial(_rwkv_chunk_kernel, rows // R_CHUNK, CHUNK_SLABS),
        grid=(M // rows, R // lanes),
        in_specs=[one, one, one, two, two, two],
        out_specs=[two, two, two, two],
        out_shape=[b16, f32, b16, b16],
        compiler_params=_params("parallel", "parallel"),
        name="rwkv_chunk",
    )(r, v, kk, lw, kd, b)


def _rwkv_scan_kernel(n_slab, n_in, gf_ref, hf_ref, qf_ref, y0f_ref, gb_ref, hb_ref, qb_ref, y0b_ref,
                      yf_ref, yb_ref, st_ref):
    C = R_CHUNK
    dirs = ((gf_ref, hf_ref, qf_ref, y0f_ref, yf_ref), (gb_ref, hb_ref, qb_ref, y0b_ref, yb_ref))

    @pl.when(pl.program_id(1) == 0)
    def _():
        st_ref[...] = jnp.zeros_like(st_ref)

    blk, _, _ = _slab_masks()
    lanes = [slice(sl * R_SLAB, (sl + 1) * R_SLAB) for sl in range(n_slab)]
    for t in range(n_in):
        todo = [(d, refs, pl.ds((t, n_in - 1 - t)[d] * C, C), ls) for d, refs in enumerate(dirs) for ls in lanes]
        st_bd = [_block_diag(st_ref[d, :, ls], blk) for d, _, _, ls in todo]
        both = [_dot(jnp.concatenate([refs[2][rs, ls], refs[0][rs, ls]], axis=0), bd)
                for (d, refs, rs, ls), bd in zip(todo, st_bd)]
        for (d, refs, rs, ls), res in zip(todo, both):
            refs[4][rs, ls] = (res[:C] + refs[3][rs, ls].astype(F32)).astype(BF16)
            st_ref[d, :, ls] = res[C:] + refs[1][rs, ls]


def _rwkv_scan(dm, g_m, h_m, q_m, y0_m):
    R, rows = dm.R, dm.tmr
    steps = (dm.CT + dm.S) // rows

    def spec(d):
        return pl.BlockSpec((None, rows, R), lambda b, s: (d, dm.chunk_block(b, d, s, rows), 0))

    def yspec(d):
        return pl.BlockSpec((rows, R), lambda b, s: (dm.chunk_block(b, d, s, rows), 0))

    y = jax.ShapeDtypeStruct((dm.M, R), BF16)
    return pl.pallas_call(
        functools.partial(_rwkv_scan_kernel, R // R_SLAB, rows // R_CHUNK),
        grid=(dm.B, steps),
        in_specs=[spec(0)] * 4 + [spec(1)] * 4,
        out_specs=[yspec(0), yspec(1)],
        out_shape=[y, y],
        scratch_shapes=[pltpu.VMEM((2, R_HEAD_DIM, R), F32)],
        compiler_params=_params("parallel", "arbitrary"),
        name="rwkv_scan",
    )(g_m, h_m, q_m, y0_m, g_m, h_m, q_m, y0_m)


def _rwkv_branch_out(y_f, y_b, bonus, g, lnw, lnb, e1_ref, e2_ref):
    o = y_f.astype(F32) + y_b.astype(F32)
    inv_n = 1.0 / R_HEAD_DIM
    mu = _head_sum(o, e1_ref, e2_ref) * inv_n
    oc = o - mu
    var = _head_sum(oc * oc, e1_ref, e2_ref) * inv_n
    o = oc * lax.rsqrt(var + R_GN_EPS) * lnw + lnb
    return ((o + bonus) * g).astype(BF16)


def _gla_kernel(dm, n_in, qf_ref, kf_ref, vf_ref, af_ref, qb_ref, kb_ref, vb_ref, ab_ref, wa_ref, ba_ref,
                yf_ref, yb_ref, st_ref, qd_ref, ki_ref, ke_ref, y_ref):
    C = G_CHUNK
    dk, dv = dm.GK // G_HEADS, dm.GV // G_HEADS
    key = [slice(h * dk, (h + 1) * dk) for h in range(G_HEADS)]
    val = [slice(h * dv, (h + 1) * dv) for h in range(G_HEADS)]
    row = [slice(c * C, (c + 1) * C) for c in range(n_in)]
    dirs = ((qf_ref, kf_ref, vf_ref, af_ref, yf_ref), (qb_ref, kb_ref, vb_ref, ab_ref, yb_ref))

    @pl.when(pl.program_id(1) == 0)
    def _():
        st_ref[...] = jnp.zeros_like(st_ref)

    ti = lax.broadcasted_iota(jnp.int32, (C, C), 0)
    si = lax.broadcasted_iota(jnp.int32, (C, C), 1)
    causal = (ti >= si, ti <= si)

    la = [_log_sigmoid(_dot(refs[3][...], wa_ref[d, 0]) + _dot(refs[3][...], wa_ref[d, 1]) + ba_ref[d])
          * (1.0 / G_GATE_NORM) for d, refs in enumerate(dirs)]
    cb = [[_cumsum_dot(causal[d].astype(BF16), la[d][r]) for r in row] for d in range(2)]
    e_last = []
    for d, refs in enumerate(dirs):
        cum = jnp.concatenate(cb[d], axis=0)
        tots = [jnp.sum(la[d][r], axis=0, keepdims=True) for r in row]
        tot = jnp.concatenate([jnp.broadcast_to(t, (C, t.shape[1])) for t in tots], axis=0)
        k = refs[1][...].astype(F32)
        qd_ref[d] = (refs[0][...].astype(F32) * (dk ** -0.5) * jnp.exp(cum)).astype(BF16)
        ki_ref[d] = (k * jnp.exp(-cum)).astype(BF16)
        ke_ref[d] = (k * jnp.exp(tot - cum)).astype(BF16)
        e_last.append([jnp.exp(t) for t in tots])
    for d, refs in enumerate(dirs):
        att = [[jnp.where(causal[d], _dot(qd_ref[d, r, ks], ki_ref[d, r, ks], NT), 0.0).astype(BF16) for ks in key]
               for r in row]
        for c, r in enumerate(row):
            for h in range(G_HEADS):
                y_ref[d, r, val[h]] = _dot(att[c][h], refs[2][r, val[h]])
    for t in range(n_in):
        todo = [(d, refs, (t, n_in - 1 - t)[d], h) for d, refs in enumerate(dirs) for h in range(G_HEADS)]
        inter = [_dot(qd_ref[d, row[c], key[h]], st_ref[d, h].astype(BF16), NT) for d, _, c, h in todo]
        upd = [_dot(refs[2][row[c], val[h]], ke_ref[d, row[c], key[h]], TN) for d, refs, c, h in todo]
        for (d, refs, c, h), o in zip(todo, inter):
            y_ref[d, row[c], val[h]] += o
        for (d, refs, c, h), u in zip(todo, upd):
            st_ref[d, h] = st_ref[d, h] * e_last[d][c][:, key[h]] + u
    yf_ref[...] = y_ref[0].astype(BF16)
    yb_ref[...] = y_ref[1].astype(BF16)


def _gla(dm, z, wa_split, ba):
    rows = dm.tmr
    steps = (dm.CT + dm.S) // rows
    GK, GV = dm.GK, dm.GV

    def zspec(width, col, d):
        return pl.BlockSpec((rows, width), lambda b, s: (dm.chunk_block(b, d, s, rows), col // width))

    def zspecs(d):
        return [zspec(GK, dm.c_gla_q, d), zspec(GK, dm.c_gla_k, d), zspec(GV, dm.c_gla_v, d),
                zspec(LORA_PAD, dm.c_gla_ac, d)]

    def yspec(d):
        return pl.BlockSpec((rows, GV), lambda b, s: (dm.chunk_block(b, d, s, rows), 0))

    y = jax.ShapeDtypeStruct((dm.M, GV), BF16)
    return pl.pallas_call(
        functools.partial(_gla_kernel, dm, rows // G_CHUNK),
        grid=(dm.B, steps),
        in_specs=zspecs(0) + zspecs(1) + [
            pl.BlockSpec((2, 2, LORA_PAD, GK), lambda b, s: (0, 0, 0, 0)),
            pl.BlockSpec((2, 1, GK), lambda b, s: (0, 0, 0)),
        ],
        out_specs=[yspec(0), yspec(1)],
        out_shape=[y, y],
        scratch_shapes=[pltpu.VMEM((2, G_HEADS, GV // G_HEADS, GK // G_HEADS), F32),
                        pltpu.VMEM((2, rows, GK), BF16), pltpu.VMEM((2, rows, GK), BF16),
                        pltpu.VMEM((2, rows, GK), BF16), pltpu.VMEM((2, rows, GV), F32)],
        compiler_params=_params("parallel", "arbitrary"),
        name="gla",
    )(z, z, z, z, z, z, z, z, wa_split, ba)


def _gla_branch_out(y_f, y_b, g, norm_w):
    o = y_f.astype(F32) + y_b.astype(F32)
    g = g.astype(F32)
    gate = g * _sigmoid(g)
    dv = norm_w.shape[1]
    heads = []
    for h in range(G_HEADS):
        vs = slice(h * dv, (h + 1) * dv)
        oh = o[:, vs]
        oh = oh * lax.rsqrt(jnp.mean(oh * oh, -1, keepdims=True) + G_EPS) * norm_w
        heads.append((oh * gate[:, vs]).astype(BF16))
    return jnp.concatenate(heads, axis=1)


def _merge_kernel(alpha, n_chunks, ctx_tiles, x_ref, mg_ref, oac_ref, oal_ref, ryf_ref, ryb_ref, bonus_ref, rg_ref, gyf_ref, gyb_ref,
                  gg_in_ref, ga_ref, gr_ref, gg_ref, wa_ref, wr_ref, wg_ref, wo_ref, lnw_ref, lnb_ref, e1_ref, e2_ref,
                  nw_ref, g_ref, b_ref, o_ref):
    tn = wo_ref.shape[0] // n_chunks
    oa = jnp.where(pl.program_id(0) < ctx_tiles, oac_ref[...], oal_ref[...])
    orw = _rwkv_branch_out(ryf_ref[...], ryb_ref[...], bonus_ref[...], rg_ref[...], lnw_ref[...], lnb_ref[...],
                           e1_ref, e2_ref)
    og = _gla_branch_out(gyf_ref[...], gyb_ref[...], gg_in_ref[...], nw_ref[...])
    acc = None
    for n in range(n_chunks):
        cs = slice(n * tn, (n + 1) * tn)
        merged = (_sigmoid(ga_ref[:, cs].astype(F32)) * _dot(oa, wa_ref[:, cs])
                  + _sigmoid(gr_ref[:, cs].astype(F32)) * _dot(orw, wr_ref[:, cs])
                  + _sigmoid(gg_ref[:, cs].astype(F32)) * _dot(og, wg_ref[:, cs]))
        part = _dot(merged.astype(BF16), wo_ref[cs, :])
        acc = part if acc is None else acc + part
    z = alpha * x_ref[...] + mg_ref[...] * acc
    o_ref[...] = _ln(z) * g_ref[...] + b_ref[...]


def _merge(dm, alpha, x, mods, z, o_att_ctx, o_att_lat, rwkv, gla, wba, wbr, wbg, w_out, l, g, b):
    D, tm = dm.D, dm.tmr
    nct = dm.MC // tm
    off = nct if o_att_ctx is None else 0
    rows = dm.M - off * tm
    if o_att_ctx is None:
        o_att_ctx = o_att_lat
    gate0 = dm.c_gate // D
    ry_f, ry_b, bonus, r_gate, rp = rwkv
    gy_f, gy_b, norm_w = gla
    dv = dm.GV // G_HEADS

    def row_spec(width, col=0):
        return pl.BlockSpec((tm, width), lambda i: (i + off, col))

    def weight(k_rows):
        return pl.BlockSpec((None, k_rows, D), lambda i: (l, 0, 0), pipeline_mode=pl.Buffered(1))

    def full(a):
        return pl.BlockSpec(a.shape, lambda i: (0,) * a.ndim)

    consts = [rp['ln_w'], rp['ln_b'], rp['e1'], rp['e2'], norm_w.reshape(1, dv), g.reshape(1, D), b.reshape(1, D)]
    return pl.pallas_call(
        functools.partial(_merge_kernel, alpha, MERGE_CHUNKS, nct - off),
        grid=(rows // tm,),
        in_specs=[
            row_spec(D), dm.mod_spec(5, tm, off),
            pl.BlockSpec((tm, A_Q_DIM), lambda i: (jnp.minimum(i + off, nct - 1), 0)),
            pl.BlockSpec((tm, A_Q_DIM), lambda i: (jnp.maximum(i + off - nct, 0), 0)),
            row_spec(dm.R), row_spec(dm.R), row_spec(dm.R), row_spec(dm.R),
            row_spec(dm.GV), row_spec(dm.GV), row_spec(dm.GV, dm.c_gla_g // dm.GV),
            row_spec(D, gate0), row_spec(D, gate0 + 1), row_spec(D, gate0 + 2),
            weight(A_Q_DIM), weight(dm.R), weight(dm.GV), weight(D),
        ] + [full(a) for a in consts],
        out_specs=pl.BlockSpec((tm, D), lambda i: (i, 0)),
        out_shape=jax.ShapeDtypeStruct((rows, D), F32),
        compiler_params=_params("parallel"),
        name="merge",
    )(x, mods, o_att_ctx, o_att_lat, ry_f, ry_b, bonus, r_gate, gy_f, gy_b, z, z, z, z, wba, wbr, wbg, w_out, *consts)


def _pad_to(a, size, axis):
    pad = [(0, 0)] * a.ndim
    pad[axis] = (0, size - a.shape[axis])
    return jnp.pad(a, pad)


def _rwkv_cols(a, R):
    c = 3 * R
    rkv = a[..., :c]
    wc0, wc1 = a[..., c:c + DECAY_LORA], a[..., c + DECAY_LORA:c + 2 * DECAY_LORA]
    c += 2 * DECAY_LORA
    ac0, ac1 = a[..., c:c + ICLR_LORA], a[..., c + ICLR_LORA:c + 2 * ICLR_LORA]
    c += 2 * ICLR_LORA
    gc = a[..., c:c + GATE_LORA]
    ax = a.ndim - 1
    return jnp.concatenate([rkv, gc] + [_pad_to(t, LORA_PAD, ax) for t in (wc0, wc1, ac0, ac1)], axis=-1)


def _in_proj_weights(dm, w_in):
    D, R, GK, GV = dm.D, dm.R, dm.GK, dm.GV
    att_cols = A_Q_DIM + 2 * A_KV_DIM
    rw_src = 3 * R + 2 * DECAY_LORA + 2 * ICLR_LORA + GATE_LORA
    gla_src = 2 * GK + 2 * GV + 2 * G_GATE_RANK
    att, rwkv, gla, gate = jnp.split(w_in, np.cumsum([att_cols, rw_src, gla_src]).tolist(), axis=-1)
    a_q, a_k, a_v = att[:, :A_Q_DIM], att[:, A_Q_DIM:A_Q_DIM + A_KV_DIM], att[:, A_Q_DIM + A_KV_DIM:]
    g_q, g_k = gla[:, :GK], gla[:, GK:2 * GK]
    g_v, g_g = gla[:, 2 * GK:2 * GK + GV], gla[:, 2 * GK + GV:2 * GK + 2 * GV]
    g_ac = _pad_to(gla[:, 2 * GK + 2 * GV:], LORA_PAD, 1)
    parts = [_rwkv_cols(rwkv, R), a_k, gate, g_v, g_g, g_q, g_k, a_q, a_v, g_ac]
    parts.append(jnp.zeros((D, dm.NP - sum(p.shape[1] for p in parts)), w_in.dtype))
    return jnp.concatenate([p.astype(BF16) for p in parts], axis=-1)


def _rope_tables(S):
    rows = S // GRID_W
    row = jnp.repeat(jnp.arange(rows, dtype=F32), GRID_W)
    col = jnp.tile(jnp.arange(GRID_W, dtype=F32), rows)
    n_freq = A_HEAD_DIM // 4
    inv_freq = ROPE_BASE ** (-jnp.arange(n_freq, dtype=F32) / n_freq)
    ang_r, ang_c = row[:, None] * inv_freq, col[:, None] * inv_freq
    cos = jnp.concatenate([jnp.cos(ang_r)] * 2 + [jnp.cos(ang_c)] * 2, axis=-1)
    sin = jnp.concatenate([-jnp.sin(ang_r), jnp.sin(ang_r), -jnp.sin(ang_c), jnp.sin(ang_c)], axis=-1)
    return cos, sin


def kernel(x, c, ctx, c_ctx, ada_w, ada_b, ln_g, ln_b, ffn_w_gu, ffn_w_down, w_in, attn_sink, rwkv_conv, rwkv_w0,
           rwkv_w2, rwkv_a0, rwkv_a2, rwkv_g2, rwkv_k_k, rwkv_k_a, rwkv_r_k, rwkv_ln_w, rwkv_ln_b, gla_wa2, gla_ba,
           gla_norm_w, w_branch_att, w_branch_rwkv, w_branch_gla, w_out):
    B, S, D = x.shape
    CT = ctx.shape[1]
    depth = ada_w.shape[0]
    dm = _Dims(B, S, CT, D)
    R = dm.R
    alpha = float((2 * depth) ** 0.25)
    assert B + 1 <= MOD_ROWS

    cc = _pad_to(jnp.concatenate([c, c_ctx[None, :]], axis=0), MOD_ROWS, 0)
    mods_all = _ada(cc, ada_w, ada_b)
    cos, sin = _rope_tables(S)
    head_of_lane = jnp.arange(R) // R_HEAD_DIM
    e1 = (head_of_lane[:, None] == jnp.arange(LORA_PAD)[None, :]).astype(BF16)
    e2 = e1.T

    w_gu, w_dn = ffn_w_gu.astype(BF16), ffn_w_down.astype(BF16)
    wba, wbr = w_branch_att.astype(BF16), w_branch_rwkv.astype(BF16)
    wbg, wo = w_branch_gla.astype(BF16), w_out.astype(BF16)

    xs = jnp.concatenate([ctx.reshape(dm.MC, D), x.reshape(dm.ML, D)], axis=0)
    for l in range(depth):
        last = l == depth - 1
        mods = mods_all[l].reshape(MOD_ROWS * N_MOD, 1, D)
        xs = _ffn(dm, alpha, xs, mods, 0, w_gu, w_dn, l, 0, ln_g[l, 0], ln_b[l, 0])

        z = _inproj(dm, xs, mods, _in_proj_weights(dm, w_in[l]))
        o_att_lat = _attn_latent(dm, z, attn_sink[l], cos, sin)
        o_att_ctx = None if last else _attn_ctx(dm, z, attn_sink[l])

        rp = {
            'conv': _rwkv_cols(rwkv_conv[l], R),
            'g2': rwkv_g2[l].astype(BF16),
            'w0': rwkv_w0[l], 'a0': rwkv_a0[l],
            'w2': _pad_to(rwkv_w2[l], LORA_PAD, 1).astype(BF16),
            'a2': _pad_to(rwkv_a2[l], LORA_PAD, 1).astype(BF16),
            'k_k': rwkv_k_k[l].reshape(1, R), 'k_a': rwkv_k_a[l].reshape(1, R), 'r_k': rwkv_r_k[l].reshape(1, R),
            'ln_w': rwkv_ln_w[l].reshape(1, R), 'ln_b': rwkv_ln_b[l].reshape(1, R),
            'e1': e1, 'e2': e2,
        }
        r, v, kk, lw, kd, b, bonus, g = _rwkv_prep(dm, z, rp)
        g_m, h_m, q_m, y0_m = _rwkv_chunk(dm, r, v, kk, lw, kd, b)
        y_rwkv_f, y_rwkv_b = _rwkv_scan(dm, g_m, h_m, q_m, y0_m)

        wa_p = jnp.stack([_pad_to(jnp.pad(gla_wa2[l, d], ((d * G_GATE_RANK, 0), (0, 0))), LORA_PAD, 0)
                          for d in range(2)])
        wa_hi = wa_p.astype(BF16)
        wa_split = jnp.stack([wa_hi, (wa_p - wa_hi.astype(F32)).astype(BF16)], axis=1)
        y_gla_f, y_gla_b = _gla(dm, z, wa_split, gla_ba[l].reshape(2, 1, dm.GK))

        xs = _merge(dm, alpha, xs, mods, z, o_att_ctx, o_att_lat, (y_rwkv_f, y_rwkv_b, bonus, g, rp),
                    (y_gla_f, y_gla_b, gla_norm_w[l]), wba, wbr, wbg, wo, l, ln_g[l, 1], ln_b[l, 1])
        xs = _ffn(dm, alpha, xs, mods, 6, w_gu, w_dn, l, 1, ln_g[l, 2], ln_b[l, 2])
    return xs.reshape(B, S, D)
```

```python
import functools
import math

import jax
import jax.numpy as jnp
import numpy as np
from jax import lax
from jax.experimental import pallas as pl
from jax.experimental.pallas import tpu as pltpu

F32 = jnp.float32
BF16 = jnp.bfloat16

N_MOD = 9
D_FF = 5632
LN_EPS = 1e-5
FFN_RES = 0.5

A_HEADS = 8
A_KV_HEADS = 2
A_GROUP = A_HEADS // A_KV_HEADS
A_HEAD_DIM = 128
A_Q_DIM = A_HEADS * A_HEAD_DIM
A_KV_DIM = A_KV_HEADS * A_HEAD_DIM
WINDOW = 128
A_BLOCK = 128
GRID_W = 64
ROPE_BASE = 10000.0
MASK_VALUE = -1e30

R_HEAD_DIM = 64
DECAY_LORA = 96
ICLR_LORA = 96
GATE_LORA = 256
R_GN_EPS = 64e-5
R_CHUNK = 64
LORA_PAD = 128
SUBLANES = 8
HALO = 16

G_HEADS = 4
G_GATE_RANK = 16
G_GATE_NORM = 16.0
G_CHUNK = 64
G_EPS = 1e-5

ADA_STREAMS = 3
ROW_TILE_WIDE = 1024
SEQ_TILE = 256
IN_PROJ_COLS = 1792
FFN_HIDDEN_TILE = 512
ADA_K_TILE = 256
MERGE_CHUNKS = 4
MOD_ROWS = 16
VMEM_LIMIT = 58 * 1024 * 1024
VMEM_LIMIT_FFN = 63 * 1024 * 1024

NN = (((1,), (0,)), ((), ()))
NT = (((1,), (1,)), ((), ()))
TN = (((0,), (0,)), ((), ()))


def _dot(a, b, dims=NN):
    return lax.dot_general(a, b, dims, preferred_element_type=F32)


def _cumsum_dot(tri, x):
    hi = x.astype(BF16)
    rest = x - hi.astype(F32)
    mid = rest.astype(BF16)
    lo = (rest - mid.astype(F32)).astype(BF16)
    return _dot(tri, hi) + _dot(tri, mid) + _dot(tri, lo)


def _sigmoid(x):
    return 1.0 / (1.0 + jnp.exp(-x))


def _log_sigmoid(x):
    return jnp.minimum(x, 0.0) - jnp.log(1.0 + jnp.exp(-jnp.abs(x)))


def _ln(x):
    mu = jnp.mean(x, -1, keepdims=True)
    xc = x - mu
    var = jnp.mean(xc * xc, -1, keepdims=True)
    return xc * lax.rsqrt(var + LN_EPS)


def _params(*sem, vmem=VMEM_LIMIT):
    return pltpu.CompilerParams(dimension_semantics=sem, vmem_limit_bytes=vmem)


class _Dims:
    def __init__(self, B, S, CT, D):
        self.B, self.S, self.CT, self.D = B, S, CT, D
        self.MC, self.ML = B * CT, B * S
        self.M = self.MC + self.ML
        self.R = D // 2
        self.GK = D // 2
        self.GV = D
        self.rw_cols = 3 * self.R + GATE_LORA + 4 * LORA_PAD
        c = 0
        self.c_rwkv = c; c += self.rw_cols
        self.c_att_k = c; c += A_KV_DIM
        self.c_gate = c; c += 3 * D
        self.c_gla_v = c; c += self.GV
        self.c_gla_g = c; c += self.GV
        self.c_gla_q = c; c += self.GK
        self.c_gla_k = c; c += self.GK
        self.c_att_q = c; c += A_Q_DIM
        self.c_att_v = c; c += A_KV_DIM
        self.c_gla_ac = c; c += LORA_PAD
        self.tn_in = IN_PROJ_COLS
        self.NP = -(-c // self.tn_in) * self.tn_in
        self.tm_in = math.gcd(ROW_TILE_WIDE, math.gcd(self.MC, S))
        self.tmr = math.gcd(SEQ_TILE, math.gcd(CT, S))

    def mod_row(self, i, tm):
        nct = self.MC // tm
        tpb = self.S // tm
        return jnp.where(i < nct, self.B, (i - nct) // tpb)

    def mod_spec(self, m, tm, off=0):
        return pl.BlockSpec((None, 1, self.D), lambda i, *_: (self.mod_row(i + off, tm) * N_MOD + m, 0, 0))

    def chunk_block(self, b, d, s, chunk):
        ncc, ncl = self.CT // chunk, self.S // chunk
        j = s - ncc
        ctx_blk = b * ncc + jnp.where(d == 0, s, ncc - 1 - s)
        lat_blk = self.B * ncc + b * ncl + jnp.where(d == 0, j, ncl - 1 - j)
        return jnp.where(s < ncc, ctx_blk, lat_blk)


def _ada_kernel(c_ref, *refs):
    w_refs, b_ref, o_ref = refs[:-2], refs[-2], refs[-1]

    @pl.when(pl.program_id(1) == 0)
    def _():
        o_ref[...] = jnp.broadcast_to(b_ref[...], o_ref.shape)

    cc = c_ref[...]
    s = (cc * _sigmoid(cc)).astype(BF16)
    wide = o_ref.shape[1] // len(w_refs)
    for j, w_ref in enumerate(w_refs):
        o_ref[:, j * wide:(j + 1) * wide] += _dot(s, w_ref[...].astype(BF16))


def _ada(cc, ada_w, ada_b):
    L, D, N = ada_w.shape
    tk = ADA_K_TILE
    wide = N // ADA_STREAMS
    cc_k = cc.reshape(MOD_ROWS, D // tk, tk).transpose(1, 0, 2)
    return pl.pallas_call(
        _ada_kernel,
        grid=(L, D // tk),
        in_specs=[
            pl.BlockSpec((None, MOD_ROWS, tk), lambda l, k: (k, 0, 0)),
        ] + [pl.BlockSpec((None, tk, wide), lambda l, k, j=j: (l, k, j)) for j in range(ADA_STREAMS)] + [
            pl.BlockSpec((None, 1, N), lambda l, k: (l, 0, 0)),
        ],
        out_specs=pl.BlockSpec((None, MOD_ROWS, N), lambda l, k: (l, 0, 0)),
        out_shape=jax.ShapeDtypeStruct((L, MOD_ROWS, N), F32),
        compiler_params=_params("parallel", "arbitrary"),
        name="ada",
    )(cc_k, *([ada_w] * ADA_STREAMS), ada_b.reshape(L, 1, N))


def _ffn_kernel(alpha, x_ref, sh_ref, sc_ref, gt_ref, wg_ref, wu_ref, wd_ref, g_ref, b_ref, o_ref, h_ref):
    f = pl.program_id(1)

    @pl.when(f == 0)
    def _():
        h = _ln(x_ref[...]) * (1.0 + sc_ref[...]) + sh_ref[...]
        h_ref[...] = h.astype(BF16)
        o_ref[...] = jnp.zeros_like(o_ref)

    h = h_ref[...]
    g = _dot(h, wg_ref[...])
    u = _dot(h, wu_ref[...])
    a = (g * _sigmoid(g) * u).astype(BF16)
    o_ref[...] += _dot(a, wd_ref[...])

    @pl.when(f == pl.num_programs(1) - 1)
    def _():
        z = alpha * x_ref[...] + (FFN_RES * gt_ref[...]) * o_ref[...]
        o_ref[...] = _ln(z) * g_ref[...] + b_ref[...]


def _ffn(dm, alpha, x, mods, m0, w_gu, w_down, l, j, g, b):
    D, tm, tf = dm.D, dm.tm_in, FFN_HIDDEN_TILE
    nf = D_FF // tf
    rows = x.shape[0]
    off = (dm.M - rows) // tm
    return pl.pallas_call(
        functools.partial(_ffn_kernel, alpha),
        grid=(rows // tm, nf),
        in_specs=[
            pl.BlockSpec((tm, D), lambda i, f: (i, 0)),
            dm.mod_spec(m0, tm, off), dm.mod_spec(m0 + 1, tm, off), dm.mod_spec(m0 + 2, tm, off),
            pl.BlockSpec((None, None, D, tf), lambda i, f: (l, j, 0, f)),
            pl.BlockSpec((None, None, D, tf), lambda i, f: (l, j, 0, f + nf)),
            pl.BlockSpec((None, None, tf, D), lambda i, f: (l, j, f, 0)),
            pl.BlockSpec((1, D), lambda i, f: (0, 0)),
            pl.BlockSpec((1, D), lambda i, f: (0, 0)),
        ],
        out_specs=pl.BlockSpec((tm, D), lambda i, f: (i, 0)),
        out_shape=jax.ShapeDtypeStruct((rows, D), F32),
        scratch_shapes=[pltpu.VMEM((tm, D), BF16)],
        compiler_params=_params("parallel", "arbitrary", vmem=VMEM_LIMIT_FFN),
        name="ffn",
    )(x, mods, mods, mods, w_gu, w_gu, w_down, g.reshape(1, D), b.reshape(1, D))


def _inproj_kernel(x_ref, sh_ref, sc_ref, w_ref, o_ref, h_ref):
    @pl.when(pl.program_id(1) == 0)
    def _():
        h = _ln(x_ref[...]) * (1.0 + sc_ref[...]) + sh_ref[...]
        h_ref[...] = h.astype(BF16)

    o_ref[...] = _dot(h_ref[...], w_ref[...]).astype(BF16)


def _inproj(dm, x, mods, w_in_p):
    D, tm, tn = dm.D, dm.tm_in, dm.tn_in
    return pl.pallas_call(
        _inproj_kernel,
        grid=(dm.M // tm, dm.NP // tn),
        in_specs=[
            pl.BlockSpec((tm, D), lambda i, n: (i, 0)),
            dm.mod_spec(3, tm), dm.mod_spec(4, tm),
            pl.BlockSpec((D, tn), lambda i, n: (0, n)),
        ],
        out_specs=pl.BlockSpec((tm, tn), lambda i, n: (i, n)),
        out_shape=jax.ShapeDtypeStruct((dm.M, dm.NP), BF16),
        scratch_shapes=[pltpu.VMEM((tm, D), BF16)],
        compiler_params=_params("parallel", "arbitrary"),
        name="inproj",
    )(x, mods, mods, w_in_p)


def _rope(x, cos, sin_signed):
    lane = lax.broadcasted_iota(jnp.int32, x.shape, 1)
    quarter = A_HEAD_DIM // 4
    swapped = jnp.where((lane % (2 * quarter)) < quarter,
                        pltpu.roll(x, A_HEAD_DIM - quarter, 1), pltpu.roll(x, quarter, 1))
    return x * cos + swapped * sin_signed


def _attn_kernel(S, nb, sink_ref, q_ref, kp_ref, kc_ref, kn_ref, vp_ref, vc_ref, vn_ref, kx_ref, vx_ref,
                 cos_ref, sin_ref, o_ref):
    n = pl.program_id(1)
    scale = A_HEAD_DIM ** -0.5

    def table(ref, blk):
        return ref[pl.ds(pl.multiple_of(blk * A_BLOCK, A_BLOCK), A_BLOCK), :]

    blk_p, blk_n = jnp.maximum(n - 1, 0), jnp.minimum(n + 1, nb - 1)
    cos_c, sin_c = table(cos_ref, n), table(sin_ref, n)
    cos_p, sin_p = table(cos_ref, blk_p), table(sin_ref, blk_p)
    cos_n, sin_n = table(cos_ref, blk_n), table(sin_ref, blk_n)

    rows, band = A_GROUP * A_BLOCK, 3 * A_BLOCK
    qpos = n * A_BLOCK + lax.broadcasted_iota(jnp.int32, (rows, band), 0) % A_BLOCK
    kpos = (n - 1) * A_BLOCK + lax.broadcasted_iota(jnp.int32, (rows, band), 1)
    valid = (jnp.abs(qpos - kpos) <= WINDOW) & (kpos >= 0) & (kpos < S)

    work = []
    for kvh in range(A_KV_HEADS):
        ks = slice(kvh * A_HEAD_DIM, (kvh + 1) * A_HEAD_DIM)
        kb = jnp.concatenate([_rope(kp_ref[:, ks].astype(F32), cos_p, sin_p),
                              _rope(kc_ref[:, ks].astype(F32), cos_c, sin_c),
                              _rope(kn_ref[:, ks].astype(F32), cos_n, sin_n)], axis=0).astype(BF16)
        heads = [kvh * A_GROUP + g for g in range(A_GROUP)]
        qs = [q_ref[:, h * A_HEAD_DIM:(h + 1) * A_HEAD_DIM] for h in heads]
        q_rope = jnp.concatenate([_rope(q.astype(F32), cos_c, sin_c) * scale for q in qs], axis=0).astype(BF16)
        q_plain = jnp.concatenate([q.astype(F32) * scale for q in qs], axis=0).astype(BF16)
        sink = jnp.concatenate([jnp.full((A_BLOCK, 1), sink_ref[h], F32) for h in heads], axis=0)
        s_loc = jnp.where(valid, _dot(q_rope, kb, NT), MASK_VALUE)
        s_ctx = _dot(q_plain, kx_ref[:, ks], NT)
        work.append(dict(ks=ks, heads=heads, sink=sink, s_loc=s_loc, s_ctx=s_ctx))
    for w in work:
        w['m'] = jnp.maximum(jnp.maximum(jnp.max(w['s_loc'], -1, keepdims=True),
                                         jnp.max(w['s_ctx'], -1, keepdims=True)), w['sink'])
    for w in work:
        w['p_loc'] = jnp.exp(w.pop('s_loc') - w['m'])
        w['p_ctx'] = jnp.exp(w.pop('s_ctx') - w['m'])
    for w in work:
        w['den'] = (jnp.sum(w['p_loc'], -1, keepdims=True) + jnp.sum(w['p_ctx'], -1, keepdims=True)
                    + jnp.exp(w['sink'] - w['m']))
    for w in work:
        ks = w['ks']
        vb = jnp.concatenate([vp_ref[:, ks], vc_ref[:, ks], vn_ref[:, ks]], axis=0)
        o = (_dot(w['p_loc'].astype(BF16), vb) + _dot(w['p_ctx'].astype(BF16), vx_ref[:, ks])) / w['den']
        for g, h in enumerate(w['heads']):
            o_ref[:, h * A_HEAD_DIM:(h + 1) * A_HEAD_DIM] = o[g * A_BLOCK:(g + 1) * A_BLOCK].astype(BF16)


def _attn_latent(dm, z, sink, cos, sin):
    B, S = dm.B, dm.S
    nb = S // A_BLOCK
    base = dm.MC // A_BLOCK
    qc = dm.c_att_q // A_Q_DIM
    kc = dm.c_att_k // A_KV_DIM
    vc = dm.c_att_v // A_KV_DIM

    def rows(shift):
        return lambda b, n: base + b * nb + jnp.clip(n + shift, 0, nb - 1)

    def kv_spec(col, shift):
        r = rows(shift)
        return pl.BlockSpec((A_BLOCK, A_KV_DIM), lambda b, n: (r(b, n), col))

    return pl.pallas_call(
        functools.partial(_attn_kernel, S, nb),
        grid=(B, nb),
        in_specs=[
            pl.BlockSpec(memory_space=pltpu.SMEM),
            pl.BlockSpec((A_BLOCK, A_Q_DIM), lambda b, n: (base + b * nb + n, qc)),
            kv_spec(kc, -1), kv_spec(kc, 0), kv_spec(kc, 1),
            kv_spec(vc, -1), kv_spec(vc, 0), kv_spec(vc, 1),
            pl.BlockSpec((dm.CT, A_KV_DIM), lambda b, n: (b, kc)),
            pl.BlockSpec((dm.CT, A_KV_DIM), lambda b, n: (b, vc)),
            pl.BlockSpec((S, A_HEAD_DIM), lambda b, n: (0, 0)),
            pl.BlockSpec((S, A_HEAD_DIM), lambda b, n: (0, 0)),
        ],
        out_specs=pl.BlockSpec((A_BLOCK, A_Q_DIM), lambda b, n: (b * nb + n, 0)),
        out_shape=jax.ShapeDtypeStruct((dm.ML, A_Q_DIM), BF16),
        compiler_params=_params("parallel", "arbitrary"),
        name="attn_latent",
    )(sink, z, z, z, z, z, z, z, z, z, cos, sin)


def _attn_ctx_kernel(CT, sink_ref, q_ref, k_ref, v_ref, o_ref):
    scale = A_HEAD_DIM ** -0.5
    for kvh in range(A_KV_HEADS):
        ks = slice(kvh * A_HEAD_DIM, (kvh + 1) * A_HEAD_DIM)
        k = k_ref[:, ks]
        v = v_ref[:, ks]
        heads = [kvh * A_GROUP + g for g in range(A_GROUP)]
        q = jnp.concatenate([q_ref[:, h * A_HEAD_DIM:(h + 1) * A_HEAD_DIM] for h in heads], axis=0)
        sink = jnp.concatenate([jnp.full((CT, 1), sink_ref[h], F32) for h in heads], axis=0)
        s = _dot(q, k, NT) * scale
        m = jnp.maximum(jnp.max(s, -1, keepdims=True), sink)
        p = jnp.exp(s - m)
        den = jnp.sum(p, -1, keepdims=True) + jnp.exp(sink - m)
        o = _dot(p.astype(BF16), v) / den
        for g, h in enumerate(heads):
            o_ref[:, h * A_HEAD_DIM:(h + 1) * A_HEAD_DIM] = o[g * CT:(g + 1) * CT].astype(BF16)


def _attn_ctx(dm, z, sink):
    CT = dm.CT
    return pl.pallas_call(
        functools.partial(_attn_ctx_kernel, CT),
        grid=(dm.B,),
        in_specs=[
            pl.BlockSpec(memory_space=pltpu.SMEM),
            pl.BlockSpec((CT, A_Q_DIM), lambda b: (b, dm.c_att_q // A_Q_DIM)),
            pl.BlockSpec((CT, A_KV_DIM), lambda b: (b, dm.c_att_k // A_KV_DIM)),
            pl.BlockSpec((CT, A_KV_DIM), lambda b: (b, dm.c_att_v // A_KV_DIM)),
        ],
        out_specs=pl.BlockSpec((CT, A_Q_DIM), lambda b: (b, 0)),
        out_shape=jax.ShapeDtypeStruct((dm.MC, A_Q_DIM), BF16),
        compiler_params=_params("parallel"),
        name="attn_ctx",
    )(sink, z, z, z)


def _head_sum(x, e1_ref, e2_ref):
    def split_dot(a, ind):
        hi = a.astype(BF16)
        lo = (a - hi.astype(F32)).astype(BF16)
        return _dot(hi, ind) + _dot(lo, ind)

    return split_dot(split_dot(x, e1_ref[...]), e2_ref[...])


def _rwkv_prep_kernel(dm, z_ref, zp_ref, zn_ref, cw_ref, g2_ref, w0_ref, w2_ref, a0_ref, a2_ref, kk_ref, ka_ref,
                      rk_ref, e1_ref, e2_ref,
                      r_out, v_out, kk_out, lw_out, kd_out, b_out, bonus_out, g_out):
    i = pl.program_id(0)
    tm, R = dm.tmr, dm.R
    nct = dm.MC // tm
    start = jnp.where(i < nct, (i * tm) % dm.CT, ((i - nct) * tm) % dm.S)
    seqlen = jnp.where(i < nct, dm.CT, dm.S)
    has_prev = (start != 0).astype(F32)
    has_next = (start + tm != seqlen).astype(F32)

    z = z_ref[...].astype(F32)
    sub = lax.broadcasted_iota(jnp.int32, (SUBLANES, 1), 0)
    halo_prev = zp_ref[HALO - 1:HALO, :].astype(F32) * has_prev
    halo_next = zn_ref[0:1, :].astype(F32) * has_next
    down, up = pltpu.roll(z, 1, 0), pltpu.roll(z, tm - 1, 0)
    z_prev = jnp.concatenate([jnp.where(sub == 0, halo_prev, down[:SUBLANES]), down[SUBLANES:]], axis=0)
    z_next = jnp.concatenate([up[:tm - SUBLANES], jnp.where(sub == SUBLANES - 1, halo_next, up[tm - SUBLANES:])],
                             axis=0)
    zc = z_prev * cw_ref[0:1, :] + z * cw_ref[1:2, :] + z_next * cw_ref[2:3, :]

    r = zc[:, 0:R]
    k = zc[:, R:2 * R]
    v = zc[:, 2 * R:3 * R]
    c0 = 3 * R
    gc = zc[:, c0:c0 + GATE_LORA]
    c0 += GATE_LORA
    g_out[...] = _dot(_sigmoid(gc).astype(BF16), g2_ref[...])

    kkr = k * kk_ref[...]
    norm = jnp.sqrt(_head_sum(kkr * kkr, e1_ref, e2_ref))
    kk = kkr / jnp.maximum(norm, 1e-12)
    r_out[...] = r
    v_out[...] = v
    kk_out[...] = kk

    bonus = jnp.zeros_like(v)
    for d in range(2):
        wc = zc[:, c0 + d * LORA_PAD:c0 + (d + 1) * LORA_PAD]
        ac = zc[:, c0 + (2 + d) * LORA_PAD:c0 + (3 + d) * LORA_PAD]
        wl = w0_ref[d:d + 1, :] + _dot(jnp.tanh(wc).astype(BF16), w2_ref[d])
        lw_out[d] = _sigmoid(wl) * (-math.exp(-0.5))
        a = _sigmoid(a0_ref[d:d + 1, :] + _dot(ac.astype(BF16), a2_ref[d]))
        kd = k * (1.0 + (a - 1.0) * ka_ref[...])
        kd_out[d] = kd
        b_out[d] = kk * a
        bonus = bonus + _head_sum(r * kd * rk_ref[...], e1_ref, e2_ref) * v
    bonus_out[...] = bonus


def _rwkv_prep(dm, z, p):
    tm, R, M, W = dm.tmr, dm.R, dm.M, dm.rw_cols
    n_halo = M // HALO
    col = dm.c_rwkv // W
    row = pl.BlockSpec((tm, R), lambda i: (i, 0))
    row2 = pl.BlockSpec((2, tm, R), lambda i: (0, i, 0))

    def full(a):
        return pl.BlockSpec(a.shape, lambda i: (0,) * a.ndim)

    consts = [p['conv'], p['g2'], p['w0'], p['w2'], p['a0'], p['a2'], p['k_k'], p['k_a'], p['r_k'], p['e1'], p['e2']]
    one = jax.ShapeDtypeStruct((M, R), F32)
    two = jax.ShapeDtypeStruct((2, M, R), F32)
    return pl.pallas_call(
        functools.partial(_rwkv_prep_kernel, dm),
        grid=(M // tm,),
        in_specs=[
            pl.BlockSpec((tm, W), lambda i: (i, col)),
            pl.BlockSpec((HALO, W), lambda i: (jnp.maximum(i * (tm // HALO) - 1, 0), col)),
            pl.BlockSpec((HALO, W), lambda i: (jnp.minimum((i + 1) * (tm // HALO), n_halo - 1), col)),
        ] + [full(a) for a in consts],
        out_specs=[row, row, row, row2, row2, row2, row, row],
        out_shape=[one, one, one, two, two, two, one, one],
        compiler_params=_params("parallel"),
        name="rwkv_prep",
    )(z, z, z, *consts)


R_SLAB = 2 * R_HEAD_DIM
CHUNK_SLABS = 4
SLAB_HEADS = R_SLAB // R_HEAD_DIM


def _block_diag(x, mask):
    return jnp.where(mask, jnp.concatenate([x] * SLAB_HEADS, axis=0), 0.0).astype(BF16)


def _diag_blocks(full, mask):
    fm = jnp.where(mask, full, 0.0)
    n = R_HEAD_DIM
    return functools.reduce(lambda acc, j: acc + fm[j * n:(j + 1) * n], range(1, SLAB_HEADS), fm[0:n])


def _slab_masks():
    n, g = R_HEAD_DIM, R_SLAB
    blk = (lax.broadcasted_iota(jnp.int32, (g, g), 0) // n) == (lax.broadcasted_iota(jnp.int32, (g, g), 1) // n)
    t_idx = lax.broadcasted_iota(jnp.int32, (n, g), 0)
    s_idx = lax.broadcasted_iota(jnp.int32, (n, g), 1) % n
    return blk, t_idx, s_idx


def _rwkv_chunk_kernel(n_sub, n_slab, r_ref, v_ref, kk_ref, lw_ref, kd_ref, b_ref, g_out, h_out, q_out, y0_out):
    C = R_CHUNK
    blk, t_idx, s_idx = _slab_masks()
    eye4 = (t_idx == s_idx).astype(F32)
    ti = lax.broadcasted_iota(jnp.int32, (C, C), 0)
    si = lax.broadcasted_iota(jnp.int32, (C, C), 1)

    def bd(x):
        return _block_diag(x, blk)

    def setup(c, d, sl):
        rs = pl.ds(c * C, C)
        ls = pl.ds(sl * R_SLAB, R_SLAB)
        r, v, kk = r_ref[rs, ls], v_ref[rs, ls], kk_ref[rs, ls]
        tri = ((ti >= si) if d == 0 else (ti <= si)).astype(BF16)
        lw = lw_ref[d, rs, ls]
        cl = _cumsum_dot(tri, lw)
        tot = jnp.sum(lw, axis=0, keepdims=True)
        e_neg = jnp.exp(-cl)
        e_end = jnp.exp(tot - cl)
        kd, b = kd_ref[d, rs, ls], b_ref[d, rs, ls]
        ch = dict(rs=rs, ls=ls, d=d, tot=tot, v_b=v.astype(BF16), v_bd=bd(v),
                  incl=(t_idx >= s_idx) if d == 0 else (t_idx <= s_idx),
                  strict=(t_idx > s_idx) if d == 0 else (t_idx < s_idx),
                  kkm=kk * jnp.exp(cl - lw), rp=r * jnp.exp(cl),
                  bp=b * e_neg, kp=kd * e_neg, be=(b * e_end).astype(BF16), ke=(kd * e_end).astype(BF16))
        ch['lhs'] = jnp.concatenate([ch['kkm'], ch['rp']], axis=0).astype(BF16)
        return ch

    chains = [setup(c, d, sl) for c in range(n_sub) for d in range(2) for sl in range(n_slab)]
    for ch in chains:
        gram_b = _dot(ch['lhs'], bd(ch.pop('bp')), NT)
        gram_k = _dot(ch.pop('lhs'), bd(ch.pop('kp')), NT)
        lb = jnp.where(ch['strict'], gram_b[:C], 0.0)
        ch['mb'] = jnp.where(ch['incl'], gram_b[C:], 0.0).astype(BF16)
        ch['lmk'] = jnp.concatenate([jnp.where(ch['strict'], gram_k[:C], 0.0),
                                     jnp.where(ch['incl'], gram_k[C:], 0.0)], axis=0).astype(BF16)
        ch['pw'] = -lb
        ch['t_inv'] = eye4 - lb
    for ch in chains:
        ch['pw'] = _dot(ch['pw'].astype(BF16), bd(ch['pw']))
        lmkv = _dot(ch.pop('lmk'), ch.pop('v_bd'))
        ch['lkv'], ch['mkv'] = lmkv[:C], lmkv[C:]
    for _ in range(int(math.log2(C)) - 2):
        for ch in chains:
            both = _dot(jnp.concatenate([ch['pw'], ch['t_inv']], axis=0).astype(BF16), bd(ch['pw']))
            ch['pw'], ch['t_inv'] = both[:C], ch['t_inv'] + both[C:]
    for ch in chains:
        ch['t_inv'] = (ch['t_inv'] + _dot(ch['t_inv'].astype(BF16), bd(ch.pop('pw')))).astype(BF16)
    for ch in chains:
        ch['a_m'] = _dot(ch['t_inv'], bd(ch.pop('kkm')))
        ch['d_m'] = -_dot(ch.pop('t_inv'), bd(ch.pop('lkv')))
    for ch in chains:
        d, rs, ls = ch['d'], ch['rs'], ch['ls']
        a_m, d_m = ch['a_m'], ch['d_m']
        q_out[d, rs, ls] = (ch['rp'] - _dot(ch['mb'], bd(a_m))).astype(BF16)
        y0_out[d, rs, ls] = (_dot(ch['mb'], bd(d_m)) + ch['mkv']).astype(BF16)
        g_cross = _dot(ch['be'], a_m.astype(BF16), TN)
        h_cross = _dot(jnp.concatenate([ch['be'], ch['ke']], axis=0),
                       jnp.concatenate([d_m.astype(BF16), ch['v_b']], axis=0), TN)
        g_out[d, rs, ls] = (eye4 * jnp.exp(ch['tot']) - _diag_blocks(g_cross, blk)).astype(BF16)
        h_out[d, rs, ls] = _diag_blocks(h_cross, blk)


def _rwkv_chunk(dm, r, v, kk, lw, kd, b):
    M, R = dm.M, dm.R
    rows = dm.tmr
    lanes = CHUNK_SLABS * R_SLAB
    one = pl.BlockSpec((rows, lanes), lambda i, h: (i, h))
    two = pl.BlockSpec((2, rows, lanes), lambda i, h: (0, i, h))
    f32 = jax.ShapeDtypeStruct((2, M, R), F32)
    b16 = jax.ShapeDtypeStruct((2, M, R), BF16)
    return pl.pallas_call(
        functools.partial(_rwkv_chunk_kernel, rows // R_CHUNK, CHUNK_SLABS),
        grid=(M // rows, R // lanes),
        in_specs=[one, one, one, two, two, two],
        out_specs=[two, two, two, two],
        out_shape=[b16, f32, b16, b16],
        compiler_params=_params("parallel", "parallel"),
        name="rwkv_chunk",
    )(r, v, kk, lw, kd, b)


def _rwkv_scan_kernel(n_slab, n_in, gf_ref, hf_ref, qf_ref, y0f_ref, gb_ref, hb_ref, qb_ref, y0b_ref,
                      yf_ref, yb_ref, st_ref):
    C = R_CHUNK
    dirs = ((gf_ref, hf_ref, qf_ref, y0f_ref, yf_ref), (gb_ref, hb_ref, qb_ref, y0b_ref, yb_ref))

    @pl.when(pl.program_id(1) == 0)
    def _():
        st_ref[...] = jnp.zeros_like(st_ref)

    blk, _, _ = _slab_masks()
    lanes = [slice(sl * R_SLAB, (sl + 1) * R_SLAB) for sl in range(n_slab)]
    for t in range(n_in):
        todo = [(d, refs, pl.ds((t, n_in - 1 - t)[d] * C, C), ls) for d, refs in enumerate(dirs) for ls in lanes]
        st_bd = [_block_diag(st_ref[d, :, ls], blk) for d, _, _, ls in todo]
        both = [_dot(jnp.concatenate([refs[2][rs, ls], refs[0][rs, ls]], axis=0), bd)
                for (d, refs, rs, ls), bd in zip(todo, st_bd)]
        for (d, refs, rs, ls), res in zip(todo, both):
            refs[4][rs, ls] = (res[:C] + refs[3][rs, ls].astype(F32)).astype(BF16)
            st_ref[d, :, ls] = res[C:] + refs[1][rs, ls]


def _rwkv_scan(dm, g_m, h_m, q_m, y0_m):
    R, rows = dm.R, dm.tmr
    steps = (dm.CT + dm.S) // rows

    def spec(d):
        return pl.BlockSpec((None, rows, R), lambda b, s: (d, dm.chunk_block(b, d, s, rows), 0))

    def yspec(d):
        return pl.BlockSpec((rows, R), lambda b, s: (dm.chunk_block(b, d, s, rows), 0))

    y = jax.ShapeDtypeStruct((dm.M, R), BF16)
    return pl.pallas_call(
        functools.partial(_rwkv_scan_kernel, R // R_SLAB, rows // R_CHUNK),
        grid=(dm.B, steps),
        in_specs=[spec(0)] * 4 + [spec(1)] * 4,
        out_specs=[yspec(0), yspec(1)],
        out_shape=[y, y],
        scratch_shapes=[pltpu.VMEM((2, R_HEAD_DIM, R), F32)],
        compiler_params=_params("parallel", "arbitrary"),
        name="rwkv_scan",
    )(g_m, h_m, q_m, y0_m, g_m, h_m, q_m, y0_m)


def _rwkv_branch_out(y_f, y_b, bonus, g, lnw, lnb, e1_ref, e2_ref):
    o = y_f.astype(F32) + y_b.astype(F32)
    inv_n = 1.0 / R_HEAD_DIM
    mu = _head_sum(o, e1_ref, e2_ref) * inv_n
    oc = o - mu
    var = _head_sum(oc * oc, e1_ref, e2_ref) * inv_n
    o = oc * lax.rsqrt(var + R_GN_EPS) * lnw + lnb
    return ((o + bonus) * g).astype(BF16)


def _gla_kernel(dm, n_in, qf_ref, kf_ref, vf_ref, af_ref, qb_ref, kb_ref, vb_ref, ab_ref, wa_ref, ba_ref,
                yf_ref, yb_ref, st_ref, qd_ref, ki_ref, ke_ref, y_ref):
    C = G_CHUNK
    dk, dv = dm.GK // G_HEADS, dm.GV // G_HEADS
    key = [slice(h * dk, (h + 1) * dk) for h in range(G_HEADS)]
    val = [slice(h * dv, (h + 1) * dv) for h in range(G_HEADS)]
    row = [slice(c * C, (c + 1) * C) for c in range(n_in)]
    dirs = ((qf_ref, kf_ref, vf_ref, af_ref, yf_ref), (qb_ref, kb_ref, vb_ref, ab_ref, yb_ref))

    @pl.when(pl.program_id(1) == 0)
    def _():
        st_ref[...] = jnp.zeros_like(st_ref)

    ti = lax.broadcasted_iota(jnp.int32, (C, C), 0)
    si = lax.broadcasted_iota(jnp.int32, (C, C), 1)
    causal = (ti >= si, ti <= si)

    la = [_log_sigmoid(_dot(refs[3][...], wa_ref[d, 0]) + _dot(refs[3][...], wa_ref[d, 1]) + ba_ref[d])
          * (1.0 / G_GATE_NORM) for d, refs in enumerate(dirs)]
    cb = [[_cumsum_dot(causal[d].astype(BF16), la[d][r]) for r in row] for d in range(2)]
    e_last = []
    for d, refs in enumerate(dirs):
        cum = jnp.concatenate(cb[d], axis=0)
        tots = [jnp.sum(la[d][r], axis=0, keepdims=True) for r in row]
        tot = jnp.concatenate([jnp.broadcast_to(t, (C, t.shape[1])) for t in tots], axis=0)
        k = refs[1][...].astype(F32)
        qd_ref[d] = (refs[0][...].astype(F32) * (dk ** -0.5) * jnp.exp(cum)).astype(BF16)
        ki_ref[d] = (k * jnp.exp(-cum)).astype(BF16)
        ke_ref[d] = (k * jnp.exp(tot - cum)).astype(BF16)
        e_last.append([jnp.exp(t) for t in tots])
    for d, refs in enumerate(dirs):
        att = [[jnp.where(causal[d], _dot(qd_ref[d, r, ks], ki_ref[d, r, ks], NT), 0.0).astype(BF16) for ks in key]
               for r in row]
        for c, r in enumerate(row):
            for h in range(G_HEADS):
                y_ref[d, r, val[h]] = _dot(att[c][h], refs[2][r, val[h]])
    for t in range(n_in):
        todo = [(d, refs, (t, n_in - 1 - t)[d], h) for d, refs in enumerate(dirs) for h in range(G_HEADS)]
        inter = [_dot(qd_ref[d, row[c], key[h]], st_ref[d, h].astype(BF16), NT) for d, _, c, h in todo]
        upd = [_dot(refs[2][row[c], val[h]], ke_ref[d, row[c], key[h]], TN) for d, refs, c, h in todo]
        for (d, refs, c, h), o in zip(todo, inter):
            y_ref[d, row[c], val[h]] += o
        for (d, refs, c, h), u in zip(todo, upd):
            st_ref[d, h] = st_ref[d, h] * e_last[d][c][:, key[h]] + u
    yf_ref[...] = y_ref[0].astype(BF16)
    yb_ref[...] = y_ref[1].astype(BF16)


def _gla(dm, z, wa_split, ba):
    rows = dm.tmr
    steps = (dm.CT + dm.S) // rows
    GK, GV = dm.GK, dm.GV

    def zspec(width, col, d):
        return pl.BlockSpec((rows, width), lambda b, s: (dm.chunk_block(b, d, s, rows), col // width))

    def zspecs(d):
        return [zspec(GK, dm.c_gla_q, d), zspec(GK, dm.c_gla_k, d), zspec(GV, dm.c_gla_v, d),
                zspec(LORA_PAD, dm.c_gla_ac, d)]

    def yspec(d):
        return pl.BlockSpec((rows, GV), lambda b, s: (dm.chunk_block(b, d, s, rows), 0))

    y = jax.ShapeDtypeStruct((dm.M, GV), BF16)
    return pl.pallas_call(
        functools.partial(_gla_kernel, dm, rows // G_CHUNK),
        grid=(dm.B, steps),
        in_specs=zspecs(0) + zspecs(1) + [
            pl.BlockSpec((2, 2, LORA_PAD, GK), lambda b, s: (0, 0, 0, 0)),
            pl.BlockSpec((2, 1, GK), lambda b, s: (0, 0, 0)),
        ],
        out_specs=[yspec(0), yspec(1)],
        out_shape=[y, y],
        scratch_shapes=[pltpu.VMEM((2, G_HEADS, GV // G_HEADS, GK // G_HEADS), F32),
                        pltpu.VMEM((2, rows, GK), BF16), pltpu.VMEM((2, rows, GK), BF16),
                        pltpu.VMEM((2, rows, GK), BF16), pltpu.VMEM((2, rows, GV), F32)],
        compiler_params=_params("parallel", "arbitrary"),
        name="gla",
    )(z, z, z, z, z, z, z, z, wa_split, ba)


def _gla_branch_out(y_f, y_b, g, norm_w):
    o = y_f.astype(F32) + y_b.astype(F32)
    g = g.astype(F32)
    gate = g * _sigmoid(g)
    dv = norm_w.shape[1]
    heads = []
    for h in range(G_HEADS):
        vs = slice(h * dv, (h + 1) * dv)
        oh = o[:, vs]
        oh = oh * lax.rsqrt(jnp.mean(oh * oh, -1, keepdims=True) + G_EPS) * norm_w
        heads.append((oh * gate[:, vs]).astype(BF16))
    return jnp.concatenate(heads, axis=1)


def _merge_kernel(alpha, n_chunks, ctx_tiles, x_ref, mg_ref, oac_ref, oal_ref, ryf_ref, ryb_ref, bonus_ref, rg_ref, gyf_ref, gyb_ref,
                  gg_in_ref, ga_ref, gr_ref, gg_ref, wa_ref, wr_ref, wg_ref, wo_ref, lnw_ref, lnb_ref, e1_ref, e2_ref,
                  nw_ref, g_ref, b_ref, o_ref):
    tn = wo_ref.shape[0] // n_chunks
    oa = jnp.where(pl.program_id(0) < ctx_tiles, oac_ref[...], oal_ref[...])
    orw = _rwkv_branch_out(ryf_ref[...], ryb_ref[...], bonus_ref[...], rg_ref[...], lnw_ref[...], lnb_ref[...],
                           e1_ref, e2_ref)
    og = _gla_branch_out(gyf_ref[...], gyb_ref[...], gg_in_ref[...], nw_ref[...])
    acc = None
    for n in range(n_chunks):
        cs = slice(n * tn, (n + 1) * tn)
        merged = (_sigmoid(ga_ref[:, cs].astype(F32)) * _dot(oa, wa_ref[:, cs])
                  + _sigmoid(gr_ref[:, cs].astype(F32)) * _dot(orw, wr_ref[:, cs])
                  + _sigmoid(gg_ref[:, cs].astype(F32)) * _dot(og, wg_ref[:, cs]))
        part = _dot(merged.astype(BF16), wo_ref[cs, :])
        acc = part if acc is None else acc + part
    z = alpha * x_ref[...] + mg_ref[...] * acc
    o_ref[...] = _ln(z) * g_ref[...] + b_ref[...]


def _merge(dm, alpha, x, mods, z, o_att_ctx, o_att_lat, rwkv, gla, wba, wbr, wbg, w_out, l, g, b):
    D, tm = dm.D, dm.tmr
    nct = dm.MC // tm
    off = nct if o_att_ctx is None else 0
    rows = dm.M - off * tm
    if o_att_ctx is None:
        o_att_ctx = o_att_lat
    gate0 = dm.c_gate // D
    ry_f, ry_b, bonus, r_gate, rp = rwkv
    gy_f, gy_b, norm_w = gla
    dv = dm.GV // G_HEADS

    def row_spec(width, col=0):
        return pl.BlockSpec((tm, width), lambda i: (i + off, col))

    def weight(k_rows):
        return pl.BlockSpec((None, k_rows, D), lambda i: (l, 0, 0), pipeline_mode=pl.Buffered(1))

    def full(a):
        return pl.BlockSpec(a.shape, lambda i: (0,) * a.ndim)

    consts = [rp['ln_w'], rp['ln_b'], rp['e1'], rp['e2'], norm_w.reshape(1, dv), g.reshape(1, D), b.reshape(1, D)]
    return pl.pallas_call(
        functools.partial(_merge_kernel, alpha, MERGE_CHUNKS, nct - off),
        grid=(rows // tm,),
        in_specs=[
            row_spec(D), dm.mod_spec(5, tm, off),
            pl.BlockSpec((tm, A_Q_DIM), lambda i: (jnp.minimum(i + off, nct - 1), 0)),
            pl.BlockSpec((tm, A_Q_DIM), lambda i: (jnp.maximum(i + off - nct, 0), 0)),
            row_spec(dm.R), row_spec(dm.R), row_spec(dm.R), row_spec(dm.R),
            row_spec(dm.GV), row_spec(dm.GV), row_spec(dm.GV, dm.c_gla_g // dm.GV),
            row_spec(D, gate0), row_spec(D, gate0 + 1), row_spec(D, gate0 + 2),
            weight(A_Q_DIM), weight(dm.R), weight(dm.GV), weight(D),
        ] + [full(a) for a in consts],
        out_specs=pl.BlockSpec((tm, D), lambda i: (i, 0)),
        out_shape=jax.ShapeDtypeStruct((rows, D), F32),
        compiler_params=_params("parallel"),
        name="merge",
    )(x, mods, o_att_ctx, o_att_lat, ry_f, ry_b, bonus, r_gate, gy_f, gy_b, z, z, z, z, wba, wbr, wbg, w_out, *consts)


def _pad_to(a, size, axis):
    pad = [(0, 0)] * a.ndim
    pad[axis] = (0, size - a.shape[axis])
    return jnp.pad(a, pad)


def _rwkv_cols(a, R):
    c = 3 * R
    rkv = a[..., :c]
    wc0, wc1 = a[..., c:c + DECAY_LORA], a[..., c + DECAY_LORA:c + 2 * DECAY_LORA]
    c += 2 * DECAY_LORA
    ac0, ac1 = a[..., c:c + ICLR_LORA], a[..., c + ICLR_LORA:c + 2 * ICLR_LORA]
    c += 2 * ICLR_LORA
    gc = a[..., c:c + GATE_LORA]
    ax = a.ndim - 1
    return jnp.concatenate([rkv, gc] + [_pad_to(t, LORA_PAD, ax) for t in (wc0, wc1, ac0, ac1)], axis=-1)


def _in_proj_weights(dm, w_in):
    D, R, GK, GV = dm.D, dm.R, dm.GK, dm.GV
    att_cols = A_Q_DIM + 2 * A_KV_DIM
    rw_src = 3 * R + 2 * DECAY_LORA + 2 * ICLR_LORA + GATE_LORA
    gla_src = 2 * GK + 2 * GV + 2 * G_GATE_RANK
    att, rwkv, gla, gate = jnp.split(w_in, np.cumsum([att_cols, rw_src, gla_src]).tolist(), axis=-1)
    a_q, a_k, a_v = att[:, :A_Q_DIM], att[:, A_Q_DIM:A_Q_DIM + A_KV_DIM], att[:, A_Q_DIM + A_KV_DIM:]
    g_q, g_k = gla[:, :GK], gla[:, GK:2 * GK]
    g_v, g_g = gla[:, 2 * GK:2 * GK + GV], gla[:, 2 * GK + GV:2 * GK + 2 * GV]
    g_ac = _pad_to(gla[:, 2 * GK + 2 * GV:], LORA_PAD, 1)
    parts = [_rwkv_cols(rwkv, R), a_k, gate, g_v, g_g, g_q, g_k, a_q, a_v, g_ac]
    parts.append(jnp.zeros((D, dm.NP - sum(p.shape[1] for p in parts)), w_in.dtype))
    return jnp.concatenate([p.astype(BF16) for p in parts], axis=-1)


def _rope_tables(S):
    rows = S // GRID_W
    row = jnp.repeat(jnp.arange(rows, dtype=F32), GRID_W)
    col = jnp.tile(jnp.arange(GRID_W, dtype=F32), rows)
    n_freq = A_HEAD_DIM // 4
    inv_freq = ROPE_BASE ** (-jnp.arange(n_freq, dtype=F32) / n_freq)
    ang_r, ang_c = row[:, None] * inv_freq, col[:, None] * inv_freq
    cos = jnp.concatenate([jnp.cos(ang_r)] * 2 + [jnp.cos(ang_c)] * 2, axis=-1)
    sin = jnp.concatenate([-jnp.sin(ang_r), jnp.sin(ang_r), -jnp.sin(ang_c), jnp.sin(ang_c)], axis=-1)
    return cos, sin


def kernel(x, c, ctx, c_ctx, ada_w, ada_b, ln_g, ln_b, ffn_w_gu, ffn_w_down, w_in, attn_sink, rwkv_conv, rwkv_w0,
           rwkv_w2, rwkv_a0, rwkv_a2, rwkv_g2, rwkv_k_k, rwkv_k_a, rwkv_r_k, rwkv_ln_w, rwkv_ln_b, gla_wa2, gla_ba,
           gla_norm_w, w_branch_att, w_branch_rwkv, w_branch_gla, w_out):
    B, S, D = x.shape
    CT = ctx.shape[1]
    depth = ada_w.shape[0]
    dm = _Dims(B, S, CT, D)
    R = dm.R
    alpha = float((2 * depth) ** 0.25)
    assert B + 1 <= MOD_ROWS

    cc = _pad_to(jnp.concatenate([c, c_ctx[None, :]], axis=0), MOD_ROWS, 0)
    mods_all = _ada(cc, ada_w, ada_b)
    cos, sin = _rope_tables(S)
    head_of_lane = jnp.arange(R) // R_HEAD_DIM
    e1 = (head_of_lane[:, None] == jnp.arange(LORA_PAD)[None, :]).astype(BF16)
    e2 = e1.T

    w_gu, w_dn = ffn_w_gu.astype(BF16), ffn_w_down.astype(BF16)
    wba, wbr = w_branch_att.astype(BF16), w_branch_rwkv.astype(BF16)
    wbg, wo = w_branch_gla.astype(BF16), w_out.astype(BF16)

    xs = jnp.concatenate([ctx.reshape(dm.MC, D), x.reshape(dm.ML, D)], axis=0)
    for l in range(depth):
        last = l == depth - 1
        mods = mods_all[l].reshape(MOD_ROWS * N_MOD, 1, D)
        xs = _ffn(dm, alpha, xs, mods, 0, w_gu, w_dn, l, 0, ln_g[l, 0], ln_b[l, 0])

        z = _inproj(dm, xs, mods, _in_proj_weights(dm, w_in[l]))
        o_att_lat = _attn_latent(dm, z, attn_sink[l], cos, sin)
        o_att_ctx = None if last else _attn_ctx(dm, z, attn_sink[l])

        rp = {
            'conv': _rwkv_cols(rwkv_conv[l], R),
            'g2': rwkv_g2[l].astype(BF16),
            'w0': rwkv_w0[l], 'a0': rwkv_a0[l],
            'w2': _pad_to(rwkv_w2[l], LORA_PAD, 1).astype(BF16),
            'a2': _pad_to(rwkv_a2[l], LORA_PAD, 1).astype(BF16),
            'k_k': rwkv_k_k[l].reshape(1, R), 'k_a': rwkv_k_a[l].reshape(1, R), 'r_k': rwkv_r_k[l].reshape(1, R),
            'ln_w': rwkv_ln_w[l].reshape(1, R), 'ln_b': rwkv_ln_b[l].reshape(1, R),
            'e1': e1, 'e2': e2,
        }
        r, v, kk, lw, kd, b, bonus, g = _rwkv_prep(dm, z, rp)
        g_m, h_m, q_m, y0_m = _rwkv_chunk(dm, r, v, kk, lw, kd, b)
        y_rwkv_f, y_rwkv_b = _rwkv_scan(dm, g_m, h_m, q_m, y0_m)

        wa_p = jnp.stack([_pad_to(jnp.pad(gla_wa2[l, d], ((d * G_GATE_RANK, 0), (0, 0))), LORA_PAD, 0)
                          for d in range(2)])
        wa_hi = wa_p.astype(BF16)
        wa_split = jnp.stack([wa_hi, (wa_p - wa_hi.astype(F32)).astype(BF16)], axis=1)
        y_gla_f, y_gla_b = _gla(dm, z, wa_split, gla_ba[l].reshape(2, 1, dm.GK))

        xs = _merge(dm, alpha, xs, mods, z, o_att_ctx, o_att_lat, (y_rwkv_f, y_rwkv_b, bonus, g, rp),
                    (y_gla_f, y_gla_b, gla_norm_w[l]), wba, wbr, wbg, wo, l, ln_g[l, 1], ln_b[l, 1])
        xs = _ffn(dm, alpha, xs, mods, 6, w_gu, w_dn, l, 1, ln_g[l, 2], ln_b[l, 2])
    return xs.reshape(B, S, D)
```
